```python
import math
import jax, jax.numpy as jnp
from jax import lax
import numpy as np

D_MODEL = 1024
BATCH = 4
SEQ = 8192
DEPTH = 2
DEC_BATCH = 32
DEC_SEQ = 4
PAST_LEN = 16384
PAGE_SIZE = 128

A_HEADS = 8
A_HD = 64
A_W = A_HEADS * A_HD
A_LORA_W = 64
A_LORA_A = 64
A_LORA_G = 128
LNX_EPS = 64e-5
L2_EPS = 1e-24
B_GROUPS = 4
B_GW = 128
B_W = B_GROUPS * B_GW
POOL_WINDOWS = (2, 4, 8, 16)
POOL_BUF = 15
C_HEADS = 8
C_HD = 64
C_W = C_HEADS * C_HD
C_KV = 2
C_HPG = C_HEADS // C_KV
C_KVW = C_KV * C_HD
CMP_BLOCK = 64
N_SEL = 16
WINDOW = 512
Q_BLOCK = 128
FORCE_SCORE = 1e4
NEG = -1e30
N_BUCKETS = 32
MAX_DIST = 128
N_MEM = 256
M_HEADS = 4
M_HD = 128
M_W = M_HEADS * M_HD
D_FF = 4 * D_MODEL
RMS_EPS = 1e-6

OFF_R = 0
OFF_WL = OFF_R + A_W
OFF_K = OFF_WL + A_LORA_W
OFF_V = OFF_K + A_W
OFF_AL = OFF_V + A_W
OFF_GL = OFF_AL + A_LORA_A
A_COLS = OFF_GL + A_LORA_G
OFF_POOL = A_COLS
OFF_Q = OFF_POOL + B_W
OFF_KV = OFF_Q + C_W
OFF_CG = OFF_KV + 6 * C_KVW
OFF_MG = OFF_CG + 3 * C_HEADS
N_IN = OFF_MG + 3 * D_MODEL

kernel_name = 'hybrid_rwkv7_pool_nsa_decoder_step'


def rms_norm(x, g):
    xf = x.astype(jnp.float32)
    y = xf * lax.rsqrt(jnp.mean(xf * xf, axis=-1, keepdims=True) + RMS_EPS)
    return (y * g.astype(jnp.float32)).astype(x.dtype)


def rel_bucket(dist):
    n = jnp.maximum(dist, 0)
    exact = N_BUCKETS // 2
    ratio = jnp.log(jnp.maximum(n, 1).astype(jnp.float32) / exact) / math.log(MAX_DIST / exact)
    large = jnp.minimum(exact + (ratio * (N_BUCKETS - exact)).astype(jnp.int32), N_BUCKETS - 1)
    return jnp.where(n < exact, n, large)


def masked_softmax(logits, mask):
    p = jax.nn.softmax(jnp.where(mask, logits.astype(jnp.float32), NEG), axis=-1)
    return p * mask


def rwkv_mix(u, u_last, s0, mu, w0, w2, a0, a2, g2, k_k, k_a, r_k, lnx_g, lnx_b):
    bn, t = u.shape[:2]
    u_prev = jnp.concatenate([u_last[:, None, :].astype(u.dtype), u[:, :-1]], axis=1)
    xs = u + (u_prev - u) * mu
    r = xs[..., OFF_R:OFF_R + A_W]
    wl = xs[..., OFF_WL:OFF_WL + A_LORA_W]
    k = xs[..., OFF_K:OFF_K + A_W]
    v = xs[..., OFF_V:OFF_V + A_W]
    al = xs[..., OFF_AL:OFF_AL + A_LORA_A]
    gl = xs[..., OFF_GL:OFF_GL + A_LORA_G]
    w = -jax.nn.softplus(-(w0 + jnp.tanh(wl) @ w2)) - 0.5
    a = jax.nn.sigmoid(a0 + al @ a2)
    g = jax.nn.sigmoid(gl) @ g2
    heads = lambda z: z.astype(jnp.float32).reshape(bn, t, A_HEADS, A_HD)
    kk = heads(k * k_k)
    kk = kk * lax.rsqrt(jnp.maximum(jnp.sum(kk * kk, -1, keepdims=True), L2_EPS))
    k = k * (1.0 + (a - 1.0) * k_a)
    rh, kh, vh, ah = heads(r), heads(k), heads(v), heads(a)
    decay = jnp.exp(-jnp.exp(heads(w)))

    def step(S, inp):
        r_t, d_t, k_t, v_t, kk_t, a_t = inp
        sa = jnp.einsum('bhij,bhj->bhi', S, -kk_t)
        S = (S * d_t[:, :, None, :] + sa[..., None] * (kk_t * a_t)[:, :, None, :]
             + v_t[..., None] * k_t[:, :, None, :])
        return S, jnp.einsum('bhij,bhj->bhi', S, r_t)

    seq = tuple(jnp.swapaxes(z, 0, 1) for z in (rh, decay, kh, vh, kk, ah))
    s_new, ys = lax.scan(step, s0.astype(jnp.float32), seq)
    y = jnp.swapaxes(ys, 0, 1)
    mean = jnp.mean(y, -1, keepdims=True)
    var = jnp.mean(jnp.square(y - mean), -1, keepdims=True)
    y = ((y - mean) * lax.rsqrt(var + LNX_EPS)).reshape(bn, t, A_W) * lnx_g + lnx_b
    bonus = (jnp.sum(rh * kh * r_k, -1, keepdims=True) * vh).reshape(bn, t, A_W)
    return ((y + bonus) * g).astype(u.dtype), s_new, u[:, -1]


def pool_mix(u, u_buf, pos0, pm, pscale):
    bn, t = u.shape[:2]
    ext = jnp.concatenate([u_buf.astype(u.dtype), u], axis=1).astype(jnp.float32)
    c0 = jnp.concatenate([jnp.zeros((bn, 1, B_W), jnp.float32), jnp.cumsum(ext, axis=1)], axis=1)
    pos = pos0 + jnp.arange(t)
    cur = ext[:, POOL_BUF:]
    outs = []
    for gi, win in enumerate(POOL_WINDOWS):
        sl = slice(gi * B_GW, (gi + 1) * B_GW)
        s = c0[:, POOL_BUF + 1:, sl] - c0[:, POOL_BUF + 1 - win:POOL_BUF + 1 - win + t, sl]
        cnt = jnp.minimum(win, pos + 1).astype(jnp.float32)
        outs.append(s / cnt[None, :, None] - cur[..., sl])
    d = jnp.stack(outs, axis=2)
    z = jnp.einsum('btgc,gcd->btgd', d, pm).reshape(bn, t, B_W) * pscale
    return z.astype(u.dtype), ext[:, -POOL_BUF:].astype(u.dtype)


def nsa_inputs(proj, q_g, k_g):
    bn, t = proj.shape[:2]
    q = rms_norm(proj[..., OFF_Q:OFF_Q + C_W].reshape(bn, t, C_HEADS, C_HD), q_g)
    kv = proj[..., OFF_KV:OFF_KV + 6 * C_KVW].reshape(bn, t, 6, C_KV, C_HD)
    k_cmp, v_cmp = kv[:, :, 0], kv[:, :, 1]
    k_slc, v_slc = rms_norm(kv[:, :, 2], k_g[1]), kv[:, :, 3]
    k_win, v_win = rms_norm(kv[:, :, 4], k_g[2]), kv[:, :, 5]
    gates = proj[..., OFF_CG:OFF_CG + 3 * C_HEADS]
    return q, gates, (k_cmp, v_cmp, k_slc, v_slc, k_win, v_win)


def compress_kv(k_raw, v, kg):
    bn, tk = k_raw.shape[:2]
    nb = -(-tk // CMP_BLOCK)
    pad = nb * CMP_BLOCK - tk

    def blk_mean(z):
        z = jnp.pad(z.astype(jnp.float32), ((0, 0), (0, pad), (0, 0), (0, 0)))
        return z.reshape(bn, nb, CMP_BLOCK, C_KV, C_HD).mean(axis=2)

    return rms_norm(blk_mean(k_raw), kg), blk_mean(v)


def to_blocks(z):
    bn, tk = z.shape[:2]
    nb = -(-tk // CMP_BLOCK)
    z = jnp.pad(z, ((0, 0), (0, nb * CMP_BLOCK - tk), (0, 0), (0, 0)))
    return z.reshape(bn, nb, CMP_BLOCK, C_KV, C_HD).transpose(0, 3, 1, 2, 4)


def nsa_query_block(q, qpos, gates, kc, vc, ksb, vsb, kw, vw, kwpos, rel_bias):
    bn, tq = q.shape[:2]
    nb = kc.shape[1]
    scale = C_HD ** -0.5
    qg = q.reshape(bn, tq, C_KV, C_HPG, C_HD)
    rb = rel_bias.reshape(N_BUCKETS, C_KV, C_HPG)
    endpos = jnp.arange(nb) * CMP_BLOCK + (CMP_BLOCK - 1)
    dist_c = qpos[:, None] - endpos[None, :]
    bias_c = rb[rel_bucket(dist_c)].transpose(2, 3, 0, 1)
    lc = jnp.einsum('btgnd,bjgd->bgntj', qg, kc).astype(jnp.float32) * scale + bias_c
    pc = masked_softmax(lc, dist_c >= 0)
    o_cmp = jnp.einsum('bgntj,bjgd->btgnd', pc, vc)
    blk = jnp.arange(nb)
    cur = (qpos // CMP_BLOCK)[:, None]
    forced = (blk == 0) | (blk == cur) | (blk == cur - 1)
    score = jnp.where(blk > cur, NEG, jnp.where(forced, FORCE_SCORE, pc.sum(axis=2)))
    ksel = min(N_SEL, nb)
    top_val, top_idx = lax.top_k(score, ksel)
    bi = jnp.arange(bn)[:, None, None, None]
    gi = jnp.arange(C_KV)[None, :, None, None]
    n_sel_keys = ksel * CMP_BLOCK
    ks = ksb[bi, gi, top_idx].reshape(bn, C_KV, tq, n_sel_keys, C_HD)
    vs = vsb[bi, gi, top_idx].reshape(bn, C_KV, tq, n_sel_keys, C_HD)
    kpos5 = top_idx[..., None] * CMP_BLOCK + jnp.arange(CMP_BLOCK)
    valid = ((top_val > 0.5 * NEG)[..., None] & (kpos5 <= qpos[None, None, :, None, None]))
    valid = valid.reshape(bn, C_KV, tq, n_sel_keys)
    dist_s = qpos[None, None, :, None] - kpos5.reshape(bn, C_KV, tq, n_sel_keys)
    bias_s = rb[rel_bucket(dist_s), gi].transpose(0, 1, 4, 2, 3)
    ls = jnp.einsum('btgnd,bgtsd->bgnts', qg, ks).astype(jnp.float32) * scale + bias_s
    ps = masked_softmax(ls, valid[:, :, None])
    o_slc = jnp.einsum('bgnts,bgtsd->btgnd', ps, vs)
    dist_w = qpos[:, None] - kwpos[None, :]
    mask_w = (dist_w >= 0) & (dist_w < WINDOW) & (kwpos >= 0)[None, :]
    bias_w = rb[rel_bucket(dist_w)].transpose(2, 3, 0, 1)
    lw = jnp.einsum('btgnd,bsgd->bgnts', qg, kw).astype(jnp.float32) * scale + bias_w
    pw = masked_softmax(lw, mask_w)
    o_win = jnp.einsum('bgnts,bsgd->btgnd', pw, vw)
    gt = jax.nn.sigmoid(gates.astype(jnp.float32)).reshape(bn, tq, 3, C_KV, C_HPG)[..., None]
    o = gt[:, :, 0] * o_cmp + gt[:, :, 1] * o_slc + gt[:, :, 2] * o_win
    return o.reshape(bn, tq, C_W).astype(q.dtype)


def nsa_prompt(q, gates, k_cmp, v_cmp, k_slc, v_slc, k_win, v_win, kg_cmp, rel_bias):
    bn, t = q.shape[:2]
    kc, vc = compress_kv(k_cmp, v_cmp, kg_cmp)
    ksb, vsb = to_blocks(k_slc), to_blocks(v_slc)
    zpad = jnp.zeros((bn, WINDOW, C_KV, C_HD), k_win.dtype)
    kwp = jnp.concatenate([zpad, k_win], axis=1)
    vwp = jnp.concatenate([zpad.astype(v_win.dtype), v_win], axis=1)

    def body(i):
        s = i * Q_BLOCK
        sl = lambda z, n: lax.dynamic_slice_in_dim(z, s, n, axis=1)
        qpos = s + jnp.arange(Q_BLOCK)
        kwpos = s - WINDOW + jnp.arange(WINDOW + Q_BLOCK)
        return nsa_query_block(sl(q, Q_BLOCK), qpos, sl(gates, Q_BLOCK), kc, vc, ksb, vsb,
                               sl(kwp, WINDOW + Q_BLOCK), sl(vwp, WINDOW + Q_BLOCK), kwpos, rel_bias)

    out = lax.map(body, jnp.arange(t // Q_BLOCK, dtype=jnp.int32))
    return jnp.swapaxes(out, 0, 1).reshape(bn, t, C_W)


def nsa_sample(q, gates, k_cmp, v_cmp, k_slc, v_slc, k_win, v_win, pool_ck, pool_cv, pool_sk, pool_sv,
               win_k, win_v, page_table, kg_cmp, rel_bias):
    bn, tq = q.shape[:2]
    past = page_table.shape[1] * PAGE_SIZE

    def logical(pool, new):
        rows = pool[page_table].reshape((bn, past) + pool.shape[2:])
        return jnp.concatenate([rows.astype(new.dtype), new], axis=1)

    kc, vc = compress_kv(logical(pool_ck, k_cmp), logical(pool_cv, v_cmp), kg_cmp)
    ksb, vsb = to_blocks(logical(pool_sk, k_slc)), to_blocks(logical(pool_sv, v_slc))
    kw = jnp.concatenate([win_k.astype(k_win.dtype), k_win], axis=1)
    vw = jnp.concatenate([win_v.astype(v_win.dtype), v_win], axis=1)
    wbuf = win_k.shape[1]
    kwpos = past - wbuf + jnp.arange(wbuf + tq)
    qpos = past + jnp.arange(tq)
    out = nsa_query_block(q, qpos, gates, kc, vc, ksb, vsb, kw, vw, kwpos, rel_bias)
    return out, kw[:, -wbuf:], vw[:, -wbuf:]


def merge_branches(x, proj, o_a, o_b, o_c, wa, wb, wc, wo):
    bn, t = x.shape[:2]
    gate = jax.nn.sigmoid(proj[..., OFF_MG:OFF_MG + 3 * D_MODEL].reshape(bn, t, 3, D_MODEL))
    h = gate[:, :, 0] * (o_a @ wa) + gate[:, :, 1] * (o_b @ wb) + gate[:, :, 2] * (o_c @ wc)
    return x + h @ wo


def mem_kv(mem, g_mem, w_k, w_v, k_g):
    bn, m = mem.shape[:2]
    mn = rms_norm(mem, g_mem)
    k = rms_norm((mn @ w_k).reshape(bn, m, M_HEADS, M_HD), k_g)
    v = (mn @ w_v).reshape(bn, m, M_HEADS, M_HD)
    return k, v


def mem_ffn(x, mk, mv, g_x, w_q, q_g, w_o, g_f, w1, w2):
    bn, t = x.shape[:2]
    q = rms_norm((rms_norm(x, g_x) @ w_q).reshape(bn, t, M_HEADS, M_HD), q_g)
    logits = jnp.einsum('bthd,bmhd->bhtm', q, mk).astype(jnp.float32) * (M_HD ** -0.5)
    p = jax.nn.softmax(logits, axis=-1)
    o = jnp.einsum('bhtm,bmhd->bthd', p, mv).reshape(bn, t, M_W).astype(x.dtype)
    x = x + o @ w_o
    h = jnp.square(jax.nn.relu(rms_norm(x, g_f) @ w1))
    return x + h @ w2


def setup_inputs(seed: int = 0) -> dict:
    key = jax.random.key(seed)
    ks = iter(jax.random.split(key, 64))
    nrm = lambda shape, s: jax.random.normal(next(ks), shape, jnp.float32) * s
    uni = lambda shape, lo, hi: jax.random.uniform(next(ks), shape, jnp.float32, lo, hi)
    gain = lambda shape: 1.0 + nrm(shape, 0.05)
    n_pages = PAST_LEN // PAGE_SIZE
    n_phys = (DEC_BATCH * n_pages * 5) // 4
    wbuf = min(WINDOW, PAST_LEN)
    perm = jax.random.permutation(next(ks), n_phys)
    page_table = perm[:DEC_BATCH * n_pages].reshape(DEC_BATCH, n_pages).astype(jnp.int32)
    pool_shape = (DEPTH, n_phys, PAGE_SIZE, C_KV, C_HD)
    return {
        'x_prompt': nrm((BATCH, SEQ, D_MODEL), 1.0),
        'x_sample': nrm((DEC_BATCH, DEC_SEQ, D_MODEL), 1.0),
        'state_rwkv': nrm((DEPTH, DEC_BATCH, A_HEADS, A_HD, A_HD), 0.3),
        'state_shift': nrm((DEPTH, DEC_BATCH, A_COLS), 1.0),
        'state_pool': nrm((DEPTH, DEC_BATCH, POOL_BUF, B_W), 1.0),
        'cache_cmp_k': nrm(pool_shape, 1.0),
        'cache_cmp_v': nrm(pool_shape, 1.0),
        'cache_slc_k': nrm(pool_shape, 1.0),
        'cache_slc_v': nrm(pool_shape, 1.0),
        'cache_win_k': nrm((DEPTH, DEC_BATCH, wbuf, C_KV, C_HD), 1.0),
        'cache_win_v': nrm((DEPTH, DEC_BATCH, wbuf, C_KV, C_HD), 1.0),
        'cache_mem_k': nrm((DEPTH, DEC_BATCH, N_MEM, M_HEADS, M_HD), 1.0),
        'cache_mem_v': nrm((DEPTH, DEC_BATCH, N_MEM, M_HEADS, M_HD), 1.0),
        'page_table': page_table,
        'mem_prompt': nrm((BATCH, N_MEM, D_MODEL), 1.0),
        'rel_bias': nrm((N_BUCKETS, C_HEADS), 0.5),
        'norm_mix_g': gain((DEPTH, D_MODEL)),
        'w_in': nrm((DEPTH, D_MODEL, N_IN), D_MODEL ** -0.5),
        'shift_mu': uni((DEPTH, A_COLS), 0.0, 1.0),
        'rwkv_w0': uni((DEPTH, A_W), -3.0, 0.0),
        'rwkv_w2': nrm((DEPTH, A_LORA_W, A_W), 0.1),
        'rwkv_a0': nrm((DEPTH, A_W), 0.5),
        'rwkv_a2': nrm((DEPTH, A_LORA_A, A_W), 0.1),
        'rwkv_g2': nrm((DEPTH, A_LORA_G, A_W), A_LORA_G ** -0.5),
        'rwkv_kk': 0.85 + nrm((DEPTH, A_W), 0.05),
        'rwkv_ka': 1.0 + nrm((DEPTH, A_W), 0.1),
        'rwkv_rk': nrm((DEPTH, A_HEADS, A_HD), 0.1),
        'rwkv_lnx_g': gain((DEPTH, A_W)),
        'rwkv_lnx_b': nrm((DEPTH, A_W), 0.01),
        'pool_w': nrm((DEPTH, B_GROUPS, B_GW, B_GW), B_GW ** -0.5),
        'pool_scale': 1.0 + nrm((DEPTH, B_W), 0.1),
        'nsa_q_g': gain((DEPTH, C_HD)),
        'nsa_k_g': gain((DEPTH, 3, C_HD)),
        'w_br_a': nrm((DEPTH, A_W, D_MODEL), A_W ** -0.5),
        'w_br_b': nrm((DEPTH, B_W, D_MODEL), B_W ** -0.5),
        'w_br_c': nrm((DEPTH, C_W, D_MODEL), C_W ** -0.5),
        'w_out': nrm((DEPTH, D_MODEL, D_MODEL), D_MODEL ** -0.5),
        'norm_memx_g': gain((DEPTH, D_MODEL)),
        'norm_mem_g': gain((DEPTH, D_MODEL)),
        'w_mq': nrm((DEPTH, D_MODEL, M_W), D_MODEL ** -0.5),
        'w_mk': nrm((DEPTH, D_MODEL, M_W), D_MODEL ** -0.5),
        'w_mv': nrm((DEPTH, D_MODEL, M_W), D_MODEL ** -0.5),
        'mem_q_g': gain((DEPTH, M_HD)),
        'mem_k_g': gain((DEPTH, M_HD)),
        'w_mo': nrm((DEPTH, M_W, D_MODEL), M_W ** -0.5),
        'norm_ffn_g': gain((DEPTH, D_MODEL)),
        'w_ff1': nrm((DEPTH, D_MODEL, D_FF), D_MODEL ** -0.5),
        'w_ff2': nrm((DEPTH, D_FF, D_MODEL), D_FF ** -0.5),
    }


def reference(x_prompt, x_sample, state_rwkv, state_shift, state_pool,
              cache_cmp_k, cache_cmp_v, cache_slc_k, cache_slc_v,
              cache_win_k, cache_win_v, cache_mem_k, cache_mem_v, page_table,
              mem_prompt, rel_bias, norm_mix_g, w_in, shift_mu,
              rwkv_w0, rwkv_w2, rwkv_a0, rwkv_a2, rwkv_g2, rwkv_kk, rwkv_ka,
              rwkv_rk, rwkv_lnx_g, rwkv_lnx_b, pool_w, pool_scale,
              nsa_q_g, nsa_k_g, w_br_a, w_br_b, w_br_c, w_out,
              norm_memx_g, norm_mem_g, w_mq, w_mk, w_mv, mem_q_g, mem_k_g, w_mo,
              norm_ffn_g, w_ff1, w_ff2):
    bp, t = x_prompt.shape[:2]
    past = page_table.shape[1] * PAGE_SIZE
    wbuf_p = min(WINDOW, t)
    xp, xsm = x_prompt, x_sample
    p_rwkv, p_shift, p_pool, p_ck, p_cv, p_sk, p_sv, p_wk, p_wv, p_mk, p_mv = ([] for _ in range(11))
    s_rwkv, s_shift, s_pool, s_ck, s_cv, s_sk, s_sv, s_wk, s_wv = ([] for _ in range(9))
    for l in range(DEPTH):
        rw = (shift_mu[l], rwkv_w0[l], rwkv_w2[l], rwkv_a0[l], rwkv_a2[l], rwkv_g2[l],
              rwkv_kk[l], rwkv_ka[l], rwkv_rk[l], rwkv_lnx_g[l], rwkv_lnx_b[l])
        proj = rms_norm(xp, norm_mix_g[l]) @ w_in[l]
        o_a, s_a, last_a = rwkv_mix(proj[..., :A_COLS], jnp.zeros((bp, A_COLS), proj.dtype),
                                    jnp.zeros((bp, A_HEADS, A_HD, A_HD), jnp.float32), *rw)
        o_b, buf_b = pool_mix(proj[..., OFF_POOL:OFF_POOL + B_W], jnp.zeros((bp, POOL_BUF, B_W), proj.dtype),
                              0, pool_w[l], pool_scale[l])
        q, gates, (kc_, vc_, ks_, vs_, kw_, vw_) = nsa_inputs(proj, nsa_q_g[l], nsa_k_g[l])
        o_c = nsa_prompt(q, gates, kc_, vc_, ks_, vs_, kw_, vw_, nsa_k_g[l][0], rel_bias)
        xp = merge_branches(xp, proj, o_a, o_b, o_c, w_br_a[l], w_br_b[l], w_br_c[l], w_out[l])
        mk, mv = mem_kv(mem_prompt, norm_mem_g[l], w_mk[l], w_mv[l], mem_k_g[l])
        xp = mem_ffn(xp, mk, mv, norm_memx_g[l], w_mq[l], mem_q_g[l], w_mo[l],
                     norm_ffn_g[l], w_ff1[l], w_ff2[l])
        p_rwkv.append(s_a); p_shift.append(last_a); p_pool.append(buf_b)
        p_ck.append(kc_); p_cv.append(vc_); p_sk.append(ks_); p_sv.append(vs_)
        p_wk.append(kw_[:, -wbuf_p:]); p_wv.append(vw_[:, -wbuf_p:]); p_mk.append(mk); p_mv.append(mv)
        proj = rms_norm(xsm, norm_mix_g[l]) @ w_in[l]
        o_a, s_a, last_a = rwkv_mix(proj[..., :A_COLS], state_shift[l], state_rwkv[l], *rw)
        o_b, buf_b = pool_mix(proj[..., OFF_POOL:OFF_POOL + B_W], state_pool[l], past,
                              pool_w[l], pool_scale[l])
        q, gates, (kc_, vc_, ks_, vs_, kw_, vw_) = nsa_inputs(proj, nsa_q_g[l], nsa_k_g[l])
        o_c, wk_new, wv_new = nsa_sample(q, gates, kc_, vc_, ks_, vs_, kw_, vw_,
                                         cache_cmp_k[l], cache_cmp_v[l], cache_slc_k[l], cache_slc_v[l],
                                         cache_win_k[l], cache_win_v[l], page_table, nsa_k_g[l][0], rel_bias)
        xsm = merge_branches(xsm, proj, o_a, o_b, o_c, w_br_a[l], w_br_b[l], w_br_c[l], w_out[l])
        xsm = mem_ffn(xsm, cache_mem_k[l], cache_mem_v[l], norm_memx_g[l], w_mq[l], mem_q_g[l], w_mo[l],
                      norm_ffn_g[l], w_ff1[l], w_ff2[l])
        s_rwkv.append(s_a); s_shift.append(last_a); s_pool.append(buf_b)
        s_ck.append(kc_); s_cv.append(vc_); s_sk.append(ks_); s_sv.append(vs_)
        s_wk.append(wk_new); s_wv.append(wv_new)
    return (xp, xsm,
            jnp.stack(p_rwkv), jnp.stack(p_shift), jnp.stack(p_pool),
            jnp.stack(p_ck), jnp.stack(p_cv), jnp.stack(p_sk), jnp.stack(p_sv),
            jnp.stack(p_wk), jnp.stack(p_wv), jnp.stack(p_mk), jnp.stack(p_mv),
            jnp.stack(s_rwkv), jnp.stack(s_shift), jnp.stack(s_pool),
            jnp.stack(s_ck), jnp.stack(s_cv), jnp.stack(s_sk), jnp.stack(s_sv),
            jnp.stack(s_wk), jnp.stack(s_wv))
```

```python
import functools
import math

import jax
import jax.numpy as jnp
from jax import lax
from jax.experimental import pallas as pl
from jax.experimental.pallas import tpu as pltpu

F32 = jnp.float32
BF16 = jnp.bfloat16

D_MODEL = 1024
DEPTH = 2
PAGE_SIZE = 128
A_HEADS = 8
A_HD = 64
A_W = A_HEADS * A_HD
A_LORA_W = 64
A_LORA_A = 64
A_LORA_G = 128
A_LORA = A_LORA_W + A_LORA_A + A_LORA_G
LNX_EPS = 64e-5
L2_EPS = 1e-24
B_GROUPS = 4
B_GW = 128
B_W = B_GROUPS * B_GW
POOL_WINDOWS = (2, 4, 8, 16)
POOL_BUF = 15
POOL_HALO = 16
C_HEADS = 8
C_HD = 64
C_W = C_HEADS * C_HD
C_KV = 2
C_HPG = C_HEADS // C_KV
C_KVW = C_KV * C_HD
CMP_BLOCK = 64
N_SEL = 16
WINDOW = 512
Q_TILE = 128
FORCE_SCORE = 1e4
NEG = -1e30
N_BUCKETS = 32
MAX_DIST = 128
N_MEM = 256
M_HEADS = 4
M_HD = 128
M_W = M_HEADS * M_HD
D_FF = 4 * D_MODEL
RMS_EPS = 1e-6

OFF_R = 0
OFF_WL = OFF_R + A_W
OFF_K = OFF_WL + A_LORA_W
OFF_V = OFF_K + A_W
OFF_AL = OFF_V + A_W
OFF_GL = OFF_AL + A_LORA_A
A_COLS = OFF_GL + A_LORA_G
OFF_POOL = A_COLS
OFF_Q = OFF_POOL + B_W
OFF_KV = OFF_Q + C_W
OFF_CG = OFF_KV + 6 * C_KVW
OFF_MG = OFF_CG + 3 * C_HEADS
N_IN = OFF_MG + 3 * D_MODEL

P_R = 0
P_K = P_R + A_W
P_V = P_K + A_W
P_POOL = P_V + A_W
P_Q = P_POOL + B_W
P_KV = P_Q + C_W
P_LORA = P_KV + 6 * C_KVW
N_MAIN = P_LORA + A_LORA
G_MG = 0
G_CG = 3 * D_MODEL
N_GATE = G_CG + 128

RWKV_CHUNK = 64
RWKV_SUB = 16
LANES = 128
VMEM_LIMIT = 56 * 1024 * 1024


def _cparams(sem):
    return pltpu.CompilerParams(dimension_semantics=sem, vmem_limit_bytes=VMEM_LIMIT)


def _dot(a, b, dims=(((1,), (0,)), ((), ()))):
    return lax.dot_general(a.astype(BF16), b.astype(BF16), dims, preferred_element_type=F32)


def _dot_nt(a, b):
    return _dot(a, b, (((1,), (1,)), ((), ())))


def _dot_tn(a, b):
    return _dot(a, b, (((0,), (0,)), ((), ())))


def _split3(x):
    hi = x.astype(BF16)
    r1 = x - hi.astype(F32)
    mid = r1.astype(BF16)
    lo = (r1 - mid.astype(F32)).astype(BF16)
    return hi, mid, lo


def _dot_exact_rhs(a, b01):
    hi, mid, lo = _split3(a)
    dims = (((1,), (0,)), ((), ()))
    dg = lambda p: lax.dot_general(p, b01, dims, preferred_element_type=F32)
    return dg(hi) + dg(mid) + dg(lo)


def _dot_exact_lhs(a01, b):
    hi, mid, lo = _split3(b)
    dims = (((1,), (0,)), ((), ()))
    dg = lambda p: lax.dot_general(a01, p, dims, preferred_element_type=F32)
    return dg(hi) + dg(mid) + dg(lo)


def _sigmoid(x):
    return 1.0 / (1.0 + jnp.exp(-x))


def _rms(x, g):
    return x * lax.rsqrt(jnp.mean(x * x, axis=-1, keepdims=True) + RMS_EPS) * g


def _norm_matmul_kernel(x_ref, g_ref, w_ref, o_ref, xn_ref):
    @pl.when(pl.program_id(1) == 0)
    def _():
        xn_ref[...] = _rms(x_ref[...], g_ref[...]).astype(BF16)

    o_ref[...] = jnp.dot(xn_ref[...], w_ref[...], preferred_element_type=F32)


def _norm_matmul(x, g, w, tm, tn):
    m, k = x.shape
    n = w.shape[1]
    return pl.pallas_call(
        _norm_matmul_kernel,
        grid=(m // tm, n // tn),
        in_specs=[pl.BlockSpec((tm, k), lambda i, j: (i, 0)),
                  pl.BlockSpec((1, k), lambda i, j: (0, 0)),
                  pl.BlockSpec((k, tn), lambda i, j: (0, j))],
        out_specs=pl.BlockSpec((tm, tn), lambda i, j: (i, j)),
        out_shape=jax.ShapeDtypeStruct((m, n), F32),
        scratch_shapes=[pltpu.VMEM((tm, k), BF16)],
        compiler_params=_cparams(("parallel", "arbitrary")),
        name="norm_matmul",
    )(x, g.reshape(1, k), w)


def _rwkv_kernel(r_ref, k_ref, v_ref, lora_ref, ulast_ref, s0_ref, mu_ref, w0_ref, w2_ref, a0_ref,
                 a2_ref, g2_ref, kk_ref, ka_ref, rk_ref, lng_ref, lnb_ref,
                 o_ref, sout_ref, s_ref, prev_ref, *, chunk, t_valid):
    c = pl.program_id(1)
    nc = pl.num_programs(1)
    C = chunk

    @pl.when(c == 0)
    def _():
        s_ref[...] = s0_ref[0]
        prev_ref[...] = ulast_ref[0]

    row = lax.broadcasted_iota(jnp.int32, (C, 1), 0)
    first = row == 0

    def shift_mix(u, lo, hi):
        prev = prev_ref[:, lo:hi]
        u_prev = jnp.where(first, prev, pltpu.roll(u, 1, 0))
        prev_ref[:, lo:hi] = u[C - 1:C, :]
        return u + (u_prev - u) * mu_ref[:, lo:hi]

    r = shift_mix(r_ref[0], 0, A_W)
    k = shift_mix(k_ref[0], A_W, 2 * A_W)
    v = shift_mix(v_ref[0], 2 * A_W, 3 * A_W)
    lora = shift_mix(lora_ref[0], 3 * A_W, 3 * A_W + A_LORA)
    wl = lora[:, 0:A_LORA_W]
    al = lora[:, A_LORA_W:A_LORA_W + A_LORA_A]
    gl = lora[:, A_LORA_W + A_LORA_A:A_LORA]

    z = -(w0_ref[...] + _dot(jnp.tanh(wl), w2_ref[...]))
    softplus = jnp.maximum(z, 0.0) + jnp.log(1.0 + jnp.exp(-jnp.abs(z)))
    w = -softplus - 0.5
    a = _sigmoid(a0_ref[...] + _dot(al, a2_ref[...]))
    g = _dot(_sigmoid(gl), g2_ref[...])

    lane = lax.broadcasted_iota(jnp.int32, (A_W, A_W), 1) // A_HD
    sub = lax.broadcasted_iota(jnp.int32, (A_W, A_W), 0) // A_HD
    head_ones = (lane == sub).astype(BF16)

    kkv = k * kk_ref[...]
    kkn = kkv * lax.rsqrt(jnp.maximum(_dot_exact_rhs(kkv * kkv, head_ones), L2_EPS))
    k2 = k * (1.0 + (a - 1.0) * ka_ref[...])
    log_d = -jnp.exp(w)
    if t_valid is not None:
        live = (c * C + row) < t_valid
        log_d = jnp.where(live, log_d, 0.0)
        kkn = jnp.where(live, kkn, 0.0)
        k2 = jnp.where(live, k2, 0.0)

    ti = lax.broadcasted_iota(jnp.int32, (C, C), 0)
    si = lax.broadcasted_iota(jnp.int32, (C, C), 1)
    lower = si <= ti
    strict = si < ti
    cum = _dot_exact_lhs(lower.astype(BF16), log_d)
    c_incl = jnp.exp(cum)
    c_excl = jnp.exp(cum - log_d)
    c_inv = jnp.exp(-cum)
    a_t = -kkn * c_excl
    r_t = r * c_incl
    b_t = kkn * a * c_inv
    k_t = k2 * c_inv
    c_end = c_incl[C - 1:C, :]

    sub_blk = (ti // RWKV_SUB) == (si // RWKV_SUB)
    eye = (ti == si).astype(F32)
    lane128 = lax.broadcasted_iota(jnp.int32, (1, 2 * A_HD), 1)
    blk_i = lax.broadcasted_iota(jnp.int32, (2 * A_HD, 2 * A_HD), 0) // A_HD
    blk_j = lax.broadcasted_iota(jnp.int32, (2 * A_HD, 2 * A_HD), 1) // A_HD
    same_head = blk_i == blk_j

    y_slabs = []
    for p in range(A_HEADS // 2):
        sl = slice(2 * A_HD * p, 2 * A_HD * (p + 1))
        ar = jnp.concatenate([a_t[:, sl], r_t[:, sl]], axis=0)
        bk = jnp.concatenate([b_t[:, sl], k_t[:, sl]], axis=0)
        v_p = v[:, sl]
        s_p = s_ref[p]
        w0 = _dot_nt(ar, s_p)
        u_p = jnp.zeros((C, 2 * A_HD), F32)
        y_p = w0[C:, :]
        for hh in range(2):
            in_head = (lane128 // A_HD) == hh
            gram = _dot_nt(jnp.where(in_head, ar, 0.0), bk)
            l_b = jnp.where(strict, gram[:C, :C], 0.0)
            l_k = jnp.where(strict, gram[:C, C:], 0.0)
            m_b = jnp.where(lower, gram[C:, :C], 0.0)
            m_k = jnp.where(lower, gram[C:, C:], 0.0)
            v_h = jnp.where(in_head, v_p, 0.0)
            dg = jnp.where(sub_blk, l_b, 0.0)
            off = l_b - dg
            t_inv = eye + dg
            pw = dg
            for _ in range(int(math.log2(RWKV_SUB)) - 1):
                pw = _dot(pw, pw)
                t_inv = t_inv + _dot(t_inv, pw)
            n1 = _dot(t_inv, off)
            n2 = _dot(n1, n1)
            full = eye + n1 + n2 + _dot(n1, n2)
            t_full = _dot(full, t_inv)
            rhs = jnp.where(in_head, w0[:C, :], 0.0) + _dot(l_k, v_h)
            u_h = _dot(t_full, rhs)
            y_p = y_p + _dot(jnp.concatenate([m_b, m_k], axis=1), jnp.concatenate([u_h, v_h], axis=0))
            u_p = u_p + u_h
        uv = jnp.concatenate([u_p, v_p], axis=0)
        s_new = s_p + jnp.where(same_head, _dot_tn(uv, bk), 0.0)
        s_ref[p] = s_new * c_end[:, sl]
        y_slabs.append(y_p)
    y = jnp.concatenate(y_slabs, axis=1)

    inv_n = 1.0 / A_HD
    mean = _dot_exact_rhs(y, head_ones) * inv_n
    yc = y - mean
    var = _dot_exact_rhs(yc * yc, head_ones) * inv_n
    yn = yc * lax.rsqrt(var + LNX_EPS) * lng_ref[...] + lnb_ref[...]
    bonus = _dot_exact_rhs(r * k2 * rk_ref[...], head_ones) * v
    o_ref[0] = (yn + bonus) * g

    @pl.when(c == nc - 1)
    def _():
        sout_ref[0] = s_ref[...]


def _rwkv(proj, ulast, s0, prm, t_valid=None):
    b, t, _ = proj.shape
    C = min(RWKV_CHUNK, t)
    nc = t // C
    vec = lambda n: pl.BlockSpec((1, n), lambda i, c: (0, 0))
    mat = lambda m, n: pl.BlockSpec((m, n), lambda i, c: (0, 0))
    col = lambda width, off: pl.BlockSpec((1, C, width), lambda i, c: (i, c, off // width))
    n_shift = 3 * A_W + A_LORA
    kern = functools.partial(_rwkv_kernel, chunk=C, t_valid=t_valid)
    return pl.pallas_call(
        kern,
        grid=(b, nc),
        in_specs=[col(A_W, P_R), col(A_W, P_K), col(A_W, P_V), col(A_LORA, P_LORA),
                  pl.BlockSpec((1, 1, n_shift), lambda i, c: (i, 0, 0)),
                  pl.BlockSpec((1, 4, 128, 128), lambda i, c: (i, 0, 0, 0)),
                  vec(n_shift), vec(A_W), mat(A_LORA_W, A_W), vec(A_W), mat(A_LORA_A, A_W),
                  mat(A_LORA_G, A_W), vec(A_W), vec(A_W), vec(A_W), vec(A_W), vec(A_W)],
        out_specs=[pl.BlockSpec((1, C, A_W), lambda i, c: (i, c, 0)),
                   pl.BlockSpec((1, 4, 128, 128), lambda i, c: (i, 0, 0, 0))],
        out_shape=[jax.ShapeDtypeStruct((b, t, A_W), F32),
                   jax.ShapeDtypeStruct((b, 4, 128, 128), F32)],
        scratch_shapes=[pltpu.VMEM((4, 128, 128), F32), pltpu.VMEM((1, n_shift), F32)],
        compiler_params=_cparams(("parallel", "arbitrary")),
        name="rwkv_chunk",
    )(proj, proj, proj, proj, ulast, s0, *prm)


def _pair_states(s):
    b = s.shape[0]
    s = s.reshape(b, 4, 2, A_HD, A_HD)
    z = jnp.zeros_like(s[:, :, 0])
    top = jnp.concatenate([s[:, :, 0], z], axis=-1)
    bot = jnp.concatenate([z, s[:, :, 1]], axis=-1)
    return jnp.concatenate([top, bot], axis=-2)


def _unpair_states(sp):
    b = sp.shape[0]
    h0 = sp[:, :, :A_HD, :A_HD]
    h1 = sp[:, :, A_HD:, A_HD:]
    return jnp.stack([h0, h1], axis=2).reshape(b, A_HEADS, A_HD, A_HD)


def _pool_kernel(u_ref, halo_ref, hist_ref, pm_ref, ps_ref, o_ref, ext_ref, *, tm, pos0):
    i = pl.program_id(1)

    @pl.when(i == 0)
    def _():
        ext_ref[0:POOL_HALO, :] = hist_ref[0]

    @pl.when(i > 0)
    def _():
        ext_ref[0:POOL_HALO, :] = halo_ref[0]

    cur = u_ref[0]
    ext_ref[POOL_HALO:POOL_HALO + tm, :] = cur
    pos = pos0 + i * tm + lax.broadcasted_iota(jnp.int32, (tm, 1), 0)
    outs = []
    for gi, win in enumerate(POOL_WINDOWS):
        lo, hi = gi * B_GW, (gi + 1) * B_GW
        s = cur[:, lo:hi]
        for back in range(1, win):
            s = s + ext_ref[POOL_HALO - back:POOL_HALO - back + tm, lo:hi]
        cnt = jnp.minimum(win, pos + 1).astype(F32)
        d = s / cnt - cur[:, lo:hi]
        outs.append(_dot(d, pm_ref[gi]))
    o_ref[0] = jnp.concatenate(outs, axis=1) * ps_ref[...]


def _pool(proj, hist, pm, ps, pos0, tm):
    b, t, _ = proj.shape
    per = tm // POOL_HALO
    kern = functools.partial(_pool_kernel, tm=tm, pos0=pos0)
    return pl.pallas_call(
        kern,
        grid=(b, t // tm),
        in_specs=[pl.BlockSpec((1, tm, B_W), lambda bi, i: (bi, i, P_POOL // B_W)),
                  pl.BlockSpec((1, POOL_HALO, B_W),
                               lambda bi, i: (bi, jnp.maximum(i * per - 1, 0), P_POOL // B_W)),
                  pl.BlockSpec((1, POOL_HALO, B_W), lambda bi, i: (bi, 0, 0)),
                  pl.BlockSpec((B_GROUPS, B_GW, B_GW), lambda bi, i: (0, 0, 0)),
                  pl.BlockSpec((1, B_W), lambda bi, i: (0, 0))],
        out_specs=pl.BlockSpec((1, tm, B_W), lambda bi, i: (bi, i, 0)),
        out_shape=jax.ShapeDtypeStruct((b, t, B_W), F32),
        scratch_shapes=[pltpu.VMEM((tm + POOL_HALO, B_W), F32)],
        compiler_params=_cparams(("parallel", "arbitrary")),
        name="pool_mix",
    )(proj, proj, hist, pm, ps.reshape(1, B_W))


def _merge_kernel(x_ref, oa_ref, ob_ref, oc_ref, mg_ref, wa_ref, wb_ref, wc_ref, wo_ref, o_ref):
    gate = lambda n: _sigmoid(mg_ref[:, n * D_MODEL:(n + 1) * D_MODEL])
    h = (gate(0) * _dot(oa_ref[...], wa_ref[...]) + gate(1) * _dot(ob_ref[...], wb_ref[...])
         + gate(2) * _dot(oc_ref[...], wc_ref[...]))
    o_ref[...] = x_ref[...] + _dot(h, wo_ref[...])


def _merge(x, oa, ob, oc, gates, wa, wb, wc, wo, tm):
    m = x.shape[0]
    row = lambda n: pl.BlockSpec((tm, n), lambda i: (i, 0))
    full = lambda a: pl.BlockSpec(a.shape, lambda i: (0, 0))
    return pl.pallas_call(
        _merge_kernel,
        grid=(m // tm,),
        in_specs=[row(D_MODEL), row(A_W), row(B_W), row(C_W), row(3 * D_MODEL),
                  full(wa), full(wb), full(wc), full(wo)],
        out_specs=row(D_MODEL),
        out_shape=jax.ShapeDtypeStruct((m, D_MODEL), F32),
        compiler_params=_cparams(("parallel",)),
        name="merge_branches",
    )(x, oa, ob, oc, gates, wa, wb, wc, wo)


def _mem_kv_kernel(mem_ref, g_ref, wk_ref, wv_ref, kg_ref, k_ref, v_ref):
    mn = _rms(mem_ref[...], g_ref[...]).astype(BF16)
    k = jnp.dot(mn, wk_ref[...], preferred_element_type=F32)
    v_ref[...] = jnp.dot(mn, wv_ref[...], preferred_element_type=F32)
    k_ref[...] = jnp.concatenate(
        [_rms(k[:, h * M_HD:(h + 1) * M_HD], kg_ref[...]) for h in range(M_HEADS)], axis=1)


def _mem_kv(mem, g, wk, wv, kg):
    m = mem.shape[0]
    tm = N_MEM
    row = lambda n: pl.BlockSpec((tm, n), lambda i: (i, 0))
    full = lambda a: pl.BlockSpec(a.shape, lambda i: (0, 0))
    g2, kg2 = g.reshape(1, D_MODEL), kg.reshape(1, M_HD)
    return pl.pallas_call(
        _mem_kv_kernel,
        grid=(m // tm,),
        in_specs=[row(D_MODEL), full(g2), full(wk), full(wv), full(kg2)],
        out_specs=[row(M_W), row(M_W)],
        out_shape=[jax.ShapeDtypeStruct((m, M_W), F32), jax.ShapeDtypeStruct((m, M_W), F32)],
        compiler_params=_cparams(("parallel",)),
        name="mem_kv",
    )(mem, g2, wk, wv, kg2)


def _mem_ffn_kernel(x_ref, mk_ref, mv_ref, gx_ref, wq_ref, qg_ref, wo_ref, gf_ref, w1_ref, w2_ref,
                    o_ref, xn_ref, *, tm, rows_per_batch, nkb):
    j = pl.program_id(1)

    @pl.when(j == 0)
    def _():
        x = x_ref[...]
        q = _dot(_rms(x, gx_ref[...]), wq_ref[...])
        mk = mk_ref[...].reshape(nkb * N_MEM, M_W)
        mv = mv_ref[...].reshape(nkb * N_MEM, M_W)
        if nkb > 1:
            qb = lax.broadcasted_iota(jnp.int32, (tm, nkb * N_MEM), 0) // rows_per_batch
            kb = lax.broadcasted_iota(jnp.int32, (tm, nkb * N_MEM), 1) // N_MEM
            same = qb == kb
        outs = []
        for h in range(M_HEADS):
            sl = slice(h * M_HD, (h + 1) * M_HD)
            qh = _rms(q[:, sl], qg_ref[...])
            logits = _dot_nt(qh, mk[:, sl]) * (M_HD ** -0.5)
            if nkb > 1:
                logits = jnp.where(same, logits, NEG)
            mx = jnp.max(logits, axis=-1, keepdims=True)
            p = jnp.exp(logits - mx)
            p = p / jnp.sum(p, axis=-1, keepdims=True)
            outs.append(_dot(p, mv[:, sl]))
        o = jnp.concatenate(outs, axis=1)
        xm = x + _dot(o, wo_ref[...])
        o_ref[...] = xm
        xn_ref[...] = _rms(xm, gf_ref[...]).astype(BF16)

    h1 = jnp.dot(xn_ref[...], w1_ref[...], preferred_element_type=F32)
    h1 = jnp.square(jnp.maximum(h1, 0.0))
    o_ref[...] += _dot(h1, w2_ref[...])


def _mem_ffn(x, mk, mv, gx, wq, qg, wo, gf, w1, w2, tm, rows_per_batch, tf=1024):
    m = x.shape[0]
    nkb = max(tm // rows_per_batch, 1)
    kern = functools.partial(_mem_ffn_kernel, tm=tm, rows_per_batch=rows_per_batch, nkb=nkb)
    full = lambda a: pl.BlockSpec(a.shape, lambda i, j: (0,) * a.ndim)
    gx2, qg2, gf2 = gx.reshape(1, D_MODEL), qg.reshape(1, M_HD), gf.reshape(1, D_MODEL)
    mem_spec = pl.BlockSpec((nkb, N_MEM, M_W), lambda i, j: ((i * tm) // (rows_per_batch * nkb), 0, 0))
    return pl.pallas_call(
        kern,
        grid=(m // tm, D_FF // tf),
        in_specs=[pl.BlockSpec((tm, D_MODEL), lambda i, j: (i, 0)), mem_spec, mem_spec,
                  full(gx2), full(wq), full(qg2), full(wo), full(gf2),
                  pl.BlockSpec((D_MODEL, tf), lambda i, j: (0, j)),
                  pl.BlockSpec((tf, D_MODEL), lambda i, j: (j, 0))],
        out_specs=pl.BlockSpec((tm, D_MODEL), lambda i, j: (i, 0)),
        out_shape=jax.ShapeDtypeStruct((m, D_MODEL), F32),
        scratch_shapes=[pltpu.VMEM((tm, D_MODEL), BF16)],
        compiler_params=_cparams(("parallel", "arbitrary")),
        name="mem_ffn",
    )(x, mk, mv, gx2, wq, qg2, wo, gf2, w1, w2)


def _half_ones():
    i = lax.broadcasted_iota(jnp.int32, (LANES, LANES), 0) // C_HD
    j = lax.broadcasted_iota(jnp.int32, (LANES, LANES), 1) // C_HD
    return (i == j).astype(BF16)


def _rms_halves(x, g, ones):
    ms = _dot_exact_rhs(x * x, ones) * (1.0 / C_HD)
    return x * lax.rsqrt(ms + RMS_EPS) * g


def _nsa_prep_kernel(q_ref, c_ref, s_ref, w_ref, qg_ref, kg_ref,
                     qn_ref, ks_ref, ks16_ref, vs16_ref, kw_ref, kw16_ref, vw16_ref, kc_ref, vc_ref,
                     *, tm):
    ones = _half_ones()
    scale = C_HD ** -0.5
    q = q_ref[0]
    qn_ref[0] = jnp.concatenate(
        [_rms_halves(q[:, n * LANES:(n + 1) * LANES], qg_ref[...], ones) * scale for n in range(C_HPG)],
        axis=1).astype(BF16)
    s = s_ref[0]
    ks = _rms_halves(s[:, :LANES], kg_ref[1:2, :], ones)
    ks_ref[0] = ks
    ks16_ref[0] = ks.astype(BF16)
    vs16_ref[0] = s[:, LANES:].astype(BF16)
    w = w_ref[0]
    kw = _rms_halves(w[:, :LANES], kg_ref[2:3, :], ones)
    kw_ref[0] = kw
    kw16_ref[0] = kw.astype(BF16)
    vw16_ref[0] = w[:, LANES:].astype(BF16)
    nb = tm // CMP_BLOCK
    bi = lax.broadcasted_iota(jnp.int32, (nb, tm), 0)
    ti = lax.broadcasted_iota(jnp.int32, (nb, tm), 1) // CMP_BLOCK
    pool = (bi == ti).astype(BF16)
    c = c_ref[0]
    means = _dot_exact_lhs(pool, c) * (1.0 / CMP_BLOCK)
    kc_ref[0] = _rms_halves(means[:, :LANES], kg_ref[0:1, :], ones)
    vc_ref[0] = means[:, LANES:]


def _nsa_prep(proj, qg, kg, tm):
    b, t, _ = proj.shape
    nbt = tm // CMP_BLOCK
    kern = functools.partial(_nsa_prep_kernel, tm=tm)
    tok = lambda n, dt: jax.ShapeDtypeStruct((b, t, n), dt)
    blk = jax.ShapeDtypeStruct((b, t // CMP_BLOCK, LANES), F32)
    col = lambda width, off: pl.BlockSpec((1, tm, width), lambda bi, i: (bi, i, off // width))
    out_tok = lambda n: pl.BlockSpec((1, tm, n), lambda bi, i: (bi, i, 0))
    out_blk = pl.BlockSpec((1, nbt, LANES), lambda bi, i: (bi, i, 0))
    return pl.pallas_call(
        kern,
        grid=(b, t // tm),
        in_specs=[col(C_W, P_Q), col(2 * LANES, P_KV), col(2 * LANES, P_KV + 2 * LANES),
                  col(2 * LANES, P_KV + 4 * LANES),
                  pl.BlockSpec((1, LANES), lambda bi, i: (0, 0)),
                  pl.BlockSpec((3, LANES), lambda bi, i: (0, 0))],
        out_specs=[out_tok(C_W), out_tok(LANES), out_tok(LANES), out_tok(LANES), out_tok(LANES),
                   out_tok(LANES), out_tok(LANES), out_blk, out_blk],
        out_shape=[tok(C_W, BF16), tok(LANES, F32), tok(LANES, BF16), tok(LANES, BF16),
                   tok(LANES, F32), tok(LANES, BF16), tok(LANES, BF16), blk, blk],
        compiler_params=_cparams(("parallel", "parallel")),
        name="nsa_prep",
    )(proj, proj, proj, proj, jnp.tile(qg.reshape(1, C_HD), (1, 2)), jnp.tile(kg, (1, 2)))


M_INIT = -1e29
ROWS = C_HPG * Q_TILE


def _softmax_step(q, k, v, m_ref, l_ref, acc_ref, bias=None, valid=None):
    s = lax.dot_general(q, k, (((1,), (1,)), ((), ())), preferred_element_type=F32)
    s3 = s.reshape(C_HPG, Q_TILE, s.shape[-1])
    if bias is not None:
        s3 = s3 + bias
    if valid is not None:
        s3 = jnp.where(valid[None], s3, NEG)
    s = s3.reshape(ROWS, s.shape[-1])
    m_old = m_ref[...]
    m_new = jnp.maximum(m_old, jnp.max(s, axis=-1, keepdims=True))
    alpha = jnp.exp(m_old - m_new)
    p = jnp.exp(s - m_new)
    l_ref[...] = alpha * l_ref[...] + jnp.sum(p, axis=-1, keepdims=True)
    acc_ref[...] = alpha * acc_ref[...] + jnp.dot(p.astype(BF16), v, preferred_element_type=F32)
    m_ref[...] = m_new


def _softmax_reset(m_ref, l_ref, acc_ref):
    m_ref[...] = jnp.full(m_ref.shape, M_INIT, F32)
    l_ref[...] = jnp.zeros(l_ref.shape, F32)
    acc_ref[...] = jnp.zeros(acc_ref.shape, F32)


def _softmax_result(l_ref, acc_ref):
    l = l_ref[...]
    return acc_ref[...] / jnp.where(l > 0.0, l, 1.0)


def _top_blocks(score, n_sel):
    nb = score.shape[-1]
    lane = lax.broadcasted_iota(jnp.int32, score.shape, 1).astype(F32)
    sel = jnp.zeros(score.shape, F32)
    work = score
    for _ in range(min(n_sel, nb)):
        mx = jnp.max(work, axis=-1, keepdims=True)
        idx = jnp.min(jnp.where(work == mx, lane, float(nb)), axis=-1, keepdims=True)
        pick = lane == idx
        sel = jnp.where(pick, 1.0, sel)
        work = jnp.where(pick, -jnp.inf, work)
    return sel > 0.5


def _nsa_attn_kernel(qn_ref, ks_ref, vs_ref, kw_ref, vw_ref, kc_ref, vc_ref, cg_ref, bnear_ref,
                     bcmp_ref, o_ref, m_ref, l_ref, acc_ref, *, nb):
    i = pl.program_id(1)
    q_all = qn_ref[0]
    lane = lax.broadcasted_iota(jnp.int32, (Q_TILE, LANES), 1)
    q_rel = lax.broadcasted_iota(jnp.int32, (Q_TILE, LANES), 0)
    causal = lane <= q_rel
    sig = _sigmoid(cg_ref[0])
    kc = kc_ref[0].astype(BF16)
    vc = vc_ref[0].astype(BF16)
    blk = lax.broadcasted_iota(jnp.int32, (Q_TILE, nb), 1)
    qpos = i * Q_TILE + lax.broadcasted_iota(jnp.int32, (Q_TILE, nb), 0)
    cur = qpos // CMP_BLOCK
    cmp_valid = qpos >= blk * CMP_BLOCK + (CMP_BLOCK - 1)
    tile = lambda ref, j: ref[0, pl.ds(pl.multiple_of(j * Q_TILE, Q_TILE), Q_TILE), :]

    combined = []
    for g in range(C_KV):
        half = (lane // C_HD) == g
        q = jnp.concatenate(
            [jnp.where(half, q_all[:, n * LANES:(n + 1) * LANES], 0.0) for n in range(C_HPG)],
            axis=0).astype(BF16)

        bias_c = pltpu.roll(bcmp_ref[g], (2 * i + 2) % LANES, 1)[:, :nb]
        lc = lax.dot_general(q, kc, (((1,), (1,)), ((), ())), preferred_element_type=F32) + bias_c
        lc3 = jnp.where(cmp_valid[None], lc.reshape(C_HPG, Q_TILE, nb), NEG)
        mc = jnp.maximum(jnp.max(lc3, axis=-1, keepdims=True), M_INIT)
        pc = jnp.exp(lc3 - mc)
        den = jnp.sum(pc, axis=-1, keepdims=True)
        pc = pc / jnp.where(den > 0.0, den, 1.0)
        o_cmp = jnp.dot(pc.reshape(ROWS, nb).astype(BF16), vc, preferred_element_type=F32)

        forced = (blk == 0) | (blk == cur) | (blk == cur - 1)
        score = jnp.where(blk > cur, NEG, jnp.where(forced, FORCE_SCORE, jnp.sum(pc, axis=0)))
        sel = _top_blocks(score, N_SEL) & (score > 0.5 * NEG)
        sel16 = sel.astype(BF16)

        def key_mask(j):
            row = lax.broadcasted_iota(jnp.int32, (nb, LANES), 0)
            col = lax.broadcasted_iota(jnp.int32, (nb, LANES), 1) // CMP_BLOCK
            expand = (row == 2 * j + col).astype(BF16)
            return jnp.dot(sel16, expand, preferred_element_type=F32) > 0.5

        _softmax_reset(m_ref, l_ref, acc_ref)

        def far(j, carry):
            _softmax_step(q, tile(ks_ref, j), tile(vs_ref, j), m_ref, l_ref, acc_ref,
                          valid=key_mask(j))
            return carry

        lax.fori_loop(0, i - 1, far, 0)

        @pl.when(i >= 1)
        def _():
            _softmax_step(q, tile(ks_ref, i - 1), tile(vs_ref, i - 1), m_ref, l_ref, acc_ref,
                          bias=bnear_ref[g, 1].reshape(C_HPG, Q_TILE, LANES), valid=key_mask(i - 1))

        _softmax_step(q, tile(ks_ref, i), tile(vs_ref, i), m_ref, l_ref, acc_ref,
                      bias=bnear_ref[g, 0].reshape(C_HPG, Q_TILE, LANES), valid=key_mask(i) & causal)
        o_slc = _softmax_result(l_ref, acc_ref)

        _softmax_reset(m_ref, l_ref, acc_ref)
        n_back = WINDOW // Q_TILE
        for back in range(n_back, 1, -1):
            @pl.when(i >= back)
            def _(back=back):
                _softmax_step(q, tile(kw_ref, i - back), tile(vw_ref, i - back), m_ref, l_ref, acc_ref,
                              valid=(lane > q_rel) if back == n_back else None)

        @pl.when(i >= 1)
        def _():
            _softmax_step(q, tile(kw_ref, i - 1), tile(vw_ref, i - 1), m_ref, l_ref, acc_ref,
                          bias=bnear_ref[g, 1].reshape(C_HPG, Q_TILE, LANES))

        _softmax_step(q, tile(kw_ref, i), tile(vw_ref, i), m_ref, l_ref, acc_ref,
                      bias=bnear_ref[g, 0].reshape(C_HPG, Q_TILE, LANES), valid=causal)
        o_win = _softmax_result(l_ref, acc_ref)

        per_head = []
        for n in range(C_HPG):
            h = g * C_HPG + n
            rows = slice(n * Q_TILE, (n + 1) * Q_TILE)
            gate = lambda br: sig[:, br * C_HEADS + h:br * C_HEADS + h + 1]
            per_head.append(gate(0) * o_cmp[rows] + gate(1) * o_slc[rows] + gate(2) * o_win[rows])
        combined.append(per_head)

    low = (lane // C_HD) == 0
    o_ref[0] = jnp.concatenate(
        [jnp.where(low, combined[0][n], combined[1][n]) for n in range(C_HPG)], axis=1)


def _nsa_attn(qn, ks16, vs16, kw16, vw16, kc, vc, gates, bnear, bcmp):
    b, t, _ = qn.shape
    nb = t // CMP_BLOCK
    kern = functools.partial(_nsa_attn_kernel, nb=nb)
    seq = pl.BlockSpec((1, t, LANES), lambda bi, i: (bi, 0, 0))
    blk = pl.BlockSpec((1, nb, LANES), lambda bi, i: (bi, 0, 0))
    return pl.pallas_call(
        kern,
        grid=(b, t // Q_TILE),
        in_specs=[pl.BlockSpec((1, Q_TILE, C_W), lambda bi, i: (bi, i, 0)), seq, seq, seq, seq, blk, blk,
                  pl.BlockSpec((1, Q_TILE, LANES), lambda bi, i: (bi, i, G_CG // LANES)),
                  pl.BlockSpec(bnear.shape, lambda bi, i: (0, 0, 0, 0)),
                  pl.BlockSpec(bcmp.shape, lambda bi, i: (0, 0, 0))],
        out_specs=pl.BlockSpec((1, Q_TILE, C_W), lambda bi, i: (bi, i, 0)),
        out_shape=jax.ShapeDtypeStruct((b, t, C_W), F32),
        scratch_shapes=[pltpu.VMEM((ROWS, 1), F32), pltpu.VMEM((ROWS, 1), F32),
                        pltpu.VMEM((ROWS, LANES), F32)],
        compiler_params=_cparams(("parallel", "arbitrary")),
        name="nsa_attn",
    )(qn, ks16, vs16, kw16, vw16, kc, vc, gates, bnear, bcmp)


def _rel_bucket(dist):
    n = jnp.maximum(dist, 0)
    exact = N_BUCKETS // 2
    ratio = jnp.log(jnp.maximum(n, 1).astype(F32) / exact) / math.log(MAX_DIST / exact)
    large = jnp.minimum(exact + (ratio * (N_BUCKETS - exact)).astype(jnp.int32), N_BUCKETS - 1)
    return jnp.where(n < exact, n, large)


def _bias_tables(rel_bias):
    tbl = rel_bias[_rel_bucket(jnp.arange(MAX_DIST)), :]
    tbl = tbl - tbl[MAX_DIST - 1:MAX_DIST, :]
    tbl = tbl.T.reshape(C_KV, C_HPG, MAX_DIST)
    qq = jnp.arange(Q_TILE)[:, None]
    cc = jnp.arange(LANES)[None, :]
    look = lambda dist: tbl[:, :, jnp.clip(dist, 0, MAX_DIST - 1)].reshape(C_KV, ROWS, LANES)
    bnear = jnp.stack([look(qq - cc), look(qq - cc + Q_TILE)], axis=1)
    bcmp = look(qq - CMP_BLOCK * cc + CMP_BLOCK * (LANES - 1) - (LANES - 1))
    return bnear, bcmp


def _rms_last(x, g):
    return x * lax.rsqrt(jnp.mean(x * x, axis=-1, keepdims=True) + RMS_EPS) * g


def _masked_softmax(logits, mask):
    p = jax.nn.softmax(jnp.where(mask, logits, NEG), axis=-1)
    return p * mask


def _nsa_sample_jax(q, gates, kvs, pools, win_k, win_v, page_table, kg_cmp, rel_bias):
    k_cmp, v_cmp, k_slc, v_slc, k_win, v_win = kvs
    bn, tq = q.shape[:2]
    past = page_table.shape[1] * PAGE_SIZE
    logical = lambda pool, new: jnp.concatenate(
        [pool[page_table].reshape((bn, past) + pool.shape[2:]), new], axis=1)
    nb = -(-(past + tq) // CMP_BLOCK)
    pad = nb * CMP_BLOCK - (past + tq)
    padded = lambda z: jnp.pad(z, ((0, 0), (0, pad), (0, 0), (0, 0)))
    blocks = lambda z: padded(z).reshape(bn, nb, CMP_BLOCK, C_KV, C_HD)
    kc = _rms_last(blocks(logical(pools[0], k_cmp)).mean(axis=2), kg_cmp)
    vc = blocks(logical(pools[1], v_cmp)).mean(axis=2)
    ksb = blocks(logical(pools[2], k_slc)).transpose(0, 3, 1, 2, 4)
    vsb = blocks(logical(pools[3], v_slc)).transpose(0, 3, 1, 2, 4)
    kw = jnp.concatenate([win_k, k_win], axis=1)
    vw = jnp.concatenate([win_v, v_win], axis=1)
    wbuf = win_k.shape[1]
    kwpos = past - wbuf + jnp.arange(wbuf + tq)
    qpos = past + jnp.arange(tq)
    scale = C_HD ** -0.5
    qg = q.reshape(bn, tq, C_KV, C_HPG, C_HD)
    rb = rel_bias.reshape(N_BUCKETS, C_KV, C_HPG)
    endpos = jnp.arange(nb) * CMP_BLOCK + (CMP_BLOCK - 1)
    dist_c = qpos[:, None] - endpos[None, :]
    bias_c = rb[_rel_bucket(dist_c)].transpose(2, 3, 0, 1)
    lc = jnp.einsum('btgnd,bjgd->bgntj', qg, kc) * scale + bias_c
    pc = _masked_softmax(lc, dist_c >= 0)
    o_cmp = jnp.einsum('bgntj,bjgd->btgnd', pc, vc)
    blk = jnp.arange(nb)
    cur = (qpos // CMP_BLOCK)[:, None]
    forced = (blk == 0) | (blk == cur) | (blk == cur - 1)
    score = jnp.where(blk > cur, NEG, jnp.where(forced, FORCE_SCORE, pc.sum(axis=2)))
    ksel = min(N_SEL, nb)
    top_val, top_idx = lax.top_k(score, ksel)
    bi = jnp.arange(bn)[:, None, None, None]
    gi = jnp.arange(C_KV)[None, :, None, None]
    n_keys = ksel * CMP_BLOCK
    ks = ksb[bi, gi, top_idx].reshape(bn, C_KV, tq, n_keys, C_HD)
    vs = vsb[bi, gi, top_idx].reshape(bn, C_KV, tq, n_keys, C_HD)
    kpos5 = top_idx[..., None] * CMP_BLOCK + jnp.arange(CMP_BLOCK)
    valid = ((top_val > 0.5 * NEG)[..., None] & (kpos5 <= qpos[None, None, :, None, None]))
    valid = valid.reshape(bn, C_KV, tq, n_keys)
    dist_s = qpos[None, None, :, None] - kpos5.reshape(bn, C_KV, tq, n_keys)
    bias_s = rb[_rel_bucket(dist_s), gi].transpose(0, 1, 4, 2, 3)
    ls = jnp.einsum('btgnd,bgtsd->bgnts', qg, ks) * scale + bias_s
    ps = _masked_softmax(ls, valid[:, :, None])
    o_slc = jnp.einsum('bgnts,bgtsd->btgnd', ps, vs)
    dist_w = qpos[:, None] - kwpos[None, :]
    mask_w = (dist_w >= 0) & (dist_w < WINDOW) & (kwpos >= 0)[None, :]
    bias_w = rb[_rel_bucket(dist_w)].transpose(2, 3, 0, 1)
    lw = jnp.einsum('btgnd,bsgd->bgnts', qg, kw) * scale + bias_w
    pw = _masked_softmax(lw, mask_w)
    o_win = jnp.einsum('bgnts,bsgd->btgnd', pw, vw)
    gt = jax.nn.sigmoid(gates).reshape(bn, tq, 3, C_KV, C_HPG)[..., None]
    o = gt[:, :, 0] * o_cmp + gt[:, :, 1] * o_slc + gt[:, :, 2] * o_win
    return o, kw[:, -wbuf:], vw[:, -wbuf:]


def _q_slab_order():
    c = jnp.arange(C_W)
    slab, half, d = c // LANES, (c % LANES) // C_HD, c % C_HD
    return (half * C_HPG + slab) * C_HD + d


def _layer_params(l, w_in, shift_mu, w_br_a, w_br_b, w_br_c, w_out, w_mq, w_mk, w_mv, w_mo, w_ff1, w_ff2):
    ar = jnp.arange
    q_order = _q_slab_order()
    main_cols = jnp.concatenate([
        OFF_R + ar(A_W), OFF_K + ar(A_W), OFF_V + ar(A_W), OFF_POOL + ar(B_W), OFF_Q + q_order,
        OFF_KV + ar(6 * C_KVW), OFF_WL + ar(A_LORA_W), OFF_AL + ar(A_LORA_A), OFF_GL + ar(A_LORA_G)])
    gate_cols = jnp.concatenate([OFF_MG + ar(3 * D_MODEL), OFF_CG + ar(3 * C_HEADS)])
    shift_cols = jnp.concatenate([OFF_R + ar(A_W), OFF_K + ar(A_W), OFF_V + ar(A_W), OFF_WL + ar(A_LORA_W),
                                  OFF_AL + ar(A_LORA_A), OFF_GL + ar(A_LORA_G)])
    w = w_in[l]
    w_gate = jnp.pad(w[:, gate_cols], ((0, 0), (0, N_GATE - gate_cols.shape[0])))
    cast = lambda a: a.astype(BF16)
    return dict(
        w_main=cast(w[:, main_cols]), w_gate=cast(w_gate), shift_cols=shift_cols,
        mu=shift_mu[l][shift_cols].reshape(1, -1),
        wa=cast(w_br_a[l]), wb=cast(w_br_b[l]), wc=cast(w_br_c[l][q_order]), wo=cast(w_out[l]),
        wq=cast(w_mq[l]), wk=cast(w_mk[l]), wv=cast(w_mv[l]), wmo=cast(w_mo[l]),
        w1=cast(w_ff1[l]), w2=cast(w_ff2[l]))


def _shift_row(main_row):
    lora = main_row[:, P_LORA:]
    return jnp.concatenate([
        main_row[:, P_R:P_R + A_W], lora[:, :A_LORA_W], main_row[:, P_K:P_K + A_W],
        main_row[:, P_V:P_V + A_W], lora[:, A_LORA_W:A_LORA_W + A_LORA_A],
        lora[:, A_LORA_W + A_LORA_A:]], axis=1)


def _kv_rows(main, which):
    b, t = main.shape[:2]
    lo = P_KV + which * C_KVW
    return main[:, :, lo:lo + C_KVW].reshape(b, t, C_KV, C_HD)


def kernel(x_prompt, x_sample, state_rwkv, state_shift, state_pool, cache_cmp_k, cache_cmp_v, cache_slc_k, cache_slc_v, cache_win_k, cache_win_v, cache_mem_k, cache_mem_v, page_table, mem_prompt, rel_bias, norm_mix_g, w_in, shift_mu, rwkv_w0, rwkv_w2, rwkv_a0, rwkv_a2, rwkv_g2, rwkv_kk, rwkv_ka, rwkv_rk, rwkv_lnx_g, rwkv_lnx_b, pool_w, pool_scale, nsa_q_g, nsa_k_g, w_br_a, w_br_b, w_br_c, w_out, norm_memx_g, norm_mem_g, w_mq, w_mk, w_mv, mem_q_g, mem_k_g, w_mo, norm_ffn_g, w_ff1, w_ff2):
    bp, t = x_prompt.shape[:2]
    bs, ts = x_sample.shape[:2]
    past = page_table.shape[1] * PAGE_SIZE
    wbuf_p = min(WINDOW, t)
    ts_pad = 16
    bnear, bcmp = _bias_tables(rel_bias)
    xp = x_prompt.reshape(bp * t, D_MODEL)
    xs = x_sample.reshape(bs * ts, D_MODEL)
    outs_p = [[] for _ in range(11)]
    outs_s = [[] for _ in range(9)]
    row = lambda a: a.reshape(1, -1)
    for l in range(DEPTH):
        prm = _layer_params(l, w_in, shift_mu, w_br_a, w_br_b, w_br_c, w_out, w_mq, w_mk, w_mv, w_mo,
                            w_ff1, w_ff2)
        rw = (prm["mu"], row(rwkv_w0[l]), rwkv_w2[l].astype(BF16), row(rwkv_a0[l]), rwkv_a2[l].astype(BF16),
              rwkv_g2[l].astype(BF16), row(rwkv_kk[l]), row(rwkv_ka[l]), row(rwkv_rk[l]),
              row(rwkv_lnx_g[l]), row(rwkv_lnx_b[l]))
        pool_wl = pool_w[l].astype(BF16)

        main = _norm_matmul(xp, norm_mix_g[l], prm["w_main"], 1024, 512).reshape(bp, t, N_MAIN)
        gate = _norm_matmul(xp, norm_mix_g[l], prm["w_gate"], 1024, 640)
        o_a, s_pairs = _rwkv(main, jnp.zeros((bp, 1, A_COLS), F32), jnp.zeros((bp, 4, 128, 128), F32), rw)
        o_b = _pool(main, jnp.zeros((bp, POOL_HALO, B_W), F32), pool_wl, pool_scale[l], 0, 512)
        qn, ks, ks16, vs16, kw, kw16, vw16, kc, vc = _nsa_prep(main, nsa_q_g[l], nsa_k_g[l], 512)
        o_c = _nsa_attn(qn, ks16, vs16, kw16, vw16, kc, vc, gate.reshape(bp, t, N_GATE), bnear, bcmp)
        xp = _merge(xp, o_a.reshape(bp * t, A_W), o_b.reshape(bp * t, B_W), o_c.reshape(bp * t, C_W), gate,
                    prm["wa"], prm["wb"], prm["wc"], prm["wo"], 512)
        mk, mv = _mem_kv(mem_prompt.reshape(bp * N_MEM, D_MODEL), norm_mem_g[l], prm["wk"], prm["wv"],
                         mem_k_g[l])
        mk, mv = mk.reshape(bp, N_MEM, M_W), mv.reshape(bp, N_MEM, M_W)
        xp = _mem_ffn(xp, mk, mv, norm_memx_g[l], prm["wq"], mem_q_g[l], prm["wmo"], norm_ffn_g[l],
                      prm["w1"], prm["w2"], 512, t)
        split = lambda z: z.reshape(bp, -1, C_KV, C_HD)
        for dst, val in zip(outs_p, (
                _unpair_states(s_pairs), _shift_row(main[:, -1]), main[:, -POOL_BUF:, P_POOL:P_POOL + B_W],
                _kv_rows(main, 0), _kv_rows(main, 1), split(ks), _kv_rows(main, 3),
                split(kw[:, -wbuf_p:]), _kv_rows(main, 5)[:, -wbuf_p:],
                mk.reshape(bp, N_MEM, M_HEADS, M_HD), mv.reshape(bp, N_MEM, M_HEADS, M_HD))):
            dst.append(val)

        main = _norm_matmul(xs, norm_mix_g[l], prm["w_main"], bs * ts, 512).reshape(bs, ts, N_MAIN)
        gate = _norm_matmul(xs, norm_mix_g[l], prm["w_gate"], bs * ts, 640)
        main_pad = jnp.pad(main, ((0, 0), (0, ts_pad - ts), (0, 0)))
        ulast = state_shift[l][:, prm["shift_cols"]].reshape(bs, 1, A_COLS)
        o_a, s_pairs = _rwkv(main_pad, ulast, _pair_states(state_rwkv[l]), rw, t_valid=ts)
        o_a = o_a[:, :ts]
        hist = jnp.concatenate([jnp.zeros((bs, POOL_HALO - POOL_BUF, B_W), F32), state_pool[l]], axis=1)
        o_b = _pool(main_pad, hist, pool_wl, pool_scale[l], past, ts_pad)[:, :ts]
        u_pool = main[:, :, P_POOL:P_POOL + B_W]
        q = main[:, :, P_Q:P_Q + C_W].reshape(bs, ts, C_HPG, C_KV, C_HD).transpose(0, 1, 3, 2, 4)
        q = _rms_last(q, nsa_q_g[l]).reshape(bs, ts, C_HEADS, C_HD)
        kvs = [_kv_rows(main, n) for n in range(6)]
        kvs[2] = _rms_last(kvs[2], nsa_k_g[l][1])
        kvs[4] = _rms_last(kvs[4], nsa_k_g[l][2])
        o_c, wk_new, wv_new = _nsa_sample_jax(
            q, gate[:, G_CG:G_CG + 3 * C_HEADS].reshape(bs, ts, 3 * C_HEADS), kvs,
            (cache_cmp_k[l], cache_cmp_v[l], cache_slc_k[l], cache_slc_v[l]),
            cache_win_k[l], cache_win_v[l], page_table, nsa_k_g[l][0], rel_bias)
        o_c = o_c.transpose(0, 1, 3, 2, 4).reshape(bs * ts, C_W)
        xs = _merge(xs, o_a.reshape(bs * ts, A_W), o_b.reshape(bs * ts, B_W), o_c, gate,
                    prm["wa"], prm["wb"], prm["wc"], prm["wo"], bs * ts)
        xs = _mem_ffn(xs, cache_mem_k[l].reshape(bs, N_MEM, M_W), cache_mem_v[l].reshape(bs, N_MEM, M_W),
                      norm_memx_g[l], prm["wq"], mem_q_g[l], prm["wmo"], norm_ffn_g[l],
                      prm["w1"], prm["w2"], 32, ts)
        for dst, val in zip(outs_s, (
                _unpair_states(s_pairs), _shift_row(main[:, -1]),
                jnp.concatenate([state_pool[l], u_pool], axis=1)[:, -POOL_BUF:],
                kvs[0], kvs[1], kvs[2], kvs[3], wk_new, wv_new)):
            dst.append(val)

    return ((xp.reshape(bp, t, D_MODEL), xs.reshape(bs, ts, D_MODEL))
            + tuple(jnp.stack(o) for o in outs_p) + tuple(jnp.stack(o) for o in outs_s))
```

```python
import functools
import math

import jax
import jax.numpy as jnp
from jax import lax
from jax.experimental import pallas as pl
from jax.experimental.pallas import tpu as pltpu

F32 = jnp.float32
BF16 = jnp.bfloat16

D_MODEL = 1024
DEPTH = 2
PAGE_SIZE = 128
A_HEADS = 8
A_HD = 64
A_W = A_HEADS * A_HD
A_LORA_W = 64
A_LORA_A = 64
A_LORA_G = 128
A_LORA = A_LORA_W + A_LORA_A + A_LORA_G
LNX_EPS = 64e-5
L2_EPS = 1e-24
B_GROUPS = 4
B_GW = 128
B_W = B_GROUPS * B_GW
POOL_WINDOWS = (2, 4, 8, 16)
POOL_BUF = 15
POOL_HALO = 16
C_HEADS = 8
C_HD = 64
C_W = C_HEADS * C_HD
C_KV = 2
C_HPG = C_HEADS // C_KV
C_KVW = C_KV * C_HD
CMP_BLOCK = 64
N_SEL = 16
WINDOW = 512
Q_TILE = 128
FORCE_SCORE = 1e4
NEG = -1e30
N_BUCKETS = 32
MAX_DIST = 128
N_MEM = 256
M_HEADS = 4
M_HD = 128
M_W = M_HEADS * M_HD
D_FF = 4 * D_MODEL
RMS_EPS = 1e-6

OFF_R = 0
OFF_WL = OFF_R + A_W
OFF_K = OFF_WL + A_LORA_W
OFF_V = OFF_K + A_W
OFF_AL = OFF_V + A_W
OFF_GL = OFF_AL + A_LORA_A
A_COLS = OFF_GL + A_LORA_G
OFF_POOL = A_COLS
OFF_Q = OFF_POOL + B_W
OFF_KV = OFF_Q + C_W
OFF_CG = OFF_KV + 6 * C_KVW
OFF_MG = OFF_CG + 3 * C_HEADS
N_IN = OFF_MG + 3 * D_MODEL

P_R = 0
P_K = P_R + A_W
P_V = P_K + A_W
P_POOL = P_V + A_W
P_Q = P_POOL + B_W
P_KV = P_Q + C_W
P_LORA = P_KV + 6 * C_KVW
N_MAIN = P_LORA + A_LORA
G_MG = 0
G_CG = 3 * D_MODEL
N_GATE = G_CG + 128

RWKV_CHUNK = 64
RWKV_SUB = 16
LANES = 128
VMEM_LIMIT = 56 * 1024 * 1024


def _cparams(sem):
    return pltpu.CompilerParams(dimension_semantics=sem, vmem_limit_bytes=VMEM_LIMIT)


def _dot(a, b, dims=(((1,), (0,)), ((), ()))):
    return lax.dot_general(a.astype(BF16), b.astype(BF16), dims, preferred_element_type=F32)


def _dot_nt(a, b):
    return _dot(a, b, (((1,), (1,)), ((), ())))


def _dot_tn(a, b):
    return _dot(a, b, (((0,), (0,)), ((), ())))


def _split3(x):
    hi = x.astype(BF16)
    r1 = x - hi.astype(F32)
    mid = r1.astype(BF16)
    lo = (r1 - mid.astype(F32)).astype(BF16)
    return hi, mid, lo


def _dot_exact_rhs(a, b01):
    hi, mid, lo = _split3(a)
    dims = (((1,), (0,)), ((), ()))
    dg = lambda p: lax.dot_general(p, b01, dims, preferred_element_type=F32)
    return dg(hi) + dg(mid) + dg(lo)


def _dot_exact_lhs(a01, b):
    hi, mid, lo = _split3(b)
    dims = (((1,), (0,)), ((), ()))
    dg = lambda p: lax.dot_general(a01, p, dims, preferred_element_type=F32)
    return dg(hi) + dg(mid) + dg(lo)


def _sigmoid(x):
    return 1.0 / (1.0 + jnp.exp(-x))


def _rms(x, g):
    return x * lax.rsqrt(jnp.mean(x * x, axis=-1, keepdims=True) + RMS_EPS) * g


def _norm_matmul_kernel(x_ref, g_ref, w_ref, o_ref, xn_ref):
    @pl.when(pl.program_id(1) == 0)
    def _():
        xn_ref[...] = _rms(x_ref[...], g_ref[...]).astype(BF16)

    o_ref[...] = jnp.dot(xn_ref[...], w_ref[...], preferred_element_type=F32)


def _norm_matmul(x, g, w, tm, tn):
    m, k = x.shape
    n = w.shape[1]
    return pl.pallas_call(
        _norm_matmul_kernel,
        grid=(m // tm, n // tn),
        in_specs=[pl.BlockSpec((tm, k), lambda i, j: (i, 0)),
                  pl.BlockSpec((1, k), lambda i, j: (0, 0)),
                  pl.BlockSpec((k, tn), lambda i, j: (0, j))],
        out_specs=pl.BlockSpec((tm, tn), lambda i, j: (i, j)),
        out_shape=jax.ShapeDtypeStruct((m, n), F32),
        scratch_shapes=[pltpu.VMEM((tm, k), BF16)],
        compiler_params=_cparams(("parallel", "arbitrary")),
        name="norm_matmul",
    )(x, g.reshape(1, k), w)


def _rwkv_kernel(r_ref, k_ref, v_ref, lora_ref, ulast_ref, s0_ref, mu_ref, w0_ref, w2_ref, a0_ref,
                 a2_ref, g2_ref, kk_ref, ka_ref, rk_ref, lng_ref, lnb_ref,
                 o_ref, sout_ref, s_ref, prev_ref, ar_ref, bk_ref, vb_ref, y_ref, *, chunk, bt, t_valid):
    c = pl.program_id(1)
    nc = pl.num_programs(1)
    C = chunk
    R = bt * C
    NP = A_HEADS // 2
    PB = 2 * C

    @pl.when(c == 0)
    def _():
        s_ref[...] = s0_ref[...].reshape(bt * NP, 2 * A_HD, 2 * A_HD)
        prev_ref[...] = ulast_ref[:, 0, :]

    row = lax.broadcasted_iota(jnp.int32, (R, 1), 0)

    def shift_mix(u3, lo, hi):
        u = u3.reshape(R, hi - lo)
        u_prev = pltpu.roll(u, 1, 0)
        for b in range(bt):
            u_prev = jnp.where(row == b * C, prev_ref[b:b + 1, lo:hi], u_prev)
        for b in range(bt):
            prev_ref[b:b + 1, lo:hi] = u[(b + 1) * C - 1:(b + 1) * C, :]
        return u + (u_prev - u) * mu_ref[:, lo:hi]

    r = shift_mix(r_ref[...], 0, A_W)
    k = shift_mix(k_ref[...], A_W, 2 * A_W)
    v = shift_mix(v_ref[...], 2 * A_W, 3 * A_W)
    lora = shift_mix(lora_ref[...], 3 * A_W, 3 * A_W + A_LORA)
    wl = lora[:, 0:A_LORA_W]
    al = lora[:, A_LORA_W:A_LORA_W + A_LORA_A]
    gl = lora[:, A_LORA_W + A_LORA_A:A_LORA]

    z = -(w0_ref[...] + _dot(jnp.tanh(wl), w2_ref[...]))
    softplus = jnp.maximum(z, 0.0) + jnp.log(1.0 + jnp.exp(-jnp.abs(z)))
    w = -softplus - 0.5
    a = _sigmoid(a0_ref[...] + _dot(al, a2_ref[...]))
    g = _dot(_sigmoid(gl), g2_ref[...])

    lane = lax.broadcasted_iota(jnp.int32, (A_W, A_W), 1) // A_HD
    sub = lax.broadcasted_iota(jnp.int32, (A_W, A_W), 0) // A_HD
    head_ones = (lane == sub).astype(BF16)

    kkv = k * kk_ref[...]
    kkn = kkv * lax.rsqrt(jnp.maximum(_dot_exact_rhs(kkv * kkv, head_ones), L2_EPS))
    k2 = k * (1.0 + (a - 1.0) * ka_ref[...])
    log_d = -jnp.exp(w)
    t_in = row % C
    if t_valid is not None:
        live = (c * C + t_in) < t_valid
        log_d = jnp.where(live, log_d, 0.0)
        kkn = jnp.where(live, kkn, 0.0)
        k2 = jnp.where(live, k2, 0.0)

    ri = lax.broadcasted_iota(jnp.int32, (R, R), 0)
    ci = lax.broadcasted_iota(jnp.int32, (R, R), 1)
    cum_mask = ((ri // C) == (ci // C)) & (ci <= ri)
    cum = _dot_exact_lhs(cum_mask.astype(BF16), log_d)
    c_incl = jnp.exp(cum)
    c_inv = jnp.exp(-cum)
    a_t = -kkn * jnp.exp(cum - log_d)
    r_t = r * c_incl
    b_t = kkn * a * c_inv
    k_t = k2 * c_inv

    lane = lax.broadcasted_iota(jnp.int32, (1, 2 * A_HD), 1) // A_HD
    def stage(ref, off, x):
        for b in range(bt):
            for p in range(NP):
                blk = x[b * C:(b + 1) * C, 2 * A_HD * p:2 * A_HD * (p + 1)]
                for hh in range(2):
                    ref[b * NP + p, off + hh * C:off + (hh + 1) * C, :] = (
                        jnp.where(lane == hh, blk, 0.0).astype(BF16))

    stage(ar_ref, 0, a_t)
    stage(ar_ref, PB, r_t)
    stage(bk_ref, 0, b_t)
    stage(bk_ref, PB, k_t)
    stage(vb_ref, 0, v)
    ar = ar_ref[...]
    bk = bk_ref[...]
    vb = vb_ref[...]

    nn = (((2,), (1,)), ((0,), (0,)))
    nt = (((2,), (2,)), ((0,), (0,)))
    tn = (((1,), (1,)), ((0,), (0,)))
    bdot = lambda x, y, dims=nn: lax.dot_general(x.astype(BF16), y.astype(BF16), dims,
                                                 preferred_element_type=F32)
    bi = lax.broadcasted_iota(jnp.int32, (PB, PB), 0)
    bj = lax.broadcasted_iota(jnp.int32, (PB, PB), 1)
    same_head = (bi // C) == (bj // C)
    strict = same_head & ((bj % C) < (bi % C))
    lower = same_head & ((bj % C) <= (bi % C))
    sub_blk = (bi // RWKV_SUB) == (bj // RWKV_SUB)
    eye = (bi == bj).astype(F32)

    gram = bdot(ar, bk, nt)
    l_b = jnp.where(strict, gram[:, :PB, :PB], 0.0)
    l_k = jnp.where(strict, gram[:, :PB, PB:], 0.0)
    m_b = jnp.where(lower, gram[:, PB:, :PB], 0.0)
    m_k = jnp.where(lower, gram[:, PB:, PB:], 0.0)
    dg = jnp.where(sub_blk, l_b, 0.0)
    off = l_b - dg
    t_inv = eye + dg
    pw = dg
    for _ in range(int(math.log2(RWKV_SUB)) - 1):
        pw = bdot(pw, pw)
        t_inv = t_inv + bdot(t_inv, pw)
    n1 = bdot(t_inv, off)
    n2 = bdot(n1, n1)
    full = eye + n1 + n2 + bdot(n1, n2)
    t_full = bdot(full, t_inv)

    s_old = s_ref[...]
    w0 = bdot(ar, s_old, nt)
    u = bdot(t_full, w0[:, :PB] + bdot(l_k, vb))
    uv = jnp.concatenate([u, vb.astype(F32)], axis=1)
    yb = w0[:, PB:] + bdot(jnp.concatenate([m_b, m_k], axis=2), uv)
    s_new = s_old + bdot(uv, bk, tn)
    for b in range(bt):
        for p in range(NP):
            sl = slice(2 * A_HD * p, 2 * A_HD * (p + 1))
            idx = b * NP + p
            s_ref[idx] = s_new[idx] * c_incl[(b + 1) * C - 1:(b + 1) * C, sl]
            y_ref[b * C:(b + 1) * C, sl] = yb[idx, :C] + yb[idx, C:]
    y = y_ref[...]

    inv_n = 1.0 / A_HD
    mean = _dot_exact_rhs(y, head_ones) * inv_n
    yc = y - mean
    var = _dot_exact_rhs(yc * yc, head_ones) * inv_n
    yn = yc * lax.rsqrt(var + LNX_EPS) * lng_ref[...] + lnb_ref[...]
    bonus = _dot_exact_rhs(r * k2 * rk_ref[...], head_ones) * v
    o_ref[...] = ((yn + bonus) * g).reshape(bt, C, A_W)

    @pl.when(c == nc - 1)
    def _():
        sout_ref[...] = s_ref[...].reshape(bt, NP, 2 * A_HD, 2 * A_HD)


RWKV_BATCH_TILE = 4


def _rwkv(proj, ulast, s0, prm, t_valid=None):
    b, t, _ = proj.shape
    C = min(RWKV_CHUNK, t)
    bt = min(RWKV_BATCH_TILE, b)
    nc = t // C
    vec = lambda n: pl.BlockSpec((1, n), lambda i, c: (0, 0))
    mat = lambda m, n: pl.BlockSpec((m, n), lambda i, c: (0, 0))
    col = lambda width, off: pl.BlockSpec((bt, C, width), lambda i, c: (i, c, off // width))
    n_shift = 3 * A_W + A_LORA
    n_pairs = bt * (A_HEADS // 2)
    kern = functools.partial(_rwkv_kernel, chunk=C, bt=bt, t_valid=t_valid)
    return pl.pallas_call(
        kern,
        grid=(b // bt, nc),
        in_specs=[col(A_W, P_R), col(A_W, P_K), col(A_W, P_V), col(A_LORA, P_LORA),
                  pl.BlockSpec((bt, 1, n_shift), lambda i, c: (i, 0, 0)),
                  pl.BlockSpec((bt, 4, 128, 128), lambda i, c: (i, 0, 0, 0)),
                  vec(n_shift), vec(A_W), mat(A_LORA_W, A_W), vec(A_W), mat(A_LORA_A, A_W),
                  mat(A_LORA_G, A_W), vec(A_W), vec(A_W), vec(A_W), vec(A_W), vec(A_W)],
        out_specs=[pl.BlockSpec((bt, C, A_W), lambda i, c: (i, c, 0)),
                   pl.BlockSpec((bt, 4, 128, 128), lambda i, c: (i, 0, 0, 0))],
        out_shape=[jax.ShapeDtypeStruct((b, t, A_W), F32),
                   jax.ShapeDtypeStruct((b, 4, 128, 128), F32)],
        scratch_shapes=[pltpu.VMEM((n_pairs, 128, 128), F32), pltpu.VMEM((bt, n_shift), F32),
                        pltpu.VMEM((n_pairs, 4 * C, 128), BF16), pltpu.VMEM((n_pairs, 4 * C, 128), BF16),
                        pltpu.VMEM((n_pairs, 2 * C, 128), BF16), pltpu.VMEM((bt * C, A_W), F32)],
        compiler_params=_cparams(("parallel", "arbitrary")),
        name="rwkv_chunk",
    )(proj, proj, proj, proj, ulast, s0, *prm)


def _pair_states(s):
    b = s.shape[0]
    s = s.reshape(b, 4, 2, A_HD, A_HD)
    z = jnp.zeros_like(s[:, :, 0])
    top = jnp.concatenate([s[:, :, 0], z], axis=-1)
    bot = jnp.concatenate([z, s[:, :, 1]], axis=-1)
    return jnp.concatenate([top, bot], axis=-2)


def _unpair_states(sp):
    b = sp.shape[0]
    h0 = sp[:, :, :A_HD, :A_HD]
    h1 = sp[:, :, A_HD:, A_HD:]
    return jnp.stack([h0, h1], axis=2).reshape(b, A_HEADS, A_HD, A_HD)


def _pool_kernel(u_ref, halo_ref, hist_ref, pm_ref, ps_ref, o_ref, ext_ref, *, tm, pos0):
    i = pl.program_id(1)

    @pl.when(i == 0)
    def _():
        ext_ref[0:POOL_HALO, :] = hist_ref[0]

    @pl.when(i > 0)
    def _():
        ext_ref[0:POOL_HALO, :] = halo_ref[0]

    cur = u_ref[0]
    ext_ref[POOL_HALO:POOL_HALO + tm, :] = cur
    pos = pos0 + i * tm + lax.broadcasted_iota(jnp.int32, (tm, 1), 0)
    outs = []
    for gi, win in enumerate(POOL_WINDOWS):
        lo, hi = gi * B_GW, (gi + 1) * B_GW
        s = cur[:, lo:hi]
        for back in range(1, win):
            s = s + ext_ref[POOL_HALO - back:POOL_HALO - back + tm, lo:hi]
        cnt = jnp.minimum(win, pos + 1).astype(F32)
        d = s / cnt - cur[:, lo:hi]
        outs.append(_dot(d, pm_ref[gi]))
    o_ref[0] = jnp.concatenate(outs, axis=1) * ps_ref[...]


def _pool(proj, hist, pm, ps, pos0, tm):
    b, t, _ = proj.shape
    per = tm // POOL_HALO
    kern = functools.partial(_pool_kernel, tm=tm, pos0=pos0)
    return pl.pallas_call(
        kern,
        grid=(b, t // tm),
        in_specs=[pl.BlockSpec((1, tm, B_W), lambda bi, i: (bi, i, P_POOL // B_W)),
                  pl.BlockSpec((1, POOL_HALO, B_W),
                               lambda bi, i: (bi, jnp.maximum(i * per - 1, 0), P_POOL // B_W)),
                  pl.BlockSpec((1, POOL_HALO, B_W), lambda bi, i: (bi, 0, 0)),
                  pl.BlockSpec((B_GROUPS, B_GW, B_GW), lambda bi, i: (0, 0, 0)),
                  pl.BlockSpec((1, B_W), lambda bi, i: (0, 0))],
        out_specs=pl.BlockSpec((1, tm, B_W), lambda bi, i: (bi, i, 0)),
        out_shape=jax.ShapeDtypeStruct((b, t, B_W), F32),
        scratch_shapes=[pltpu.VMEM((tm + POOL_HALO, B_W), F32)],
        compiler_params=_cparams(("parallel", "arbitrary")),
        name="pool_mix",
    )(proj, proj, hist, pm, ps.reshape(1, B_W))


def _merge_kernel(x_ref, oa_ref, ob_ref, oc_ref, mg_ref, wa_ref, wb_ref, wc_ref, wo_ref, o_ref):
    gate = lambda n: _sigmoid(mg_ref[:, n * D_MODEL:(n + 1) * D_MODEL])
    h = (gate(0) * _dot(oa_ref[...], wa_ref[...]) + gate(1) * _dot(ob_ref[...], wb_ref[...])
         + gate(2) * _dot(oc_ref[...], wc_ref[...]))
    o_ref[...] = x_ref[...] + _dot(h, wo_ref[...])


def _merge(x, oa, ob, oc, gates, wa, wb, wc, wo, tm):
    m = x.shape[0]
    row = lambda n: pl.BlockSpec((tm, n), lambda i: (i, 0))
    full = lambda a: pl.BlockSpec(a.shape, lambda i: (0, 0))
    return pl.pallas_call(
        _merge_kernel,
        grid=(m // tm,),
        in_specs=[row(D_MODEL), row(A_W), row(B_W), row(C_W), row(3 * D_MODEL),
                  full(wa), full(wb), full(wc), full(wo)],
        out_specs=row(D_MODEL),
        out_shape=jax.ShapeDtypeStruct((m, D_MODEL), F32),
        compiler_params=_cparams(("parallel",)),
        name="merge_branches",
    )(x, oa, ob, oc, gates, wa, wb, wc, wo)


def _mem_kv_kernel(mem_ref, g_ref, wk_ref, wv_ref, kg_ref, k_ref, v_ref):
    mn = _rms(mem_ref[...], g_ref[...]).astype(BF16)
    k = jnp.dot(mn, wk_ref[...], preferred_element_type=F32)
    v_ref[...] = jnp.dot(mn, wv_ref[...], preferred_element_type=F32)
    k_ref[...] = jnp.concatenate(
        [_rms(k[:, h * M_HD:(h + 1) * M_HD], kg_ref[...]) for h in range(M_HEADS)], axis=1)


def _mem_kv(mem, g, wk, wv, kg):
    m = mem.shape[0]
    tm = N_MEM
    row = lambda n: pl.BlockSpec((tm, n), lambda i: (i, 0))
    full = lambda a: pl.BlockSpec(a.shape, lambda i: (0, 0))
    g2, kg2 = g.reshape(1, D_MODEL), kg.reshape(1, M_HD)
    return pl.pallas_call(
        _mem_kv_kernel,
        grid=(m // tm,),
        in_specs=[row(D_MODEL), full(g2), full(wk), full(wv), full(kg2)],
        out_specs=[row(M_W), row(M_W)],
        out_shape=[jax.ShapeDtypeStruct((m, M_W), F32), jax.ShapeDtypeStruct((m, M_W), F32)],
        compiler_params=_cparams(("parallel",)),
        name="mem_kv",
    )(mem, g2, wk, wv, kg2)


def _mem_ffn_kernel(x_ref, mk_ref, mv_ref, gx_ref, wq_ref, qg_ref, wo_ref, gf_ref, w1_ref, w2_ref,
                    o_ref, xn_ref, *, tm, rows_per_batch, nkb):
    j = pl.program_id(1)

    @pl.when(j == 0)
    def _():
        x = x_ref[...]
        q = _dot(_rms(x, gx_ref[...]), wq_ref[...])
        mk = mk_ref[...].reshape(nkb * N_MEM, M_W)
        mv = mv_ref[...].reshape(nkb * N_MEM, M_W)
        if nkb > 1:
            qb = lax.broadcasted_iota(jnp.int32, (tm, nkb * N_MEM), 0) // rows_per_batch
            kb = lax.broadcasted_iota(jnp.int32, (tm, nkb * N_MEM), 1) // N_MEM
            same = qb == kb
        outs = []
        for h in range(M_HEADS):
            sl = slice(h * M_HD, (h + 1) * M_HD)
            qh = _rms(q[:, sl], qg_ref[...])
            logits = _dot_nt(qh, mk[:, sl]) * (M_HD ** -0.5)
            if nkb > 1:
                logits = jnp.where(same, logits, NEG)
            mx = jnp.max(logits, axis=-1, keepdims=True)
            p = jnp.exp(logits - mx)
            p = p / jnp.sum(p, axis=-1, keepdims=True)
            outs.append(_dot(p, mv[:, sl]))
        o = jnp.concatenate(outs, axis=1)
        xm = x + _dot(o, wo_ref[...])
        o_ref[...] = xm
        xn_ref[...] = _rms(xm, gf_ref[...]).astype(BF16)

    h1 = jnp.dot(xn_ref[...], w1_ref[...], preferred_element_type=F32)
    h1 = jnp.square(jnp.maximum(h1, 0.0))
    o_ref[...] += _dot(h1, w2_ref[...])


def _mem_ffn(x, mk, mv, gx, wq, qg, wo, gf, w1, w2, tm, rows_per_batch, tf=1024):
    m = x.shape[0]
    nkb = max(tm // rows_per_batch, 1)
    kern = functools.partial(_mem_ffn_kernel, tm=tm, rows_per_batch=rows_per_batch, nkb=nkb)
    full = lambda a: pl.BlockSpec(a.shape, lambda i, j: (0,) * a.ndim)
    gx2, qg2, gf2 = gx.reshape(1, D_MODEL), qg.reshape(1, M_HD), gf.reshape(1, D_MODEL)
    mem_spec = pl.BlockSpec((nkb, N_MEM, M_W), lambda i, j: ((i * tm) // (rows_per_batch * nkb), 0, 0))
    return pl.pallas_call(
        kern,
        grid=(m // tm, D_FF // tf),
        in_specs=[pl.BlockSpec((tm, D_MODEL), lambda i, j: (i, 0)), mem_spec, mem_spec,
                  full(gx2), full(wq), full(qg2), full(wo), full(gf2),
                  pl.BlockSpec((D_MODEL, tf), lambda i, j: (0, j)),
                  pl.BlockSpec((tf, D_MODEL), lambda i, j: (j, 0))],
        out_specs=pl.BlockSpec((tm, D_MODEL), lambda i, j: (i, 0)),
        out_shape=jax.ShapeDtypeStruct((m, D_MODEL), F32),
        scratch_shapes=[pltpu.VMEM((tm, D_MODEL), BF16)],
        compiler_params=_cparams(("parallel", "arbitrary")),
        name="mem_ffn",
    )(x, mk, mv, gx2, wq, qg2, wo, gf2, w1, w2)


def _half_ones():
    i = lax.broadcasted_iota(jnp.int32, (LANES, LANES), 0) // C_HD
    j = lax.broadcasted_iota(jnp.int32, (LANES, LANES), 1) // C_HD
    return (i == j).astype(BF16)


def _rms_halves(x, g, ones):
    ms = _dot_exact_rhs(x * x, ones) * (1.0 / C_HD)
    return x * lax.rsqrt(ms + RMS_EPS) * g


def _nsa_prep_kernel(q_ref, c_ref, s_ref, w_ref, qg_ref, kg_ref,
                     qn_ref, ks_ref, ks16_ref, vs16_ref, kw_ref, kw16_ref, vw16_ref, kc_ref, vc_ref,
                     *, tm):
    ones = _half_ones()
    scale = C_HD ** -0.5
    q = q_ref[0]
    qn_ref[0] = jnp.concatenate(
        [_rms_halves(q[:, n * LANES:(n + 1) * LANES], qg_ref[...], ones) * scale for n in range(C_HPG)],
        axis=1).astype(BF16)
    s = s_ref[0]
    ks = _rms_halves(s[:, :LANES], kg_ref[1:2, :], ones)
    ks_ref[0] = ks
    ks16_ref[0] = ks.astype(BF16)
    vs16_ref[0] = s[:, LANES:].T.astype(BF16)
    w = w_ref[0]
    kw = _rms_halves(w[:, :LANES], kg_ref[2:3, :], ones)
    kw_ref[0] = kw
    kw16_ref[0] = kw.astype(BF16)
    vw16_ref[0] = w[:, LANES:].T.astype(BF16)
    nb = tm // CMP_BLOCK
    bi = lax.broadcasted_iota(jnp.int32, (nb, tm), 0)
    ti = lax.broadcasted_iota(jnp.int32, (nb, tm), 1) // CMP_BLOCK
    pool = (bi == ti).astype(BF16)
    c = c_ref[0]
    means = _dot_exact_lhs(pool, c) * (1.0 / CMP_BLOCK)
    kc_ref[0] = _rms_halves(means[:, :LANES], kg_ref[0:1, :], ones)
    vc_ref[0] = means[:, LANES:]


def _nsa_prep(proj, qg, kg, tm):
    b, t, _ = proj.shape
    nbt = tm // CMP_BLOCK
    kern = functools.partial(_nsa_prep_kernel, tm=tm)
    tok = lambda n, dt: jax.ShapeDtypeStruct((b, t, n), dt)
    blk = jax.ShapeDtypeStruct((b, t // CMP_BLOCK, LANES), F32)
    col = lambda width, off: pl.BlockSpec((1, tm, width), lambda bi, i: (bi, i, off // width))
    out_tok = lambda n: pl.BlockSpec((1, tm, n), lambda bi, i: (bi, i, 0))
    out_blk = pl.BlockSpec((1, nbt, LANES), lambda bi, i: (bi, i, 0))
    out_t = pl.BlockSpec((1, LANES, tm), lambda bi, i: (bi, 0, i))
    tok_t = jax.ShapeDtypeStruct((b, LANES, t), BF16)
    return pl.pallas_call(
        kern,
        grid=(b, t // tm),
        in_specs=[col(C_W, P_Q), col(2 * LANES, P_KV), col(2 * LANES, P_KV + 2 * LANES),
                  col(2 * LANES, P_KV + 4 * LANES),
                  pl.BlockSpec((1, LANES), lambda bi, i: (0, 0)),
                  pl.BlockSpec((3, LANES), lambda bi, i: (0, 0))],
        out_specs=[out_tok(C_W), out_tok(LANES), out_tok(LANES), out_t, out_tok(LANES),
                   out_tok(LANES), out_t, out_blk, out_blk],
        out_shape=[tok(C_W, BF16), tok(LANES, F32), tok(LANES, BF16), tok_t,
                   tok(LANES, F32), tok(LANES, BF16), tok_t, blk, blk],
        compiler_params=_cparams(("parallel", "parallel")),
        name="nsa_prep",
    )(proj, proj, proj, proj, jnp.tile(qg.reshape(1, C_HD), (1, 2)), jnp.tile(kg, (1, 2)))


M_INIT = -1e29
ROWS = C_HPG * Q_TILE
SEL_PAD = 8
FAR_TILES = 4


def _heads(x):
    return jnp.concatenate([x] * C_HPG, axis=1)


def _softmax_step(qs, k, v_t, state, tail_bias=None, valid=None):
    m_ref, l_ref, acc_ref = state
    old = [(m_ref[g], l_ref[g], acc_ref[g]) for g in range(C_KV)]
    new = []
    for g in range(C_KV):
        m_old, l_old, acc_old = old[g]
        s = lax.dot_general(k, qs[g], (((1,), (1,)), ((), ())), preferred_element_type=F32)
        if tail_bias is not None:
            cut = s.shape[0] - tail_bias[g].shape[0]
            tail = s[cut:] + tail_bias[g]
            s = jnp.concatenate([s[:cut], tail], axis=0) if cut else tail
        if valid is not None:
            s = jnp.where(_heads(valid[g]), s, NEG)
        m_new = jnp.maximum(m_old, jnp.max(s, axis=0, keepdims=True))
        alpha = jnp.exp(m_old - m_new)
        p = jnp.exp(s - m_new)
        new.append((m_new, alpha * l_old + jnp.sum(p, axis=0, keepdims=True),
                    alpha * acc_old + jnp.dot(v_t, p.astype(BF16), preferred_element_type=F32)))
    for g in range(C_KV):
        m_ref[g], l_ref[g], acc_ref[g] = new[g]


def _softmax_reset(state):
    m_ref, l_ref, acc_ref = state
    m_ref[...] = jnp.full(m_ref.shape, M_INIT, F32)
    l_ref[...] = jnp.zeros(l_ref.shape, F32)
    acc_ref[...] = jnp.zeros(acc_ref.shape, F32)


def _softmax_result(state, g):
    _, l_ref, acc_ref = state
    l = l_ref[g]
    return acc_ref[g] / jnp.where(l > 0.0, l, 1.0)


def _top_blocks(score, n_sel):
    nb = score.shape[0]
    blk = lax.broadcasted_iota(jnp.int32, score.shape, 0).astype(F32)
    sel = jnp.zeros(score.shape, F32)
    work = score
    for _ in range(min(n_sel, nb)):
        mx = jnp.max(work, axis=0, keepdims=True)
        idx = jnp.min(jnp.where(work == mx, blk, float(nb)), axis=0, keepdims=True)
        pick = blk == idx
        sel = jnp.where(pick, 1.0, sel)
        work = jnp.where(pick, -jnp.inf, work)
    return sel


def _nsa_attn_kernel(qn_ref, ks_ref, vs_ref, kw_ref, vw_ref, kc_ref, vc_ref, cg_ref, bnear_ref,
                     bcmp_ref, o_ref, m_ref, l_ref, acc_ref, sel_ref, *, nb):
    i = pl.program_id(1)
    q_all = qn_ref[0]
    lane = lax.broadcasted_iota(jnp.int32, (Q_TILE, LANES), 1)
    key = lax.broadcasted_iota(jnp.int32, (Q_TILE, LANES), 0)
    causal = key <= lane
    sig_t = _sigmoid(cg_ref[0]).T
    kc = kc_ref[0].astype(BF16)
    vc_t = vc_ref[0].T.astype(BF16)
    blk = lax.broadcasted_iota(jnp.int32, (nb, Q_TILE), 0)
    qpos = i * Q_TILE + lax.broadcasted_iota(jnp.int32, (nb, Q_TILE), 1)
    cur = qpos // CMP_BLOCK
    cmp_valid = _heads(qpos >= blk * CMP_BLOCK + (CMP_BLOCK - 1))
    forced = (blk == 0) | (blk == cur) | (blk == cur - 1)
    state = (m_ref, l_ref, acc_ref)
    n_back = WINDOW // Q_TILE
    k_rows = lambda ref, j, n: ref[0, pl.ds(pl.multiple_of((j + n_back) * Q_TILE, Q_TILE), n * Q_TILE), :]
    v_cols = lambda ref, j, n: ref[0, :, pl.ds(pl.multiple_of((j + n_back) * Q_TILE, Q_TILE), n * Q_TILE)]

    def key_mask(g, j, n):
        rows = [sel_ref[g, pl.ds(SEL_PAD + 2 * j + r, 1), :] for r in range(2 * n)]
        return jnp.concatenate(
            [jnp.where(key < CMP_BLOCK, rows[2 * t], rows[2 * t + 1]) for t in range(n)], axis=0) > 0.5

    qs = []
    for g in range(C_KV):
        half = (lane // C_HD) == g
        qs.append(jnp.concatenate(
            [jnp.where(half, q_all[:, n * LANES:(n + 1) * LANES], 0.0) for n in range(C_HPG)],
            axis=0).astype(BF16))

    o_cmp, scores = [], []
    for g in range(C_KV):
        bias_c = pltpu.roll(bcmp_ref[g], (2 * i + 2) % LANES, 0)[:nb, :]
        lc = lax.dot_general(kc, qs[g], (((1,), (1,)), ((), ())), preferred_element_type=F32) + bias_c
        lc = jnp.where(cmp_valid, lc, NEG)
        mc = jnp.maximum(jnp.max(lc, axis=0, keepdims=True), M_INIT)
        pc = jnp.exp(lc - mc)
        den = jnp.sum(pc, axis=0, keepdims=True)
        pc = pc / jnp.where(den > 0.0, den, 1.0)
        o_cmp.append(jnp.dot(vc_t, pc.astype(BF16), preferred_element_type=F32))
        psum = sum(pc[:, n * Q_TILE:(n + 1) * Q_TILE] for n in range(C_HPG))
        scores.append(jnp.where(blk > cur, NEG, jnp.where(forced, FORCE_SCORE, psum)))

    score = jnp.concatenate(scores, axis=1)
    sel = jnp.where(score > 0.5 * NEG, _top_blocks(score, N_SEL), 0.0)
    for g in range(C_KV):
        sel_ref[g, 0:SEL_PAD, :] = jnp.zeros((SEL_PAD, Q_TILE), F32)
        sel_ref[g, SEL_PAD:SEL_PAD + nb, :] = sel[:, g * Q_TILE:(g + 1) * Q_TILE]

    _softmax_reset(state)
    n_far = jnp.maximum(i - 1, 0)

    def far_step(j, n):
        _softmax_step(qs, k_rows(ks_ref, j, n), v_cols(vs_ref, j, n), state,
                      valid=[key_mask(g, j, n) for g in range(C_KV)])

    def far_many(jq, carry):
        far_step(FAR_TILES * jq, FAR_TILES)
        return carry

    def far_one(j, carry):
        far_step(j, 1)
        return carry

    n_many = n_far // FAR_TILES
    lax.fori_loop(0, n_many, far_many, 0)
    lax.fori_loop(FAR_TILES * n_many, n_far, far_one, 0)

    near_causal = jnp.concatenate([key >= 0, causal], axis=0)
    bias2 = [bnear_ref[g] for g in range(C_KV)]
    _softmax_step(qs, k_rows(ks_ref, i - 1, 2), v_cols(vs_ref, i - 1, 2), state, tail_bias=bias2,
                  valid=[key_mask(g, i - 1, 2) & near_causal for g in range(C_KV)])
    o_slc = [_softmax_result(state, g) for g in range(C_KV)]

    _softmax_reset(state)
    exists = lambda back: key >= jnp.where(i >= back, 0, Q_TILE)
    win_valid = jnp.concatenate(
        [(key > lane) & exists(n_back)] + [exists(back) for back in range(n_back - 1, 0, -1)] + [causal],
        axis=0)
    _softmax_step(qs, k_rows(kw_ref, i - n_back, n_back + 1), v_cols(vw_ref, i - n_back, n_back + 1), state,
                  tail_bias=bias2, valid=[win_valid] * C_KV)
    o_win = [_softmax_result(state, g) for g in range(C_KV)]

    low = (key // C_HD) == 0
    for n in range(C_HPG):
        cols = slice(n * Q_TILE, (n + 1) * Q_TILE)
        per_group = []
        for g in range(C_KV):
            h = g * C_HPG + n
            gate = lambda br: sig_t[br * C_HEADS + h:br * C_HEADS + h + 1, :]
            per_group.append(gate(0) * o_cmp[g][:, cols] + gate(1) * o_slc[g][:, cols]
                             + gate(2) * o_win[g][:, cols])
        o_ref[0, :, n * LANES:(n + 1) * LANES] = jnp.where(low, per_group[0], per_group[1]).T


def _nsa_attn(qn, ks16, vs16t, kw16, vw16t, kc, vc, gates, bnear, bcmp):
    b, t, _ = qn.shape
    nb = t // CMP_BLOCK
    tp = t + WINDOW
    kern = functools.partial(_nsa_attn_kernel, nb=nb)
    seq = pl.BlockSpec((1, tp, LANES), lambda bi, i: (bi, 0, 0))
    seq_t = pl.BlockSpec((1, LANES, tp), lambda bi, i: (bi, 0, 0))
    blk = pl.BlockSpec((1, nb, LANES), lambda bi, i: (bi, 0, 0))
    return pl.pallas_call(
        kern,
        grid=(b, t // Q_TILE),
        in_specs=[pl.BlockSpec((1, Q_TILE, C_W), lambda bi, i: (bi, i, 0)), seq, seq_t, seq, seq_t, blk, blk,
                  pl.BlockSpec((1, Q_TILE, LANES), lambda bi, i: (bi, i, G_CG // LANES)),
                  pl.BlockSpec(bnear.shape, lambda bi, i: (0, 0, 0)),
                  pl.BlockSpec(bcmp.shape, lambda bi, i: (0, 0, 0))],
        out_specs=pl.BlockSpec((1, Q_TILE, C_W), lambda bi, i: (bi, i, 0)),
        out_shape=jax.ShapeDtypeStruct((b, t, C_W), F32),
        scratch_shapes=[pltpu.VMEM((C_KV, 1, ROWS), F32), pltpu.VMEM((C_KV, 1, ROWS), F32),
                        pltpu.VMEM((C_KV, LANES, ROWS), F32),
                        pltpu.VMEM((C_KV, SEL_PAD + nb, Q_TILE), F32)],
        compiler_params=_cparams(("parallel", "arbitrary")),
        name="nsa_attn",
    )(qn, ks16, vs16t, kw16, vw16t, kc, vc, gates, bnear, bcmp)


def _rel_bucket(dist):
    n = jnp.maximum(dist, 0)
    exact = N_BUCKETS // 2
    ratio = jnp.log(jnp.maximum(n, 1).astype(F32) / exact) / math.log(MAX_DIST / exact)
    large = jnp.minimum(exact + (ratio * (N_BUCKETS - exact)).astype(jnp.int32), N_BUCKETS - 1)
    return jnp.where(n < exact, n, large)


def _bias_tables(rel_bias):
    tbl = rel_bias[_rel_bucket(jnp.arange(MAX_DIST)), :]
    tbl = tbl - tbl[MAX_DIST - 1:MAX_DIST, :]
    tbl = tbl.T.reshape(C_KV, C_HPG, MAX_DIST)
    qq = jnp.arange(Q_TILE)[:, None]
    cc = jnp.arange(LANES)[None, :]
    look = lambda dist: tbl[:, :, jnp.clip(dist, 0, MAX_DIST - 1)].reshape(C_KV, ROWS, LANES)
    bnear = jnp.stack([look(qq - cc), look(qq - cc + Q_TILE)], axis=1)
    bcmp = look(qq - CMP_BLOCK * cc + CMP_BLOCK * (LANES - 1) - (LANES - 1))
    bnear = bnear.transpose(0, 1, 3, 2)
    return jnp.concatenate([bnear[:, 1], bnear[:, 0]], axis=1), bcmp.transpose(0, 2, 1)


def _rms_last(x, g):
    return x * lax.rsqrt(jnp.mean(x * x, axis=-1, keepdims=True) + RMS_EPS) * g


def _masked_softmax(logits, mask):
    p = jax.nn.softmax(jnp.where(mask, logits, NEG), axis=-1)
    return p * mask


def _nsa_sample_jax(q, gates, kvs, pools, win_k, win_v, page_table, kg_cmp, rel_bias):
    k_cmp, v_cmp, k_slc, v_slc, k_win, v_win = kvs
    bn, tq = q.shape[:2]
    past = page_table.shape[1] * PAGE_SIZE
    logical = lambda pool, new: jnp.concatenate(
        [pool[page_table].reshape((bn, past) + pool.shape[2:]), new], axis=1)
    nb = -(-(past + tq) // CMP_BLOCK)
    pad = nb * CMP_BLOCK - (past + tq)
    padded = lambda z: jnp.pad(z, ((0, 0), (0, pad), (0, 0), (0, 0)))
    blocks = lambda z: padded(z).reshape(bn, nb, CMP_BLOCK, C_KV, C_HD)
    kc = _rms_last(blocks(logical(pools[0], k_cmp)).mean(axis=2), kg_cmp)
    vc = blocks(logical(pools[1], v_cmp)).mean(axis=2)
    ksb = blocks(logical(pools[2], k_slc)).transpose(0, 3, 1, 2, 4)
    vsb = blocks(logical(pools[3], v_slc)).transpose(0, 3, 1, 2, 4)
    kw = jnp.concatenate([win_k, k_win], axis=1)
    vw = jnp.concatenate([win_v, v_win], axis=1)
    wbuf = win_k.shape[1]
    kwpos = past - wbuf + jnp.arange(wbuf + tq)
    qpos = past + jnp.arange(tq)
    scale = C_HD ** -0.5
    qg = q.reshape(bn, tq, C_KV, C_HPG, C_HD)
    rb = rel_bias.reshape(N_BUCKETS, C_KV, C_HPG)
    endpos = jnp.arange(nb) * CMP_BLOCK + (CMP_BLOCK - 1)
    dist_c = qpos[:, None] - endpos[None, :]
    bias_c = rb[_rel_bucket(dist_c)].transpose(2, 3, 0, 1)
    lc = jnp.einsum('btgnd,bjgd->bgntj', qg, kc) * scale + bias_c
    pc = _masked_softmax(lc, dist_c >= 0)
    o_cmp = jnp.einsum('bgntj,bjgd->btgnd', pc, vc)
    blk = jnp.arange(nb)
    cur = (qpos // CMP_BLOCK)[:, None]
    forced = (blk == 0) | (blk == cur) | (blk == cur - 1)
    score = jnp.where(blk > cur, NEG, jnp.where(forced, FORCE_SCORE, pc.sum(axis=2)))
    ksel = min(N_SEL, nb)
    top_val, top_idx = lax.top_k(score, ksel)
    bi = jnp.arange(bn)[:, None, None, None]
    gi = jnp.arange(C_KV)[None, :, None, None]
    n_keys = ksel * CMP_BLOCK
    ks = ksb[bi, gi, top_idx].reshape(bn, C_KV, tq, n_keys, C_HD)
    vs = vsb[bi, gi, top_idx].reshape(bn, C_KV, tq, n_keys, C_HD)
    kpos5 = top_idx[..., None] * CMP_BLOCK + jnp.arange(CMP_BLOCK)
    valid = ((top_val > 0.5 * NEG)[..., None] & (kpos5 <= qpos[None, None, :, None, None]))
    valid = valid.reshape(bn, C_KV, tq, n_keys)
    dist_s = qpos[None, None, :, None] - kpos5.reshape(bn, C_KV, tq, n_keys)
    bias_s = rb[_rel_bucket(dist_s), gi].transpose(0, 1, 4, 2, 3)
    ls = jnp.einsum('btgnd,bgtsd->bgnts', qg, ks) * scale + bias_s
    ps = _masked_softmax(ls, valid[:, :, None])
    o_slc = jnp.einsum('bgnts,bgtsd->btgnd', ps, vs)
    dist_w = qpos[:, None] - kwpos[None, :]
    mask_w = (dist_w >= 0) & (dist_w < WINDOW) & (kwpos >= 0)[None, :]
    bias_w = rb[_rel_bucket(dist_w)].transpose(2, 3, 0, 1)
    lw = jnp.einsum('btgnd,bsgd->bgnts', qg, kw) * scale + bias_w
    pw = _masked_softmax(lw, mask_w)
    o_win = jnp.einsum('bgnts,bsgd->btgnd', pw, vw)
    gt = jax.nn.sigmoid(gates).reshape(bn, tq, 3, C_KV, C_HPG)[..., None]
    o = gt[:, :, 0] * o_cmp + gt[:, :, 1] * o_slc + gt[:, :, 2] * o_win
    return o, kw[:, -wbuf:], vw[:, -wbuf:]


def _shift_order(z):
    part = lambda off, n: z[..., off:off + n]
    return jnp.concatenate([part(OFF_R, A_W), part(OFF_K, A_W), part(OFF_V, A_W), part(OFF_WL, A_LORA_W),
                            part(OFF_AL, A_LORA_A), part(OFF_GL, A_LORA_G)], axis=-1)


def _layer_params(l, w_in, shift_mu, w_br_a, w_br_b, w_br_c, w_out, w_mq, w_mk, w_mv, w_mo, w_ff1, w_ff2):
    w = w_in[l]
    cast = lambda a: a.astype(BF16)
    cols = lambda off, n: cast(w[:, off:off + n])
    w_q = cols(OFF_Q, C_W).reshape(D_MODEL, C_KV, C_HPG, C_HD).transpose(0, 2, 1, 3).reshape(D_MODEL, C_W)
    w_main = jnp.concatenate([
        cols(OFF_R, A_W), cols(OFF_K, A_W), cols(OFF_V, A_W), cols(OFF_POOL, B_W), w_q,
        cols(OFF_KV, 6 * C_KVW), cols(OFF_WL, A_LORA_W), cols(OFF_AL, A_LORA_A), cols(OFF_GL, A_LORA_G)],
        axis=1)
    w_gate = jnp.concatenate([cols(OFF_MG, 3 * D_MODEL), cols(OFF_CG, 3 * C_HEADS),
                              jnp.zeros((D_MODEL, N_GATE - G_CG - 3 * C_HEADS), BF16)], axis=1)
    wc = w_br_c[l].reshape(C_KV, C_HPG, C_HD, D_MODEL).transpose(1, 0, 2, 3).reshape(C_W, D_MODEL)
    return dict(
        w_main=w_main, w_gate=w_gate, mu=_shift_order(shift_mu[l].reshape(1, -1)),
        wa=cast(w_br_a[l]), wb=cast(w_br_b[l]), wc=cast(wc), wo=cast(w_out[l]),
        wq=cast(w_mq[l]), wk=cast(w_mk[l]), wv=cast(w_mv[l]), wmo=cast(w_mo[l]),
        w1=cast(w_ff1[l]), w2=cast(w_ff2[l]))


def _shift_row(main_row):
    lora = main_row[:, P_LORA:]
    return jnp.concatenate([
        main_row[:, P_R:P_R + A_W], lora[:, :A_LORA_W], main_row[:, P_K:P_K + A_W],
        main_row[:, P_V:P_V + A_W], lora[:, A_LORA_W:A_LORA_W + A_LORA_A],
        lora[:, A_LORA_W + A_LORA_A:]], axis=1)


def _kv_rows(main, which):
    b, t = main.shape[:2]
    lo = P_KV + which * C_KVW
    return main[:, :, lo:lo + C_KVW].reshape(b, t, C_KV, C_HD)


def kernel(x_prompt, x_sample, state_rwkv, state_shift, state_pool, cache_cmp_k, cache_cmp_v, cache_slc_k, cache_slc_v, cache_win_k, cache_win_v, cache_mem_k, cache_mem_v, page_table, mem_prompt, rel_bias, norm_mix_g, w_in, shift_mu, rwkv_w0, rwkv_w2, rwkv_a0, rwkv_a2, rwkv_g2, rwkv_kk, rwkv_ka, rwkv_rk, rwkv_lnx_g, rwkv_lnx_b, pool_w, pool_scale, nsa_q_g, nsa_k_g, w_br_a, w_br_b, w_br_c, w_out, norm_memx_g, norm_mem_g, w_mq, w_mk, w_mv, mem_q_g, mem_k_g, w_mo, norm_ffn_g, w_ff1, w_ff2):
    bp, t = x_prompt.shape[:2]
    bs, ts = x_sample.shape[:2]
    past = page_table.shape[1] * PAGE_SIZE
    wbuf_p = min(WINDOW, t)
    ts_pad = 16
    bnear, bcmp = _bias_tables(rel_bias)
    xp = x_prompt.reshape(bp * t, D_MODEL)
    xs = x_sample.reshape(bs * ts, D_MODEL)
    outs_p = [[] for _ in range(11)]
    outs_s = [[] for _ in range(9)]
    row = lambda a: a.reshape(1, -1)
    for l in range(DEPTH):
        prm = _layer_params(l, w_in, shift_mu, w_br_a, w_br_b, w_br_c, w_out, w_mq, w_mk, w_mv, w_mo,
                            w_ff1, w_ff2)
        rw = (prm["mu"], row(rwkv_w0[l]), rwkv_w2[l].astype(BF16), row(rwkv_a0[l]), rwkv_a2[l].astype(BF16),
              rwkv_g2[l].astype(BF16), row(rwkv_kk[l]), row(rwkv_ka[l]), row(rwkv_rk[l]),
              row(rwkv_lnx_g[l]), row(rwkv_lnx_b[l]))
        pool_wl = pool_w[l].astype(BF16)

        main = _norm_matmul(xp, norm_mix_g[l], prm["w_main"], 1024, 512).reshape(bp, t, N_MAIN)
        gate = _norm_matmul(xp, norm_mix_g[l], prm["w_gate"], 1024, 640)
        o_a, s_pairs = _rwkv(main, jnp.zeros((bp, 1, A_COLS), F32), jnp.zeros((bp, 4, 128, 128), F32), rw)
        o_b = _pool(main, jnp.zeros((bp, POOL_HALO, B_W), F32), pool_wl, pool_scale[l], 0, 512)
        qn, ks, ks16, vs16, kw, kw16, vw16, kc, vc = _nsa_prep(main, nsa_q_g[l], nsa_k_g[l], 512)
        front = lambda z: jnp.pad(z, ((0, 0), (WINDOW, 0), (0, 0)))
        front_t = lambda z: jnp.pad(z, ((0, 0), (0, 0), (WINDOW, 0)))
        o_c = _nsa_attn(qn, front(ks16), front_t(vs16), front(kw16), front_t(vw16), kc, vc,
                        gate.reshape(bp, t, N_GATE), bnear, bcmp)
        xp = _merge(xp, o_a.reshape(bp * t, A_W), o_b.reshape(bp * t, B_W), o_c.reshape(bp * t, C_W), gate,
                    prm["wa"], prm["wb"], prm["wc"], prm["wo"], 512)
        mk, mv = _mem_kv(mem_prompt.reshape(bp * N_MEM, D_MODEL), norm_mem_g[l], prm["wk"], prm["wv"],
                         mem_k_g[l])
        mk, mv = mk.reshape(bp, N_MEM, M_W), mv.reshape(bp, N_MEM, M_W)
        xp = _mem_ffn(xp, mk, mv, norm_memx_g[l], prm["wq"], mem_q_g[l], prm["wmo"], norm_ffn_g[l],
                      prm["w1"], prm["w2"], 512, t)
        split = lambda z: z.reshape(bp, -1, C_KV, C_HD)
        for dst, val in zip(outs_p, (
                _unpair_states(s_pairs), _shift_row(main[:, -1]), main[:, -POOL_BUF:, P_POOL:P_POOL + B_W],
                _kv_rows(main, 0), _kv_rows(main, 1), split(ks), _kv_rows(main, 3),
                split(kw[:, -wbuf_p:]), _kv_rows(main, 5)[:, -wbuf_p:],
                mk.reshape(bp, N_MEM, M_HEADS, M_HD), mv.reshape(bp, N_MEM, M_HEADS, M_HD))):
            dst.append(val)

        main = _norm_matmul(xs, norm_mix_g[l], prm["w_main"], bs * ts, 512).reshape(bs, ts, N_MAIN)
        gate = _norm_matmul(xs, norm_mix_g[l], prm["w_gate"], bs * ts, 640)
        main_pad = jnp.pad(main, ((0, 0), (0, ts_pad - ts), (0, 0)))
        ulast = _shift_order(state_shift[l]).reshape(bs, 1, A_COLS)
        o_a, s_pairs = _rwkv(main_pad, ulast, _pair_states(state_rwkv[l]), rw, t_valid=ts)
        o_a = o_a[:, :ts]
        hist = jnp.concatenate([jnp.zeros((bs, POOL_HALO - POOL_BUF, B_W), F32), state_pool[l]], axis=1)
        o_b = _pool(main_pad, hist, pool_wl, pool_scale[l], past, ts_pad)[:, :ts]
        u_pool = main[:, :, P_POOL:P_POOL + B_W]
        q = main[:, :, P_Q:P_Q + C_W].reshape(bs, ts, C_HPG, C_KV, C_HD).transpose(0, 1, 3, 2, 4)
        q = _rms_last(q, nsa_q_g[l]).reshape(bs, ts, C_HEADS, C_HD)
        kvs = [_kv_rows(main, n) for n in range(6)]
        kvs[2] = _rms_last(kvs[2], nsa_k_g[l][1])
        kvs[4] = _rms_last(kvs[4], nsa_k_g[l][2])
        o_c, wk_new, wv_new = _nsa_sample_jax(
            q, gate[:, G_CG:G_CG + 3 * C_HEADS].reshape(bs, ts, 3 * C_HEADS), kvs,
            (cache_cmp_k[l], cache_cmp_v[l], cache_slc_k[l], cache_slc_v[l]),
            cache_win_k[l], cache_win_v[l], page_table, nsa_k_g[l][0], rel_bias)
        o_c = o_c.transpose(0, 1, 3, 2, 4).reshape(bs * ts, C_W)
        xs = _merge(xs, o_a.reshape(bs * ts, A_W), o_b.reshape(bs * ts, B_W), o_c, gate,
                    prm["wa"], prm["wb"], prm["wc"], prm["wo"], bs * ts)
        xs = _mem_ffn(xs, cache_mem_k[l].reshape(bs, N_MEM, M_W), cache_mem_v[l].reshape(bs, N_MEM, M_W),
                      norm_memx_g[l], prm["wq"], mem_q_g[l], prm["wmo"], norm_ffn_g[l],
                      prm["w1"], prm["w2"], 32, ts)
        for dst, val in zip(outs_s, (
                _unpair_states(s_pairs), _shift_row(main[:, -1]),
                jnp.concatenate([state_pool[l], u_pool], axis=1)[:, -POOL_BUF:],
                kvs[0], kvs[1], kvs[2], kvs[3], wk_new, wv_new)):
            dst.append(val)

    return ((xp.reshape(bp, t, D_MODEL), xs.reshape(bs, ts, D_MODEL))
            + tuple(jnp.stack(o) for o in outs_p) + tuple(jnp.stack(o) for o in outs_s))
```

```python
import functools
import math

import jax
import jax.numpy as jnp
from jax import lax
from jax.experimental import pallas as pl
from jax.experimental.pallas import tpu as pltpu

F32 = jnp.float32
BF16 = jnp.bfloat16

D_MODEL = 1024
DEPTH = 2
PAGE_SIZE = 128
A_HEADS = 8
A_HD = 64
A_W = A_HEADS * A_HD
A_LORA_W = 64
A_LORA_A = 64
A_LORA_G = 128
A_LORA = A_LORA_W + A_LORA_A + A_LORA_G
LNX_EPS = 64e-5
L2_EPS = 1e-24
B_GROUPS = 4
B_GW = 128
B_W = B_GROUPS * B_GW
POOL_WINDOWS = (2, 4, 8, 16)
POOL_BUF = 15
POOL_HALO = 16
C_HEADS = 8
C_HD = 64
C_W = C_HEADS * C_HD
C_KV = 2
C_HPG = C_HEADS // C_KV
C_KVW = C_KV * C_HD
CMP_BLOCK = 64
N_SEL = 16
WINDOW = 512
Q_TILE = 128
FORCE_SCORE = 1e4
NEG = -1e30
N_BUCKETS = 32
MAX_DIST = 128
N_MEM = 256
M_HEADS = 4
M_HD = 128
M_W = M_HEADS * M_HD
D_FF = 4 * D_MODEL
RMS_EPS = 1e-6

OFF_R = 0
OFF_WL = OFF_R + A_W
OFF_K = OFF_WL + A_LORA_W
OFF_V = OFF_K + A_W
OFF_AL = OFF_V + A_W
OFF_GL = OFF_AL + A_LORA_A
A_COLS = OFF_GL + A_LORA_G
OFF_POOL = A_COLS
OFF_Q = OFF_POOL + B_W
OFF_KV = OFF_Q + C_W
OFF_CG = OFF_KV + 6 * C_KVW
OFF_MG = OFF_CG + 3 * C_HEADS
N_IN = OFF_MG + 3 * D_MODEL

P_R = 0
P_K = P_R + A_W
P_V = P_K + A_W
P_POOL = P_V + A_W
P_Q = P_POOL + B_W
P_KV = P_Q + C_W
P_LORA = P_KV + 6 * C_KVW
N_MAIN = P_LORA + A_LORA
G_MG = 0
G_CG = 3 * D_MODEL
N_GATE = G_CG + 128

RWKV_CHUNK = 64
RWKV_SUB = 16
LANES = 128
VMEM_LIMIT = 56 * 1024 * 1024


def _cparams(sem):
    return pltpu.CompilerParams(dimension_semantics=sem, vmem_limit_bytes=VMEM_LIMIT)


def _dot(a, b, dims=(((1,), (0,)), ((), ()))):
    return lax.dot_general(a.astype(BF16), b.astype(BF16), dims, preferred_element_type=F32)


def _dot_nt(a, b):
    return _dot(a, b, (((1,), (1,)), ((), ())))


def _dot_tn(a, b):
    return _dot(a, b, (((0,), (0,)), ((), ())))


def _split3(x):
    hi = x.astype(BF16)
    r1 = x - hi.astype(F32)
    mid = r1.astype(BF16)
    lo = (r1 - mid.astype(F32)).astype(BF16)
    return hi, mid, lo


def _dot_exact_rhs(a, b01):
    hi, mid, lo = _split3(a)
    dims = (((1,), (0,)), ((), ()))
    dg = lambda p: lax.dot_general(p, b01, dims, preferred_element_type=F32)
    return dg(hi) + dg(mid) + dg(lo)


def _dot_exact_lhs(a01, b):
    hi, mid, lo = _split3(b)
    dims = (((1,), (0,)), ((), ()))
    dg = lambda p: lax.dot_general(a01, p, dims, preferred_element_type=F32)
    return dg(hi) + dg(mid) + dg(lo)


def _sigmoid(x):
    return 1.0 / (1.0 + jnp.exp(-x))


def _rms(x, g):
    return x * lax.rsqrt(jnp.mean(x * x, axis=-1, keepdims=True) + RMS_EPS) * g


def _norm_matmul_kernel(x_ref, g_ref, w_ref, o_ref, xn_ref):
    @pl.when(pl.program_id(1) == 0)
    def _():
        xn_ref[...] = _rms(x_ref[...], g_ref[...]).astype(BF16)

    o_ref[...] = jnp.dot(xn_ref[...], w_ref[...], preferred_element_type=F32)


def _norm_matmul(x, g, w, tm, tn):
    m, k = x.shape
    n = w.shape[1]
    return pl.pallas_call(
        _norm_matmul_kernel,
        grid=(m // tm, n // tn),
        in_specs=[pl.BlockSpec((tm, k), lambda i, j: (i, 0)),
                  pl.BlockSpec((1, k), lambda i, j: (0, 0)),
                  pl.BlockSpec((k, tn), lambda i, j: (0, j))],
        out_specs=pl.BlockSpec((tm, tn), lambda i, j: (i, j)),
        out_shape=jax.ShapeDtypeStruct((m, n), F32),
        scratch_shapes=[pltpu.VMEM((tm, k), BF16)],
        compiler_params=_cparams(("parallel", "arbitrary")),
        name="norm_matmul",
    )(x, g.reshape(1, k), w)


def _rwkv_kernel(r_ref, k_ref, v_ref, lora_ref, ulast_ref, s0_ref, mu_ref, w0_ref, w2_ref, a0_ref,
                 a2_ref, g2_ref, kk_ref, ka_ref, rk_ref, lng_ref, lnb_ref,
                 o_ref, sout_ref, s_ref, prev_ref, ar_ref, bk_ref, vb_ref, y_ref, *, chunk, bt, t_valid):
    c = pl.program_id(1)
    nc = pl.num_programs(1)
    C = chunk
    R = bt * C
    NP = A_HEADS // 2
    PB = 2 * C

    @pl.when(c == 0)
    def _():
        s_ref[...] = s0_ref[...].reshape(bt * NP, 2 * A_HD, 2 * A_HD)
        prev_ref[...] = ulast_ref[:, 0, :]

    row = lax.broadcasted_iota(jnp.int32, (R, 1), 0)

    def shift_mix(u3, lo, hi):
        u = u3.reshape(R, hi - lo)
        u_prev = pltpu.roll(u, 1, 0)
        for b in range(bt):
            u_prev = jnp.where(row == b * C, prev_ref[b:b + 1, lo:hi], u_prev)
        for b in range(bt):
            prev_ref[b:b + 1, lo:hi] = u[(b + 1) * C - 1:(b + 1) * C, :]
        return u + (u_prev - u) * mu_ref[:, lo:hi]

    r = shift_mix(r_ref[...], 0, A_W)
    k = shift_mix(k_ref[...], A_W, 2 * A_W)
    v = shift_mix(v_ref[...], 2 * A_W, 3 * A_W)
    lora = shift_mix(lora_ref[...], 3 * A_W, 3 * A_W + A_LORA)
    wl = lora[:, 0:A_LORA_W]
    al = lora[:, A_LORA_W:A_LORA_W + A_LORA_A]
    gl = lora[:, A_LORA_W + A_LORA_A:A_LORA]

    z = -(w0_ref[...] + _dot(jnp.tanh(wl), w2_ref[...]))
    softplus = jnp.maximum(z, 0.0) + jnp.log(1.0 + jnp.exp(-jnp.abs(z)))
    w = -softplus - 0.5
    a = _sigmoid(a0_ref[...] + _dot(al, a2_ref[...]))
    g = _dot(_sigmoid(gl), g2_ref[...])

    lane = lax.broadcasted_iota(jnp.int32, (A_W, A_W), 1) // A_HD
    sub = lax.broadcasted_iota(jnp.int32, (A_W, A_W), 0) // A_HD
    head_ones = (lane == sub).astype(BF16)

    kkv = k * kk_ref[...]
    kkn = kkv * lax.rsqrt(jnp.maximum(_dot_exact_rhs(kkv * kkv, head_ones), L2_EPS))
    k2 = k * (1.0 + (a - 1.0) * ka_ref[...])
    log_d = -jnp.exp(w)
    t_in = row % C
    if t_valid is not None:
        live = (c * C + t_in) < t_valid
        log_d = jnp.where(live, log_d, 0.0)
        kkn = jnp.where(live, kkn, 0.0)
        k2 = jnp.where(live, k2, 0.0)

    ri = lax.broadcasted_iota(jnp.int32, (R, R), 0)
    ci = lax.broadcasted_iota(jnp.int32, (R, R), 1)
    cum_mask = ((ri // C) == (ci // C)) & (ci <= ri)
    cum = _dot_exact_lhs(cum_mask.astype(BF16), log_d)
    c_incl = jnp.exp(cum)
    c_inv = jnp.exp(-cum)
    a_t = -kkn * jnp.exp(cum - log_d)
    r_t = r * c_incl
    b_t = kkn * a * c_inv
    k_t = k2 * c_inv

    lane = lax.broadcasted_iota(jnp.int32, (1, 2 * A_HD), 1) // A_HD
    def stage(ref, off, x):
        for b in range(bt):
            for p in range(NP):
                blk = x[b * C:(b + 1) * C, 2 * A_HD * p:2 * A_HD * (p + 1)]
                for hh in range(2):
                    ref[b * NP + p, off + hh * C:off + (hh + 1) * C, :] = (
                        jnp.where(lane == hh, blk, 0.0).astype(BF16))

    stage(ar_ref, 0, a_t)
    stage(ar_ref, PB, r_t)
    stage(bk_ref, 0, b_t)
    stage(bk_ref, PB, k_t)
    stage(vb_ref, 0, v)
    ar = ar_ref[...]
    bk = bk_ref[...]
    vb = vb_ref[...]

    nn = (((2,), (1,)), ((0,), (0,)))
    nt = (((2,), (2,)), ((0,), (0,)))
    tn = (((1,), (1,)), ((0,), (0,)))
    bdot = lambda x, y, dims=nn: lax.dot_general(x.astype(BF16), y.astype(BF16), dims,
                                                 preferred_element_type=F32)
    bi = lax.broadcasted_iota(jnp.int32, (PB, PB), 0)
    bj = lax.broadcasted_iota(jnp.int32, (PB, PB), 1)
    same_head = (bi // C) == (bj // C)
    strict = same_head & ((bj % C) < (bi % C))
    lower = same_head & ((bj % C) <= (bi % C))
    sub_blk = (bi // RWKV_SUB) == (bj // RWKV_SUB)
    eye = (bi == bj).astype(F32)

    gram = bdot(ar, bk, nt)
    l_b = jnp.where(strict, gram[:, :PB, :PB], 0.0)
    l_k = jnp.where(strict, gram[:, :PB, PB:], 0.0)
    m_b = jnp.where(lower, gram[:, PB:, :PB], 0.0)
    m_k = jnp.where(lower, gram[:, PB:, PB:], 0.0)
    dg = jnp.where(sub_blk, l_b, 0.0)
    off = l_b - dg
    t_inv = eye + dg
    pw = dg
    for _ in range(int(math.log2(RWKV_SUB)) - 1):
        pw = bdot(pw, pw)
        t_inv = t_inv + bdot(t_inv, pw)
    n1 = bdot(t_inv, off)
    n2 = bdot(n1, n1)
    full = eye + n1 + n2 + bdot(n1, n2)
    t_full = bdot(full, t_inv)

    s_old = s_ref[...]
    w0 = bdot(ar, s_old, nt)
    u = bdot(t_full, w0[:, :PB] + bdot(l_k, vb))
    uv = jnp.concatenate([u, vb.astype(F32)], axis=1)
    yb = w0[:, PB:] + bdot(jnp.concatenate([m_b, m_k], axis=2), uv)
    s_new = s_old + bdot(uv, bk, tn)
    for b in range(bt):
        for p in range(NP):
            sl = slice(2 * A_HD * p, 2 * A_HD * (p + 1))
            idx = b * NP + p
            s_ref[idx] = s_new[idx] * c_incl[(b + 1) * C - 1:(b + 1) * C, sl]
            y_ref[b * C:(b + 1) * C, sl] = yb[idx, :C] + yb[idx, C:]
    y = y_ref[...]

    inv_n = 1.0 / A_HD
    mean = _dot_exact_rhs(y, head_ones) * inv_n
    yc = y - mean
    var = _dot_exact_rhs(yc * yc, head_ones) * inv_n
    yn = yc * lax.rsqrt(var + LNX_EPS) * lng_ref[...] + lnb_ref[...]
    bonus = _dot_exact_rhs(r * k2 * rk_ref[...], head_ones) * v
    o_ref[...] = ((yn + bonus) * g).reshape(bt, C, A_W)

    @pl.when(c == nc - 1)
    def _():
        sout_ref[...] = s_ref[...].reshape(bt, NP, 2 * A_HD, 2 * A_HD)


RWKV_BATCH_TILE = 4


def _rwkv(proj, ulast, s0, prm, t_valid=None):
    b, t, _ = proj.shape
    C = min(RWKV_CHUNK, t)
    bt = min(RWKV_BATCH_TILE, b)
    nc = t // C
    vec = lambda n: pl.BlockSpec((1, n), lambda i, c: (0, 0))
    mat = lambda m, n: pl.BlockSpec((m, n), lambda i, c: (0, 0))
    col = lambda width, off: pl.BlockSpec((bt, C, width), lambda i, c: (i, c, off // width))
    n_shift = 3 * A_W + A_LORA
    n_pairs = bt * (A_HEADS // 2)
    kern = functools.partial(_rwkv_kernel, chunk=C, bt=bt, t_valid=t_valid)
    return pl.pallas_call(
        kern,
        grid=(b // bt, nc),
        in_specs=[col(A_W, P_R), col(A_W, P_K), col(A_W, P_V), col(A_LORA, P_LORA),
                  pl.BlockSpec((bt, 1, n_shift), lambda i, c: (i, 0, 0)),
                  pl.BlockSpec((bt, 4, 128, 128), lambda i, c: (i, 0, 0, 0)),
                  vec(n_shift), vec(A_W), mat(A_LORA_W, A_W), vec(A_W), mat(A_LORA_A, A_W),
                  mat(A_LORA_G, A_W), vec(A_W), vec(A_W), vec(A_W), vec(A_W), vec(A_W)],
        out_specs=[pl.BlockSpec((bt, C, A_W), lambda i, c: (i, c, 0)),
                   pl.BlockSpec((bt, 4, 128, 128), lambda i, c: (i, 0, 0, 0))],
        out_shape=[jax.ShapeDtypeStruct((b, t, A_W), F32),
                   jax.ShapeDtypeStruct((b, 4, 128, 128), F32)],
        scratch_shapes=[pltpu.VMEM((n_pairs, 128, 128), F32), pltpu.VMEM((bt, n_shift), F32),
                        pltpu.VMEM((n_pairs, 4 * C, 128), BF16), pltpu.VMEM((n_pairs, 4 * C, 128), BF16),
                        pltpu.VMEM((n_pairs, 2 * C, 128), BF16), pltpu.VMEM((bt * C, A_W), F32)],
        compiler_params=_cparams(("parallel", "arbitrary")),
        name="rwkv_chunk",
    )(proj, proj, proj, proj, ulast, s0, *prm)


def _pair_states(s):
    b = s.shape[0]
    s = s.reshape(b, 4, 2, A_HD, A_HD)
    z = jnp.zeros_like(s[:, :, 0])
    top = jnp.concatenate([s[:, :, 0], z], axis=-1)
    bot = jnp.concatenate([z, s[:, :, 1]], axis=-1)
    return jnp.concatenate([top, bot], axis=-2)


def _unpair_states(sp):
    b = sp.shape[0]
    h0 = sp[:, :, :A_HD, :A_HD]
    h1 = sp[:, :, A_HD:, A_HD:]
    return jnp.stack([h0, h1], axis=2).reshape(b, A_HEADS, A_HD, A_HD)


def _pool_kernel(u_ref, halo_ref, hist_ref, pm_ref, ps_ref, o_ref, ext_ref, *, tm, pos0):
    i = pl.program_id(1)

    @pl.when(i == 0)
    def _():
        ext_ref[0:POOL_HALO, :] = hist_ref[0]

    @pl.when(i > 0)
    def _():
        ext_ref[0:POOL_HALO, :] = halo_ref[0]

    cur = u_ref[0]
    ext_ref[POOL_HALO:POOL_HALO + tm, :] = cur
    pos = pos0 + i * tm + lax.broadcasted_iota(jnp.int32, (tm, 1), 0)
    outs = []
    for gi, win in enumerate(POOL_WINDOWS):
        lo, hi = gi * B_GW, (gi + 1) * B_GW
        s = cur[:, lo:hi]
        for back in range(1, win):
            s = s + ext_ref[POOL_HALO - back:POOL_HALO - back + tm, lo:hi]
        cnt = jnp.minimum(win, pos + 1).astype(F32)
        d = s / cnt - cur[:, lo:hi]
        outs.append(_dot(d, pm_ref[gi]))
    o_ref[0] = jnp.concatenate(outs, axis=1) * ps_ref[...]


def _pool(proj, hist, pm, ps, pos0, tm):
    b, t, _ = proj.shape
    per = tm // POOL_HALO
    kern = functools.partial(_pool_kernel, tm=tm, pos0=pos0)
    return pl.pallas_call(
        kern,
        grid=(b, t // tm),
        in_specs=[pl.BlockSpec((1, tm, B_W), lambda bi, i: (bi, i, P_POOL // B_W)),
                  pl.BlockSpec((1, POOL_HALO, B_W),
                               lambda bi, i: (bi, jnp.maximum(i * per - 1, 0), P_POOL // B_W)),
                  pl.BlockSpec((1, POOL_HALO, B_W), lambda bi, i: (bi, 0, 0)),
                  pl.BlockSpec((B_GROUPS, B_GW, B_GW), lambda bi, i: (0, 0, 0)),
                  pl.BlockSpec((1, B_W), lambda bi, i: (0, 0))],
        out_specs=pl.BlockSpec((1, tm, B_W), lambda bi, i: (bi, i, 0)),
        out_shape=jax.ShapeDtypeStruct((b, t, B_W), F32),
        scratch_shapes=[pltpu.VMEM((tm + POOL_HALO, B_W), F32)],
        compiler_params=_cparams(("parallel", "arbitrary")),
        name="pool_mix",
    )(proj, proj, hist, pm, ps.reshape(1, B_W))


def _merge_kernel(x_ref, oa_ref, ob_ref, oc_ref, mg_ref, wa_ref, wb_ref, wc_ref, wo_ref, o_ref):
    gate = lambda n: _sigmoid(mg_ref[:, n * D_MODEL:(n + 1) * D_MODEL])
    h = (gate(0) * _dot(oa_ref[...], wa_ref[...]) + gate(1) * _dot(ob_ref[...], wb_ref[...])
         + gate(2) * _dot(oc_ref[...], wc_ref[...]))
    o_ref[...] = x_ref[...] + _dot(h, wo_ref[...])


def _merge(x, oa, ob, oc, gates, wa, wb, wc, wo, tm):
    m = x.shape[0]
    row = lambda n: pl.BlockSpec((tm, n), lambda i: (i, 0))
    full = lambda a: pl.BlockSpec(a.shape, lambda i: (0, 0))
    return pl.pallas_call(
        _merge_kernel,
        grid=(m // tm,),
        in_specs=[row(D_MODEL), row(A_W), row(B_W), row(C_W), row(3 * D_MODEL),
                  full(wa), full(wb), full(wc), full(wo)],
        out_specs=row(D_MODEL),
        out_shape=jax.ShapeDtypeStruct((m, D_MODEL), F32),
        compiler_params=_cparams(("parallel",)),
        name="merge_branches",
    )(x, oa, ob, oc, gates, wa, wb, wc, wo)


def _mem_kv_kernel(mem_ref, g_ref, wk_ref, wv_ref, kg_ref, k_ref, v_ref):
    mn = _rms(mem_ref[...], g_ref[...]).astype(BF16)
    k = jnp.dot(mn, wk_ref[...], preferred_element_type=F32)
    v_ref[...] = jnp.dot(mn, wv_ref[...], preferred_element_type=F32)
    k_ref[...] = jnp.concatenate(
        [_rms(k[:, h * M_HD:(h + 1) * M_HD], kg_ref[...]) for h in range(M_HEADS)], axis=1)


def _mem_kv(mem, g, wk, wv, kg):
    m = mem.shape[0]
    tm = N_MEM
    row = lambda n: pl.BlockSpec((tm, n), lambda i: (i, 0))
    full = lambda a: pl.BlockSpec(a.shape, lambda i: (0, 0))
    g2, kg2 = g.reshape(1, D_MODEL), kg.reshape(1, M_HD)
    return pl.pallas_call(
        _mem_kv_kernel,
        grid=(m // tm,),
        in_specs=[row(D_MODEL), full(g2), full(wk), full(wv), full(kg2)],
        out_specs=[row(M_W), row(M_W)],
        out_shape=[jax.ShapeDtypeStruct((m, M_W), F32), jax.ShapeDtypeStruct((m, M_W), F32)],
        compiler_params=_cparams(("parallel",)),
        name="mem_kv",
    )(mem, g2, wk, wv, kg2)


def _mem_ffn_kernel(x_ref, mk_ref, mv_ref, gx_ref, wq_ref, qg_ref, wo_ref, gf_ref, w1_ref, w2_ref,
                    o_ref, xn_ref, *, tm, rows_per_batch, nkb):
    j = pl.program_id(1)

    @pl.when(j == 0)
    def _():
        x = x_ref[...]
        q = _dot(_rms(x, gx_ref[...]), wq_ref[...])
        mk = mk_ref[...].reshape(nkb * N_MEM, M_W)
        mv = mv_ref[...].reshape(nkb * N_MEM, M_W)
        if nkb > 1:
            qb = lax.broadcasted_iota(jnp.int32, (tm, nkb * N_MEM), 0) // rows_per_batch
            kb = lax.broadcasted_iota(jnp.int32, (tm, nkb * N_MEM), 1) // N_MEM
            same = qb == kb
        outs = []
        for h in range(M_HEADS):
            sl = slice(h * M_HD, (h + 1) * M_HD)
            qh = _rms(q[:, sl], qg_ref[...])
            logits = _dot_nt(qh, mk[:, sl]) * (M_HD ** -0.5)
            if nkb > 1:
                logits = jnp.where(same, logits, NEG)
            mx = jnp.max(logits, axis=-1, keepdims=True)
            p = jnp.exp(logits - mx)
            p = p / jnp.sum(p, axis=-1, keepdims=True)
            outs.append(_dot(p, mv[:, sl]))
        o = jnp.concatenate(outs, axis=1)
        xm = x + _dot(o, wo_ref[...])
        o_ref[...] = xm
        xn_ref[...] = _rms(xm, gf_ref[...]).astype(BF16)

    h1 = jnp.dot(xn_ref[...], w1_ref[...], preferred_element_type=F32)
    h1 = jnp.square(jnp.maximum(h1, 0.0))
    o_ref[...] += _dot(h1, w2_ref[...])


def _mem_ffn(x, mk, mv, gx, wq, qg, wo, gf, w1, w2, tm, rows_per_batch, tf=1024):
    m = x.shape[0]
    nkb = max(tm // rows_per_batch, 1)
    kern = functools.partial(_mem_ffn_kernel, tm=tm, rows_per_batch=rows_per_batch, nkb=nkb)
    full = lambda a: pl.BlockSpec(a.shape, lambda i, j: (0,) * a.ndim)
    gx2, qg2, gf2 = gx.reshape(1, D_MODEL), qg.reshape(1, M_HD), gf.reshape(1, D_MODEL)
    mem_spec = pl.BlockSpec((nkb, N_MEM, M_W), lambda i, j: ((i * tm) // (rows_per_batch * nkb), 0, 0))
    return pl.pallas_call(
        kern,
        grid=(m // tm, D_FF // tf),
        in_specs=[pl.BlockSpec((tm, D_MODEL), lambda i, j: (i, 0)), mem_spec, mem_spec,
                  full(gx2), full(wq), full(qg2), full(wo), full(gf2),
                  pl.BlockSpec((D_MODEL, tf), lambda i, j: (0, j)),
                  pl.BlockSpec((tf, D_MODEL), lambda i, j: (j, 0))],
        out_specs=pl.BlockSpec((tm, D_MODEL), lambda i, j: (i, 0)),
        out_shape=jax.ShapeDtypeStruct((m, D_MODEL), F32),
        scratch_shapes=[pltpu.VMEM((tm, D_MODEL), BF16)],
        compiler_params=_cparams(("parallel", "arbitrary")),
        name="mem_ffn",
    )(x, mk, mv, gx2, wq, qg2, wo, gf2, w1, w2)


def _half_ones():
    i = lax.broadcasted_iota(jnp.int32, (LANES, LANES), 0) // C_HD
    j = lax.broadcasted_iota(jnp.int32, (LANES, LANES), 1) // C_HD
    return (i == j).astype(BF16)


def _rms_halves(x, g, ones):
    ms = _dot_exact_rhs(x * x, ones) * (1.0 / C_HD)
    return x * lax.rsqrt(ms + RMS_EPS) * g


def _nsa_prep_kernel(q_ref, c_ref, s_ref, w_ref, qg_ref, kg_ref,
                     qn_ref, ks16_ref, vs16_ref, kw16_ref, vw16_ref, kc_ref, vc_ref, rows_ref, *, tm):
    ones = _half_ones()
    scale = C_HD ** -0.5
    q = q_ref[0]
    qn_ref[0] = jnp.concatenate(
        [_rms_halves(q[:, n * LANES:(n + 1) * LANES], qg_ref[...], ones) * scale for n in range(C_HPG)],
        axis=1).astype(BF16)
    c = c_ref[0]
    rows_ref[0, 0] = c[:, :LANES].T
    rows_ref[0, 1] = c[:, LANES:].T
    s = s_ref[0]
    ks = _rms_halves(s[:, :LANES], kg_ref[1:2, :], ones)
    vs_t = s[:, LANES:].T
    ks16_ref[0] = ks.astype(BF16)
    vs16_ref[0] = vs_t.astype(BF16)
    rows_ref[0, 2] = ks.T
    rows_ref[0, 3] = vs_t
    w = w_ref[0]
    kw = _rms_halves(w[:, :LANES], kg_ref[2:3, :], ones)
    vw_t = w[:, LANES:].T
    kw16_ref[0] = kw.astype(BF16)
    vw16_ref[0] = vw_t.astype(BF16)
    rows_ref[0, 4] = kw.T
    rows_ref[0, 5] = vw_t
    nb = tm // CMP_BLOCK
    bi = lax.broadcasted_iota(jnp.int32, (nb, tm), 0)
    ti = lax.broadcasted_iota(jnp.int32, (nb, tm), 1) // CMP_BLOCK
    pool = (bi == ti).astype(BF16)
    means = _dot_exact_lhs(pool, c) * (1.0 / CMP_BLOCK)
    kc_ref[0] = _rms_halves(means[:, :LANES], kg_ref[0:1, :], ones)
    vc_ref[0] = means[:, LANES:]


def _nsa_prep(proj, qg, kg, tm):
    b, t, _ = proj.shape
    nbt = tm // CMP_BLOCK
    kern = functools.partial(_nsa_prep_kernel, tm=tm)
    tok = lambda n, dt: jax.ShapeDtypeStruct((b, t, n), dt)
    blk = jax.ShapeDtypeStruct((b, t // CMP_BLOCK, LANES), F32)
    col = lambda width, off: pl.BlockSpec((1, tm, width), lambda bi, i: (bi, i, off // width))
    out_tok = lambda n: pl.BlockSpec((1, tm, n), lambda bi, i: (bi, i, 0))
    out_blk = pl.BlockSpec((1, nbt, LANES), lambda bi, i: (bi, i, 0))
    out_t = pl.BlockSpec((1, LANES, tm), lambda bi, i: (bi, 0, i))
    tok_t = jax.ShapeDtypeStruct((b, LANES, t), BF16)
    return pl.pallas_call(
        kern,
        grid=(b, t // tm),
        in_specs=[col(C_W, P_Q), col(2 * LANES, P_KV), col(2 * LANES, P_KV + 2 * LANES),
                  col(2 * LANES, P_KV + 4 * LANES),
                  pl.BlockSpec((1, LANES), lambda bi, i: (0, 0)),
                  pl.BlockSpec((3, LANES), lambda bi, i: (0, 0))],
        out_specs=[out_tok(C_W), out_tok(LANES), out_t, out_tok(LANES), out_t, out_blk, out_blk,
                   pl.BlockSpec((1, 6, LANES, tm), lambda bi, i: (bi, 0, 0, i))],
        out_shape=[tok(C_W, BF16), tok(LANES, BF16), tok_t, tok(LANES, BF16), tok_t, blk, blk,
                   jax.ShapeDtypeStruct((b, 6, LANES, t), F32)],
        compiler_params=_cparams(("parallel", "parallel")),
        name="nsa_prep",
    )(proj, proj, proj, proj, jnp.tile(qg.reshape(1, C_HD), (1, 2)), jnp.tile(kg, (1, 2)))


M_INIT = -1e29
ROWS = C_HPG * Q_TILE
SEL_PAD = 8
FAR_TILES = 4


def _heads(x):
    return jnp.concatenate([x] * C_HPG, axis=1)


def _softmax_step(qs, k, v_t, state, tail_bias=None, valid=None):
    m_ref, l_ref, acc_ref = state
    old = [(m_ref[g], l_ref[g], acc_ref[g]) for g in range(C_KV)]
    new = []
    for g in range(C_KV):
        m_old, l_old, acc_old = old[g]
        s = lax.dot_general(k, qs[g], (((1,), (1,)), ((), ())), preferred_element_type=F32)
        if tail_bias is not None:
            cut = s.shape[0] - tail_bias[g].shape[0]
            tail = s[cut:] + tail_bias[g]
            s = jnp.concatenate([s[:cut], tail], axis=0) if cut else tail
        if valid is not None:
            s = jnp.where(_heads(valid[g]), s, NEG)
        m_new = jnp.maximum(m_old, jnp.max(s, axis=0, keepdims=True))
        alpha = jnp.exp(m_old - m_new)
        p = jnp.exp(s - m_new)
        new.append((m_new, alpha * l_old + jnp.sum(p, axis=0, keepdims=True),
                    alpha * acc_old + jnp.dot(v_t, p.astype(BF16), preferred_element_type=F32)))
    for g in range(C_KV):
        m_ref[g], l_ref[g], acc_ref[g] = new[g]


def _softmax_reset(state):
    m_ref, l_ref, acc_ref = state
    m_ref[...] = jnp.full(m_ref.shape, M_INIT, F32)
    l_ref[...] = jnp.zeros(l_ref.shape, F32)
    acc_ref[...] = jnp.zeros(acc_ref.shape, F32)


def _softmax_result(state, g):
    _, l_ref, acc_ref = state
    l = l_ref[g]
    return acc_ref[g] / jnp.where(l > 0.0, l, 1.0)


def _top_blocks(score, n_sel):
    nb = score.shape[0]
    blk = lax.broadcasted_iota(jnp.int32, score.shape, 0).astype(F32)
    sel = jnp.zeros(score.shape, F32)
    work = score
    for _ in range(min(n_sel, nb)):
        mx = jnp.max(work, axis=0, keepdims=True)
        idx = jnp.min(jnp.where(work == mx, blk, float(nb)), axis=0, keepdims=True)
        pick = blk == idx
        sel = jnp.where(pick, 1.0, sel)
        work = jnp.where(pick, -jnp.inf, work)
    return sel


def _nsa_attn_kernel(qn_ref, ks_ref, vs_ref, kw_ref, vw_ref, kc_ref, vc_ref, cg_ref, bnear_ref,
                     bcmp_ref, o_ref, m_ref, l_ref, acc_ref, sel_ref, *, nb):
    i = pl.program_id(1)
    q_all = qn_ref[0]
    lane = lax.broadcasted_iota(jnp.int32, (Q_TILE, LANES), 1)
    key = lax.broadcasted_iota(jnp.int32, (Q_TILE, LANES), 0)
    causal = key <= lane
    sig_t = _sigmoid(cg_ref[0]).T
    kc = kc_ref[0].astype(BF16)
    vc_t = vc_ref[0].T.astype(BF16)
    blk = lax.broadcasted_iota(jnp.int32, (nb, Q_TILE), 0)
    qpos = i * Q_TILE + lax.broadcasted_iota(jnp.int32, (nb, Q_TILE), 1)
    cur = qpos // CMP_BLOCK
    cmp_valid = _heads(qpos >= blk * CMP_BLOCK + (CMP_BLOCK - 1))
    forced = (blk == 0) | (blk == cur) | (blk == cur - 1)
    state = (m_ref, l_ref, acc_ref)
    n_back = WINDOW // Q_TILE
    k_rows = lambda ref, j, n: ref[0, pl.ds(pl.multiple_of((j + n_back) * Q_TILE, Q_TILE), n * Q_TILE), :]
    v_cols = lambda ref, j, n: ref[0, :, pl.ds(pl.multiple_of((j + n_back) * Q_TILE, Q_TILE), n * Q_TILE)]

    def key_mask(g, j, n):
        rows = [sel_ref[g, pl.ds(SEL_PAD + 2 * j + r, 1), :] for r in range(2 * n)]
        return jnp.concatenate(
            [jnp.where(key < CMP_BLOCK, rows[2 * t], rows[2 * t + 1]) for t in range(n)], axis=0) > 0.5

    qs = []
    for g in range(C_KV):
        half = (lane // C_HD) == g
        qs.append(jnp.concatenate(
            [jnp.where(half, q_all[:, n * LANES:(n + 1) * LANES], 0.0) for n in range(C_HPG)],
            axis=0).astype(BF16))

    o_cmp, scores = [], []
    for g in range(C_KV):
        bias_c = pltpu.roll(bcmp_ref[g], (2 * i + 2) % LANES, 0)[:nb, :]
        lc = lax.dot_general(kc, qs[g], (((1,), (1,)), ((), ())), preferred_element_type=F32) + bias_c
        lc = jnp.where(cmp_valid, lc, NEG)
        mc = jnp.maximum(jnp.max(lc, axis=0, keepdims=True), M_INIT)
        pc = jnp.exp(lc - mc)
        den = jnp.sum(pc, axis=0, keepdims=True)
        pc = pc / jnp.where(den > 0.0, den, 1.0)
        o_cmp.append(jnp.dot(vc_t, pc.astype(BF16), preferred_element_type=F32))
        psum = sum(pc[:, n * Q_TILE:(n + 1) * Q_TILE] for n in range(C_HPG))
        scores.append(jnp.where(blk > cur, NEG, jnp.where(forced, FORCE_SCORE, psum)))

    score = jnp.concatenate(scores, axis=1)
    sel = jnp.where(score > 0.5 * NEG, _top_blocks(score, N_SEL), 0.0)
    for g in range(C_KV):
        sel_ref[g, 0:SEL_PAD, :] = jnp.zeros((SEL_PAD, Q_TILE), F32)
        sel_ref[g, SEL_PAD:SEL_PAD + nb, :] = sel[:, g * Q_TILE:(g + 1) * Q_TILE]

    _softmax_reset(state)
    n_far = jnp.maximum(i - 1, 0)

    def far_step(j, n):
        _softmax_step(qs, k_rows(ks_ref, j, n), v_cols(vs_ref, j, n), state,
                      valid=[key_mask(g, j, n) for g in range(C_KV)])

    def far_many(jq, carry):
        far_step(FAR_TILES * jq, FAR_TILES)
        return carry

    def far_one(j, carry):
        far_step(j, 1)
        return carry

    n_many = n_far // FAR_TILES
    lax.fori_loop(0, n_many, far_many, 0)
    lax.fori_loop(FAR_TILES * n_many, n_far, far_one, 0)

    near_causal = jnp.concatenate([key >= 0, causal], axis=0)
    bias2 = [bnear_ref[g] for g in range(C_KV)]
    _softmax_step(qs, k_rows(ks_ref, i - 1, 2), v_cols(vs_ref, i - 1, 2), state, tail_bias=bias2,
                  valid=[key_mask(g, i - 1, 2) & near_causal for g in range(C_KV)])
    o_slc = [_softmax_result(state, g) for g in range(C_KV)]

    _softmax_reset(state)
    exists = lambda back: key >= jnp.where(i >= back, 0, Q_TILE)
    win_valid = jnp.concatenate(
        [(key > lane) & exists(n_back)] + [exists(back) for back in range(n_back - 1, 0, -1)] + [causal],
        axis=0)
    _softmax_step(qs, k_rows(kw_ref, i - n_back, n_back + 1), v_cols(vw_ref, i - n_back, n_back + 1), state,
                  tail_bias=bias2, valid=[win_valid] * C_KV)
    o_win = [_softmax_result(state, g) for g in range(C_KV)]

    low = (key // C_HD) == 0
    for n in range(C_HPG):
        cols = slice(n * Q_TILE, (n + 1) * Q_TILE)
        per_group = []
        for g in range(C_KV):
            h = g * C_HPG + n
            gate = lambda br: sig_t[br * C_HEADS + h:br * C_HEADS + h + 1, :]
            per_group.append(gate(0) * o_cmp[g][:, cols] + gate(1) * o_slc[g][:, cols]
                             + gate(2) * o_win[g][:, cols])
        o_ref[0, :, n * LANES:(n + 1) * LANES] = jnp.where(low, per_group[0], per_group[1]).T


def _nsa_attn(qn, ks16, vs16t, kw16, vw16t, kc, vc, gates, bnear, bcmp):
    b, t, _ = qn.shape
    nb = t // CMP_BLOCK
    tp = t + WINDOW
    kern = functools.partial(_nsa_attn_kernel, nb=nb)
    seq = pl.BlockSpec((1, tp, LANES), lambda bi, i: (bi, 0, 0))
    seq_t = pl.BlockSpec((1, LANES, tp), lambda bi, i: (bi, 0, 0))
    blk = pl.BlockSpec((1, nb, LANES), lambda bi, i: (bi, 0, 0))
    return pl.pallas_call(
        kern,
        grid=(b, t // Q_TILE),
        in_specs=[pl.BlockSpec((1, Q_TILE, C_W), lambda bi, i: (bi, i, 0)), seq, seq_t, seq, seq_t, blk, blk,
                  pl.BlockSpec((1, Q_TILE, LANES), lambda bi, i: (bi, i, G_CG // LANES)),
                  pl.BlockSpec(bnear.shape, lambda bi, i: (0, 0, 0)),
                  pl.BlockSpec(bcmp.shape, lambda bi, i: (0, 0, 0))],
        out_specs=pl.BlockSpec((1, Q_TILE, C_W), lambda bi, i: (bi, i, 0)),
        out_shape=jax.ShapeDtypeStruct((b, t, C_W), F32),
        scratch_shapes=[pltpu.VMEM((C_KV, 1, ROWS), F32), pltpu.VMEM((C_KV, 1, ROWS), F32),
                        pltpu.VMEM((C_KV, LANES, ROWS), F32),
                        pltpu.VMEM((C_KV, SEL_PAD + nb, Q_TILE), F32)],
        compiler_params=_cparams(("parallel", "arbitrary")),
        name="nsa_attn",
    )(qn, ks16, vs16t, kw16, vw16t, kc, vc, gates, bnear, bcmp)


def _rel_bucket(dist):
    n = jnp.maximum(dist, 0)
    exact = N_BUCKETS // 2
    ratio = jnp.log(jnp.maximum(n, 1).astype(F32) / exact) / math.log(MAX_DIST / exact)
    large = jnp.minimum(exact + (ratio * (N_BUCKETS - exact)).astype(jnp.int32), N_BUCKETS - 1)
    return jnp.where(n < exact, n, large)


def _bias_tables(rel_bias):
    tbl = rel_bias[_rel_bucket(jnp.arange(MAX_DIST)), :]
    tbl = tbl - tbl[MAX_DIST - 1:MAX_DIST, :]
    tbl = tbl.T.reshape(C_KV, C_HPG, MAX_DIST)
    qq = jnp.arange(Q_TILE)[:, None]
    cc = jnp.arange(LANES)[None, :]
    look = lambda dist: tbl[:, :, jnp.clip(dist, 0, MAX_DIST - 1)].reshape(C_KV, ROWS, LANES)
    bnear = jnp.stack([look(qq - cc), look(qq - cc + Q_TILE)], axis=1)
    bcmp = look(qq - CMP_BLOCK * cc + CMP_BLOCK * (LANES - 1) - (LANES - 1))
    bnear = bnear.transpose(0, 1, 3, 2)
    return jnp.concatenate([bnear[:, 1], bnear[:, 0]], axis=1), bcmp.transpose(0, 2, 1)


PAGES_PER_STEP = 8
S_ROWS = C_KV * C_HPG
BLK_LANES = 384
NEW_PAD = 8


def _page_specs(layer, n):
    spec = lambda j: pl.BlockSpec((1, 1, LANES, PAGE_SIZE),
                                  lambda b, c, pt, j=j: (layer, pt[b, c * n + j], 0, 0))
    return [spec(j) for j in range(n)]


def _cmp_pages_kernel(pt_ref, *refs, n):
    k_refs, v_refs, (kc_ref, vc_ref) = refs[:n], refs[n:2 * n], refs[2 * n:]
    row = lax.broadcasted_iota(jnp.int32, (n * PAGE_SIZE, 2 * n), 0) // CMP_BLOCK
    col = lax.broadcasted_iota(jnp.int32, (n * PAGE_SIZE, 2 * n), 1)
    pool = (row == col).astype(BF16)
    for refs_in, out in ((k_refs, kc_ref), (v_refs, vc_ref)):
        pages = jnp.concatenate([r[0, 0] for r in refs_in], axis=1)
        out[0, 0] = _dot_exact_rhs(pages, pool) * (1.0 / CMP_BLOCK)


def _cmp_pages(page_table, cache_k, cache_v, layer):
    b, n_pages = page_table.shape
    n = min(PAGES_PER_STEP, n_pages)
    chunks = n_pages // n
    out = jax.ShapeDtypeStruct((b, chunks, LANES, 2 * n), F32)
    out_spec = pl.BlockSpec((1, 1, LANES, 2 * n), lambda bi, c, pt: (bi, c, 0, 0))
    return pl.pallas_call(
        functools.partial(_cmp_pages_kernel, n=n),
        grid_spec=pltpu.PrefetchScalarGridSpec(
            num_scalar_prefetch=1, grid=(b, chunks),
            in_specs=_page_specs(layer, n) + _page_specs(layer, n),
            out_specs=[out_spec, out_spec]),
        out_shape=[out, out],
        compiler_params=_cparams(("parallel", "arbitrary")),
        name="nsa_sample_cmp_pages",
    )(page_table, *([cache_k] * n), *([cache_v] * n))


def _top_blocks_lanes(score, n_sel):
    nb = score.shape[-1]
    blk = lax.broadcasted_iota(jnp.int32, score.shape, 1).astype(F32)
    sel = jnp.zeros(score.shape, F32)
    work = score
    for _ in range(min(n_sel, nb)):
        mx = jnp.max(work, axis=-1, keepdims=True)
        idx = jnp.min(jnp.where(work == mx, blk, float(nb)), axis=-1, keepdims=True)
        pick = blk == idx
        sel = jnp.where(pick, 1.0, sel)
        work = jnp.where(pick, -jnp.inf, work)
    return sel


def _nsa_sample_head_kernel(q_ref, newc_ref, news_ref, neww_ref, kc_ref, vc_ref, wk_ref, wv_ref,
                            qg_ref, kg_ref, kgc_ref, bc_ref, bw_ref, bn_ref,
                            q32_ref, sel_ref, ocmp_ref, owin_ref, new_ref, q_scr, *, ts, past, wbuf):
    n_rows = S_ROWS * ts
    ones = _half_ones()
    scale = C_HD ** -0.5
    lane = lax.broadcasted_iota(jnp.int32, (ts, LANES), 1)
    q = q_ref[0]
    for n in range(C_HPG):
        qn = _rms_halves(q[:, n * LANES:(n + 1) * LANES], qg_ref[...], ones) * scale
        for g in range(C_KV):
            lo = (g * C_HPG + n) * ts
            q_scr[lo:lo + ts, :] = jnp.where((lane // C_HD) == g, qn, 0.0)
    q32 = q_scr[...].astype(BF16)
    q32_ref[0] = q32

    newc, news, neww = newc_ref[0], news_ref[0], neww_ref[0]
    ks_new = _rms_halves(news[:, :LANES], kg_ref[1:2, :], ones)
    kw_new = _rms_halves(neww[:, :LANES], kg_ref[2:3, :], ones)
    new_ref[0, 0] = newc[:, :LANES]
    new_ref[0, 1] = newc[:, LANES:]
    new_ref[0, 2] = ks_new
    new_ref[0, 3] = news[:, LANES:]
    new_ref[0, 4] = kw_new
    new_ref[0, 5] = neww[:, LANES:]

    t_of_row = lax.broadcasted_iota(jnp.int32, (n_rows, 1), 0) % ts
    qpos = past + t_of_row

    n_cached = past // CMP_BLOCK
    pad_rows = jnp.zeros((NEW_PAD - ts, LANES), F32)
    ones_rows = jnp.full((NEW_PAD, BLK_LANES), 1.0, BF16)
    new_mean = lambda rows: _dot_tn_exact(jnp.concatenate([rows, pad_rows], axis=0),
                                          ones_rows) * (1.0 / CMP_BLOCK)
    is_new = lax.broadcasted_iota(jnp.int32, (LANES, BLK_LANES), 1) == n_cached
    kc_all = jnp.where(is_new, new_mean(newc[:, :LANES]), kc_ref[0])
    vc_all = jnp.where(is_new, new_mean(newc[:, LANES:]), vc_ref[0])
    feat = lax.broadcasted_iota(jnp.int32, (LANES, 1), 0) // C_HD
    sq = kc_all * kc_all
    ms = jnp.where(feat == 0, jnp.sum(sq[:C_HD], axis=0, keepdims=True),
                   jnp.sum(sq[C_HD:], axis=0, keepdims=True)) * (1.0 / C_HD)
    kc_n = kc_all * lax.rsqrt(ms + RMS_EPS) * kgc_ref[:, 0:1]
    blk = lax.broadcasted_iota(jnp.int32, (n_rows, BLK_LANES), 1)
    lc = jnp.dot(q32, kc_n.astype(BF16), preferred_element_type=F32) + bc_ref[...]
    lc = jnp.where(qpos >= blk * CMP_BLOCK + (CMP_BLOCK - 1), lc, NEG)
    mc = jnp.maximum(jnp.max(lc, axis=-1, keepdims=True), M_INIT)
    pc = jnp.exp(lc - mc)
    den = jnp.sum(pc, axis=-1, keepdims=True)
    pc = pc / jnp.where(den > 0.0, den, 1.0)
    ocmp_ref[0] = lax.dot_general(pc.astype(BF16), vc_all.astype(BF16), (((1,), (1,)), ((), ())),
                                  preferred_element_type=F32)

    gi = lax.broadcasted_iota(jnp.int32, (C_KV * ts, n_rows), 0)
    ri = lax.broadcasted_iota(jnp.int32, (C_KV * ts, n_rows), 1)
    same = ((gi // ts) == (ri // (C_HPG * ts))) & ((gi % ts) == (ri % ts))
    psum = _dot_exact_lhs(same.astype(BF16), pc)
    sblk = lax.broadcasted_iota(jnp.int32, (C_KV * ts, BLK_LANES), 1)
    scur = (past + lax.broadcasted_iota(jnp.int32, (C_KV * ts, 1), 0) % ts) // CMP_BLOCK
    forced = (sblk == 0) | (sblk == scur) | (sblk == scur - 1)
    score = jnp.where(sblk > scur, NEG, jnp.where(forced, FORCE_SCORE, psum))
    sel = jnp.where(score > 0.5 * NEG, _top_blocks_lanes(score, N_SEL), 0.0)
    sel_ref[0] = lax.dot_general(same.astype(BF16), sel.astype(BF16), (((0,), (0,)), ((), ())),
                                 preferred_element_type=F32)

    key = lax.broadcasted_iota(jnp.int32, (n_rows, wbuf), 1)
    lw = jnp.dot(q32, wk_ref[0, 0].astype(BF16), preferred_element_type=F32) + bw_ref[...]
    kpos = past - wbuf + key
    dist = qpos - kpos
    lw = jnp.where((dist >= 0) & (dist < WINDOW) & (kpos >= 0), lw, NEG)
    kw_pad = jnp.concatenate([kw_new, pad_rows], axis=0)
    vw_pad = jnp.concatenate([neww[:, LANES:], pad_rows], axis=0)
    nkey = lax.broadcasted_iota(jnp.int32, (n_rows, NEW_PAD), 1)
    ln = lax.dot_general(q32, kw_pad.astype(BF16), (((1,), (1,)), ((), ())),
                         preferred_element_type=F32) + bn_ref[...]
    ln = jnp.where((nkey <= t_of_row) & (nkey < ts), ln, NEG)
    mw = jnp.maximum(jnp.max(lw, axis=-1, keepdims=True), jnp.max(ln, axis=-1, keepdims=True))
    pw, pn = jnp.exp(lw - mw), jnp.exp(ln - mw)
    den = jnp.sum(pw, axis=-1, keepdims=True) + jnp.sum(pn, axis=-1, keepdims=True)
    ow = lax.dot_general(pw.astype(BF16), wv_ref[0, 0].astype(BF16), (((1,), (1,)), ((), ())),
                         preferred_element_type=F32) + _dot(pn, vw_pad)
    owin_ref[0] = ow / den


def _dot_tn_exact(a, b01):
    dims = (((0,), (0,)), ((), ()))
    return sum(lax.dot_general(p, b01, dims, preferred_element_type=F32) for p in _split3(a))


def _nsa_sample_head(main, kc_t, vc_t, win_k, win_v, layer, qg, kg, tables, past):
    b, ts, _ = main.shape
    wbuf = win_k.shape[-1]
    n_rows = S_ROWS * ts
    bias_c, bias_w, bias_n = tables
    kern = functools.partial(_nsa_sample_head_kernel, ts=ts, past=past, wbuf=wbuf)
    col = lambda width, off: pl.BlockSpec((1, ts, width), lambda i: (i, 0, off // width))
    full = lambda a: pl.BlockSpec(a.shape, lambda i: (0,) * a.ndim)
    per_b = lambda a: pl.BlockSpec((1,) + a.shape[1:], lambda i: (i,) + (0,) * (a.ndim - 1))
    win = pl.BlockSpec((1, 1, LANES, wbuf), lambda i: (layer, i, 0, 0))
    qg2 = jnp.tile(qg.reshape(1, C_HD), (1, 2))
    kg2 = jnp.tile(kg, (1, 2))
    kg_col = jnp.tile(kg2[0].reshape(LANES, 1), (1, LANES))
    rows = lambda n, dt: jax.ShapeDtypeStruct((b, n_rows, n), dt)
    out_rows = lambda n: pl.BlockSpec((1, n_rows, n), lambda i: (i, 0, 0))
    return pl.pallas_call(
        kern,
        grid=(b,),
        in_specs=[col(C_W, P_Q), col(2 * LANES, P_KV), col(2 * LANES, P_KV + 2 * LANES),
                  col(2 * LANES, P_KV + 4 * LANES), per_b(kc_t), per_b(vc_t), win, win,
                  full(qg2), full(kg2), full(kg_col), full(bias_c), full(bias_w), full(bias_n)],
        out_specs=[out_rows(LANES), out_rows(BLK_LANES), out_rows(LANES), out_rows(LANES),
                   pl.BlockSpec((1, 6, ts, LANES), lambda i: (i, 0, 0, 0))],
        out_shape=[rows(LANES, BF16), rows(BLK_LANES, F32), rows(LANES, F32), rows(LANES, F32),
                   jax.ShapeDtypeStruct((b, 6, ts, LANES), F32)],
        scratch_shapes=[pltpu.VMEM((n_rows, LANES), F32)],
        compiler_params=_cparams(("parallel",)),
        name="nsa_sample_head",
    )(main, main, main, main, kc_t, vc_t, win_k, win_v, qg2, kg2, kg_col, bias_c, bias_w, bias_n)


def _slc_pages_kernel(pt_ref, *refs, n, ts, last_chunk):
    k_refs, v_refs = refs[:n], refs[n:2 * n]
    (q_ref, sel_ref, ocmp_ref, owin_ref, gate_ref, new_ref, blast_ref, bn_ref,
     o_ref, m_ref, l_ref, acc_ref) = refs[2 * n:]
    c = pl.program_id(1)
    n_rows = S_ROWS * ts
    nt = (((1,), (1,)), ((), ()))

    @pl.when(c == 0)
    def _():
        m_ref[...] = jnp.full(m_ref.shape, M_INIT, F32)
        l_ref[...] = jnp.zeros(l_ref.shape, F32)
        acc_ref[...] = jnp.zeros(acc_ref.shape, F32)

    q32 = q_ref[0]
    keys = jnp.concatenate([r[0, 0] for r in k_refs], axis=1).astype(BF16)
    vals = jnp.concatenate([r[0, 0] for r in v_refs], axis=1).astype(BF16)
    s = jnp.dot(q32, keys, preferred_element_type=F32)
    cut = (n - 1) * PAGE_SIZE
    tail = s[:, cut:] + jnp.where(c == last_chunk, blast_ref[...], 0.0)
    s = jnp.concatenate([s[:, :cut], tail], axis=1) if cut else tail
    row = lax.broadcasted_iota(jnp.int32, (LANES, n * PAGE_SIZE), 0)
    col = lax.broadcasted_iota(jnp.int32, (LANES, n * PAGE_SIZE), 1) // CMP_BLOCK
    expand = (row == col).astype(BF16)
    picked = jnp.dot(sel_ref[0, 0].astype(BF16), expand, preferred_element_type=F32) > 0.5
    s = jnp.where(picked, s, NEG)
    m_old = m_ref[...]
    m_new = jnp.maximum(m_old, jnp.max(s, axis=-1, keepdims=True))
    alpha = jnp.exp(m_old - m_new)
    p = jnp.exp(s - m_new)
    l_ref[...] = alpha * l_ref[...] + jnp.sum(p, axis=-1, keepdims=True)
    acc_ref[...] = alpha * acc_ref[...] + lax.dot_general(p.astype(BF16), vals, nt,
                                                          preferred_element_type=F32)
    m_ref[...] = m_new

    @pl.when(c == last_chunk)
    def _():
        t_of_row = lax.broadcasted_iota(jnp.int32, (n_rows, 1), 0) % ts
        pad_rows = jnp.zeros((NEW_PAD - ts, LANES), F32)
        k_new = jnp.concatenate([new_ref[0, 2], pad_rows], axis=0)
        v_new = jnp.concatenate([new_ref[0, 3], pad_rows], axis=0)
        nkey = lax.broadcasted_iota(jnp.int32, (n_rows, NEW_PAD), 1)
        sn = lax.dot_general(q32, k_new.astype(BF16), nt, preferred_element_type=F32) + bn_ref[...]
        sn = jnp.where((nkey <= t_of_row) & (nkey < ts), sn, NEG)
        m_prev = m_ref[...]
        m_fin = jnp.maximum(m_prev, jnp.max(sn, axis=-1, keepdims=True))
        a_fin = jnp.exp(m_prev - m_fin)
        pn = jnp.exp(sn - m_fin)
        l_fin = a_fin * l_ref[...] + jnp.sum(pn, axis=-1, keepdims=True)
        o_slc = (a_fin * acc_ref[...] + _dot(pn, v_new)) / l_fin

        ri = lax.broadcasted_iota(jnp.int32, (n_rows, NEW_PAD), 0) % ts
        ti = lax.broadcasted_iota(jnp.int32, (n_rows, NEW_PAD), 1)
        gates = jnp.concatenate([_sigmoid(gate_ref[0]), pad_rows], axis=0)
        per_row = _dot_exact_lhs((ri == ti).astype(BF16), gates)
        head = lax.broadcasted_iota(jnp.int32, (n_rows, LANES), 0) // ts
        glane = lax.broadcasted_iota(jnp.int32, (n_rows, LANES), 1)
        gate = lambda br: jnp.sum(jnp.where(glane == br * C_HEADS + head, per_row, 0.0), axis=-1,
                                  keepdims=True)
        comb = gate(0) * ocmp_ref[0] + gate(1) * o_slc + gate(2) * owin_ref[0]
        low = (lax.broadcasted_iota(jnp.int32, (ts, LANES), 1) // C_HD) == 0
        for n_ in range(C_HPG):
            top = comb[n_ * ts:(n_ + 1) * ts]
            bot = comb[(C_HPG + n_) * ts:(C_HPG + n_ + 1) * ts]
            o_ref[0, :, n_ * LANES:(n_ + 1) * LANES] = jnp.where(low, top, bot)


def _slc_pages(page_table, cache_k, cache_v, layer, q32, sel_chunks, o_cmp, o_win, gates, new_rows,
               bias_last, bias_n):
    b, n_pages = page_table.shape
    n = min(PAGES_PER_STEP, n_pages)
    chunks = n_pages // n
    ts = new_rows.shape[2]
    n_rows = S_ROWS * ts
    kern = functools.partial(_slc_pages_kernel, n=n, ts=ts, last_chunk=chunks - 1)
    per_b = lambda a: pl.BlockSpec((1,) + a.shape[1:], lambda bi, c, pt: (bi,) + (0,) * (a.ndim - 1))
    full = lambda a: pl.BlockSpec(a.shape, lambda bi, c, pt: (0,) * a.ndim)
    return pl.pallas_call(
        kern,
        grid_spec=pltpu.PrefetchScalarGridSpec(
            num_scalar_prefetch=1, grid=(b, chunks),
            in_specs=_page_specs(layer, n) + _page_specs(layer, n) + [
                per_b(q32), pl.BlockSpec((1, 1, n_rows, LANES), lambda bi, c, pt: (bi, c, 0, 0)),
                per_b(o_cmp), per_b(o_win),
                pl.BlockSpec((1, ts, LANES), lambda bi, c, pt: (bi, 0, G_CG // LANES)),
                per_b(new_rows), full(bias_last), full(bias_n)],
            out_specs=pl.BlockSpec((1, ts, C_W), lambda bi, c, pt: (bi, 0, 0)),
            scratch_shapes=[pltpu.VMEM((n_rows, 1), F32), pltpu.VMEM((n_rows, 1), F32),
                            pltpu.VMEM((n_rows, LANES), F32)]),
        out_shape=jax.ShapeDtypeStruct((b, ts, C_W), F32),
        compiler_params=_cparams(("parallel", "arbitrary")),
        name="nsa_sample_slc_pages",
    )(page_table, *([cache_k] * n), *([cache_v] * n), q32, sel_chunks, o_cmp, o_win, gates, new_rows,
      bias_last, bias_n)


def _sample_bias_tables(rel_bias, past, ts, wbuf):
    tbl = rel_bias[_rel_bucket(jnp.arange(MAX_DIST)), :]
    tbl = (tbl - tbl[MAX_DIST - 1:MAX_DIST, :]).T
    t = jnp.tile(jnp.arange(ts), S_ROWS)[:, None]
    head = jnp.repeat(jnp.arange(S_ROWS), ts)[:, None]
    look = lambda dist: tbl[head, jnp.clip(dist, 0, MAX_DIST - 1)]
    blk = jnp.arange(BLK_LANES)[None, :]
    bias_c = look(past + t - CMP_BLOCK * blk - (CMP_BLOCK - 1))
    bias_w = look(wbuf + t - jnp.arange(wbuf)[None, :])
    bias_n = look(t - jnp.arange(NEW_PAD)[None, :])
    bias_last = look(PAGE_SIZE + t - jnp.arange(PAGE_SIZE)[None, :])
    return bias_c, bias_w, bias_n, bias_last


def _nsa_sample(main, gate, caches, win_k, win_v, page_table, layer, qg, kg, rel_bias):
    b, ts, _ = main.shape
    n_pages = page_table.shape[1]
    past = n_pages * PAGE_SIZE
    wbuf = win_k.shape[-1]
    n_rows = S_ROWS * ts
    bias_c, bias_w, bias_n, bias_last = _sample_bias_tables(rel_bias, past, ts, wbuf)
    kc_chunks, vc_chunks = _cmp_pages(page_table, caches[0], caches[1], layer)
    n_cached = past // CMP_BLOCK
    unchunk = lambda z: z.transpose(0, 2, 1, 3).reshape(b, LANES, n_cached)
    fit = lambda z: jnp.pad(z, ((0, 0), (0, 0), (0, BLK_LANES - n_cached)))
    q32, sel, o_cmp, o_win, new_rows = _nsa_sample_head(
        main, fit(unchunk(kc_chunks)), fit(unchunk(vc_chunks)), win_k, win_v, layer, qg, kg,
        (bias_c, bias_w, bias_n), past)
    per_step = 2 * min(PAGES_PER_STEP, n_pages)
    sel_chunks = sel[:, :, :n_cached].reshape(b, n_rows, n_cached // per_step, per_step)
    sel_chunks = jnp.pad(sel_chunks.transpose(0, 2, 1, 3), ((0, 0), (0, 0), (0, 0), (0, LANES - per_step)))
    o = _slc_pages(page_table, caches[2], caches[3], layer, q32, sel_chunks, o_cmp, o_win,
                   gate.reshape(b, ts, N_GATE), new_rows, bias_last, bias_n)
    return o.reshape(b * ts, C_W), new_rows


def _shift_order(z):
    part = lambda off, n: z[..., off:off + n]
    return jnp.concatenate([part(OFF_R, A_W), part(OFF_K, A_W), part(OFF_V, A_W), part(OFF_WL, A_LORA_W),
                            part(OFF_AL, A_LORA_A), part(OFF_GL, A_LORA_G)], axis=-1)


def _layer_params(l, w_in, shift_mu, w_br_a, w_br_b, w_br_c, w_out, w_mq, w_mk, w_mv, w_mo, w_ff1, w_ff2):
    w = w_in[l]
    cast = lambda a: a.astype(BF16)
    cols = lambda off, n: cast(w[:, off:off + n])
    w_q = cols(OFF_Q, C_W).reshape(D_MODEL, C_KV, C_HPG, C_HD).transpose(0, 2, 1, 3).reshape(D_MODEL, C_W)
    w_main = jnp.concatenate([
        cols(OFF_R, A_W), cols(OFF_K, A_W), cols(OFF_V, A_W), cols(OFF_POOL, B_W), w_q,
        cols(OFF_KV, 6 * C_KVW), cols(OFF_WL, A_LORA_W), cols(OFF_AL, A_LORA_A), cols(OFF_GL, A_LORA_G)],
        axis=1)
    w_gate = jnp.concatenate([cols(OFF_MG, 3 * D_MODEL), cols(OFF_CG, 3 * C_HEADS),
                              jnp.zeros((D_MODEL, N_GATE - G_CG - 3 * C_HEADS), BF16)], axis=1)
    wc = w_br_c[l].reshape(C_KV, C_HPG, C_HD, D_MODEL).transpose(1, 0, 2, 3).reshape(C_W, D_MODEL)
    return dict(
        w_main=w_main, w_gate=w_gate, mu=_shift_order(shift_mu[l].reshape(1, -1)),
        wa=cast(w_br_a[l]), wb=cast(w_br_b[l]), wc=cast(wc), wo=cast(w_out[l]),
        wq=cast(w_mq[l]), wk=cast(w_mk[l]), wv=cast(w_mv[l]), wmo=cast(w_mo[l]),
        w1=cast(w_ff1[l]), w2=cast(w_ff2[l]))


def _shift_row(main_row):
    lora = main_row[:, P_LORA:]
    return jnp.concatenate([
        main_row[:, P_R:P_R + A_W], lora[:, :A_LORA_W], main_row[:, P_K:P_K + A_W],
        main_row[:, P_V:P_V + A_W], lora[:, A_LORA_W:A_LORA_W + A_LORA_A],
        lora[:, A_LORA_W + A_LORA_A:]], axis=1)


def _feature_major(z):
    z = jnp.moveaxis(z, -3, -1)
    return z.reshape(z.shape[:-3] + (C_KVW, z.shape[-1]))


def _token_major(z):
    b, _, rows = z.shape
    return z.reshape(b, C_KV, C_HD, rows).transpose(0, 3, 1, 2)


def kernel(x_prompt, x_sample, state_rwkv, state_shift, state_pool, cache_cmp_k, cache_cmp_v, cache_slc_k, cache_slc_v, cache_win_k, cache_win_v, cache_mem_k, cache_mem_v, page_table, mem_prompt, rel_bias, norm_mix_g, w_in, shift_mu, rwkv_w0, rwkv_w2, rwkv_a0, rwkv_a2, rwkv_g2, rwkv_kk, rwkv_ka, rwkv_rk, rwkv_lnx_g, rwkv_lnx_b, pool_w, pool_scale, nsa_q_g, nsa_k_g, w_br_a, w_br_b, w_br_c, w_out, norm_memx_g, norm_mem_g, w_mq, w_mk, w_mv, mem_q_g, mem_k_g, w_mo, norm_ffn_g, w_ff1, w_ff2):
    bp, t = x_prompt.shape[:2]
    bs, ts = x_sample.shape[:2]
    past = page_table.shape[1] * PAGE_SIZE
    wbuf_p = min(WINDOW, t)
    ts_pad = 16
    bnear, bcmp = _bias_tables(rel_bias)
    caches = [_feature_major(z) for z in (cache_cmp_k, cache_cmp_v, cache_slc_k, cache_slc_v)]
    win_k, win_v = _feature_major(cache_win_k), _feature_major(cache_win_v)
    xp = x_prompt.reshape(bp * t, D_MODEL)
    xs = x_sample.reshape(bs * ts, D_MODEL)
    outs_p = [[] for _ in range(11)]
    outs_s = [[] for _ in range(9)]
    row = lambda a: a.reshape(1, -1)
    for l in range(DEPTH):
        prm = _layer_params(l, w_in, shift_mu, w_br_a, w_br_b, w_br_c, w_out, w_mq, w_mk, w_mv, w_mo,
                            w_ff1, w_ff2)
        rw = (prm["mu"], row(rwkv_w0[l]), rwkv_w2[l].astype(BF16), row(rwkv_a0[l]), rwkv_a2[l].astype(BF16),
              rwkv_g2[l].astype(BF16), row(rwkv_kk[l]), row(rwkv_ka[l]), row(rwkv_rk[l]),
              row(rwkv_lnx_g[l]), row(rwkv_lnx_b[l]))
        pool_wl = pool_w[l].astype(BF16)

        main = _norm_matmul(xp, norm_mix_g[l], prm["w_main"], 1024, 512).reshape(bp, t, N_MAIN)
        gate = _norm_matmul(xp, norm_mix_g[l], prm["w_gate"], 1024, 640)
        o_a, s_pairs = _rwkv(main, jnp.zeros((bp, 1, A_COLS), F32), jnp.zeros((bp, 4, 128, 128), F32), rw)
        o_b = _pool(main, jnp.zeros((bp, POOL_HALO, B_W), F32), pool_wl, pool_scale[l], 0, 512)
        qn, ks16, vs16, kw16, vw16, kc, vc, kv_rows = _nsa_prep(main, nsa_q_g[l], nsa_k_g[l], 512)
        front = lambda z: jnp.pad(z, ((0, 0), (WINDOW, 0), (0, 0)))
        front_t = lambda z: jnp.pad(z, ((0, 0), (0, 0), (WINDOW, 0)))
        o_c = _nsa_attn(qn, front(ks16), front_t(vs16), front(kw16), front_t(vw16), kc, vc,
                        gate.reshape(bp, t, N_GATE), bnear, bcmp)
        xp = _merge(xp, o_a.reshape(bp * t, A_W), o_b.reshape(bp * t, B_W), o_c.reshape(bp * t, C_W), gate,
                    prm["wa"], prm["wb"], prm["wc"], prm["wo"], 512)
        mk, mv = _mem_kv(mem_prompt.reshape(bp * N_MEM, D_MODEL), norm_mem_g[l], prm["wk"], prm["wv"],
                         mem_k_g[l])
        mk, mv = mk.reshape(bp, N_MEM, M_W), mv.reshape(bp, N_MEM, M_W)
        xp = _mem_ffn(xp, mk, mv, norm_memx_g[l], prm["wq"], mem_q_g[l], prm["wmo"], norm_ffn_g[l],
                      prm["w1"], prm["w2"], 512, t)
        for dst, val in zip(outs_p, (
                _unpair_states(s_pairs), _shift_row(main[:, -1]), main[:, -POOL_BUF:, P_POOL:P_POOL + B_W],
                _token_major(kv_rows[:, 0]), _token_major(kv_rows[:, 1]), _token_major(kv_rows[:, 2]),
                _token_major(kv_rows[:, 3]), _token_major(kv_rows[:, 4, :, -wbuf_p:]),
                _token_major(kv_rows[:, 5, :, -wbuf_p:]),
                mk.reshape(bp, N_MEM, M_HEADS, M_HD), mv.reshape(bp, N_MEM, M_HEADS, M_HD))):
            dst.append(val)

        main = _norm_matmul(xs, norm_mix_g[l], prm["w_main"], bs * ts, 512).reshape(bs, ts, N_MAIN)
        gate = _norm_matmul(xs, norm_mix_g[l], prm["w_gate"], bs * ts, 640)
        main_pad = jnp.pad(main, ((0, 0), (0, ts_pad - ts), (0, 0)))
        ulast = _shift_order(state_shift[l]).reshape(bs, 1, A_COLS)
        o_a, s_pairs = _rwkv(main_pad, ulast, _pair_states(state_rwkv[l]), rw, t_valid=ts)
        o_a = o_a[:, :ts]
        hist = jnp.concatenate([jnp.zeros((bs, POOL_HALO - POOL_BUF, B_W), F32), state_pool[l]], axis=1)
        o_b = _pool(main_pad, hist, pool_wl, pool_scale[l], past, ts_pad)[:, :ts]
        u_pool = main[:, :, P_POOL:P_POOL + B_W]
        o_c, new_rows = _nsa_sample(main, gate, caches, win_k, win_v, page_table, l, nsa_q_g[l], nsa_k_g[l],
                                    rel_bias)
        new_kv = [new_rows[:, n].reshape(bs, ts, C_KV, C_HD) for n in range(6)]
        slide = lambda buf, n: _token_major(jnp.concatenate(
            [buf[l][:, :, ts:], new_rows[:, n].transpose(0, 2, 1)], axis=2))
        xs = _merge(xs, o_a.reshape(bs * ts, A_W), o_b.reshape(bs * ts, B_W), o_c, gate,
                    prm["wa"], prm["wb"], prm["wc"], prm["wo"], bs * ts)
        xs = _mem_ffn(xs, cache_mem_k[l].reshape(bs, N_MEM, M_W), cache_mem_v[l].reshape(bs, N_MEM, M_W),
                      norm_memx_g[l], prm["wq"], mem_q_g[l], prm["wmo"], norm_ffn_g[l],
                      prm["w1"], prm["w2"], 32, ts)
        for dst, val in zip(outs_s, (
                _unpair_states(s_pairs), _shift_row(main[:, -1]),
                jnp.concatenate([state_pool[l], u_pool], axis=1)[:, -POOL_BUF:],
                new_kv[0], new_kv[1], new_kv[2], new_kv[3], slide(win_k, 4), slide(win_v, 5))):
            dst.append(val)

    return ((xp.reshape(bp, t, D_MODEL), xs.reshape(bs, ts, D_MODEL))
            + tuple(jnp.stack(o) for o in outs_p) + tuple(jnp.stack(o) for o in outs_s))
```

```python
import functools
import math

import jax
import jax.numpy as jnp
from jax import lax
from jax.experimental import pallas as pl
from jax.experimental.pallas import tpu as pltpu

F32 = jnp.float32
BF16 = jnp.bfloat16

D_MODEL = 1024
DEPTH = 2
PAGE_SIZE = 128
A_HEADS = 8
A_HD = 64
A_W = A_HEADS * A_HD
A_LORA_W = 64
A_LORA_A = 64
A_LORA_G = 128
A_LORA = A_LORA_W + A_LORA_A + A_LORA_G
LNX_EPS = 64e-5
L2_EPS = 1e-24
B_GROUPS = 4
B_GW = 128
B_W = B_GROUPS * B_GW
POOL_WINDOWS = (2, 4, 8, 16)
POOL_BUF = 15
POOL_HALO = 16
C_HEADS = 8
C_HD = 64
C_W = C_HEADS * C_HD
C_KV = 2
C_HPG = C_HEADS // C_KV
C_KVW = C_KV * C_HD
CMP_BLOCK = 64
N_SEL = 16
WINDOW = 512
Q_TILE = 128
FORCE_SCORE = 1e4
NEG = -1e30
N_BUCKETS = 32
MAX_DIST = 128
N_MEM = 256
M_HEADS = 4
M_HD = 128
M_W = M_HEADS * M_HD
D_FF = 4 * D_MODEL
RMS_EPS = 1e-6

OFF_R = 0
OFF_WL = OFF_R + A_W
OFF_K = OFF_WL + A_LORA_W
OFF_V = OFF_K + A_W
OFF_AL = OFF_V + A_W
OFF_GL = OFF_AL + A_LORA_A
A_COLS = OFF_GL + A_LORA_G
OFF_POOL = A_COLS
OFF_Q = OFF_POOL + B_W
OFF_KV = OFF_Q + C_W
OFF_CG = OFF_KV + 6 * C_KVW
OFF_MG = OFF_CG + 3 * C_HEADS
N_IN = OFF_MG + 3 * D_MODEL

P_R = 0
P_K = P_R + A_W
P_V = P_K + A_W
P_POOL = P_V + A_W
P_Q = P_POOL + B_W
P_KV = P_Q + C_W
P_LORA = P_KV + 6 * C_KVW
N_MAIN = P_LORA + A_LORA
G_MG = 0
G_CG = 3 * D_MODEL
N_GATE = G_CG + 128

RWKV_CHUNK = 64
RWKV_SUB = 16
LANES = 128
VMEM_LIMIT = 56 * 1024 * 1024


def _cparams(sem):
    return pltpu.CompilerParams(dimension_semantics=sem, vmem_limit_bytes=VMEM_LIMIT)


def _dot(a, b, dims=(((1,), (0,)), ((), ()))):
    return lax.dot_general(a.astype(BF16), b.astype(BF16), dims, preferred_element_type=F32)


def _dot_nt(a, b):
    return _dot(a, b, (((1,), (1,)), ((), ())))


def _dot_tn(a, b):
    return _dot(a, b, (((0,), (0,)), ((), ())))


def _split3(x):
    hi = x.astype(BF16)
    r1 = x - hi.astype(F32)
    mid = r1.astype(BF16)
    lo = (r1 - mid.astype(F32)).astype(BF16)
    return hi, mid, lo


def _dot_exact_rhs(a, b01):
    hi, mid, lo = _split3(a)
    dims = (((1,), (0,)), ((), ()))
    dg = lambda p: lax.dot_general(p, b01, dims, preferred_element_type=F32)
    return dg(hi) + dg(mid) + dg(lo)


def _dot_exact_lhs(a01, b):
    hi, mid, lo = _split3(b)
    dims = (((1,), (0,)), ((), ()))
    dg = lambda p: lax.dot_general(a01, p, dims, preferred_element_type=F32)
    return dg(hi) + dg(mid) + dg(lo)


def _sigmoid(x):
    return 1.0 / (1.0 + jnp.exp(-x))


def _rms(x, g):
    return x * lax.rsqrt(jnp.mean(x * x, axis=-1, keepdims=True) + RMS_EPS) * g


def _norm_matmul_kernel(x_ref, g_ref, w_ref, o_ref, xn_ref):
    @pl.when(pl.program_id(1) == 0)
    def _():
        xn_ref[...] = _rms(x_ref[...], g_ref[...]).astype(BF16)

    o_ref[...] = jnp.dot(xn_ref[...], w_ref[...], preferred_element_type=F32)


def _norm_matmul(x, g, w, tm, tn):
    m, k = x.shape
    n = w.shape[1]
    return pl.pallas_call(
        _norm_matmul_kernel,
        grid=(m // tm, n // tn),
        in_specs=[pl.BlockSpec((tm, k), lambda i, j: (i, 0)),
                  pl.BlockSpec((1, k), lambda i, j: (0, 0)),
                  pl.BlockSpec((k, tn), lambda i, j: (0, j))],
        out_specs=pl.BlockSpec((tm, tn), lambda i, j: (i, j)),
        out_shape=jax.ShapeDtypeStruct((m, n), F32),
        scratch_shapes=[pltpu.VMEM((tm, k), BF16)],
        compiler_params=_cparams(("parallel", "arbitrary")),
        name="norm_matmul",
    )(x, g.reshape(1, k), w)


def _rwkv_kernel(r_ref, k_ref, v_ref, lora_ref, ulast_ref, s0_ref, mu_ref, w0_ref, w2_ref, a0_ref,
                 a2_ref, g2_ref, kk_ref, ka_ref, rk_ref, lng_ref, lnb_ref,
                 o_ref, sout_ref, s_ref, prev_ref, ar_ref, bk_ref, vb_ref, y_ref, *, chunk, bt, t_valid):
    c = pl.program_id(1)
    nc = pl.num_programs(1)
    C = chunk
    R = bt * C
    NP = A_HEADS // 2
    PB = 2 * C

    @pl.when(c == 0)
    def _():
        s_ref[...] = s0_ref[...].reshape(bt * NP, 2 * A_HD, 2 * A_HD)
        prev_ref[...] = ulast_ref[:, 0, :]

    row = lax.broadcasted_iota(jnp.int32, (R, 1), 0)

    def shift_mix(u3, lo, hi):
        u = u3.reshape(R, hi - lo)
        u_prev = pltpu.roll(u, 1, 0)
        for b in range(bt):
            u_prev = jnp.where(row == b * C, prev_ref[b:b + 1, lo:hi], u_prev)
        for b in range(bt):
            prev_ref[b:b + 1, lo:hi] = u[(b + 1) * C - 1:(b + 1) * C, :]
        return u + (u_prev - u) * mu_ref[:, lo:hi]

    r = shift_mix(r_ref[...], 0, A_W)
    k = shift_mix(k_ref[...], A_W, 2 * A_W)
    v = shift_mix(v_ref[...], 2 * A_W, 3 * A_W)
    lora = shift_mix(lora_ref[...], 3 * A_W, 3 * A_W + A_LORA)
    wl = lora[:, 0:A_LORA_W]
    al = lora[:, A_LORA_W:A_LORA_W + A_LORA_A]
    gl = lora[:, A_LORA_W + A_LORA_A:A_LORA]

    z = -(w0_ref[...] + _dot(jnp.tanh(wl), w2_ref[...]))
    softplus = jnp.maximum(z, 0.0) + jnp.log(1.0 + jnp.exp(-jnp.abs(z)))
    w = -softplus - 0.5
    a = _sigmoid(a0_ref[...] + _dot(al, a2_ref[...]))
    g = _dot(_sigmoid(gl), g2_ref[...])

    lane = lax.broadcasted_iota(jnp.int32, (A_W, A_W), 1) // A_HD
    sub = lax.broadcasted_iota(jnp.int32, (A_W, A_W), 0) // A_HD
    head_ones = (lane == sub).astype(BF16)

    kkv = k * kk_ref[...]
    kkn = kkv * lax.rsqrt(jnp.maximum(_dot_exact_rhs(kkv * kkv, head_ones), L2_EPS))
    k2 = k * (1.0 + (a - 1.0) * ka_ref[...])
    log_d = -jnp.exp(w)
    t_in = row % C
    if t_valid is not None:
        live = (c * C + t_in) < t_valid
        log_d = jnp.where(live, log_d, 0.0)
        kkn = jnp.where(live, kkn, 0.0)
        k2 = jnp.where(live, k2, 0.0)

    ri = lax.broadcasted_iota(jnp.int32, (R, R), 0)
    ci = lax.broadcasted_iota(jnp.int32, (R, R), 1)
    cum_mask = ((ri // C) == (ci // C)) & (ci <= ri)
    cum = _dot_exact_lhs(cum_mask.astype(BF16), log_d)
    c_incl = jnp.exp(cum)
    c_inv = jnp.exp(-cum)
    a_t = -kkn * jnp.exp(cum - log_d)
    r_t = r * c_incl
    b_t = kkn * a * c_inv
    k_t = k2 * c_inv

    lane = lax.broadcasted_iota(jnp.int32, (1, 2 * A_HD), 1) // A_HD
    def stage(ref, off, x):
        for b in range(bt):
            for p in range(NP):
                blk = x[b * C:(b + 1) * C, 2 * A_HD * p:2 * A_HD * (p + 1)]
                for hh in range(2):
                    ref[b * NP + p, off + hh * C:off + (hh + 1) * C, :] = (
                        jnp.where(lane == hh, blk, 0.0).astype(BF16))

    stage(ar_ref, 0, a_t)
    stage(ar_ref, PB, r_t)
    stage(bk_ref, 0, b_t)
    stage(bk_ref, PB, k_t)
    stage(vb_ref, 0, v)
    ar = ar_ref[...]
    bk = bk_ref[...]
    vb = vb_ref[...]

    nn = (((2,), (1,)), ((0,), (0,)))
    nt = (((2,), (2,)), ((0,), (0,)))
    tn = (((1,), (1,)), ((0,), (0,)))
    bdot = lambda x, y, dims=nn: lax.dot_general(x.astype(BF16), y.astype(BF16), dims,
                                                 preferred_element_type=F32)
    bi = lax.broadcasted_iota(jnp.int32, (PB, PB), 0)
    bj = lax.broadcasted_iota(jnp.int32, (PB, PB), 1)
    same_head = (bi // C) == (bj // C)
    strict = same_head & ((bj % C) < (bi % C))
    lower = same_head & ((bj % C) <= (bi % C))
    sub_blk = (bi // RWKV_SUB) == (bj // RWKV_SUB)
    eye = (bi == bj).astype(F32)

    gram = bdot(ar, bk, nt)
    l_b = jnp.where(strict, gram[:, :PB, :PB], 0.0)
    l_k = jnp.where(strict, gram[:, :PB, PB:], 0.0)
    m_b = jnp.where(lower, gram[:, PB:, :PB], 0.0)
    m_k = jnp.where(lower, gram[:, PB:, PB:], 0.0)
    dg = jnp.where(sub_blk, l_b, 0.0)
    off = l_b - dg
    t_inv = eye + dg
    pw = dg
    for _ in range(int(math.log2(RWKV_SUB)) - 1):
        pw = bdot(pw, pw)
        t_inv = t_inv + bdot(t_inv, pw)
    n1 = bdot(t_inv, off)
    n2 = bdot(n1, n1)
    full = eye + n1 + n2 + bdot(n1, n2)
    t_full = bdot(full, t_inv)

    s_old = s_ref[...]
    w0 = bdot(ar, s_old, nt)
    u = bdot(t_full, w0[:, :PB] + bdot(l_k, vb))
    uv = jnp.concatenate([u, vb.astype(F32)], axis=1)
    yb = w0[:, PB:] + bdot(jnp.concatenate([m_b, m_k], axis=2), uv)
    s_new = s_old + bdot(uv, bk, tn)
    for b in range(bt):
        for p in range(NP):
            sl = slice(2 * A_HD * p, 2 * A_HD * (p + 1))
            idx = b * NP + p
            s_ref[idx] = s_new[idx] * c_incl[(b + 1) * C - 1:(b + 1) * C, sl]
            y_ref[b * C:(b + 1) * C, sl] = yb[idx, :C] + yb[idx, C:]
    y = y_ref[...]

    inv_n = 1.0 / A_HD
    mean = _dot_exact_rhs(y, head_ones) * inv_n
    yc = y - mean
    var = _dot_exact_rhs(yc * yc, head_ones) * inv_n
    yn = yc * lax.rsqrt(var + LNX_EPS) * lng_ref[...] + lnb_ref[...]
    bonus = _dot_exact_rhs(r * k2 * rk_ref[...], head_ones) * v
    o_ref[...] = ((yn + bonus) * g).reshape(bt, C, A_W)

    @pl.when(c == nc - 1)
    def _():
        sout_ref[...] = s_ref[...].reshape(bt, NP, 2 * A_HD, 2 * A_HD)


RWKV_BATCH_TILE = 4


def _rwkv(proj, ulast, s0, prm, t_valid=None):
    b, t, _ = proj.shape
    C = min(RWKV_CHUNK, t)
    bt = min(RWKV_BATCH_TILE, b)
    nc = t // C
    vec = lambda n: pl.BlockSpec((1, n), lambda i, c: (0, 0))
    mat = lambda m, n: pl.BlockSpec((m, n), lambda i, c: (0, 0))
    col = lambda width, off: pl.BlockSpec((bt, C, width), lambda i, c: (i, c, off // width))
    n_shift = 3 * A_W + A_LORA
    n_pairs = bt * (A_HEADS // 2)
    kern = functools.partial(_rwkv_kernel, chunk=C, bt=bt, t_valid=t_valid)
    return pl.pallas_call(
        kern,
        grid=(b // bt, nc),
        in_specs=[col(A_W, P_R), col(A_W, P_K), col(A_W, P_V), col(A_LORA, P_LORA),
                  pl.BlockSpec((bt, 1, n_shift), lambda i, c: (i, 0, 0)),
                  pl.BlockSpec((bt, 4, 128, 128), lambda i, c: (i, 0, 0, 0)),
                  vec(n_shift), vec(A_W), mat(A_LORA_W, A_W), vec(A_W), mat(A_LORA_A, A_W),
                  mat(A_LORA_G, A_W), vec(A_W), vec(A_W), vec(A_W), vec(A_W), vec(A_W)],
        out_specs=[pl.BlockSpec((bt, C, A_W), lambda i, c: (i, c, 0)),
                   pl.BlockSpec((bt, 4, 128, 128), lambda i, c: (i, 0, 0, 0))],
        out_shape=[jax.ShapeDtypeStruct((b, t, A_W), F32),
                   jax.ShapeDtypeStruct((b, 4, 128, 128), F32)],
        scratch_shapes=[pltpu.VMEM((n_pairs, 128, 128), F32), pltpu.VMEM((bt, n_shift), F32),
                        pltpu.VMEM((n_pairs, 4 * C, 128), BF16), pltpu.VMEM((n_pairs, 4 * C, 128), BF16),
                        pltpu.VMEM((n_pairs, 2 * C, 128), BF16), pltpu.VMEM((bt * C, A_W), F32)],
        compiler_params=_cparams(("parallel", "arbitrary")),
        name="rwkv_chunk",
    )(proj, proj, proj, proj, ulast, s0, *prm)


def _pair_states(s):
    b = s.shape[0]
    s = s.reshape(b, 4, 2, A_HD, A_HD)
    z = jnp.zeros_like(s[:, :, 0])
    top = jnp.concatenate([s[:, :, 0], z], axis=-1)
    bot = jnp.concatenate([z, s[:, :, 1]], axis=-1)
    return jnp.concatenate([top, bot], axis=-2)


def _unpair_states(sp):
    b = sp.shape[0]
    h0 = sp[:, :, :A_HD, :A_HD]
    h1 = sp[:, :, A_HD:, A_HD:]
    return jnp.stack([h0, h1], axis=2).reshape(b, A_HEADS, A_HD, A_HD)


def _pool_kernel(u_ref, halo_ref, hist_ref, pm_ref, ps_ref, o_ref, ext_ref, *, tm, pos0):
    i = pl.program_id(1)

    @pl.when(i == 0)
    def _():
        ext_ref[0:POOL_HALO, :] = hist_ref[0]

    @pl.when(i > 0)
    def _():
        ext_ref[0:POOL_HALO, :] = halo_ref[0]

    cur = u_ref[0]
    ext_ref[POOL_HALO:POOL_HALO + tm, :] = cur
    pos = pos0 + i * tm + lax.broadcasted_iota(jnp.int32, (tm, 1), 0)
    outs = []
    for gi, win in enumerate(POOL_WINDOWS):
        lo, hi = gi * B_GW, (gi + 1) * B_GW
        s = cur[:, lo:hi]
        for back in range(1, win):
            s = s + ext_ref[POOL_HALO - back:POOL_HALO - back + tm, lo:hi]
        cnt = jnp.minimum(win, pos + 1).astype(F32)
        d = s / cnt - cur[:, lo:hi]
        outs.append(_dot(d, pm_ref[gi]))
    o_ref[0] = jnp.concatenate(outs, axis=1) * ps_ref[...]


def _pool(proj, hist, pm, ps, pos0, tm):
    b, t, _ = proj.shape
    per = tm // POOL_HALO
    kern = functools.partial(_pool_kernel, tm=tm, pos0=pos0)
    return pl.pallas_call(
        kern,
        grid=(b, t // tm),
        in_specs=[pl.BlockSpec((1, tm, B_W), lambda bi, i: (bi, i, P_POOL // B_W)),
                  pl.BlockSpec((1, POOL_HALO, B_W),
                               lambda bi, i: (bi, jnp.maximum(i * per - 1, 0), P_POOL // B_W)),
                  pl.BlockSpec((1, POOL_HALO, B_W), lambda bi, i: (bi, 0, 0)),
                  pl.BlockSpec((B_GROUPS, B_GW, B_GW), lambda bi, i: (0, 0, 0)),
                  pl.BlockSpec((1, B_W), lambda bi, i: (0, 0))],
        out_specs=pl.BlockSpec((1, tm, B_W), lambda bi, i: (bi, i, 0)),
        out_shape=jax.ShapeDtypeStruct((b, t, B_W), F32),
        scratch_shapes=[pltpu.VMEM((tm + POOL_HALO, B_W), F32)],
        compiler_params=_cparams(("parallel", "arbitrary")),
        name="pool_mix",
    )(proj, proj, hist, pm, ps.reshape(1, B_W))


def _merge_kernel(x_ref, oa_ref, ob_ref, oc_ref, mg_ref, wa_ref, wb_ref, wc_ref, wo_ref, o_ref):
    gate = lambda n: _sigmoid(mg_ref[:, n * D_MODEL:(n + 1) * D_MODEL])
    h = (gate(0) * _dot(oa_ref[...], wa_ref[...]) + gate(1) * _dot(ob_ref[...], wb_ref[...])
         + gate(2) * _dot(oc_ref[...], wc_ref[...]))
    o_ref[...] = x_ref[...] + _dot(h, wo_ref[...])


def _merge(x, oa, ob, oc, gates, wa, wb, wc, wo, tm):
    m = x.shape[0]
    row = lambda n: pl.BlockSpec((tm, n), lambda i: (i, 0))
    full = lambda a: pl.BlockSpec(a.shape, lambda i: (0, 0))
    return pl.pallas_call(
        _merge_kernel,
        grid=(m // tm,),
        in_specs=[row(D_MODEL), row(A_W), row(B_W), row(C_W), row(3 * D_MODEL),
                  full(wa), full(wb), full(wc), full(wo)],
        out_specs=row(D_MODEL),
        out_shape=jax.ShapeDtypeStruct((m, D_MODEL), F32),
        compiler_params=_cparams(("parallel",)),
        name="merge_branches",
    )(x, oa, ob, oc, gates, wa, wb, wc, wo)


def _mem_kv_kernel(mem_ref, g_ref, wk_ref, wv_ref, kg_ref, k_ref, v_ref):
    mn = _rms(mem_ref[...], g_ref[...]).astype(BF16)
    k = jnp.dot(mn, wk_ref[...], preferred_element_type=F32)
    v_ref[...] = jnp.dot(mn, wv_ref[...], preferred_element_type=F32)
    k_ref[...] = jnp.concatenate(
        [_rms(k[:, h * M_HD:(h + 1) * M_HD], kg_ref[...]) for h in range(M_HEADS)], axis=1)


def _mem_kv(mem, g, wk, wv, kg):
    m = mem.shape[0]
    tm = N_MEM
    row = lambda n: pl.BlockSpec((tm, n), lambda i: (i, 0))
    full = lambda a: pl.BlockSpec(a.shape, lambda i: (0, 0))
    g2, kg2 = g.reshape(1, D_MODEL), kg.reshape(1, M_HD)
    return pl.pallas_call(
        _mem_kv_kernel,
        grid=(m // tm,),
        in_specs=[row(D_MODEL), full(g2), full(wk), full(wv), full(kg2)],
        out_specs=[row(M_W), row(M_W)],
        out_shape=[jax.ShapeDtypeStruct((m, M_W), F32), jax.ShapeDtypeStruct((m, M_W), F32)],
        compiler_params=_cparams(("parallel",)),
        name="mem_kv",
    )(mem, g2, wk, wv, kg2)


def _mem_ffn_kernel(x_ref, mk_ref, mv_ref, gx_ref, wq_ref, qg_ref, wo_ref, gf_ref, w1_ref, w2_ref,
                    o_ref, xn_ref, *, tm, rows_per_batch, nkb):
    j = pl.program_id(1)

    @pl.when(j == 0)
    def _():
        x = x_ref[...]
        q = _dot(_rms(x, gx_ref[...]), wq_ref[...])
        mk = mk_ref[...].reshape(nkb * N_MEM, M_W)
        mv = mv_ref[...].reshape(nkb * N_MEM, M_W)
        if nkb > 1:
            qb = lax.broadcasted_iota(jnp.int32, (tm, nkb * N_MEM), 0) // rows_per_batch
            kb = lax.broadcasted_iota(jnp.int32, (tm, nkb * N_MEM), 1) // N_MEM
            same = qb == kb
        outs = []
        for h in range(M_HEADS):
            sl = slice(h * M_HD, (h + 1) * M_HD)
            qh = _rms(q[:, sl], qg_ref[...])
            logits = _dot_nt(qh, mk[:, sl]) * (M_HD ** -0.5)
            if nkb > 1:
                logits = jnp.where(same, logits, NEG)
            mx = jnp.max(logits, axis=-1, keepdims=True)
            p = jnp.exp(logits - mx)
            p = p / jnp.sum(p, axis=-1, keepdims=True)
            outs.append(_dot(p, mv[:, sl]))
        o = jnp.concatenate(outs, axis=1)
        xm = x + _dot(o, wo_ref[...])
        o_ref[...] = xm
        xn_ref[...] = _rms(xm, gf_ref[...]).astype(BF16)

    h1 = jnp.dot(xn_ref[...], w1_ref[...], preferred_element_type=F32)
    h1 = jnp.square(jnp.maximum(h1, 0.0))
    o_ref[...] += _dot(h1, w2_ref[...])


def _mem_ffn(x, mk, mv, gx, wq, qg, wo, gf, w1, w2, tm, rows_per_batch, tf=1024):
    m = x.shape[0]
    nkb = max(tm // rows_per_batch, 1)
    kern = functools.partial(_mem_ffn_kernel, tm=tm, rows_per_batch=rows_per_batch, nkb=nkb)
    full = lambda a: pl.BlockSpec(a.shape, lambda i, j: (0,) * a.ndim)
    gx2, qg2, gf2 = gx.reshape(1, D_MODEL), qg.reshape(1, M_HD), gf.reshape(1, D_MODEL)
    mem_spec = pl.BlockSpec((nkb, N_MEM, M_W), lambda i, j: ((i * tm) // (rows_per_batch * nkb), 0, 0))
    return pl.pallas_call(
        kern,
        grid=(m // tm, D_FF // tf),
        in_specs=[pl.BlockSpec((tm, D_MODEL), lambda i, j: (i, 0)), mem_spec, mem_spec,
                  full(gx2), full(wq), full(qg2), full(wo), full(gf2),
                  pl.BlockSpec((D_MODEL, tf), lambda i, j: (0, j)),
                  pl.BlockSpec((tf, D_MODEL), lambda i, j: (j, 0))],
        out_specs=pl.BlockSpec((tm, D_MODEL), lambda i, j: (i, 0)),
        out_shape=jax.ShapeDtypeStruct((m, D_MODEL), F32),
        scratch_shapes=[pltpu.VMEM((tm, D_MODEL), BF16)],
        compiler_params=_cparams(("parallel", "arbitrary")),
        name="mem_ffn",
    )(x, mk, mv, gx2, wq, qg2, wo, gf2, w1, w2)


def _half_ones():
    i = lax.broadcasted_iota(jnp.int32, (LANES, LANES), 0) // C_HD
    j = lax.broadcasted_iota(jnp.int32, (LANES, LANES), 1) // C_HD
    return (i == j).astype(BF16)


def _rms_halves(x, g, ones):
    ms = _dot_exact_rhs(x * x, ones) * (1.0 / C_HD)
    return x * lax.rsqrt(ms + RMS_EPS) * g


def _nsa_prep_kernel(q_ref, c_ref, s_ref, w_ref, qg_ref, kg_ref,
                     qn_ref, ks16_ref, vs16_ref, kw16_ref, vw16_ref, kc_ref, vc_ref, rows_ref, *, tm):
    ones = _half_ones()
    scale = C_HD ** -0.5 * LOG2E
    q = q_ref[0]
    for n in range(C_HPG):
        qn_ref[0, n] = (_rms_halves(q[:, n * LANES:(n + 1) * LANES], qg_ref[...], ones) * scale).T.astype(BF16)
    c = c_ref[0]
    rows_ref[0, 0] = c[:, :LANES].T
    rows_ref[0, 1] = c[:, LANES:].T
    s = s_ref[0]
    ks = _rms_halves(s[:, :LANES], kg_ref[1:2, :], ones)
    vs_t = s[:, LANES:].T
    ks16_ref[0] = ks.astype(BF16)
    vs16_ref[0] = vs_t.astype(BF16)
    rows_ref[0, 2] = ks.T
    rows_ref[0, 3] = vs_t
    w = w_ref[0]
    kw = _rms_halves(w[:, :LANES], kg_ref[2:3, :], ones)
    vw_t = w[:, LANES:].T
    kw16_ref[0] = kw.astype(BF16)
    vw16_ref[0] = vw_t.astype(BF16)
    rows_ref[0, 4] = kw.T
    rows_ref[0, 5] = vw_t
    nb = tm // CMP_BLOCK
    bi = lax.broadcasted_iota(jnp.int32, (nb, tm), 0)
    ti = lax.broadcasted_iota(jnp.int32, (nb, tm), 1) // CMP_BLOCK
    pool = (bi == ti).astype(BF16)
    means = _dot_exact_lhs(pool, c) * (1.0 / CMP_BLOCK)
    kc_ref[0] = _rms_halves(means[:, :LANES], kg_ref[0:1, :], ones)
    vc_ref[0] = means[:, LANES:]


def _nsa_prep(proj, qg, kg, tm):
    b, t, _ = proj.shape
    nbt = tm // CMP_BLOCK
    kern = functools.partial(_nsa_prep_kernel, tm=tm)
    tok = lambda n, dt: jax.ShapeDtypeStruct((b, t, n), dt)
    blk = jax.ShapeDtypeStruct((b, t // CMP_BLOCK, LANES), F32)
    col = lambda width, off: pl.BlockSpec((1, tm, width), lambda bi, i: (bi, i, off // width))
    out_tok = lambda n: pl.BlockSpec((1, tm, n), lambda bi, i: (bi, i, 0))
    out_blk = pl.BlockSpec((1, nbt, LANES), lambda bi, i: (bi, i, 0))
    out_t = pl.BlockSpec((1, LANES, tm), lambda bi, i: (bi, 0, i))
    tok_t = jax.ShapeDtypeStruct((b, LANES, t), BF16)
    return pl.pallas_call(
        kern,
        grid=(b, t // tm),
        in_specs=[col(C_W, P_Q), col(2 * LANES, P_KV), col(2 * LANES, P_KV + 2 * LANES),
                  col(2 * LANES, P_KV + 4 * LANES),
                  pl.BlockSpec((1, LANES), lambda bi, i: (0, 0)),
                  pl.BlockSpec((3, LANES), lambda bi, i: (0, 0))],
        out_specs=[pl.BlockSpec((1, C_HPG, LANES, tm), lambda bi, i: (bi, 0, 0, i)),
                   out_tok(LANES), out_t, out_tok(LANES), out_t, out_blk, out_blk,
                   pl.BlockSpec((1, 6, LANES, tm), lambda bi, i: (bi, 0, 0, i))],
        out_shape=[jax.ShapeDtypeStruct((b, C_HPG, LANES, t), BF16),
                   tok(LANES, BF16), tok_t, tok(LANES, BF16), tok_t, blk, blk,
                   jax.ShapeDtypeStruct((b, 6, LANES, t), F32)],
        compiler_params=_cparams(("parallel", "parallel")),
        name="nsa_prep",
    )(proj, proj, proj, proj, jnp.tile(qg.reshape(1, C_HD), (1, 2)), jnp.tile(kg, (1, 2)))


M_INIT = -1e29
ROWS = C_HPG * Q_TILE
FAR_TILES = 8
FRONT_TILES = max(WINDOW // Q_TILE, FAR_TILES)
SEL_PAD = 2 * FRONT_TILES
SUM_ROWS = 16
LOG2E = 1.4426950408889634


def _heads(x):
    return jnp.concatenate([x] * C_HPG, axis=1)


def _softmax_step(qts, k, v_t, state, tail_bias=None, valid=None, block_bias=None):
    m_ref, l_ref, acc_ref = state
    n_keys = k.shape[0]
    old = [(m_ref[g], l_ref[g], acc_ref[g]) for g in range(C_KV)]
    v_ext = jnp.concatenate([v_t, jnp.ones((SUM_ROWS, n_keys), BF16)], axis=0)
    if block_bias is not None:
        blk = lax.broadcasted_iota(jnp.int32, (n_keys, LANES), 0) // CMP_BLOCK
        col = lax.broadcasted_iota(jnp.int32, (n_keys, LANES), 1)
        k = jnp.concatenate([k, (blk == col).astype(BF16)], axis=1)
    new = []
    for g in range(C_KV):
        m_old, l_old, acc_old = old[g]
        rhs = qts[g] if block_bias is None else jnp.concatenate([qts[g], block_bias[g]], axis=0)
        s = jnp.dot(k, rhs, preferred_element_type=F32)
        if tail_bias is not None:
            cut = s.shape[0] - tail_bias[g].shape[0]
            tail = s[cut:] + tail_bias[g]
            s = jnp.concatenate([s[:cut], tail], axis=0) if cut else tail
        if valid is not None:
            s = jnp.where(_heads(valid[g]), s, NEG)
        s = s.astype(BF16)
        m_new = jnp.maximum(m_old, jnp.max(s, axis=0, keepdims=True).astype(F32))
        alpha = jnp.exp2(m_old - m_new)
        p = jnp.exp2(s - m_new.astype(BF16))
        pv = jnp.dot(v_ext, p, preferred_element_type=F32)
        new.append((m_new, alpha * l_old + pv[LANES:LANES + 1], alpha * acc_old + pv[:LANES]))
    for g in range(C_KV):
        m_ref[g], l_ref[g], acc_ref[g] = new[g]


def _softmax_reset(state):
    m_ref, l_ref, acc_ref = state
    m_ref[...] = jnp.full(m_ref.shape, M_INIT, F32)
    l_ref[...] = jnp.zeros(l_ref.shape, F32)
    acc_ref[...] = jnp.zeros(acc_ref.shape, F32)


def _softmax_result(state, g):
    _, l_ref, acc_ref = state
    l = l_ref[g]
    return acc_ref[g] / jnp.where(l > 0.0, l, 1.0)


def _top_blocks(score, n_sel):
    nb = score.shape[0]
    blk = lax.broadcasted_iota(jnp.int32, score.shape, 0).astype(F32)
    sel = jnp.zeros(score.shape, F32)
    work = score
    for _ in range(min(n_sel, nb)):
        mx = jnp.max(work, axis=0, keepdims=True)
        idx = jnp.min(jnp.where(work == mx, blk, float(nb)), axis=0, keepdims=True)
        pick = blk == idx
        sel = jnp.where(pick, 1.0, sel)
        work = jnp.where(pick, -jnp.inf, work)
    return sel


def _nsa_attn_kernel(qt_ref, ks_ref, vs_ref, kw_ref, vw_ref, kc_ref, vc_ref, cg_ref, bnear_ref,
                     bcmp_ref, o_ref, m_ref, l_ref, acc_ref, sel_ref, *, nb):
    i = pl.program_id(1)
    lane = lax.broadcasted_iota(jnp.int32, (Q_TILE, LANES), 1)
    key = lax.broadcasted_iota(jnp.int32, (Q_TILE, LANES), 0)
    causal = key <= lane
    sig_t = _sigmoid(cg_ref[0]).T
    kc = kc_ref[0].astype(BF16)
    vc_t = vc_ref[0].T.astype(BF16)
    blk = lax.broadcasted_iota(jnp.int32, (nb, Q_TILE), 0)
    qpos = i * Q_TILE + lax.broadcasted_iota(jnp.int32, (nb, Q_TILE), 1)
    cur = qpos // CMP_BLOCK
    cmp_valid = _heads(qpos >= blk * CMP_BLOCK + (CMP_BLOCK - 1))
    forced = (blk == 0) | (blk == cur) | (blk == cur - 1)
    state = (m_ref, l_ref, acc_ref)
    n_back = WINDOW // Q_TILE
    k_rows = lambda ref, j, n: ref[0, pl.ds(pl.multiple_of((j + FRONT_TILES) * Q_TILE, Q_TILE), n * Q_TILE), :]
    v_cols = lambda ref, j, n: ref[0, :, pl.ds(pl.multiple_of((j + FRONT_TILES) * Q_TILE, Q_TILE), n * Q_TILE)]

    def key_mask(g, j, n):
        rows = [sel_ref[g, pl.ds(SEL_PAD + 2 * j + r, 1), :] for r in range(2 * n)]
        return jnp.concatenate(
            [jnp.where(key < CMP_BLOCK, rows[2 * t], rows[2 * t + 1]) for t in range(n)], axis=0) > 0.5

    def block_bias(g, j, n, tiles=None):
        rows = sel_ref[g, pl.ds(SEL_PAD + 2 * j, 2 * n), :]
        if tiles is not None:
            t = j + lax.broadcasted_iota(jnp.int32, (2 * n, Q_TILE), 0) // 2
            rows = jnp.where((t >= tiles[0]) & (t < tiles[1]), rows, 0.0)
        bias = _heads((rows - 1.0) * (-NEG))
        return jnp.concatenate([bias, jnp.zeros((LANES - 2 * n, ROWS), F32)], axis=0).astype(BF16)

    q_t = jnp.concatenate([qt_ref[0, n] for n in range(C_HPG)], axis=1)
    feat = lax.broadcasted_iota(jnp.int32, (LANES, ROWS), 0) // C_HD
    qts = [jnp.where(feat == g, q_t, jnp.zeros_like(q_t)) for g in range(C_KV)]

    o_cmp, scores = [], []
    for g in range(C_KV):
        bias_c = pltpu.roll(bcmp_ref[g], (2 * i + 2) % LANES, 0)[:nb, :]
        lc = jnp.dot(kc, qts[g], preferred_element_type=F32) + bias_c
        lc = jnp.where(cmp_valid, lc, NEG)
        mc = jnp.maximum(jnp.max(lc, axis=0, keepdims=True), M_INIT)
        pc = jnp.exp2(lc - mc)
        den = jnp.sum(pc, axis=0, keepdims=True)
        pc = pc / jnp.where(den > 0.0, den, 1.0)
        o_cmp.append(jnp.dot(vc_t, pc.astype(BF16), preferred_element_type=F32))
        psum = sum(pc[:, n * Q_TILE:(n + 1) * Q_TILE] for n in range(C_HPG))
        scores.append(jnp.where(blk > cur, NEG, jnp.where(forced, FORCE_SCORE, psum)))

    score = jnp.concatenate(scores, axis=1)
    sel = jnp.where(score > 0.5 * NEG, _top_blocks(score, N_SEL), 0.0)
    for g in range(C_KV):
        sel_ref[g, 0:SEL_PAD, :] = jnp.zeros((SEL_PAD, Q_TILE), F32)
        sel_ref[g, SEL_PAD:SEL_PAD + nb, :] = sel[:, g * Q_TILE:(g + 1) * Q_TILE]

    _softmax_reset(state)
    n_far = jnp.maximum(i - 1, 0)

    def far_step(j, tiles=None):
        _softmax_step(qts, k_rows(ks_ref, j, FAR_TILES), v_cols(vs_ref, j, FAR_TILES), state,
                      block_bias=[block_bias(g, j, FAR_TILES, tiles) for g in range(C_KV)])

    def far_many(jq, carry):
        far_step(FAR_TILES * jq)
        return carry

    n_many = n_far // FAR_TILES
    lax.fori_loop(0, n_many, far_many, 0)

    @pl.when(n_far % FAR_TILES != 0)
    def _():
        far_step(n_far - FAR_TILES, (FAR_TILES * n_many, n_far))

    near_causal = jnp.concatenate([key >= 0, causal], axis=0)
    bias2 = [bnear_ref[g] for g in range(C_KV)]
    _softmax_step(qts, k_rows(ks_ref, i - 1, 2), v_cols(vs_ref, i - 1, 2), state, tail_bias=bias2,
                  valid=[key_mask(g, i - 1, 2) & near_causal for g in range(C_KV)])
    o_slc = [_softmax_result(state, g) for g in range(C_KV)]

    _softmax_reset(state)
    exists = lambda back: key >= jnp.where(i >= back, 0, Q_TILE)
    win_valid = jnp.concatenate(
        [(key > lane) & exists(n_back)] + [exists(back) for back in range(n_back - 1, 0, -1)] + [causal],
        axis=0)
    _softmax_step(qts, k_rows(kw_ref, i - n_back, n_back + 1), v_cols(vw_ref, i - n_back, n_back + 1), state,
                  tail_bias=bias2, valid=[win_valid] * C_KV)
    o_win = [_softmax_result(state, g) for g in range(C_KV)]

    low = (key // C_HD) == 0
    for n in range(C_HPG):
        cols = slice(n * Q_TILE, (n + 1) * Q_TILE)
        per_group = []
        for g in range(C_KV):
            h = g * C_HPG + n
            gate = lambda br: sig_t[br * C_HEADS + h:br * C_HEADS + h + 1, :]
            per_group.append(gate(0) * o_cmp[g][:, cols] + gate(1) * o_slc[g][:, cols]
                             + gate(2) * o_win[g][:, cols])
        o_ref[0, :, n * LANES:(n + 1) * LANES] = jnp.where(low, per_group[0], per_group[1]).T


def _nsa_attn(qt, ks16, vs16t, kw16, vw16t, kc, vc, gates, bnear, bcmp):
    b, _, _, t = qt.shape
    nb = t // CMP_BLOCK
    tp = t + FRONT_TILES * Q_TILE
    kern = functools.partial(_nsa_attn_kernel, nb=nb)
    seq = pl.BlockSpec((1, tp, LANES), lambda bi, i: (bi, 0, 0))
    seq_t = pl.BlockSpec((1, LANES, tp), lambda bi, i: (bi, 0, 0))
    blk = pl.BlockSpec((1, nb, LANES), lambda bi, i: (bi, 0, 0))
    return pl.pallas_call(
        kern,
        grid=(b, t // Q_TILE),
        in_specs=[pl.BlockSpec((1, C_HPG, LANES, Q_TILE), lambda bi, i: (bi, 0, 0, i)),
                  seq, seq_t, seq, seq_t, blk, blk,
                  pl.BlockSpec((1, Q_TILE, LANES), lambda bi, i: (bi, i, G_CG // LANES)),
                  pl.BlockSpec(bnear.shape, lambda bi, i: (0, 0, 0)),
                  pl.BlockSpec(bcmp.shape, lambda bi, i: (0, 0, 0))],
        out_specs=pl.BlockSpec((1, Q_TILE, C_W), lambda bi, i: (bi, i, 0)),
        out_shape=jax.ShapeDtypeStruct((b, t, C_W), F32),
        scratch_shapes=[pltpu.VMEM((C_KV, 1, ROWS), F32), pltpu.VMEM((C_KV, 1, ROWS), F32),
                        pltpu.VMEM((C_KV, LANES, ROWS), F32),
                        pltpu.VMEM((C_KV, SEL_PAD + nb, Q_TILE), F32)],
        compiler_params=_cparams(("parallel", "arbitrary")),
        name="nsa_attn",
    )(qt, ks16, vs16t, kw16, vw16t, kc, vc, gates, bnear, bcmp)


def _rel_bucket(dist):
    n = jnp.maximum(dist, 0)
    exact = N_BUCKETS // 2
    ratio = jnp.log(jnp.maximum(n, 1).astype(F32) / exact) / math.log(MAX_DIST / exact)
    large = jnp.minimum(exact + (ratio * (N_BUCKETS - exact)).astype(jnp.int32), N_BUCKETS - 1)
    return jnp.where(n < exact, n, large)


def _one_hot(idx, n):
    return (jnp.clip(idx, 0, n - 1)[..., None] == jnp.arange(n)).astype(F32)


def _distance_table(rel_bias):
    tbl = jnp.dot(_one_hot(_rel_bucket(jnp.arange(MAX_DIST)), N_BUCKETS), rel_bias,
                  precision=lax.Precision.HIGHEST)
    return (tbl - tbl[MAX_DIST - 1:MAX_DIST, :]).T


def _bias_tables(rel_bias):
    tbl = _distance_table(rel_bias) * LOG2E
    tbl = tbl.reshape(C_KV, C_HPG, MAX_DIST)
    qq = jnp.arange(Q_TILE)[:, None]
    cc = jnp.arange(LANES)[None, :]
    look = lambda dist: jnp.einsum('ghd,qcd->ghqc', tbl, _one_hot(dist, MAX_DIST),
                                   precision=lax.Precision.HIGHEST).reshape(C_KV, ROWS, LANES)
    bnear = jnp.stack([look(qq - cc), look(qq - cc + Q_TILE)], axis=1)
    bcmp = look(qq - CMP_BLOCK * cc + CMP_BLOCK * (LANES - 1) - (LANES - 1))
    bnear = bnear.transpose(0, 1, 3, 2)
    return jnp.concatenate([bnear[:, 1], bnear[:, 0]], axis=1), bcmp.transpose(0, 2, 1)


PAGES_PER_STEP = 16
S_ROWS = C_KV * C_HPG
BLK_LANES = 384
NEW_PAD = 8


def _page_specs(layer, n):
    spec = lambda j: pl.BlockSpec((1, 1, LANES, PAGE_SIZE),
                                  lambda b, c, pt, j=j: (layer, pt[b, c * n + j], 0, 0))
    return [spec(j) for j in range(n)]


def _cmp_pages_kernel(pt_ref, *refs, n):
    k_refs, v_refs, (kc_ref, vc_ref) = refs[:n], refs[n:2 * n], refs[2 * n:]
    row = lax.broadcasted_iota(jnp.int32, (n * PAGE_SIZE, 2 * n), 0) // CMP_BLOCK
    col = lax.broadcasted_iota(jnp.int32, (n * PAGE_SIZE, 2 * n), 1)
    pool = (row == col).astype(BF16)
    for refs_in, out in ((k_refs, kc_ref), (v_refs, vc_ref)):
        pages = jnp.concatenate([r[0, 0] for r in refs_in], axis=1)
        out[0, 0] = _dot_exact_rhs(pages, pool) * (1.0 / CMP_BLOCK)


def _cmp_pages(page_table, cache_k, cache_v, layer):
    b, n_pages = page_table.shape
    n = min(PAGES_PER_STEP, n_pages)
    chunks = n_pages // n
    out = jax.ShapeDtypeStruct((b, chunks, LANES, 2 * n), F32)
    out_spec = pl.BlockSpec((1, 1, LANES, 2 * n), lambda bi, c, pt: (bi, c, 0, 0))
    return pl.pallas_call(
        functools.partial(_cmp_pages_kernel, n=n),
        grid_spec=pltpu.PrefetchScalarGridSpec(
            num_scalar_prefetch=1, grid=(b, chunks),
            in_specs=_page_specs(layer, n) + _page_specs(layer, n),
            out_specs=[out_spec, out_spec]),
        out_shape=[out, out],
        compiler_params=_cparams(("parallel", "arbitrary")),
        name="nsa_sample_cmp_pages",
    )(page_table, *([cache_k] * n), *([cache_v] * n))


def _top_blocks_lanes(score, n_sel):
    nb = score.shape[-1]
    blk = lax.broadcasted_iota(jnp.int32, score.shape, 1).astype(F32)
    sel = jnp.zeros(score.shape, F32)
    work = score
    for _ in range(min(n_sel, nb)):
        mx = jnp.max(work, axis=-1, keepdims=True)
        idx = jnp.min(jnp.where(work == mx, blk, float(nb)), axis=-1, keepdims=True)
        pick = blk == idx
        sel = jnp.where(pick, 1.0, sel)
        work = jnp.where(pick, -jnp.inf, work)
    return sel


def _nsa_sample_head_kernel(q_ref, newc_ref, news_ref, neww_ref, kc_ref, vc_ref, wk_ref, wv_ref,
                            qg_ref, kg_ref, kgc_ref, bc_ref, bw_ref, bn_ref,
                            q32_ref, sel_ref, ocmp_ref, owin_ref, new_ref, q_scr, *, ts, past, wbuf):
    n_rows = S_ROWS * ts
    ones = _half_ones()
    scale = C_HD ** -0.5
    lane = lax.broadcasted_iota(jnp.int32, (ts, LANES), 1)
    q = q_ref[0]
    for n in range(C_HPG):
        qn = _rms_halves(q[:, n * LANES:(n + 1) * LANES], qg_ref[...], ones) * scale
        for g in range(C_KV):
            lo = (g * C_HPG + n) * ts
            q_scr[lo:lo + ts, :] = jnp.where((lane // C_HD) == g, qn, 0.0)
    q32 = q_scr[...].astype(BF16)
    q32_ref[0] = q32

    newc, news, neww = newc_ref[0], news_ref[0], neww_ref[0]
    ks_new = _rms_halves(news[:, :LANES], kg_ref[1:2, :], ones)
    kw_new = _rms_halves(neww[:, :LANES], kg_ref[2:3, :], ones)
    new_ref[0, 0] = newc[:, :LANES]
    new_ref[0, 1] = newc[:, LANES:]
    new_ref[0, 2] = ks_new
    new_ref[0, 3] = news[:, LANES:]
    new_ref[0, 4] = kw_new
    new_ref[0, 5] = neww[:, LANES:]

    t_of_row = lax.broadcasted_iota(jnp.int32, (n_rows, 1), 0) % ts
    qpos = past + t_of_row

    n_cached = past // CMP_BLOCK
    pad_rows = jnp.zeros((NEW_PAD - ts, LANES), F32)
    ones_rows = jnp.full((NEW_PAD, BLK_LANES), 1.0, BF16)
    new_mean = lambda rows: _dot_tn_exact(jnp.concatenate([rows, pad_rows], axis=0),
                                          ones_rows) * (1.0 / CMP_BLOCK)
    is_new = lax.broadcasted_iota(jnp.int32, (LANES, BLK_LANES), 1) == n_cached
    kc_all = jnp.where(is_new, new_mean(newc[:, :LANES]), kc_ref[0])
    vc_all = jnp.where(is_new, new_mean(newc[:, LANES:]), vc_ref[0])
    feat = lax.broadcasted_iota(jnp.int32, (LANES, 1), 0) // C_HD
    sq = kc_all * kc_all
    ms = jnp.where(feat == 0, jnp.sum(sq[:C_HD], axis=0, keepdims=True),
                   jnp.sum(sq[C_HD:], axis=0, keepdims=True)) * (1.0 / C_HD)
    kc_n = kc_all * lax.rsqrt(ms + RMS_EPS) * kgc_ref[:, 0:1]
    blk = lax.broadcasted_iota(jnp.int32, (n_rows, BLK_LANES), 1)
    lc = jnp.dot(q32, kc_n.astype(BF16), preferred_element_type=F32) + bc_ref[...]
    lc = jnp.where(qpos >= blk * CMP_BLOCK + (CMP_BLOCK - 1), lc, NEG)
    mc = jnp.maximum(jnp.max(lc, axis=-1, keepdims=True), M_INIT)
    pc = jnp.exp(lc - mc)
    den = jnp.sum(pc, axis=-1, keepdims=True)
    pc = pc / jnp.where(den > 0.0, den, 1.0)
    ocmp_ref[0] = lax.dot_general(pc.astype(BF16), vc_all.astype(BF16), (((1,), (1,)), ((), ())),
                                  preferred_element_type=F32)

    gi = lax.broadcasted_iota(jnp.int32, (C_KV * ts, n_rows), 0)
    ri = lax.broadcasted_iota(jnp.int32, (C_KV * ts, n_rows), 1)
    same = ((gi // ts) == (ri // (C_HPG * ts))) & ((gi % ts) == (ri % ts))
    psum = _dot_exact_lhs(same.astype(BF16), pc)
    sblk = lax.broadcasted_iota(jnp.int32, (C_KV * ts, BLK_LANES), 1)
    scur = (past + lax.broadcasted_iota(jnp.int32, (C_KV * ts, 1), 0) % ts) // CMP_BLOCK
    forced = (sblk == 0) | (sblk == scur) | (sblk == scur - 1)
    score = jnp.where(sblk > scur, NEG, jnp.where(forced, FORCE_SCORE, psum))
    sel = jnp.where(score > 0.5 * NEG, _top_blocks_lanes(score, N_SEL), 0.0)
    sel_ref[0] = lax.dot_general(same.astype(BF16), sel.astype(BF16), (((0,), (0,)), ((), ())),
                                 preferred_element_type=F32)

    key = lax.broadcasted_iota(jnp.int32, (n_rows, wbuf), 1)
    lw = jnp.dot(q32, wk_ref[0, 0].astype(BF16), preferred_element_type=F32) + bw_ref[...]
    kpos = past - wbuf + key
    dist = qpos - kpos
    lw = jnp.where((dist >= 0) & (dist < WINDOW) & (kpos >= 0), lw, NEG)
    kw_pad = jnp.concatenate([kw_new, pad_rows], axis=0)
    vw_pad = jnp.concatenate([neww[:, LANES:], pad_rows], axis=0)
    nkey = lax.broadcasted_iota(jnp.int32, (n_rows, NEW_PAD), 1)
    ln = lax.dot_general(q32, kw_pad.astype(BF16), (((1,), (1,)), ((), ())),
                         preferred_element_type=F32) + bn_ref[...]
    ln = jnp.where((nkey <= t_of_row) & (nkey < ts), ln, NEG)
    mw = jnp.maximum(jnp.max(lw, axis=-1, keepdims=True), jnp.max(ln, axis=-1, keepdims=True))
    pw, pn = jnp.exp(lw - mw), jnp.exp(ln - mw)
    den = jnp.sum(pw, axis=-1, keepdims=True) + jnp.sum(pn, axis=-1, keepdims=True)
    ow = lax.dot_general(pw.astype(BF16), wv_ref[0, 0].astype(BF16), (((1,), (1,)), ((), ())),
                         preferred_element_type=F32) + _dot(pn, vw_pad)
    owin_ref[0] = ow / den


def _dot_tn_exact(a, b01):
    dims = (((0,), (0,)), ((), ()))
    return sum(lax.dot_general(p, b01, dims, preferred_element_type=F32) for p in _split3(a))


def _nsa_sample_head(main, kc_t, vc_t, win_k, win_v, layer, qg, kg, tables, past):
    b, ts, _ = main.shape
    wbuf = win_k.shape[-1]
    n_rows = S_ROWS * ts
    bias_c, bias_w, bias_n = tables
    kern = functools.partial(_nsa_sample_head_kernel, ts=ts, past=past, wbuf=wbuf)
    col = lambda width, off: pl.BlockSpec((1, ts, width), lambda i: (i, 0, off // width))
    full = lambda a: pl.BlockSpec(a.shape, lambda i: (0,) * a.ndim)
    per_b = lambda a: pl.BlockSpec((1,) + a.shape[1:], lambda i: (i,) + (0,) * (a.ndim - 1))
    win = pl.BlockSpec((1, 1, LANES, wbuf), lambda i: (layer, i, 0, 0))
    qg2 = jnp.tile(qg.reshape(1, C_HD), (1, 2))
    kg2 = jnp.tile(kg, (1, 2))
    kg_col = jnp.tile(kg2[0].reshape(LANES, 1), (1, LANES))
    rows = lambda n, dt: jax.ShapeDtypeStruct((b, n_rows, n), dt)
    out_rows = lambda n: pl.BlockSpec((1, n_rows, n), lambda i: (i, 0, 0))
    return pl.pallas_call(
        kern,
        grid=(b,),
        in_specs=[col(C_W, P_Q), col(2 * LANES, P_KV), col(2 * LANES, P_KV + 2 * LANES),
                  col(2 * LANES, P_KV + 4 * LANES), per_b(kc_t), per_b(vc_t), win, win,
                  full(qg2), full(kg2), full(kg_col), full(bias_c), full(bias_w), full(bias_n)],
        out_specs=[out_rows(LANES), out_rows(BLK_LANES), out_rows(LANES), out_rows(LANES),
                   pl.BlockSpec((1, 6, ts, LANES), lambda i: (i, 0, 0, 0))],
        out_shape=[rows(LANES, BF16), rows(BLK_LANES, F32), rows(LANES, F32), rows(LANES, F32),
                   jax.ShapeDtypeStruct((b, 6, ts, LANES), F32)],
        scratch_shapes=[pltpu.VMEM((n_rows, LANES), F32)],
        compiler_params=_cparams(("parallel",)),
        name="nsa_sample_head",
    )(main, main, main, main, kc_t, vc_t, win_k, win_v, qg2, kg2, kg_col, bias_c, bias_w, bias_n)


def _slc_pages_kernel(pt_ref, *refs, n, ts, last_chunk):
    k_refs, v_refs = refs[:n], refs[n:2 * n]
    (q_ref, sel_ref, ocmp_ref, owin_ref, gate_ref, new_ref, blast_ref, bn_ref,
     o_ref, m_ref, l_ref, acc_ref) = refs[2 * n:]
    c = pl.program_id(1)
    n_rows = S_ROWS * ts
    nt = (((1,), (1,)), ((), ()))

    @pl.when(c == 0)
    def _():
        m_ref[...] = jnp.full(m_ref.shape, M_INIT, F32)
        l_ref[...] = jnp.zeros(l_ref.shape, F32)
        acc_ref[...] = jnp.zeros(acc_ref.shape, F32)

    q32 = q_ref[0]
    keys = jnp.concatenate([r[0, 0] for r in k_refs], axis=1).astype(BF16)
    vals = jnp.concatenate([r[0, 0] for r in v_refs], axis=1).astype(BF16)
    s = jnp.dot(q32, keys, preferred_element_type=F32)
    cut = (n - 1) * PAGE_SIZE
    tail = s[:, cut:] + jnp.where(c == last_chunk, blast_ref[...], 0.0)
    s = jnp.concatenate([s[:, :cut], tail], axis=1) if cut else tail
    row = lax.broadcasted_iota(jnp.int32, (LANES, n * PAGE_SIZE), 0)
    col = lax.broadcasted_iota(jnp.int32, (LANES, n * PAGE_SIZE), 1) // CMP_BLOCK
    expand = (row == col).astype(BF16)
    picked = jnp.dot(sel_ref[0, 0].astype(BF16), expand, preferred_element_type=F32) > 0.5
    s = jnp.where(picked, s, NEG)
    m_old = m_ref[...]
    m_new = jnp.maximum(m_old, jnp.max(s, axis=-1, keepdims=True))
    alpha = jnp.exp(m_old - m_new)
    p = jnp.exp(s - m_new)
    l_ref[...] = alpha * l_ref[...] + jnp.sum(p, axis=-1, keepdims=True)
    acc_ref[...] = alpha * acc_ref[...] + lax.dot_general(p.astype(BF16), vals, nt,
                                                          preferred_element_type=F32)
    m_ref[...] = m_new

    @pl.when(c == last_chunk)
    def _():
        t_of_row = lax.broadcasted_iota(jnp.int32, (n_rows, 1), 0) % ts
        pad_rows = jnp.zeros((NEW_PAD - ts, LANES), F32)
        k_new = jnp.concatenate([new_ref[0, 2], pad_rows], axis=0)
        v_new = jnp.concatenate([new_ref[0, 3], pad_rows], axis=0)
        nkey = lax.broadcasted_iota(jnp.int32, (n_rows, NEW_PAD), 1)
        sn = lax.dot_general(q32, k_new.astype(BF16), nt, preferred_element_type=F32) + bn_ref[...]
        sn = jnp.where((nkey <= t_of_row) & (nkey < ts), sn, NEG)
        m_prev = m_ref[...]
        m_fin = jnp.maximum(m_prev, jnp.max(sn, axis=-1, keepdims=True))
        a_fin = jnp.exp(m_prev - m_fin)
        pn = jnp.exp(sn - m_fin)
        l_fin = a_fin * l_ref[...] + jnp.sum(pn, axis=-1, keepdims=True)
        o_slc = (a_fin * acc_ref[...] + _dot(pn, v_new)) / l_fin

        ri = lax.broadcasted_iota(jnp.int32, (n_rows, NEW_PAD), 0) % ts
        ti = lax.broadcasted_iota(jnp.int32, (n_rows, NEW_PAD), 1)
        gates = jnp.concatenate([_sigmoid(gate_ref[0]), pad_rows], axis=0)
        per_row = _dot_exact_lhs((ri == ti).astype(BF16), gates)
        head = lax.broadcasted_iota(jnp.int32, (n_rows, LANES), 0) // ts
        glane = lax.broadcasted_iota(jnp.int32, (n_rows, LANES), 1)
        gate = lambda br: jnp.sum(jnp.where(glane == br * C_HEADS + head, per_row, 0.0), axis=-1,
                                  keepdims=True)
        comb = gate(0) * ocmp_ref[0] + gate(1) * o_slc + gate(2) * owin_ref[0]
        low = (lax.broadcasted_iota(jnp.int32, (ts, LANES), 1) // C_HD) == 0
        for n_ in range(C_HPG):
            top = comb[n_ * ts:(n_ + 1) * ts]
            bot = comb[(C_HPG + n_) * ts:(C_HPG + n_ + 1) * ts]
            o_ref[0, :, n_ * LANES:(n_ + 1) * LANES] = jnp.where(low, top, bot)


def _slc_pages(page_table, cache_k, cache_v, layer, q32, sel_chunks, o_cmp, o_win, gates, new_rows,
               bias_last, bias_n):
    b, n_pages = page_table.shape
    n = min(PAGES_PER_STEP, n_pages)
    chunks = n_pages // n
    ts = new_rows.shape[2]
    n_rows = S_ROWS * ts
    kern = functools.partial(_slc_pages_kernel, n=n, ts=ts, last_chunk=chunks - 1)
    per_b = lambda a: pl.BlockSpec((1,) + a.shape[1:], lambda bi, c, pt: (bi,) + (0,) * (a.ndim - 1))
    full = lambda a: pl.BlockSpec(a.shape, lambda bi, c, pt: (0,) * a.ndim)
    return pl.pallas_call(
        kern,
        grid_spec=pltpu.PrefetchScalarGridSpec(
            num_scalar_prefetch=1, grid=(b, chunks),
            in_specs=_page_specs(layer, n) + _page_specs(layer, n) + [
                per_b(q32), pl.BlockSpec((1, 1, n_rows, LANES), lambda bi, c, pt: (bi, c, 0, 0)),
                per_b(o_cmp), per_b(o_win),
                pl.BlockSpec((1, ts, LANES), lambda bi, c, pt: (bi, 0, G_CG // LANES)),
                per_b(new_rows), full(bias_last), full(bias_n)],
            out_specs=pl.BlockSpec((1, ts, C_W), lambda bi, c, pt: (bi, 0, 0)),
            scratch_shapes=[pltpu.VMEM((n_rows, 1), F32), pltpu.VMEM((n_rows, 1), F32),
                            pltpu.VMEM((n_rows, LANES), F32)]),
        out_shape=jax.ShapeDtypeStruct((b, ts, C_W), F32),
        compiler_params=_cparams(("parallel", "arbitrary")),
        name="nsa_sample_slc_pages",
    )(page_table, *([cache_k] * n), *([cache_v] * n), q32, sel_chunks, o_cmp, o_win, gates, new_rows,
      bias_last, bias_n)


def _sample_bias_tables(rel_bias, past, ts, wbuf):
    tbl = jnp.repeat(_distance_table(rel_bias), ts, axis=0)
    t = jnp.tile(jnp.arange(ts), S_ROWS)[:, None]
    look = lambda dist: jnp.einsum('rd,rcd->rc', tbl, _one_hot(dist, MAX_DIST),
                                   precision=lax.Precision.HIGHEST)
    blk = jnp.arange(BLK_LANES)[None, :]
    bias_c = look(past + t - CMP_BLOCK * blk - (CMP_BLOCK - 1))
    bias_w = look(wbuf + t - jnp.arange(wbuf)[None, :])
    bias_n = look(t - jnp.arange(NEW_PAD)[None, :])
    bias_last = look(PAGE_SIZE + t - jnp.arange(PAGE_SIZE)[None, :])
    return bias_c, bias_w, bias_n, bias_last


def _nsa_sample(main, gate, caches, win_k, win_v, page_table, layer, qg, kg, rel_bias):
    b, ts, _ = main.shape
    n_pages = page_table.shape[1]
    past = n_pages * PAGE_SIZE
    wbuf = win_k.shape[-1]
    n_rows = S_ROWS * ts
    bias_c, bias_w, bias_n, bias_last = _sample_bias_tables(rel_bias, past, ts, wbuf)
    kc_chunks, vc_chunks = _cmp_pages(page_table, caches[0], caches[1], layer)
    n_cached = past // CMP_BLOCK
    unchunk = lambda z: z.transpose(0, 2, 1, 3).reshape(b, LANES, n_cached)
    fit = lambda z: jnp.pad(z, ((0, 0), (0, 0), (0, BLK_LANES - n_cached)))
    q32, sel, o_cmp, o_win, new_rows = _nsa_sample_head(
        main, fit(unchunk(kc_chunks)), fit(unchunk(vc_chunks)), win_k, win_v, layer, qg, kg,
        (bias_c, bias_w, bias_n), past)
    per_step = 2 * min(PAGES_PER_STEP, n_pages)
    sel_chunks = sel[:, :, :n_cached].reshape(b, n_rows, n_cached // per_step, per_step)
    sel_chunks = jnp.pad(sel_chunks.transpose(0, 2, 1, 3), ((0, 0), (0, 0), (0, 0), (0, LANES - per_step)))
    o = _slc_pages(page_table, caches[2], caches[3], layer, q32, sel_chunks, o_cmp, o_win,
                   gate.reshape(b, ts, N_GATE), new_rows, bias_last, bias_n)
    return o.reshape(b * ts, C_W), new_rows


def _shift_order(z):
    part = lambda off, n: z[..., off:off + n]
    return jnp.concatenate([part(OFF_R, A_W), part(OFF_K, A_W), part(OFF_V, A_W), part(OFF_WL, A_LORA_W),
                            part(OFF_AL, A_LORA_A), part(OFF_GL, A_LORA_G)], axis=-1)


def _layer_params(l, w_in, shift_mu, w_br_a, w_br_b, w_br_c, w_out, w_mq, w_mk, w_mv, w_mo, w_ff1, w_ff2):
    w = w_in[l]
    cast = lambda a: a.astype(BF16)
    cols = lambda off, n: cast(w[:, off:off + n])
    w_q = cols(OFF_Q, C_W).reshape(D_MODEL, C_KV, C_HPG, C_HD).transpose(0, 2, 1, 3).reshape(D_MODEL, C_W)
    w_main = jnp.concatenate([
        cols(OFF_R, A_W), cols(OFF_K, A_W), cols(OFF_V, A_W), cols(OFF_POOL, B_W), w_q,
        cols(OFF_KV, 6 * C_KVW), cols(OFF_WL, A_LORA_W), cols(OFF_AL, A_LORA_A), cols(OFF_GL, A_LORA_G)],
        axis=1)
    w_gate = jnp.concatenate([cols(OFF_MG, 3 * D_MODEL), cols(OFF_CG, 3 * C_HEADS),
                              jnp.zeros((D_MODEL, N_GATE - G_CG - 3 * C_HEADS), BF16)], axis=1)
    wc = w_br_c[l].reshape(C_KV, C_HPG, C_HD, D_MODEL).transpose(1, 0, 2, 3).reshape(C_W, D_MODEL)
    return dict(
        w_main=w_main, w_gate=w_gate, mu=_shift_order(shift_mu[l].reshape(1, -1)),
        wa=cast(w_br_a[l]), wb=cast(w_br_b[l]), wc=cast(wc), wo=cast(w_out[l]),
        wq=cast(w_mq[l]), wk=cast(w_mk[l]), wv=cast(w_mv[l]), wmo=cast(w_mo[l]),
        w1=cast(w_ff1[l]), w2=cast(w_ff2[l]))


def _shift_row(main_row):
    lora = main_row[:, P_LORA:]
    return jnp.concatenate([
        main_row[:, P_R:P_R + A_W], lora[:, :A_LORA_W], main_row[:, P_K:P_K + A_W],
        main_row[:, P_V:P_V + A_W], lora[:, A_LORA_W:A_LORA_W + A_LORA_A],
        lora[:, A_LORA_W + A_LORA_A:]], axis=1)


def _feature_major(z):
    z = jnp.moveaxis(z, -3, -1)
    return z.reshape(z.shape[:-3] + (C_KVW, z.shape[-1]))


def _token_major(z):
    b, _, rows = z.shape
    return z.reshape(b, C_KV, C_HD, rows).transpose(0, 3, 1, 2)


def kernel(x_prompt, x_sample, state_rwkv, state_shift, state_pool, cache_cmp_k, cache_cmp_v, cache_slc_k, cache_slc_v, cache_win_k, cache_win_v, cache_mem_k, cache_mem_v, page_table, mem_prompt, rel_bias, norm_mix_g, w_in, shift_mu, rwkv_w0, rwkv_w2, rwkv_a0, rwkv_a2, rwkv_g2, rwkv_kk, rwkv_ka, rwkv_rk, rwkv_lnx_g, rwkv_lnx_b, pool_w, pool_scale, nsa_q_g, nsa_k_g, w_br_a, w_br_b, w_br_c, w_out, norm_memx_g, norm_mem_g, w_mq, w_mk, w_mv, mem_q_g, mem_k_g, w_mo, norm_ffn_g, w_ff1, w_ff2):
    bp, t = x_prompt.shape[:2]
    bs, ts = x_sample.shape[:2]
    past = page_table.shape[1] * PAGE_SIZE
    wbuf_p = min(WINDOW, t)
    ts_pad = 16
    bnear, bcmp = _bias_tables(rel_bias)
    caches = [_feature_major(z) for z in (cache_cmp_k, cache_cmp_v, cache_slc_k, cache_slc_v)]
    win_k, win_v = _feature_major(cache_win_k), _feature_major(cache_win_v)
    xp = x_prompt.reshape(bp * t, D_MODEL)
    xs = x_sample.reshape(bs * ts, D_MODEL)
    outs_p = [[] for _ in range(11)]
    outs_s = [[] for _ in range(9)]
    row = lambda a: a.reshape(1, -1)
    for l in range(DEPTH):
        prm = _layer_params(l, w_in, shift_mu, w_br_a, w_br_b, w_br_c, w_out, w_mq, w_mk, w_mv, w_mo,
                            w_ff1, w_ff2)
        rw = (prm["mu"], row(rwkv_w0[l]), rwkv_w2[l].astype(BF16), row(rwkv_a0[l]), rwkv_a2[l].astype(BF16),
              rwkv_g2[l].astype(BF16), row(rwkv_kk[l]), row(rwkv_ka[l]), row(rwkv_rk[l]),
              row(rwkv_lnx_g[l]), row(rwkv_lnx_b[l]))
        pool_wl = pool_w[l].astype(BF16)

        main = _norm_matmul(xp, norm_mix_g[l], prm["w_main"], 512, N_MAIN).reshape(bp, t, N_MAIN)
        gate = _norm_matmul(xp, norm_mix_g[l], prm["w_gate"], 512, N_GATE)
        o_a, s_pairs = _rwkv(main, jnp.zeros((bp, 1, A_COLS), F32), jnp.zeros((bp, 4, 128, 128), F32), rw)
        o_b = _pool(main, jnp.zeros((bp, POOL_HALO, B_W), F32), pool_wl, pool_scale[l], 0, 512)
        qt, ks16, vs16, kw16, vw16, kc, vc, kv_rows = _nsa_prep(main, nsa_q_g[l], nsa_k_g[l], 512)
        front = lambda z: jnp.pad(z, ((0, 0), (FRONT_TILES * Q_TILE, 0), (0, 0)))
        front_t = lambda z: jnp.pad(z, ((0, 0), (0, 0), (FRONT_TILES * Q_TILE, 0)))
        o_c = _nsa_attn(qt, front(ks16), front_t(vs16), front(kw16), front_t(vw16), kc, vc,
                        gate.reshape(bp, t, N_GATE), bnear, bcmp)
        xp = _merge(xp, o_a.reshape(bp * t, A_W), o_b.reshape(bp * t, B_W), o_c.reshape(bp * t, C_W), gate,
                    prm["wa"], prm["wb"], prm["wc"], prm["wo"], 512)
        mk, mv = _mem_kv(mem_prompt.reshape(bp * N_MEM, D_MODEL), norm_mem_g[l], prm["wk"], prm["wv"],
                         mem_k_g[l])
        mk, mv = mk.reshape(bp, N_MEM, M_W), mv.reshape(bp, N_MEM, M_W)
        xp = _mem_ffn(xp, mk, mv, norm_memx_g[l], prm["wq"], mem_q_g[l], prm["wmo"], norm_ffn_g[l],
                      prm["w1"], prm["w2"], 512, t)
        for dst, val in zip(outs_p, (
                _unpair_states(s_pairs), _shift_row(main[:, -1]), main[:, -POOL_BUF:, P_POOL:P_POOL + B_W],
                _token_major(kv_rows[:, 0]), _token_major(kv_rows[:, 1]), _token_major(kv_rows[:, 2]),
                _token_major(kv_rows[:, 3]), _token_major(kv_rows[:, 4, :, -wbuf_p:]),
                _token_major(kv_rows[:, 5, :, -wbuf_p:]),
                mk.reshape(bp, N_MEM, M_HEADS, M_HD), mv.reshape(bp, N_MEM, M_HEADS, M_HD))):
            dst.append(val)

        main = _norm_matmul(xs, norm_mix_g[l], prm["w_main"], bs * ts, N_MAIN).reshape(bs, ts, N_MAIN)
        gate = _norm_matmul(xs, norm_mix_g[l], prm["w_gate"], bs * ts, N_GATE)
        main_pad = jnp.pad(main, ((0, 0), (0, ts_pad - ts), (0, 0)))
        ulast = _shift_order(state_shift[l]).reshape(bs, 1, A_COLS)
        o_a, s_pairs = _rwkv(main_pad, ulast, _pair_states(state_rwkv[l]), rw, t_valid=ts)
        o_a = o_a[:, :ts]
        hist = jnp.concatenate([jnp.zeros((bs, POOL_HALO - POOL_BUF, B_W), F32), state_pool[l]], axis=1)
        o_b = _pool(main_pad, hist, pool_wl, pool_scale[l], past, ts_pad)[:, :ts]
        u_pool = main[:, :, P_POOL:P_POOL + B_W]
        o_c, new_rows = _nsa_sample(main, gate, caches, win_k, win_v, page_table, l, nsa_q_g[l], nsa_k_g[l],
                                    rel_bias)
        new_kv = [new_rows[:, n].reshape(bs, ts, C_KV, C_HD) for n in range(6)]
        slide = lambda buf, n: _token_major(jnp.concatenate(
            [buf[l][:, :, ts:], new_rows[:, n].transpose(0, 2, 1)], axis=2))
        xs = _merge(xs, o_a.reshape(bs * ts, A_W), o_b.reshape(bs * ts, B_W), o_c, gate,
                    prm["wa"], prm["wb"], prm["wc"], prm["wo"], bs * ts)
        xs = _mem_ffn(xs, cache_mem_k[l].reshape(bs, N_MEM, M_W), cache_mem_v[l].reshape(bs, N_MEM, M_W),
                      norm_memx_g[l], prm["wq"], mem_q_g[l], prm["wmo"], norm_ffn_g[l],
                      prm["w1"], prm["w2"], 32, ts)
        for dst, val in zip(outs_s, (
                _unpair_states(s_pairs), _shift_row(main[:, -1]),
                jnp.concatenate([state_pool[l], u_pool], axis=1)[:, -POOL_BUF:],
                new_kv[0], new_kv[1], new_kv[2], new_kv[3], slide(win_k, 4), slide(win_v, 5))):
            dst.append(val)

    return ((xp.reshape(bp, t, D_MODEL), xs.reshape(bs, ts, D_MODEL))
            + tuple(jnp.stack(o) for o in outs_p) + tuple(jnp.stack(o) for o in outs_s))
```

```python
import functools
import math

import jax
import jax.numpy as jnp
from jax import lax
from jax.experimental import pallas as pl
from jax.experimental.pallas import tpu as pltpu

F32 = jnp.float32
BF16 = jnp.bfloat16

D_MODEL = 1024
DEPTH = 2
PAGE_SIZE = 128
A_HEADS = 8
A_HD = 64
A_W = A_HEADS * A_HD
A_LORA_W = 64
A_LORA_A = 64
A_LORA_G = 128
A_LORA = A_LORA_W + A_LORA_A + A_LORA_G
LNX_EPS = 64e-5
L2_EPS = 1e-24
B_GROUPS = 4
B_GW = 128
B_W = B_GROUPS * B_GW
POOL_WINDOWS = (2, 4, 8, 16)
POOL_BUF = 15
POOL_HALO = 16
C_HEADS = 8
C_HD = 64
C_W = C_HEADS * C_HD
C_KV = 2
C_HPG = C_HEADS // C_KV
C_KVW = C_KV * C_HD
CMP_BLOCK = 64
N_SEL = 16
WINDOW = 512
Q_TILE = 128
FORCE_SCORE = 1e4
NEG = -1e30
N_BUCKETS = 32
MAX_DIST = 128
N_MEM = 256
M_HEADS = 4
M_HD = 128
M_W = M_HEADS * M_HD
D_FF = 4 * D_MODEL
RMS_EPS = 1e-6

OFF_R = 0
OFF_WL = OFF_R + A_W
OFF_K = OFF_WL + A_LORA_W
OFF_V = OFF_K + A_W
OFF_AL = OFF_V + A_W
OFF_GL = OFF_AL + A_LORA_A
A_COLS = OFF_GL + A_LORA_G
OFF_POOL = A_COLS
OFF_Q = OFF_POOL + B_W
OFF_KV = OFF_Q + C_W
OFF_CG = OFF_KV + 6 * C_KVW
OFF_MG = OFF_CG + 3 * C_HEADS
N_IN = OFF_MG + 3 * D_MODEL

P_R = 0
P_K = P_R + A_W
P_V = P_K + A_W
P_POOL = P_V + A_W
P_Q = P_POOL + B_W
P_KV = P_Q + C_W
P_LORA = P_KV + 6 * C_KVW
N_MAIN = P_LORA + A_LORA
G_MG = 0
G_CG = 3 * D_MODEL
N_GATE = G_CG + 128

RWKV_CHUNK = 64
RWKV_SUB = 16
LANES = 128
VMEM_LIMIT = 56 * 1024 * 1024


def _cparams(sem):
    return pltpu.CompilerParams(dimension_semantics=sem, vmem_limit_bytes=VMEM_LIMIT)


def _dot(a, b, dims=(((1,), (0,)), ((), ()))):
    return lax.dot_general(a.astype(BF16), b.astype(BF16), dims, preferred_element_type=F32)


def _dot_nt(a, b):
    return _dot(a, b, (((1,), (1,)), ((), ())))


def _dot_tn(a, b):
    return _dot(a, b, (((0,), (0,)), ((), ())))


def _split(x, terms):
    parts = []
    for n in range(terms):
        part = x.astype(BF16)
        parts.append(part)
        if n + 1 < terms:
            x = x - part.astype(F32)
    return parts


def _dot_exact_rhs(a, b01, terms=2):
    dims = (((1,), (0,)), ((), ()))
    return sum(lax.dot_general(p, b01, dims, preferred_element_type=F32) for p in _split(a, terms))


def _dot_exact_lhs(a01, b, terms=2):
    dims = (((1,), (0,)), ((), ()))
    return sum(lax.dot_general(a01, p, dims, preferred_element_type=F32) for p in _split(b, terms))


def _sigmoid(x):
    return 1.0 / (1.0 + jnp.exp(-x))


def _rms(x, g):
    return x * lax.rsqrt(jnp.mean(x * x, axis=-1, keepdims=True) + RMS_EPS) * g


def _norm_matmul_kernel(x_ref, g_ref, w_ref, o_ref, xn_ref):
    @pl.when(pl.program_id(1) == 0)
    def _():
        xn_ref[...] = _rms(x_ref[...], g_ref[...]).astype(BF16)

    o_ref[...] = jnp.dot(xn_ref[...], w_ref[...], preferred_element_type=F32)


def _norm_matmul(x, g, w, tm, tn):
    m, k = x.shape
    n = w.shape[1]
    return pl.pallas_call(
        _norm_matmul_kernel,
        grid=(m // tm, n // tn),
        in_specs=[pl.BlockSpec((tm, k), lambda i, j: (i, 0)),
                  pl.BlockSpec((1, k), lambda i, j: (0, 0)),
                  pl.BlockSpec((k, tn), lambda i, j: (0, j))],
        out_specs=pl.BlockSpec((tm, tn), lambda i, j: (i, j)),
        out_shape=jax.ShapeDtypeStruct((m, n), F32),
        scratch_shapes=[pltpu.VMEM((tm, k), BF16)],
        compiler_params=_cparams(("parallel", "arbitrary")),
        name="norm_matmul",
    )(x, g.reshape(1, k), w)


def _rwkv_kernel(r_ref, k_ref, v_ref, lora_ref, ulast_ref, s0_ref, mu_ref, w0_ref, w2_ref, a0_ref,
                 a2_ref, g2_ref, kk_ref, ka_ref, rk_ref, lng_ref, lnb_ref,
                 o_ref, sout_ref, s_ref, prev_ref, ar_ref, bk_ref, vb_ref, y_ref, *, chunk, bt, t_valid):
    c = pl.program_id(1)
    nc = pl.num_programs(1)
    C = chunk
    R = bt * C
    NP = A_HEADS // 2
    PB = 2 * C

    @pl.when(c == 0)
    def _():
        s_ref[...] = s0_ref[...].reshape(bt * NP, 2 * A_HD, 2 * A_HD)
        prev_ref[...] = ulast_ref[:, 0, :]

    row = lax.broadcasted_iota(jnp.int32, (R, 1), 0)

    def shift_mix(u3, lo, hi):
        u = u3.reshape(R, hi - lo)
        u_prev = pltpu.roll(u, 1, 0)
        for b in range(bt):
            u_prev = jnp.where(row == b * C, prev_ref[b:b + 1, lo:hi], u_prev)
        for b in range(bt):
            prev_ref[b:b + 1, lo:hi] = u[(b + 1) * C - 1:(b + 1) * C, :]
        return u + (u_prev - u) * mu_ref[:, lo:hi]

    r = shift_mix(r_ref[...], 0, A_W)
    k = shift_mix(k_ref[...], A_W, 2 * A_W)
    v = shift_mix(v_ref[...], 2 * A_W, 3 * A_W)
    lora = shift_mix(lora_ref[...], 3 * A_W, 3 * A_W + A_LORA)
    wl = lora[:, 0:A_LORA_W]
    al = lora[:, A_LORA_W:A_LORA_W + A_LORA_A]
    gl = lora[:, A_LORA_W + A_LORA_A:A_LORA]

    z = -(w0_ref[...] + _dot(jnp.tanh(wl), w2_ref[...]))
    softplus = jnp.maximum(z, 0.0) + jnp.log(1.0 + jnp.exp(-jnp.abs(z)))
    w = -softplus - 0.5
    a = _sigmoid(a0_ref[...] + _dot(al, a2_ref[...]))
    g = _dot(_sigmoid(gl), g2_ref[...])

    lane = lax.broadcasted_iota(jnp.int32, (A_W, A_W), 1) // A_HD
    sub = lax.broadcasted_iota(jnp.int32, (A_W, A_W), 0) // A_HD
    head_ones = (lane == sub).astype(BF16)

    kkv = k * kk_ref[...]
    head_sum = lambda z: _dot_exact_rhs(z, head_ones, terms=1)
    kkn = kkv * lax.rsqrt(jnp.maximum(head_sum(kkv * kkv), L2_EPS))
    k2 = k * (1.0 + (a - 1.0) * ka_ref[...])
    log_d = -jnp.exp(w)
    t_in = row % C
    if t_valid is not None:
        live = (c * C + t_in) < t_valid
        log_d = jnp.where(live, log_d, 0.0)
        kkn = jnp.where(live, kkn, 0.0)
        k2 = jnp.where(live, k2, 0.0)

    ri = lax.broadcasted_iota(jnp.int32, (R, R), 0)
    ci = lax.broadcasted_iota(jnp.int32, (R, R), 1)
    cum_mask = ((ri // C) == (ci // C)) & (ci <= ri)
    cum = _dot_exact_lhs(cum_mask.astype(BF16), log_d)
    c_incl = jnp.exp(cum)
    c_inv = jnp.exp(-cum)
    a_t = -kkn * jnp.exp(cum - log_d)
    r_t = r * c_incl
    b_t = kkn * a * c_inv
    k_t = k2 * c_inv

    lane = lax.broadcasted_iota(jnp.int32, (1, 2 * A_HD), 1) // A_HD
    def stage(ref, off, x):
        for b in range(bt):
            for p in range(NP):
                blk = x[b * C:(b + 1) * C, 2 * A_HD * p:2 * A_HD * (p + 1)]
                for hh in range(2):
                    ref[b * NP + p, off + hh * C:off + (hh + 1) * C, :] = (
                        jnp.where(lane == hh, blk, 0.0).astype(BF16))

    stage(ar_ref, 0, a_t)
    stage(ar_ref, PB, r_t)
    stage(bk_ref, 0, b_t)
    stage(bk_ref, PB, k_t)
    stage(vb_ref, 0, v)
    ar = ar_ref[...]
    bk = bk_ref[...]
    vb = vb_ref[...]

    nn = (((2,), (1,)), ((0,), (0,)))
    nt = (((2,), (2,)), ((0,), (0,)))
    tn = (((1,), (1,)), ((0,), (0,)))
    bdot = lambda x, y, dims=nn: lax.dot_general(x.astype(BF16), y.astype(BF16), dims,
                                                 preferred_element_type=F32)
    bi = lax.broadcasted_iota(jnp.int32, (PB, PB), 0)
    bj = lax.broadcasted_iota(jnp.int32, (PB, PB), 1)
    same_head = (bi // C) == (bj // C)
    strict = same_head & ((bj % C) < (bi % C))
    lower = same_head & ((bj % C) <= (bi % C))
    sub_blk = (bi // RWKV_SUB) == (bj // RWKV_SUB)
    eye = (bi == bj).astype(F32)

    gram = bdot(ar, bk, nt)
    l_b = jnp.where(strict, gram[:, :PB, :PB], 0.0)
    l_k = jnp.where(strict, gram[:, :PB, PB:], 0.0)
    m_b = jnp.where(lower, gram[:, PB:, :PB], 0.0)
    m_k = jnp.where(lower, gram[:, PB:, PB:], 0.0)
    dg = jnp.where(sub_blk, l_b, 0.0)
    off = l_b - dg
    t_inv = eye + dg
    pw = dg
    for _ in range(int(math.log2(RWKV_SUB)) - 1):
        pw = bdot(pw, pw)
        t_inv = t_inv + bdot(t_inv, pw)
    n1 = bdot(t_inv, off)
    n2 = bdot(n1, n1)
    full = eye + n1 + n2 + bdot(n1, n2)
    t_full = bdot(full, t_inv)

    s_old = s_ref[...]
    w0 = bdot(ar, s_old, nt)
    u = bdot(t_full, w0[:, :PB] + bdot(l_k, vb))
    uv = jnp.concatenate([u, vb.astype(F32)], axis=1)
    yb = w0[:, PB:] + bdot(jnp.concatenate([m_b, m_k], axis=2), uv)
    s_new = s_old + bdot(uv, bk, tn)
    for b in range(bt):
        for p in range(NP):
            sl = slice(2 * A_HD * p, 2 * A_HD * (p + 1))
            idx = b * NP + p
            s_ref[idx] = s_new[idx] * c_incl[(b + 1) * C - 1:(b + 1) * C, sl]
            y_ref[b * C:(b + 1) * C, sl] = yb[idx, :C] + yb[idx, C:]
    y = y_ref[...]

    inv_n = 1.0 / A_HD
    mean = head_sum(y) * inv_n
    yc = y - mean
    var = head_sum(yc * yc) * inv_n
    yn = yc * lax.rsqrt(var + LNX_EPS) * lng_ref[...] + lnb_ref[...]
    bonus = head_sum(r * k2 * rk_ref[...]) * v
    o_ref[...] = ((yn + bonus) * g).reshape(bt, C, A_W)

    @pl.when(c == nc - 1)
    def _():
        sout_ref[...] = s_ref[...].reshape(bt, NP, 2 * A_HD, 2 * A_HD)


RWKV_BATCH_TILE = 4


def _rwkv(proj, ulast, s0, prm, t_valid=None):
    b, t, _ = proj.shape
    C = min(RWKV_CHUNK, t)
    bt = min(RWKV_BATCH_TILE, b)
    nc = t // C
    vec = lambda n: pl.BlockSpec((1, n), lambda i, c: (0, 0))
    mat = lambda m, n: pl.BlockSpec((m, n), lambda i, c: (0, 0))
    col = lambda width, off: pl.BlockSpec((bt, C, width), lambda i, c: (i, c, off // width))
    n_shift = 3 * A_W + A_LORA
    n_pairs = bt * (A_HEADS // 2)
    kern = functools.partial(_rwkv_kernel, chunk=C, bt=bt, t_valid=t_valid)
    return pl.pallas_call(
        kern,
        grid=(b // bt, nc),
        in_specs=[col(A_W, P_R), col(A_W, P_K), col(A_W, P_V), col(A_LORA, P_LORA),
                  pl.BlockSpec((bt, 1, n_shift), lambda i, c: (i, 0, 0)),
                  pl.BlockSpec((bt, 4, 128, 128), lambda i, c: (i, 0, 0, 0)),
                  vec(n_shift), vec(A_W), mat(A_LORA_W, A_W), vec(A_W), mat(A_LORA_A, A_W),
                  mat(A_LORA_G, A_W), vec(A_W), vec(A_W), vec(A_W), vec(A_W), vec(A_W)],
        out_specs=[pl.BlockSpec((bt, C, A_W), lambda i, c: (i, c, 0)),
                   pl.BlockSpec((bt, 4, 128, 128), lambda i, c: (i, 0, 0, 0))],
        out_shape=[jax.ShapeDtypeStruct((b, t, A_W), F32),
                   jax.ShapeDtypeStruct((b, 4, 128, 128), F32)],
        scratch_shapes=[pltpu.VMEM((n_pairs, 128, 128), F32), pltpu.VMEM((bt, n_shift), F32),
                        pltpu.VMEM((n_pairs, 4 * C, 128), BF16), pltpu.VMEM((n_pairs, 4 * C, 128), BF16),
                        pltpu.VMEM((n_pairs, 2 * C, 128), BF16), pltpu.VMEM((bt * C, A_W), F32)],
        compiler_params=_cparams(("parallel", "arbitrary")),
        name="rwkv_chunk",
    )(proj, proj, proj, proj, ulast, s0, *prm)


def _pair_states(s):
    b = s.shape[0]
    s = s.reshape(b, 4, 2, A_HD, A_HD)
    z = jnp.zeros_like(s[:, :, 0])
    top = jnp.concatenate([s[:, :, 0], z], axis=-1)
    bot = jnp.concatenate([z, s[:, :, 1]], axis=-1)
    return jnp.concatenate([top, bot], axis=-2)


def _unpair_states(sp):
    b = sp.shape[0]
    h0 = sp[:, :, :A_HD, :A_HD]
    h1 = sp[:, :, A_HD:, A_HD:]
    return jnp.stack([h0, h1], axis=2).reshape(b, A_HEADS, A_HD, A_HD)


def _pool_kernel(u_ref, halo_ref, hist_ref, pm_ref, ps_ref, o_ref, ext_ref, *, tm, pos0):
    i = pl.program_id(1)

    @pl.when(i == 0)
    def _():
        ext_ref[0:POOL_HALO, :] = hist_ref[0]

    @pl.when(i > 0)
    def _():
        ext_ref[0:POOL_HALO, :] = halo_ref[0]

    cur = u_ref[0]
    ext_ref[POOL_HALO:POOL_HALO + tm, :] = cur
    pos = pos0 + i * tm + lax.broadcasted_iota(jnp.int32, (tm, 1), 0)
    outs = []
    for gi, win in enumerate(POOL_WINDOWS):
        lo, hi = gi * B_GW, (gi + 1) * B_GW
        s = cur[:, lo:hi]
        for back in range(1, win):
            s = s + ext_ref[POOL_HALO - back:POOL_HALO - back + tm, lo:hi]
        cnt = jnp.minimum(win, pos + 1).astype(F32)
        d = s / cnt - cur[:, lo:hi]
        outs.append(_dot(d, pm_ref[gi]))
    o_ref[0] = jnp.concatenate(outs, axis=1) * ps_ref[...]


def _pool(proj, hist, pm, ps, pos0, tm):
    b, t, _ = proj.shape
    per = tm // POOL_HALO
    kern = functools.partial(_pool_kernel, tm=tm, pos0=pos0)
    return pl.pallas_call(
        kern,
        grid=(b, t // tm),
        in_specs=[pl.BlockSpec((1, tm, B_W), lambda bi, i: (bi, i, P_POOL // B_W)),
                  pl.BlockSpec((1, POOL_HALO, B_W),
                               lambda bi, i: (bi, jnp.maximum(i * per - 1, 0), P_POOL // B_W)),
                  pl.BlockSpec((1, POOL_HALO, B_W), lambda bi, i: (bi, 0, 0)),
                  pl.BlockSpec((B_GROUPS, B_GW, B_GW), lambda bi, i: (0, 0, 0)),
                  pl.BlockSpec((1, B_W), lambda bi, i: (0, 0))],
        out_specs=pl.BlockSpec((1, tm, B_W), lambda bi, i: (bi, i, 0)),
        out_shape=jax.ShapeDtypeStruct((b, t, B_W), F32),
        scratch_shapes=[pltpu.VMEM((tm + POOL_HALO, B_W), F32)],
        compiler_params=_cparams(("parallel", "arbitrary")),
        name="pool_mix",
    )(proj, proj, hist, pm, ps.reshape(1, B_W))


def _merge_kernel(x_ref, oa_ref, ob_ref, oc_ref, mg_ref, wa_ref, wb_ref, wc_ref, wo_ref, o_ref):
    gate = lambda n: _sigmoid(mg_ref[:, n * D_MODEL:(n + 1) * D_MODEL])
    h = (gate(0) * _dot(oa_ref[...], wa_ref[...]) + gate(1) * _dot(ob_ref[...], wb_ref[...])
         + gate(2) * _dot(oc_ref[...], wc_ref[...]))
    o_ref[...] = x_ref[...] + _dot(h, wo_ref[...])


def _merge(x, oa, ob, oc, gates, wa, wb, wc, wo, tm):
    m = x.shape[0]
    row = lambda n: pl.BlockSpec((tm, n), lambda i: (i, 0))
    full = lambda a: pl.BlockSpec(a.shape, lambda i: (0, 0))
    return pl.pallas_call(
        _merge_kernel,
        grid=(m // tm,),
        in_specs=[row(D_MODEL), row(A_W), row(B_W), row(C_W), row(3 * D_MODEL),
                  full(wa), full(wb), full(wc), full(wo)],
        out_specs=row(D_MODEL),
        out_shape=jax.ShapeDtypeStruct((m, D_MODEL), F32),
        compiler_params=_cparams(("parallel",)),
        name="merge_branches",
    )(x, oa, ob, oc, gates, wa, wb, wc, wo)


def _mem_kv_kernel(mem_ref, g_ref, wk_ref, wv_ref, kg_ref, k_ref, v_ref):
    mn = _rms(mem_ref[...], g_ref[...]).astype(BF16)
    k = jnp.dot(mn, wk_ref[...], preferred_element_type=F32)
    v_ref[...] = jnp.dot(mn, wv_ref[...], preferred_element_type=F32)
    k_ref[...] = jnp.concatenate(
        [_rms(k[:, h * M_HD:(h + 1) * M_HD], kg_ref[...]) for h in range(M_HEADS)], axis=1)


def _mem_kv(mem, g, wk, wv, kg):
    m = mem.shape[0]
    tm = N_MEM
    row = lambda n: pl.BlockSpec((tm, n), lambda i: (i, 0))
    full = lambda a: pl.BlockSpec(a.shape, lambda i: (0, 0))
    g2, kg2 = g.reshape(1, D_MODEL), kg.reshape(1, M_HD)
    return pl.pallas_call(
        _mem_kv_kernel,
        grid=(m // tm,),
        in_specs=[row(D_MODEL), full(g2), full(wk), full(wv), full(kg2)],
        out_specs=[row(M_W), row(M_W)],
        out_shape=[jax.ShapeDtypeStruct((m, M_W), F32), jax.ShapeDtypeStruct((m, M_W), F32)],
        compiler_params=_cparams(("parallel",)),
        name="mem_kv",
    )(mem, g2, wk, wv, kg2)


def _mem_ffn_kernel(x_ref, mk_ref, mv_ref, gx_ref, wq_ref, qg_ref, wo_ref, gf_ref, w1_ref, w2_ref,
                    o_ref, xn_ref, *, tm, rows_per_batch, nkb):
    j = pl.program_id(1)

    @pl.when(j == 0)
    def _():
        x = x_ref[...]
        q = _dot(_rms(x, gx_ref[...]), wq_ref[...])
        mk = mk_ref[...].reshape(nkb * N_MEM, M_W)
        mv = mv_ref[...].reshape(nkb * N_MEM, M_W)
        if nkb > 1:
            qb = lax.broadcasted_iota(jnp.int32, (tm, nkb * N_MEM), 0) // rows_per_batch
            kb = lax.broadcasted_iota(jnp.int32, (tm, nkb * N_MEM), 1) // N_MEM
            same = qb == kb
        outs = []
        for h in range(M_HEADS):
            sl = slice(h * M_HD, (h + 1) * M_HD)
            qh = _rms(q[:, sl], qg_ref[...])
            logits = _dot_nt(qh, mk[:, sl]) * (M_HD ** -0.5)
            if nkb > 1:
                logits = jnp.where(same, logits, NEG)
            mx = jnp.max(logits, axis=-1, keepdims=True)
            p = jnp.exp(logits - mx)
            p = p / jnp.sum(p, axis=-1, keepdims=True)
            outs.append(_dot(p, mv[:, sl]))
        o = jnp.concatenate(outs, axis=1)
        xm = x + _dot(o, wo_ref[...])
        o_ref[...] = xm
        xn_ref[...] = _rms(xm, gf_ref[...]).astype(BF16)

    h1 = jnp.dot(xn_ref[...], w1_ref[...], preferred_element_type=F32)
    h1 = jnp.square(jnp.maximum(h1, 0.0))
    o_ref[...] += _dot(h1, w2_ref[...])


def _mem_ffn(x, mk, mv, gx, wq, qg, wo, gf, w1, w2, tm, rows_per_batch, tf=1024):
    m = x.shape[0]
    nkb = max(tm // rows_per_batch, 1)
    kern = functools.partial(_mem_ffn_kernel, tm=tm, rows_per_batch=rows_per_batch, nkb=nkb)
    full = lambda a: pl.BlockSpec(a.shape, lambda i, j: (0,) * a.ndim)
    gx2, qg2, gf2 = gx.reshape(1, D_MODEL), qg.reshape(1, M_HD), gf.reshape(1, D_MODEL)
    mem_spec = pl.BlockSpec((nkb, N_MEM, M_W), lambda i, j: ((i * tm) // (rows_per_batch * nkb), 0, 0))
    return pl.pallas_call(
        kern,
        grid=(m // tm, D_FF // tf),
        in_specs=[pl.BlockSpec((tm, D_MODEL), lambda i, j: (i, 0)), mem_spec, mem_spec,
                  full(gx2), full(wq), full(qg2), full(wo), full(gf2),
                  pl.BlockSpec((D_MODEL, tf), lambda i, j: (0, j)),
                  pl.BlockSpec((tf, D_MODEL), lambda i, j: (j, 0))],
        out_specs=pl.BlockSpec((tm, D_MODEL), lambda i, j: (i, 0)),
        out_shape=jax.ShapeDtypeStruct((m, D_MODEL), F32),
        scratch_shapes=[pltpu.VMEM((tm, D_MODEL), BF16)],
        compiler_params=_cparams(("parallel", "arbitrary")),
        name="mem_ffn",
    )(x, mk, mv, gx2, wq, qg2, wo, gf2, w1, w2)


def _half_ones():
    i = lax.broadcasted_iota(jnp.int32, (LANES, LANES), 0) // C_HD
    j = lax.broadcasted_iota(jnp.int32, (LANES, LANES), 1) // C_HD
    return (i == j).astype(BF16)


def _rms_halves(x, g, ones):
    ms = _dot_exact_rhs(x * x, ones) * (1.0 / C_HD)
    return x * lax.rsqrt(ms + RMS_EPS) * g


def _nsa_prep_kernel(q_ref, c_ref, s_ref, w_ref, qg_ref, kg_ref,
                     qn_ref, ks16_ref, vs16_ref, kw16_ref, vw16_ref, kc_ref, vc_ref, rows_ref, *, tm):
    ones = _half_ones()
    scale = C_HD ** -0.5 * LOG2E
    q = q_ref[0]
    for n in range(C_HPG):
        qn_ref[0, n] = (_rms_halves(q[:, n * LANES:(n + 1) * LANES], qg_ref[...], ones) * scale).T.astype(BF16)
    c = c_ref[0]
    rows_ref[0, 0] = c[:, :LANES].T
    rows_ref[0, 1] = c[:, LANES:].T
    s = s_ref[0]
    ks = _rms_halves(s[:, :LANES], kg_ref[1:2, :], ones)
    vs_t = s[:, LANES:].T
    ks16_ref[0] = ks.astype(BF16)
    vs16_ref[0] = vs_t.astype(BF16)
    rows_ref[0, 2] = ks.T
    rows_ref[0, 3] = vs_t
    w = w_ref[0]
    kw = _rms_halves(w[:, :LANES], kg_ref[2:3, :], ones)
    vw_t = w[:, LANES:].T
    kw16_ref[0] = kw.astype(BF16)
    vw16_ref[0] = vw_t.astype(BF16)
    rows_ref[0, 4] = kw.T
    rows_ref[0, 5] = vw_t
    nb = tm // CMP_BLOCK
    bi = lax.broadcasted_iota(jnp.int32, (nb, tm), 0)
    ti = lax.broadcasted_iota(jnp.int32, (nb, tm), 1) // CMP_BLOCK
    pool = (bi == ti).astype(BF16)
    means = _dot_exact_lhs(pool, c) * (1.0 / CMP_BLOCK)
    kc_ref[0] = _rms_halves(means[:, :LANES], kg_ref[0:1, :], ones)
    vc_ref[0] = means[:, LANES:]


def _nsa_prep(proj, qg, kg, tm):
    b, t, _ = proj.shape
    nbt = tm // CMP_BLOCK
    kern = functools.partial(_nsa_prep_kernel, tm=tm)
    tok = lambda n, dt: jax.ShapeDtypeStruct((b, t, n), dt)
    blk = jax.ShapeDtypeStruct((b, t // CMP_BLOCK, LANES), F32)
    col = lambda width, off: pl.BlockSpec((1, tm, width), lambda bi, i: (bi, i, off // width))
    out_tok = lambda n: pl.BlockSpec((1, tm, n), lambda bi, i: (bi, i, 0))
    out_blk = pl.BlockSpec((1, nbt, LANES), lambda bi, i: (bi, i, 0))
    out_t = pl.BlockSpec((1, LANES, tm), lambda bi, i: (bi, 0, i))
    tok_t = jax.ShapeDtypeStruct((b, LANES, t), BF16)
    return pl.pallas_call(
        kern,
        grid=(b, t // tm),
        in_specs=[col(C_W, P_Q), col(2 * LANES, P_KV), col(2 * LANES, P_KV + 2 * LANES),
                  col(2 * LANES, P_KV + 4 * LANES),
                  pl.BlockSpec((1, LANES), lambda bi, i: (0, 0)),
                  pl.BlockSpec((3, LANES), lambda bi, i: (0, 0))],
        out_specs=[pl.BlockSpec((1, C_HPG, LANES, tm), lambda bi, i: (bi, 0, 0, i)),
                   out_tok(LANES), out_t, out_tok(LANES), out_t, out_blk, out_blk,
                   pl.BlockSpec((1, 6, LANES, tm), lambda bi, i: (bi, 0, 0, i))],
        out_shape=[jax.ShapeDtypeStruct((b, C_HPG, LANES, t), BF16),
                   tok(LANES, BF16), tok_t, tok(LANES, BF16), tok_t, blk, blk,
                   jax.ShapeDtypeStruct((b, 6, LANES, t), F32)],
        compiler_params=_cparams(("parallel", "parallel")),
        name="nsa_prep",
    )(proj, proj, proj, proj, jnp.tile(qg.reshape(1, C_HD), (1, 2)), jnp.tile(kg, (1, 2)))


M_INIT = -1e29
ROWS = C_HPG * Q_TILE
FAR_TILES = 8
FRONT_TILES = max(WINDOW // Q_TILE, FAR_TILES)
SEL_PAD = 2 * FRONT_TILES
SUM_ROWS = 16
LOG2E = 1.4426950408889634


def _heads(x):
    return jnp.concatenate([x] * C_HPG, axis=1)


def _softmax_step(qts, k, v_t, state, tail_bias=None, valid=None, block_bias=None):
    m_ref, l_ref, acc_ref = state
    n_keys = k.shape[0]
    old = [(m_ref[g], l_ref[g], acc_ref[g]) for g in range(C_KV)]
    v_ext = jnp.concatenate([v_t, jnp.ones((SUM_ROWS, n_keys), BF16)], axis=0)
    if block_bias is not None:
        blk = lax.broadcasted_iota(jnp.int32, (n_keys, LANES), 0) // CMP_BLOCK
        col = lax.broadcasted_iota(jnp.int32, (n_keys, LANES), 1)
        k = jnp.concatenate([k, (blk == col).astype(BF16)], axis=1)
    new = []
    for g in range(C_KV):
        m_old, l_old, acc_old = old[g]
        rhs = qts[g] if block_bias is None else jnp.concatenate([qts[g], block_bias[g]], axis=0)
        s = jnp.dot(k, rhs, preferred_element_type=F32)
        if tail_bias is not None:
            cut = s.shape[0] - tail_bias[g].shape[0]
            tail = s[cut:] + tail_bias[g]
            s = jnp.concatenate([s[:cut], tail], axis=0) if cut else tail
        if valid is not None:
            s = jnp.where(_heads(valid[g]), s, NEG)
        s = s.astype(BF16)
        m_new = jnp.maximum(m_old, jnp.max(s, axis=0, keepdims=True).astype(F32))
        alpha = jnp.exp2(m_old - m_new)
        p = jnp.exp2(s - m_new.astype(BF16))
        pv = jnp.dot(v_ext, p, preferred_element_type=F32)
        new.append((m_new, alpha * l_old + pv[LANES:LANES + 1], alpha * acc_old + pv[:LANES]))
    for g in range(C_KV):
        m_ref[g], l_ref[g], acc_ref[g] = new[g]


def _softmax_reset(state):
    m_ref, l_ref, acc_ref = state
    m_ref[...] = jnp.full(m_ref.shape, M_INIT, F32)
    l_ref[...] = jnp.zeros(l_ref.shape, F32)
    acc_ref[...] = jnp.zeros(acc_ref.shape, F32)


def _softmax_result(state, g):
    _, l_ref, acc_ref = state
    l = l_ref[g]
    return acc_ref[g] / jnp.where(l > 0.0, l, 1.0)


def _top_blocks(score, n_sel):
    nb = score.shape[0]
    blk = lax.broadcasted_iota(jnp.int32, score.shape, 0).astype(F32)
    sel = jnp.zeros(score.shape, F32)
    work = score
    for _ in range(min(n_sel, nb)):
        mx = jnp.max(work, axis=0, keepdims=True)
        idx = jnp.min(jnp.where(work == mx, blk, float(nb)), axis=0, keepdims=True)
        pick = blk == idx
        sel = jnp.where(pick, 1.0, sel)
        work = jnp.where(pick, -jnp.inf, work)
    return sel


def _nsa_attn_kernel(qt_ref, ks_ref, vs_ref, kw_ref, vw_ref, kc_ref, vc_ref, cg_ref, bnear_ref,
                     bcmp_ref, o_ref, m_ref, l_ref, acc_ref, sel_ref, *, nb):
    i = pl.program_id(1)
    lane = lax.broadcasted_iota(jnp.int32, (Q_TILE, LANES), 1)
    key = lax.broadcasted_iota(jnp.int32, (Q_TILE, LANES), 0)
    causal = key <= lane
    sig_t = _sigmoid(cg_ref[0]).T
    kc = kc_ref[0].astype(BF16)
    vc_t = vc_ref[0].T.astype(BF16)
    blk = lax.broadcasted_iota(jnp.int32, (nb, Q_TILE), 0)
    qpos = i * Q_TILE + lax.broadcasted_iota(jnp.int32, (nb, Q_TILE), 1)
    cur = qpos // CMP_BLOCK
    cmp_valid = _heads(qpos >= blk * CMP_BLOCK + (CMP_BLOCK - 1))
    forced = (blk == 0) | (blk == cur) | (blk == cur - 1)
    state = (m_ref, l_ref, acc_ref)
    n_back = WINDOW // Q_TILE
    k_rows = lambda ref, j, n: ref[0, pl.ds(pl.multiple_of((j + FRONT_TILES) * Q_TILE, Q_TILE), n * Q_TILE), :]
    v_cols = lambda ref, j, n: ref[0, :, pl.ds(pl.multiple_of((j + FRONT_TILES) * Q_TILE, Q_TILE), n * Q_TILE)]

    def key_mask(g, j, n):
        rows = [sel_ref[g, pl.ds(SEL_PAD + 2 * j + r, 1), :] for r in range(2 * n)]
        return jnp.concatenate(
            [jnp.where(key < CMP_BLOCK, rows[2 * t], rows[2 * t + 1]) for t in range(n)], axis=0) > 0.5

    def block_bias(g, j, n, tiles=None):
        rows = sel_ref[g, pl.ds(SEL_PAD + 2 * j, 2 * n), :]
        if tiles is not None:
            t = j + lax.broadcasted_iota(jnp.int32, (2 * n, Q_TILE), 0) // 2
            rows = jnp.where((t >= tiles[0]) & (t < tiles[1]), rows, 0.0)
        bias = _heads((rows - 1.0) * (-NEG))
        return jnp.concatenate([bias, jnp.zeros((LANES - 2 * n, ROWS), F32)], axis=0).astype(BF16)

    q_t = jnp.concatenate([qt_ref[0, n] for n in range(C_HPG)], axis=1)
    feat = lax.broadcasted_iota(jnp.int32, (LANES, ROWS), 0) // C_HD
    qts = [jnp.where(feat == g, q_t, jnp.zeros_like(q_t)) for g in range(C_KV)]

    o_cmp, scores = [], []
    for g in range(C_KV):
        bias_c = pltpu.roll(bcmp_ref[g], (2 * i + 2) % LANES, 0)[:nb, :]
        lc = jnp.dot(kc, qts[g], preferred_element_type=F32) + bias_c
        lc = jnp.where(cmp_valid, lc, NEG)
        mc = jnp.maximum(jnp.max(lc, axis=0, keepdims=True), M_INIT)
        pc = jnp.exp2(lc - mc)
        den = jnp.sum(pc, axis=0, keepdims=True)
        pc = pc / jnp.where(den > 0.0, den, 1.0)
        o_cmp.append(jnp.dot(vc_t, pc.astype(BF16), preferred_element_type=F32))
        psum = sum(pc[:, n * Q_TILE:(n + 1) * Q_TILE] for n in range(C_HPG))
        scores.append(jnp.where(blk > cur, NEG, jnp.where(forced, FORCE_SCORE, psum)))

    score = jnp.concatenate(scores, axis=1)
    sel = jnp.where(score > 0.5 * NEG, _top_blocks(score, N_SEL), 0.0)
    for g in range(C_KV):
        sel_ref[g, 0:SEL_PAD, :] = jnp.zeros((SEL_PAD, Q_TILE), F32)
        sel_ref[g, SEL_PAD:SEL_PAD + nb, :] = sel[:, g * Q_TILE:(g + 1) * Q_TILE]

    _softmax_reset(state)
    n_far = jnp.maximum(i - 1, 0)

    def far_step(j, tiles=None):
        _softmax_step(qts, k_rows(ks_ref, j, FAR_TILES), v_cols(vs_ref, j, FAR_TILES), state,
                      block_bias=[block_bias(g, j, FAR_TILES, tiles) for g in range(C_KV)])

    def far_many(jq, carry):
        far_step(FAR_TILES * jq)
        return carry

    n_many = n_far // FAR_TILES
    lax.fori_loop(0, n_many, far_many, 0)

    @pl.when(n_far % FAR_TILES != 0)
    def _():
        far_step(n_far - FAR_TILES, (FAR_TILES * n_many, n_far))

    near_causal = jnp.concatenate([key >= 0, causal], axis=0)
    bias2 = [bnear_ref[g] for g in range(C_KV)]
    _softmax_step(qts, k_rows(ks_ref, i - 1, 2), v_cols(vs_ref, i - 1, 2), state, tail_bias=bias2,
                  valid=[key_mask(g, i - 1, 2) & near_causal for g in range(C_KV)])
    o_slc = [_softmax_result(state, g) for g in range(C_KV)]

    _softmax_reset(state)
    exists = lambda back: key >= jnp.where(i >= back, 0, Q_TILE)
    win_valid = jnp.concatenate(
        [(key > lane) & exists(n_back)] + [exists(back) for back in range(n_back - 1, 0, -1)] + [causal],
        axis=0)
    _softmax_step(qts, k_rows(kw_ref, i - n_back, n_back + 1), v_cols(vw_ref, i - n_back, n_back + 1), state,
                  tail_bias=bias2, valid=[win_valid] * C_KV)
    o_win = [_softmax_result(state, g) for g in range(C_KV)]

    low = (key // C_HD) == 0
    for n in range(C_HPG):
        cols = slice(n * Q_TILE, (n + 1) * Q_TILE)
        per_group = []
        for g in range(C_KV):
            h = g * C_HPG + n
            gate = lambda br: sig_t[br * C_HEADS + h:br * C_HEADS + h + 1, :]
            per_group.append(gate(0) * o_cmp[g][:, cols] + gate(1) * o_slc[g][:, cols]
                             + gate(2) * o_win[g][:, cols])
        o_ref[0, :, n * LANES:(n + 1) * LANES] = jnp.where(low, per_group[0], per_group[1]).T


def _nsa_attn(qt, ks16, vs16t, kw16, vw16t, kc, vc, gates, bnear, bcmp):
    b, _, _, t = qt.shape
    nb = t // CMP_BLOCK
    tp = t + FRONT_TILES * Q_TILE
    kern = functools.partial(_nsa_attn_kernel, nb=nb)
    seq = pl.BlockSpec((1, tp, LANES), lambda bi, i: (bi, 0, 0))
    seq_t = pl.BlockSpec((1, LANES, tp), lambda bi, i: (bi, 0, 0))
    blk = pl.BlockSpec((1, nb, LANES), lambda bi, i: (bi, 0, 0))
    return pl.pallas_call(
        kern,
        grid=(b, t // Q_TILE),
        in_specs=[pl.BlockSpec((1, C_HPG, LANES, Q_TILE), lambda bi, i: (bi, 0, 0, i)),
                  seq, seq_t, seq, seq_t, blk, blk,
                  pl.BlockSpec((1, Q_TILE, LANES), lambda bi, i: (bi, i, G_CG // LANES)),
                  pl.BlockSpec(bnear.shape, lambda bi, i: (0, 0, 0)),
                  pl.BlockSpec(bcmp.shape, lambda bi, i: (0, 0, 0))],
        out_specs=pl.BlockSpec((1, Q_TILE, C_W), lambda bi, i: (bi, i, 0)),
        out_shape=jax.ShapeDtypeStruct((b, t, C_W), F32),
        scratch_shapes=[pltpu.VMEM((C_KV, 1, ROWS), F32), pltpu.VMEM((C_KV, 1, ROWS), F32),
                        pltpu.VMEM((C_KV, LANES, ROWS), F32),
                        pltpu.VMEM((C_KV, SEL_PAD + nb, Q_TILE), F32)],
        compiler_params=_cparams(("parallel", "arbitrary")),
        name="nsa_attn",
    )(qt, ks16, vs16t, kw16, vw16t, kc, vc, gates, bnear, bcmp)


def _rel_bucket(dist):
    n = jnp.maximum(dist, 0)
    exact = N_BUCKETS // 2
    ratio = jnp.log(jnp.maximum(n, 1).astype(F32) / exact) / math.log(MAX_DIST / exact)
    large = jnp.minimum(exact + (ratio * (N_BUCKETS - exact)).astype(jnp.int32), N_BUCKETS - 1)
    return jnp.where(n < exact, n, large)


def _one_hot(idx, n):
    return (jnp.clip(idx, 0, n - 1)[..., None] == jnp.arange(n)).astype(F32)


def _distance_table(rel_bias):
    tbl = jnp.dot(_one_hot(_rel_bucket(jnp.arange(MAX_DIST)), N_BUCKETS), rel_bias,
                  precision=lax.Precision.HIGHEST)
    return (tbl - tbl[MAX_DIST - 1:MAX_DIST, :]).T


def _bias_tables(rel_bias):
    tbl = _distance_table(rel_bias) * LOG2E
    tbl = tbl.reshape(C_KV, C_HPG, MAX_DIST)
    qq = jnp.arange(Q_TILE)[:, None]
    cc = jnp.arange(LANES)[None, :]
    look = lambda dist: jnp.einsum('ghd,qcd->ghqc', tbl, _one_hot(dist, MAX_DIST),
                                   precision=lax.Precision.HIGHEST).reshape(C_KV, ROWS, LANES)
    bnear = jnp.stack([look(qq - cc), look(qq - cc + Q_TILE)], axis=1)
    bcmp = look(qq - CMP_BLOCK * cc + CMP_BLOCK * (LANES - 1) - (LANES - 1))
    bnear = bnear.transpose(0, 1, 3, 2)
    return jnp.concatenate([bnear[:, 1], bnear[:, 0]], axis=1), bcmp.transpose(0, 2, 1)


PAGES_PER_STEP = 16
S_ROWS = C_KV * C_HPG
BLK_LANES = 384
NEW_PAD = 8


def _page_specs(layer, n):
    spec = lambda j: pl.BlockSpec((1, 1, LANES, PAGE_SIZE),
                                  lambda b, c, pt, j=j: (layer, pt[b, c * n + j], 0, 0))
    return [spec(j) for j in range(n)]


def _cmp_pages_kernel(pt_ref, *refs, n):
    k_refs, v_refs, (kc_ref, vc_ref) = refs[:n], refs[n:2 * n], refs[2 * n:]
    row = lax.broadcasted_iota(jnp.int32, (n * PAGE_SIZE, 2 * n), 0) // CMP_BLOCK
    col = lax.broadcasted_iota(jnp.int32, (n * PAGE_SIZE, 2 * n), 1)
    pool = (row == col).astype(BF16)
    for refs_in, out in ((k_refs, kc_ref), (v_refs, vc_ref)):
        pages = jnp.concatenate([r[0, 0] for r in refs_in], axis=1)
        out[0, 0] = _dot_exact_rhs(pages, pool) * (1.0 / CMP_BLOCK)


def _cmp_pages(page_table, cache_k, cache_v, layer):
    b, n_pages = page_table.shape
    n = min(PAGES_PER_STEP, n_pages)
    chunks = n_pages // n
    out = jax.ShapeDtypeStruct((b, chunks, LANES, 2 * n), F32)
    out_spec = pl.BlockSpec((1, 1, LANES, 2 * n), lambda bi, c, pt: (bi, c, 0, 0))
    return pl.pallas_call(
        functools.partial(_cmp_pages_kernel, n=n),
        grid_spec=pltpu.PrefetchScalarGridSpec(
            num_scalar_prefetch=1, grid=(b, chunks),
            in_specs=_page_specs(layer, n) + _page_specs(layer, n),
            out_specs=[out_spec, out_spec]),
        out_shape=[out, out],
        compiler_params=_cparams(("parallel", "arbitrary")),
        name="nsa_sample_cmp_pages",
    )(page_table, *([cache_k] * n), *([cache_v] * n))


def _top_blocks_lanes(score, n_sel):
    nb = score.shape[-1]
    blk = lax.broadcasted_iota(jnp.int32, score.shape, 1).astype(F32)
    sel = jnp.zeros(score.shape, F32)
    work = score
    for _ in range(min(n_sel, nb)):
        mx = jnp.max(work, axis=-1, keepdims=True)
        idx = jnp.min(jnp.where(work == mx, blk, float(nb)), axis=-1, keepdims=True)
        pick = blk == idx
        sel = jnp.where(pick, 1.0, sel)
        work = jnp.where(pick, -jnp.inf, work)
    return sel


def _nsa_sample_head_kernel(q_ref, newc_ref, news_ref, neww_ref, kc_ref, vc_ref, wk_ref, wv_ref,
                            qg_ref, kg_ref, kgc_ref, bc_ref, bw_ref, bn_ref,
                            q32_ref, sel_ref, ocmp_ref, owin_ref, new_ref, q_scr, *, ts, past, wbuf):
    n_rows = S_ROWS * ts
    ones = _half_ones()
    scale = C_HD ** -0.5
    lane = lax.broadcasted_iota(jnp.int32, (ts, LANES), 1)
    q = q_ref[0]
    for n in range(C_HPG):
        qn = _rms_halves(q[:, n * LANES:(n + 1) * LANES], qg_ref[...], ones) * scale
        for g in range(C_KV):
            lo = (g * C_HPG + n) * ts
            q_scr[lo:lo + ts, :] = jnp.where((lane // C_HD) == g, qn, 0.0)
    q32 = q_scr[...].astype(BF16)
    q32_ref[0] = q32

    newc, news, neww = newc_ref[0], news_ref[0], neww_ref[0]
    ks_new = _rms_halves(news[:, :LANES], kg_ref[1:2, :], ones)
    kw_new = _rms_halves(neww[:, :LANES], kg_ref[2:3, :], ones)
    new_ref[0, 0] = newc[:, :LANES]
    new_ref[0, 1] = newc[:, LANES:]
    new_ref[0, 2] = ks_new
    new_ref[0, 3] = news[:, LANES:]
    new_ref[0, 4] = kw_new
    new_ref[0, 5] = neww[:, LANES:]

    t_of_row = lax.broadcasted_iota(jnp.int32, (n_rows, 1), 0) % ts
    qpos = past + t_of_row

    n_cached = past // CMP_BLOCK
    pad_rows = jnp.zeros((NEW_PAD - ts, LANES), F32)
    ones_rows = jnp.full((NEW_PAD, BLK_LANES), 1.0, BF16)
    new_mean = lambda rows: _dot_tn_exact(jnp.concatenate([rows, pad_rows], axis=0),
                                          ones_rows) * (1.0 / CMP_BLOCK)
    is_new = lax.broadcasted_iota(jnp.int32, (LANES, BLK_LANES), 1) == n_cached
    kc_all = jnp.where(is_new, new_mean(newc[:, :LANES]), kc_ref[0])
    vc_all = jnp.where(is_new, new_mean(newc[:, LANES:]), vc_ref[0])
    feat = lax.broadcasted_iota(jnp.int32, (LANES, 1), 0) // C_HD
    sq = kc_all * kc_all
    ms = jnp.where(feat == 0, jnp.sum(sq[:C_HD], axis=0, keepdims=True),
                   jnp.sum(sq[C_HD:], axis=0, keepdims=True)) * (1.0 / C_HD)
    kc_n = kc_all * lax.rsqrt(ms + RMS_EPS) * kgc_ref[:, 0:1]
    blk = lax.broadcasted_iota(jnp.int32, (n_rows, BLK_LANES), 1)
    lc = jnp.dot(q32, kc_n.astype(BF16), preferred_element_type=F32) + bc_ref[...]
    lc = jnp.where(qpos >= blk * CMP_BLOCK + (CMP_BLOCK - 1), lc, NEG)
    mc = jnp.maximum(jnp.max(lc, axis=-1, keepdims=True), M_INIT)
    pc = jnp.exp(lc - mc)
    den = jnp.sum(pc, axis=-1, keepdims=True)
    pc = pc / jnp.where(den > 0.0, den, 1.0)
    ocmp_ref[0] = lax.dot_general(pc.astype(BF16), vc_all.astype(BF16), (((1,), (1,)), ((), ())),
                                  preferred_element_type=F32)

    gi = lax.broadcasted_iota(jnp.int32, (C_KV * ts, n_rows), 0)
    ri = lax.broadcasted_iota(jnp.int32, (C_KV * ts, n_rows), 1)
    same = ((gi // ts) == (ri // (C_HPG * ts))) & ((gi % ts) == (ri % ts))
    psum = _dot_exact_lhs(same.astype(BF16), pc, terms=3)
    sblk = lax.broadcasted_iota(jnp.int32, (C_KV * ts, BLK_LANES), 1)
    scur = (past + lax.broadcasted_iota(jnp.int32, (C_KV * ts, 1), 0) % ts) // CMP_BLOCK
    forced = (sblk == 0) | (sblk == scur) | (sblk == scur - 1)
    score = jnp.where(sblk > scur, NEG, jnp.where(forced, FORCE_SCORE, psum))
    sel = jnp.where(score > 0.5 * NEG, _top_blocks_lanes(score, N_SEL), 0.0)
    sel_ref[0] = lax.dot_general(same.astype(BF16), sel.astype(BF16), (((0,), (0,)), ((), ())),
                                 preferred_element_type=F32)

    key = lax.broadcasted_iota(jnp.int32, (n_rows, wbuf), 1)
    lw = jnp.dot(q32, wk_ref[0, 0].astype(BF16), preferred_element_type=F32) + bw_ref[...]
    kpos = past - wbuf + key
    dist = qpos - kpos
    lw = jnp.where((dist >= 0) & (dist < WINDOW) & (kpos >= 0), lw, NEG)
    kw_pad = jnp.concatenate([kw_new, pad_rows], axis=0)
    vw_pad = jnp.concatenate([neww[:, LANES:], pad_rows], axis=0)
    nkey = lax.broadcasted_iota(jnp.int32, (n_rows, NEW_PAD), 1)
    ln = lax.dot_general(q32, kw_pad.astype(BF16), (((1,), (1,)), ((), ())),
                         preferred_element_type=F32) + bn_ref[...]
    ln = jnp.where((nkey <= t_of_row) & (nkey < ts), ln, NEG)
    mw = jnp.maximum(jnp.max(lw, axis=-1, keepdims=True), jnp.max(ln, axis=-1, keepdims=True))
    pw, pn = jnp.exp(lw - mw), jnp.exp(ln - mw)
    den = jnp.sum(pw, axis=-1, keepdims=True) + jnp.sum(pn, axis=-1, keepdims=True)
    ow = lax.dot_general(pw.astype(BF16), wv_ref[0, 0].astype(BF16), (((1,), (1,)), ((), ())),
                         preferred_element_type=F32) + _dot(pn, vw_pad)
    owin_ref[0] = ow / den


def _dot_tn_exact(a, b01, terms=2):
    dims = (((0,), (0,)), ((), ()))
    return sum(lax.dot_general(p, b01, dims, preferred_element_type=F32) for p in _split(a, terms))


def _nsa_sample_head(main, kc_t, vc_t, win_k, win_v, layer, qg, kg, tables, past):
    b, ts, _ = main.shape
    wbuf = win_k.shape[-1]
    n_rows = S_ROWS * ts
    bias_c, bias_w, bias_n = tables
    kern = functools.partial(_nsa_sample_head_kernel, ts=ts, past=past, wbuf=wbuf)
    col = lambda width, off: pl.BlockSpec((1, ts, width), lambda i: (i, 0, off // width))
    full = lambda a: pl.BlockSpec(a.shape, lambda i: (0,) * a.ndim)
    per_b = lambda a: pl.BlockSpec((1,) + a.shape[1:], lambda i: (i,) + (0,) * (a.ndim - 1))
    win = pl.BlockSpec((1, 1, LANES, wbuf), lambda i: (layer, i, 0, 0))
    qg2 = jnp.tile(qg.reshape(1, C_HD), (1, 2))
    kg2 = jnp.tile(kg, (1, 2))
    kg_col = jnp.tile(kg2[0].reshape(LANES, 1), (1, LANES))
    rows = lambda n, dt: jax.ShapeDtypeStruct((b, n_rows, n), dt)
    out_rows = lambda n: pl.BlockSpec((1, n_rows, n), lambda i: (i, 0, 0))
    return pl.pallas_call(
        kern,
        grid=(b,),
        in_specs=[col(C_W, P_Q), col(2 * LANES, P_KV), col(2 * LANES, P_KV + 2 * LANES),
                  col(2 * LANES, P_KV + 4 * LANES), per_b(kc_t), per_b(vc_t), win, win,
                  full(qg2), full(kg2), full(kg_col), full(bias_c), full(bias_w), full(bias_n)],
        out_specs=[out_rows(LANES), out_rows(BLK_LANES), out_rows(LANES), out_rows(LANES),
                   pl.BlockSpec((1, 6, ts, LANES), lambda i: (i, 0, 0, 0))],
        out_shape=[rows(LANES, BF16), rows(BLK_LANES, F32), rows(LANES, F32), rows(LANES, F32),
                   jax.ShapeDtypeStruct((b, 6, ts, LANES), F32)],
        scratch_shapes=[pltpu.VMEM((n_rows, LANES), F32)],
        compiler_params=_cparams(("parallel",)),
        name="nsa_sample_head",
    )(main, main, main, main, kc_t, vc_t, win_k, win_v, qg2, kg2, kg_col, bias_c, bias_w, bias_n)


def _slc_pages_kernel(pt_ref, *refs, n, ts, last_chunk):
    k_refs, v_refs = refs[:n], refs[n:2 * n]
    (q_ref, sel_ref, ocmp_ref, owin_ref, gate_ref, new_ref, blast_ref, bn_ref,
     o_ref, m_ref, l_ref, acc_ref) = refs[2 * n:]
    c = pl.program_id(1)
    n_rows = S_ROWS * ts
    nt = (((1,), (1,)), ((), ()))

    @pl.when(c == 0)
    def _():
        m_ref[...] = jnp.full(m_ref.shape, M_INIT, F32)
        l_ref[...] = jnp.zeros(l_ref.shape, F32)
        acc_ref[...] = jnp.zeros(acc_ref.shape, F32)

    q32 = q_ref[0]
    keys = jnp.concatenate([r[0, 0] for r in k_refs], axis=1).astype(BF16)
    vals = jnp.concatenate([r[0, 0] for r in v_refs], axis=1).astype(BF16)
    s = jnp.dot(q32, keys, preferred_element_type=F32)
    cut = (n - 1) * PAGE_SIZE
    tail = s[:, cut:] + jnp.where(c == last_chunk, blast_ref[...], 0.0)
    s = jnp.concatenate([s[:, :cut], tail], axis=1) if cut else tail
    row = lax.broadcasted_iota(jnp.int32, (LANES, n * PAGE_SIZE), 0)
    col = lax.broadcasted_iota(jnp.int32, (LANES, n * PAGE_SIZE), 1) // CMP_BLOCK
    expand = (row == col).astype(BF16)
    picked = jnp.dot(sel_ref[0, 0].astype(BF16), expand, preferred_element_type=F32) > 0.5
    s = jnp.where(picked, s, NEG)
    m_old = m_ref[...]
    m_new = jnp.maximum(m_old, jnp.max(s, axis=-1, keepdims=True))
    alpha = jnp.exp(m_old - m_new)
    p = jnp.exp(s - m_new)
    l_ref[...] = alpha * l_ref[...] + jnp.sum(p, axis=-1, keepdims=True)
    acc_ref[...] = alpha * acc_ref[...] + lax.dot_general(p.astype(BF16), vals, nt,
                                                          preferred_element_type=F32)
    m_ref[...] = m_new

    @pl.when(c == last_chunk)
    def _():
        t_of_row = lax.broadcasted_iota(jnp.int32, (n_rows, 1), 0) % ts
        pad_rows = jnp.zeros((NEW_PAD - ts, LANES), F32)
        k_new = jnp.concatenate([new_ref[0, 2], pad_rows], axis=0)
        v_new = jnp.concatenate([new_ref[0, 3], pad_rows], axis=0)
        nkey = lax.broadcasted_iota(jnp.int32, (n_rows, NEW_PAD), 1)
        sn = lax.dot_general(q32, k_new.astype(BF16), nt, preferred_element_type=F32) + bn_ref[...]
        sn = jnp.where((nkey <= t_of_row) & (nkey < ts), sn, NEG)
        m_prev = m_ref[...]
        m_fin = jnp.maximum(m_prev, jnp.max(sn, axis=-1, keepdims=True))
        a_fin = jnp.exp(m_prev - m_fin)
        pn = jnp.exp(sn - m_fin)
        l_fin = a_fin * l_ref[...] + jnp.sum(pn, axis=-1, keepdims=True)
        o_slc = (a_fin * acc_ref[...] + _dot(pn, v_new)) / l_fin

        ri = lax.broadcasted_iota(jnp.int32, (n_rows, NEW_PAD), 0) % ts
        ti = lax.broadcasted_iota(jnp.int32, (n_rows, NEW_PAD), 1)
        gates = jnp.concatenate([_sigmoid(gate_ref[0]), pad_rows], axis=0)
        per_row = _dot_exact_lhs((ri == ti).astype(BF16), gates)
        head = lax.broadcasted_iota(jnp.int32, (n_rows, LANES), 0) // ts
        glane = lax.broadcasted_iota(jnp.int32, (n_rows, LANES), 1)
        gate = lambda br: jnp.sum(jnp.where(glane == br * C_HEADS + head, per_row, 0.0), axis=-1,
                                  keepdims=True)
        comb = gate(0) * ocmp_ref[0] + gate(1) * o_slc + gate(2) * owin_ref[0]
        low = (lax.broadcasted_iota(jnp.int32, (ts, LANES), 1) // C_HD) == 0
        for n_ in range(C_HPG):
            top = comb[n_ * ts:(n_ + 1) * ts]
            bot = comb[(C_HPG + n_) * ts:(C_HPG + n_ + 1) * ts]
            o_ref[0, :, n_ * LANES:(n_ + 1) * LANES] = jnp.where(low, top, bot)


def _slc_pages(page_table, cache_k, cache_v, layer, q32, sel_chunks, o_cmp, o_win, gates, new_rows,
               bias_last, bias_n):
    b, n_pages = page_table.shape
    n = min(PAGES_PER_STEP, n_pages)
    chunks = n_pages // n
    ts = new_rows.shape[2]
    n_rows = S_ROWS * ts
    kern = functools.partial(_slc_pages_kernel, n=n, ts=ts, last_chunk=chunks - 1)
    per_b = lambda a: pl.BlockSpec((1,) + a.shape[1:], lambda bi, c, pt: (bi,) + (0,) * (a.ndim - 1))
    full = lambda a: pl.BlockSpec(a.shape, lambda bi, c, pt: (0,) * a.ndim)
    return pl.pallas_call(
        kern,
        grid_spec=pltpu.PrefetchScalarGridSpec(
            num_scalar_prefetch=1, grid=(b, chunks),
            in_specs=_page_specs(layer, n) + _page_specs(layer, n) + [
                per_b(q32), pl.BlockSpec((1, 1, n_rows, LANES), lambda bi, c, pt: (bi, c, 0, 0)),
                per_b(o_cmp), per_b(o_win),
                pl.BlockSpec((1, ts, LANES), lambda bi, c, pt: (bi, 0, G_CG // LANES)),
                per_b(new_rows), full(bias_last), full(bias_n)],
            out_specs=pl.BlockSpec((1, ts, C_W), lambda bi, c, pt: (bi, 0, 0)),
            scratch_shapes=[pltpu.VMEM((n_rows, 1), F32), pltpu.VMEM((n_rows, 1), F32),
                            pltpu.VMEM((n_rows, LANES), F32)]),
        out_shape=jax.ShapeDtypeStruct((b, ts, C_W), F32),
        compiler_params=_cparams(("parallel", "arbitrary")),
        name="nsa_sample_slc_pages",
    )(page_table, *([cache_k] * n), *([cache_v] * n), q32, sel_chunks, o_cmp, o_win, gates, new_rows,
      bias_last, bias_n)


def _sample_bias_tables(rel_bias, past, ts, wbuf):
    tbl = jnp.repeat(_distance_table(rel_bias), ts, axis=0)
    t = jnp.tile(jnp.arange(ts), S_ROWS)[:, None]
    look = lambda dist: jnp.einsum('rd,rcd->rc', tbl, _one_hot(dist, MAX_DIST),
                                   precision=lax.Precision.HIGHEST)
    blk = jnp.arange(BLK_LANES)[None, :]
    bias_c = look(past + t - CMP_BLOCK * blk - (CMP_BLOCK - 1))
    bias_w = look(wbuf + t - jnp.arange(wbuf)[None, :])
    bias_n = look(t - jnp.arange(NEW_PAD)[None, :])
    bias_last = look(PAGE_SIZE + t - jnp.arange(PAGE_SIZE)[None, :])
    return bias_c, bias_w, bias_n, bias_last


def _nsa_sample(main, gate, caches, win_k, win_v, page_table, layer, qg, kg, rel_bias):
    b, ts, _ = main.shape
    n_pages = page_table.shape[1]
    past = n_pages * PAGE_SIZE
    wbuf = win_k.shape[-1]
    n_rows = S_ROWS * ts
    bias_c, bias_w, bias_n, bias_last = _sample_bias_tables(rel_bias, past, ts, wbuf)
    kc_chunks, vc_chunks = _cmp_pages(page_table, caches[0], caches[1], layer)
    n_cached = past // CMP_BLOCK
    unchunk = lambda z: z.transpose(0, 2, 1, 3).reshape(b, LANES, n_cached)
    fit = lambda z: jnp.pad(z, ((0, 0), (0, 0), (0, BLK_LANES - n_cached)))
    q32, sel, o_cmp, o_win, new_rows = _nsa_sample_head(
        main, fit(unchunk(kc_chunks)), fit(unchunk(vc_chunks)), win_k, win_v, layer, qg, kg,
        (bias_c, bias_w, bias_n), past)
    per_step = 2 * min(PAGES_PER_STEP, n_pages)
    sel_chunks = sel[:, :, :n_cached].reshape(b, n_rows, n_cached // per_step, per_step)
    sel_chunks = jnp.pad(sel_chunks.transpose(0, 2, 1, 3), ((0, 0), (0, 0), (0, 0), (0, LANES - per_step)))
    o = _slc_pages(page_table, caches[2], caches[3], layer, q32, sel_chunks, o_cmp, o_win,
                   gate.reshape(b, ts, N_GATE), new_rows, bias_last, bias_n)
    return o.reshape(b * ts, C_W), new_rows


def _shift_order(z):
    part = lambda off, n: z[..., off:off + n]
    return jnp.concatenate([part(OFF_R, A_W), part(OFF_K, A_W), part(OFF_V, A_W), part(OFF_WL, A_LORA_W),
                            part(OFF_AL, A_LORA_A), part(OFF_GL, A_LORA_G)], axis=-1)


def _layer_params(l, w_in, shift_mu, w_br_a, w_br_b, w_br_c, w_out, w_mq, w_mk, w_mv, w_mo, w_ff1, w_ff2):
    w = w_in[l]
    cast = lambda a: a.astype(BF16)
    cols = lambda off, n: cast(w[:, off:off + n])
    w_q = cols(OFF_Q, C_W).reshape(D_MODEL, C_KV, C_HPG, C_HD).transpose(0, 2, 1, 3).reshape(D_MODEL, C_W)
    w_main = jnp.concatenate([
        cols(OFF_R, A_W), cols(OFF_K, A_W), cols(OFF_V, A_W), cols(OFF_POOL, B_W), w_q,
        cols(OFF_KV, 6 * C_KVW), cols(OFF_WL, A_LORA_W), cols(OFF_AL, A_LORA_A), cols(OFF_GL, A_LORA_G)],
        axis=1)
    w_gate = jnp.concatenate([cols(OFF_MG, 3 * D_MODEL), cols(OFF_CG, 3 * C_HEADS),
                              jnp.zeros((D_MODEL, N_GATE - G_CG - 3 * C_HEADS), BF16)], axis=1)
    wc = w_br_c[l].reshape(C_KV, C_HPG, C_HD, D_MODEL).transpose(1, 0, 2, 3).reshape(C_W, D_MODEL)
    return dict(
        w_main=w_main, w_gate=w_gate, mu=_shift_order(shift_mu[l].reshape(1, -1)),
        wa=cast(w_br_a[l]), wb=cast(w_br_b[l]), wc=cast(wc), wo=cast(w_out[l]),
        wq=cast(w_mq[l]), wk=cast(w_mk[l]), wv=cast(w_mv[l]), wmo=cast(w_mo[l]),
        w1=cast(w_ff1[l]), w2=cast(w_ff2[l]))


def _shift_row(main_row):
    lora = main_row[:, P_LORA:]
    return jnp.concatenate([
        main_row[:, P_R:P_R + A_W], lora[:, :A_LORA_W], main_row[:, P_K:P_K + A_W],
        main_row[:, P_V:P_V + A_W], lora[:, A_LORA_W:A_LORA_W + A_LORA_A],
        lora[:, A_LORA_W + A_LORA_A:]], axis=1)


def _feature_major(z):
    z = jnp.moveaxis(z, -3, -1)
    return z.reshape(z.shape[:-3] + (C_KVW, z.shape[-1]))


def _token_major(z):
    b, _, rows = z.shape
    return z.reshape(b, C_KV, C_HD, rows).transpose(0, 3, 1, 2)


def kernel(x_prompt, x_sample, state_rwkv, state_shift, state_pool, cache_cmp_k, cache_cmp_v, cache_slc_k, cache_slc_v, cache_win_k, cache_win_v, cache_mem_k, cache_mem_v, page_table, mem_prompt, rel_bias, norm_mix_g, w_in, shift_mu, rwkv_w0, rwkv_w2, rwkv_a0, rwkv_a2, rwkv_g2, rwkv_kk, rwkv_ka, rwkv_rk, rwkv_lnx_g, rwkv_lnx_b, pool_w, pool_scale, nsa_q_g, nsa_k_g, w_br_a, w_br_b, w_br_c, w_out, norm_memx_g, norm_mem_g, w_mq, w_mk, w_mv, mem_q_g, mem_k_g, w_mo, norm_ffn_g, w_ff1, w_ff2):
    bp, t = x_prompt.shape[:2]
    bs, ts = x_sample.shape[:2]
    past = page_table.shape[1] * PAGE_SIZE
    wbuf_p = min(WINDOW, t)
    ts_pad = 16
    bnear, bcmp = _bias_tables(rel_bias)
    caches = [_feature_major(z) for z in (cache_cmp_k, cache_cmp_v, cache_slc_k, cache_slc_v)]
    win_k, win_v = _feature_major(cache_win_k), _feature_major(cache_win_v)
    xp = x_prompt.reshape(bp * t, D_MODEL)
    xs = x_sample.reshape(bs * ts, D_MODEL)
    outs_p = [[] for _ in range(11)]
    outs_s = [[] for _ in range(9)]
    row = lambda a: a.reshape(1, -1)
    for l in range(DEPTH):
        prm = _layer_params(l, w_in, shift_mu, w_br_a, w_br_b, w_br_c, w_out, w_mq, w_mk, w_mv, w_mo,
                            w_ff1, w_ff2)
        rw = (prm["mu"], row(rwkv_w0[l]), rwkv_w2[l].astype(BF16), row(rwkv_a0[l]), rwkv_a2[l].astype(BF16),
              rwkv_g2[l].astype(BF16), row(rwkv_kk[l]), row(rwkv_ka[l]), row(rwkv_rk[l]),
              row(rwkv_lnx_g[l]), row(rwkv_lnx_b[l]))
        pool_wl = pool_w[l].astype(BF16)

        main = _norm_matmul(xp, norm_mix_g[l], prm["w_main"], 512, N_MAIN).reshape(bp, t, N_MAIN)
        gate = _norm_matmul(xp, norm_mix_g[l], prm["w_gate"], 512, N_GATE)
        o_a, s_pairs = _rwkv(main, jnp.zeros((bp, 1, A_COLS), F32), jnp.zeros((bp, 4, 128, 128), F32), rw)
        o_b = _pool(main, jnp.zeros((bp, POOL_HALO, B_W), F32), pool_wl, pool_scale[l], 0, 512)
        qt, ks16, vs16, kw16, vw16, kc, vc, kv_rows = _nsa_prep(main, nsa_q_g[l], nsa_k_g[l], 512)
        front = lambda z: jnp.pad(z, ((0, 0), (FRONT_TILES * Q_TILE, 0), (0, 0)))
        front_t = lambda z: jnp.pad(z, ((0, 0), (0, 0), (FRONT_TILES * Q_TILE, 0)))
        o_c = _nsa_attn(qt, front(ks16), front_t(vs16), front(kw16), front_t(vw16), kc, vc,
                        gate.reshape(bp, t, N_GATE), bnear, bcmp)
        xp = _merge(xp, o_a.reshape(bp * t, A_W), o_b.reshape(bp * t, B_W), o_c.reshape(bp * t, C_W), gate,
                    prm["wa"], prm["wb"], prm["wc"], prm["wo"], 512)
        mk, mv = _mem_kv(mem_prompt.reshape(bp * N_MEM, D_MODEL), norm_mem_g[l], prm["wk"], prm["wv"],
                         mem_k_g[l])
        mk, mv = mk.reshape(bp, N_MEM, M_W), mv.reshape(bp, N_MEM, M_W)
        xp = _mem_ffn(xp, mk, mv, norm_memx_g[l], prm["wq"], mem_q_g[l], prm["wmo"], norm_ffn_g[l],
                      prm["w1"], prm["w2"], 1024, t)
        for dst, val in zip(outs_p, (
                _unpair_states(s_pairs), _shift_row(main[:, -1]), main[:, -POOL_BUF:, P_POOL:P_POOL + B_W],
                _token_major(kv_rows[:, 0]), _token_major(kv_rows[:, 1]), _token_major(kv_rows[:, 2]),
                _token_major(kv_rows[:, 3]), _token_major(kv_rows[:, 4, :, -wbuf_p:]),
                _token_major(kv_rows[:, 5, :, -wbuf_p:]),
                mk.reshape(bp, N_MEM, M_HEADS, M_HD), mv.reshape(bp, N_MEM, M_HEADS, M_HD))):
            dst.append(val)

        main = _norm_matmul(xs, norm_mix_g[l], prm["w_main"], bs * ts, N_MAIN).reshape(bs, ts, N_MAIN)
        gate = _norm_matmul(xs, norm_mix_g[l], prm["w_gate"], bs * ts, N_GATE)
        main_pad = jnp.pad(main, ((0, 0), (0, ts_pad - ts), (0, 0)))
        ulast = _shift_order(state_shift[l]).reshape(bs, 1, A_COLS)
        o_a, s_pairs = _rwkv(main_pad, ulast, _pair_states(state_rwkv[l]), rw, t_valid=ts)
        o_a = o_a[:, :ts]
        hist = jnp.concatenate([jnp.zeros((bs, POOL_HALO - POOL_BUF, B_W), F32), state_pool[l]], axis=1)
        o_b = _pool(main_pad, hist, pool_wl, pool_scale[l], past, ts_pad)[:, :ts]
        u_pool = main[:, :, P_POOL:P_POOL + B_W]
        o_c, new_rows = _nsa_sample(main, gate, caches, win_k, win_v, page_table, l, nsa_q_g[l], nsa_k_g[l],
                                    rel_bias)
        new_kv = [new_rows[:, n].reshape(bs, ts, C_KV, C_HD) for n in range(6)]
        slide = lambda buf, n: _token_major(jnp.concatenate(
            [buf[l][:, :, ts:], new_rows[:, n].transpose(0, 2, 1)], axis=2))
        xs = _merge(xs, o_a.reshape(bs * ts, A_W), o_b.reshape(bs * ts, B_W), o_c, gate,
                    prm["wa"], prm["wb"], prm["wc"], prm["wo"], bs * ts)
        xs = _mem_ffn(xs, cache_mem_k[l].reshape(bs, N_MEM, M_W), cache_mem_v[l].reshape(bs, N_MEM, M_W),
                      norm_memx_g[l], prm["wq"], mem_q_g[l], prm["wmo"], norm_ffn_g[l],
                      prm["w1"], prm["w2"], 32, ts)
        for dst, val in zip(outs_s, (
                _unpair_states(s_pairs), _shift_row(main[:, -1]),
                jnp.concatenate([state_pool[l], u_pool], axis=1)[:, -POOL_BUF:],
                new_kv[0], new_kv[1], new_kv[2], new_kv[3], slide(win_k, 4), slide(win_v, 5))):
            dst.append(val)

    return ((xp.reshape(bp, t, D_MODEL), xs.reshape(bs, ts, D_MODEL))
            + tuple(jnp.stack(o) for o in outs_p) + tuple(jnp.stack(o) for o in outs_s))
```

```python
import functools
import math

import jax
import jax.numpy as jnp
from jax import lax
from jax.experimental import pallas as pl
from jax.experimental.pallas import tpu as pltpu

F32 = jnp.float32
BF16 = jnp.bfloat16

D_MODEL = 1024
DEPTH = 2
PAGE_SIZE = 128
A_HEADS = 8
A_HD = 64
A_W = A_HEADS * A_HD
A_LORA_W = 64
A_LORA_A = 64
A_LORA_G = 128
A_LORA = A_LORA_W + A_LORA_A + A_LORA_G
LNX_EPS = 64e-5
L2_EPS = 1e-24
B_GROUPS = 4
B_GW = 128
B_W = B_GROUPS * B_GW
POOL_WINDOWS = (2, 4, 8, 16)
POOL_BUF = 15
POOL_HALO = 16
C_HEADS = 8
C_HD = 64
C_W = C_HEADS * C_HD
C_KV = 2
C_HPG = C_HEADS // C_KV
C_KVW = C_KV * C_HD
CMP_BLOCK = 64
N_SEL = 16
WINDOW = 512
Q_TILE = 128
N_FORCED = 3
NEG = -1e30
N_BUCKETS = 32
MAX_DIST = 128
N_MEM = 256
M_HEADS = 4
M_HD = 128
M_W = M_HEADS * M_HD
D_FF = 4 * D_MODEL
RMS_EPS = 1e-6

OFF_R = 0
OFF_WL = OFF_R + A_W
OFF_K = OFF_WL + A_LORA_W
OFF_V = OFF_K + A_W
OFF_AL = OFF_V + A_W
OFF_GL = OFF_AL + A_LORA_A
A_COLS = OFF_GL + A_LORA_G
OFF_POOL = A_COLS
OFF_Q = OFF_POOL + B_W
OFF_KV = OFF_Q + C_W
OFF_CG = OFF_KV + 6 * C_KVW
OFF_MG = OFF_CG + 3 * C_HEADS
N_IN = OFF_MG + 3 * D_MODEL

P_R = 0
P_K = P_R + A_W
P_V = P_K + A_W
P_POOL = P_V + A_W
P_Q = P_POOL + B_W
P_KV = P_Q + C_W
P_LORA = P_KV + 6 * C_KVW
N_MAIN = P_LORA + A_LORA
G_MG = 0
G_CG = 3 * D_MODEL
N_GATE = G_CG + 128

BRANCH_DTYPE = BF16
RWKV_CHUNK = 64
RWKV_SUB = 16
LANES = 128
VMEM_LIMIT = 56 * 1024 * 1024


def _cparams(sem):
    return pltpu.CompilerParams(dimension_semantics=sem, vmem_limit_bytes=VMEM_LIMIT)


def _dot(a, b, dims=(((1,), (0,)), ((), ()))):
    return lax.dot_general(a.astype(BF16), b.astype(BF16), dims, preferred_element_type=F32)


def _dot_nt(a, b):
    return _dot(a, b, (((1,), (1,)), ((), ())))


def _split(x, terms):
    parts = []
    for n in range(terms):
        part = x.astype(BF16)
        parts.append(part)
        if n + 1 < terms:
            x = x - part.astype(F32)
    return parts


def _dot_exact_rhs(a, b01, terms=2):
    dims = (((1,), (0,)), ((), ()))
    return sum(lax.dot_general(p, b01, dims, preferred_element_type=F32) for p in _split(a, terms))


def _dot_exact_lhs(a01, b, terms=2):
    dims = (((1,), (0,)), ((), ()))
    return sum(lax.dot_general(a01, p, dims, preferred_element_type=F32) for p in _split(b, terms))


def _sigmoid(x):
    return 1.0 / (1.0 + jnp.exp(-x))


def _rms(x, g):
    return x * lax.rsqrt(jnp.mean(x * x, axis=-1, keepdims=True) + RMS_EPS) * g


def _norm_matmul_kernel(x_ref, g_ref, w_ref, o_ref, xn_ref):
    @pl.when(pl.program_id(1) == 0)
    def _():
        xn_ref[...] = _rms(x_ref[...], g_ref[...]).astype(BF16)

    o_ref[...] = jnp.dot(xn_ref[...], w_ref[...], preferred_element_type=F32)


def _norm_matmul(x, g, w, tm, tn):
    m, k = x.shape
    n = w.shape[1]
    return pl.pallas_call(
        _norm_matmul_kernel,
        grid=(m // tm, n // tn),
        in_specs=[pl.BlockSpec((tm, k), lambda i, j: (i, 0)),
                  pl.BlockSpec((1, k), lambda i, j: (0, 0)),
                  pl.BlockSpec((k, tn), lambda i, j: (0, j))],
        out_specs=pl.BlockSpec((tm, tn), lambda i, j: (i, j)),
        out_shape=jax.ShapeDtypeStruct((m, n), F32),
        scratch_shapes=[pltpu.VMEM((tm, k), BF16)],
        compiler_params=_cparams(("parallel", "arbitrary")),
        name="norm_matmul",
    )(x, g.reshape(1, k), w)


def _rwkv_kernel(r_ref, k_ref, v_ref, lora_ref, ulast_ref, s0_ref, mu_ref, w0_ref, w2_ref, a0_ref,
                 a2_ref, g2_ref, kk_ref, ka_ref, rk_ref, lng_ref, lnb_ref,
                 o_ref, sout_ref, s_ref, prev_ref, ar_ref, bk_ref, vb_ref, y_ref, *, chunk, bt, t_valid):
    c = pl.program_id(1)
    nc = pl.num_programs(1)
    C = chunk
    R = bt * C
    NP = A_HEADS // 2
    PB = 2 * C

    @pl.when(c == 0)
    def _():
        s_ref[...] = s0_ref[...].reshape(bt * NP, 2 * A_HD, 2 * A_HD)
        prev_ref[...] = ulast_ref[:, 0, :]

    row = lax.broadcasted_iota(jnp.int32, (R, 1), 0)

    def shift_mix(u3, lo, hi):
        u = u3.reshape(R, hi - lo)
        u_prev = pltpu.roll(u, 1, 0)
        for b in range(bt):
            u_prev = jnp.where(row == b * C, prev_ref[b:b + 1, lo:hi], u_prev)
        for b in range(bt):
            prev_ref[b:b + 1, lo:hi] = u[(b + 1) * C - 1:(b + 1) * C, :]
        return u + (u_prev - u) * mu_ref[:, lo:hi]

    r = shift_mix(r_ref[...], 0, A_W)
    k = shift_mix(k_ref[...], A_W, 2 * A_W)
    v = shift_mix(v_ref[...], 2 * A_W, 3 * A_W)
    lora = shift_mix(lora_ref[...], 3 * A_W, 3 * A_W + A_LORA)
    wl = lora[:, 0:A_LORA_W]
    al = lora[:, A_LORA_W:A_LORA_W + A_LORA_A]
    gl = lora[:, A_LORA_W + A_LORA_A:A_LORA]

    z = -(w0_ref[...] + _dot(jnp.tanh(wl), w2_ref[...]))
    softplus = jnp.maximum(z, 0.0) + jnp.log(1.0 + jnp.exp(-jnp.abs(z)))
    w = -softplus - 0.5
    a = _sigmoid(a0_ref[...] + _dot(al, a2_ref[...]))
    g = _dot(_sigmoid(gl), g2_ref[...])

    lane = lax.broadcasted_iota(jnp.int32, (A_W, A_W), 1) // A_HD
    sub = lax.broadcasted_iota(jnp.int32, (A_W, A_W), 0) // A_HD
    head_ones = (lane == sub).astype(BF16)

    kkv = k * kk_ref[...]
    head_sum = lambda z: _dot_exact_rhs(z, head_ones, terms=1)
    kkn = kkv * lax.rsqrt(jnp.maximum(head_sum(kkv * kkv), L2_EPS))
    k2 = k * (1.0 + (a - 1.0) * ka_ref[...])
    log_d = -jnp.exp(w)
    t_in = row % C
    if t_valid is not None:
        live = (c * C + t_in) < t_valid
        log_d = jnp.where(live, log_d, 0.0)
        kkn = jnp.where(live, kkn, 0.0)
        k2 = jnp.where(live, k2, 0.0)

    ri = lax.broadcasted_iota(jnp.int32, (R, R), 0)
    ci = lax.broadcasted_iota(jnp.int32, (R, R), 1)
    cum_mask = ((ri // C) == (ci // C)) & (ci <= ri)
    cum = _dot_exact_lhs(cum_mask.astype(BF16), log_d)
    c_incl = jnp.exp(cum)
    c_inv = jnp.exp(-cum)
    a_t = -kkn * jnp.exp(cum - log_d)
    r_t = r * c_incl
    b_t = kkn * a * c_inv
    k_t = k2 * c_inv

    lane = lax.broadcasted_iota(jnp.int32, (1, 2 * A_HD), 1) // A_HD
    def stage(ref, off, x):
        for b in range(bt):
            for p in range(NP):
                blk = x[b * C:(b + 1) * C, 2 * A_HD * p:2 * A_HD * (p + 1)]
                for hh in range(2):
                    ref[b * NP + p, off + hh * C:off + (hh + 1) * C, :] = (
                        jnp.where(lane == hh, blk, 0.0).astype(BF16))

    stage(ar_ref, 0, a_t)
    stage(ar_ref, PB, r_t)
    stage(bk_ref, 0, b_t)
    stage(bk_ref, PB, k_t)
    stage(vb_ref, 0, v)
    ar = ar_ref[...]
    bk = bk_ref[...]
    vb = vb_ref[...]

    nn = (((2,), (1,)), ((0,), (0,)))
    nt = (((2,), (2,)), ((0,), (0,)))
    tn = (((1,), (1,)), ((0,), (0,)))
    bdot = lambda x, y, dims=nn: lax.dot_general(x.astype(BF16), y.astype(BF16), dims,
                                                 preferred_element_type=F32)
    bi = lax.broadcasted_iota(jnp.int32, (PB, PB), 0)
    bj = lax.broadcasted_iota(jnp.int32, (PB, PB), 1)
    same_head = (bi // C) == (bj // C)
    strict = same_head & ((bj % C) < (bi % C))
    lower = same_head & ((bj % C) <= (bi % C))
    sub_blk = (bi // RWKV_SUB) == (bj // RWKV_SUB)
    eye = (bi == bj).astype(F32)

    gram = bdot(ar, bk, nt)
    l_b = jnp.where(strict, gram[:, :PB, :PB], 0.0)
    l_k = jnp.where(strict, gram[:, :PB, PB:], 0.0)
    m_b = jnp.where(lower, gram[:, PB:, :PB], 0.0)
    m_k = jnp.where(lower, gram[:, PB:, PB:], 0.0)
    dg = jnp.where(sub_blk, l_b, 0.0)
    off = l_b - dg
    t_inv = eye + dg
    pw = dg
    for _ in range(int(math.log2(RWKV_SUB)) - 1):
        pw = bdot(pw, pw)
        t_inv = t_inv + bdot(t_inv, pw)
    n1 = bdot(t_inv, off)
    n2 = bdot(n1, n1)
    full = eye + n1 + n2 + bdot(n1, n2)
    t_full = bdot(full, t_inv)

    s_old = s_ref[...]
    w0 = bdot(ar, s_old, nt)
    u = bdot(t_full, w0[:, :PB] + bdot(l_k, vb))
    uv = jnp.concatenate([u, vb.astype(F32)], axis=1)
    yb = w0[:, PB:] + bdot(jnp.concatenate([m_b, m_k], axis=2), uv)
    s_new = s_old + bdot(uv, bk, tn)
    for b in range(bt):
        for p in range(NP):
            sl = slice(2 * A_HD * p, 2 * A_HD * (p + 1))
            idx = b * NP + p
            s_ref[idx] = s_new[idx] * c_incl[(b + 1) * C - 1:(b + 1) * C, sl]
            y_ref[b * C:(b + 1) * C, sl] = yb[idx, :C] + yb[idx, C:]
    y = y_ref[...]

    inv_n = 1.0 / A_HD
    mean = head_sum(y) * inv_n
    yc = y - mean
    var = head_sum(yc * yc) * inv_n
    yn = yc * lax.rsqrt(var + LNX_EPS) * lng_ref[...] + lnb_ref[...]
    bonus = head_sum(r * k2 * rk_ref[...]) * v
    o_ref[...] = ((yn + bonus) * g).reshape(bt, C, A_W).astype(o_ref.dtype)

    @pl.when(c == nc - 1)
    def _():
        sout_ref[...] = s_ref[...].reshape(bt, NP, 2 * A_HD, 2 * A_HD)


RWKV_BATCH_TILE = 4


def _rwkv(proj, ulast, s0, prm, t_valid=None):
    b, t, _ = proj.shape
    C = min(RWKV_CHUNK, t)
    bt = min(RWKV_BATCH_TILE, b)
    nc = t // C
    vec = lambda n: pl.BlockSpec((1, n), lambda i, c: (0, 0))
    mat = lambda m, n: pl.BlockSpec((m, n), lambda i, c: (0, 0))
    col = lambda width, off: pl.BlockSpec((bt, C, width), lambda i, c: (i, c, off // width))
    n_shift = 3 * A_W + A_LORA
    n_pairs = bt * (A_HEADS // 2)
    kern = functools.partial(_rwkv_kernel, chunk=C, bt=bt, t_valid=t_valid)
    return pl.pallas_call(
        kern,
        grid=(b // bt, nc),
        in_specs=[col(A_W, P_R), col(A_W, P_K), col(A_W, P_V), col(A_LORA, P_LORA),
                  pl.BlockSpec((bt, 1, n_shift), lambda i, c: (i, 0, 0)),
                  pl.BlockSpec((bt, 4, 128, 128), lambda i, c: (i, 0, 0, 0)),
                  vec(n_shift), vec(A_W), mat(A_LORA_W, A_W), vec(A_W), mat(A_LORA_A, A_W),
                  mat(A_LORA_G, A_W), vec(A_W), vec(A_W), vec(A_W), vec(A_W), vec(A_W)],
        out_specs=[pl.BlockSpec((bt, C, A_W), lambda i, c: (i, c, 0)),
                   pl.BlockSpec((bt, 4, 128, 128), lambda i, c: (i, 0, 0, 0))],
        out_shape=[jax.ShapeDtypeStruct((b, t, A_W), BRANCH_DTYPE),
                   jax.ShapeDtypeStruct((b, 4, 128, 128), F32)],
        scratch_shapes=[pltpu.VMEM((n_pairs, 128, 128), F32), pltpu.VMEM((bt, n_shift), F32),
                        pltpu.VMEM((n_pairs, 4 * C, 128), BF16), pltpu.VMEM((n_pairs, 4 * C, 128), BF16),
                        pltpu.VMEM((n_pairs, 2 * C, 128), BF16), pltpu.VMEM((bt * C, A_W), F32)],
        compiler_params=_cparams(("parallel", "arbitrary")),
        name="rwkv_chunk",
    )(proj, proj, proj, proj, ulast, s0, *prm)


def _pair_states(s):
    b = s.shape[0]
    s = s.reshape(b, 4, 2, A_HD, A_HD)
    z = jnp.zeros_like(s[:, :, 0])
    top = jnp.concatenate([s[:, :, 0], z], axis=-1)
    bot = jnp.concatenate([z, s[:, :, 1]], axis=-1)
    return jnp.concatenate([top, bot], axis=-2)


def _unpair_states(sp):
    b = sp.shape[0]
    h0 = sp[:, :, :A_HD, :A_HD]
    h1 = sp[:, :, A_HD:, A_HD:]
    return jnp.stack([h0, h1], axis=2).reshape(b, A_HEADS, A_HD, A_HD)


def _pool_kernel(u_ref, halo_ref, hist_ref, pm_ref, ps_ref, o_ref, ext_ref, *, tm, pos0):
    i = pl.program_id(1)

    @pl.when(i == 0)
    def _():
        ext_ref[0:POOL_HALO, :] = hist_ref[0]

    @pl.when(i > 0)
    def _():
        ext_ref[0:POOL_HALO, :] = halo_ref[0]

    cur = u_ref[0]
    ext_ref[POOL_HALO:POOL_HALO + tm, :] = cur
    pos = pos0 + i * tm + lax.broadcasted_iota(jnp.int32, (tm, 1), 0)
    outs = []
    for gi, win in enumerate(POOL_WINDOWS):
        lo, hi = gi * B_GW, (gi + 1) * B_GW
        s = cur[:, lo:hi]
        for back in range(1, win):
            s = s + ext_ref[POOL_HALO - back:POOL_HALO - back + tm, lo:hi]
        cnt = jnp.minimum(win, pos + 1).astype(F32)
        d = s / cnt - cur[:, lo:hi]
        outs.append(_dot(d, pm_ref[gi]))
    o_ref[0] = (jnp.concatenate(outs, axis=1) * ps_ref[...]).astype(o_ref.dtype)


def _pool(proj, hist, pm, ps, pos0, tm):
    b, t, _ = proj.shape
    per = tm // POOL_HALO
    kern = functools.partial(_pool_kernel, tm=tm, pos0=pos0)
    return pl.pallas_call(
        kern,
        grid=(b, t // tm),
        in_specs=[pl.BlockSpec((1, tm, B_W), lambda bi, i: (bi, i, P_POOL // B_W)),
                  pl.BlockSpec((1, POOL_HALO, B_W),
                               lambda bi, i: (bi, jnp.maximum(i * per - 1, 0), P_POOL // B_W)),
                  pl.BlockSpec((1, POOL_HALO, B_W), lambda bi, i: (bi, 0, 0)),
                  pl.BlockSpec((B_GROUPS, B_GW, B_GW), lambda bi, i: (0, 0, 0)),
                  pl.BlockSpec((1, B_W), lambda bi, i: (0, 0))],
        out_specs=pl.BlockSpec((1, tm, B_W), lambda bi, i: (bi, i, 0)),
        out_shape=jax.ShapeDtypeStruct((b, t, B_W), BRANCH_DTYPE),
        scratch_shapes=[pltpu.VMEM((tm + POOL_HALO, B_W), F32)],
        compiler_params=_cparams(("parallel", "arbitrary")),
        name="pool_mix",
    )(proj, proj, hist, pm, ps.reshape(1, B_W))


def _merge_kernel(x_ref, oa_ref, ob_ref, oc_ref, mg_ref, wa_ref, wb_ref, wc_ref, wo_ref, o_ref):
    gate = lambda n: _sigmoid(mg_ref[:, n * D_MODEL:(n + 1) * D_MODEL])
    h = (gate(0) * _dot(oa_ref[...], wa_ref[...]) + gate(1) * _dot(ob_ref[...], wb_ref[...])
         + gate(2) * _dot(oc_ref[...], wc_ref[...]))
    o_ref[...] = x_ref[...] + _dot(h, wo_ref[...])


def _merge(x, oa, ob, oc, gates, wa, wb, wc, wo, tm):
    m = x.shape[0]
    row = lambda n: pl.BlockSpec((tm, n), lambda i: (i, 0))
    full = lambda a: pl.BlockSpec(a.shape, lambda i: (0, 0))
    return pl.pallas_call(
        _merge_kernel,
        grid=(m // tm,),
        in_specs=[row(D_MODEL), row(A_W), row(B_W), row(C_W), row(3 * D_MODEL),
                  full(wa), full(wb), full(wc), full(wo)],
        out_specs=row(D_MODEL),
        out_shape=jax.ShapeDtypeStruct((m, D_MODEL), F32),
        compiler_params=_cparams(("parallel",)),
        name="merge_branches",
    )(x, oa, ob, oc, gates, wa, wb, wc, wo)


def _mem_kv_kernel(mem_ref, g_ref, wk_ref, wv_ref, kg_ref, k_ref, v_ref):
    mn = _rms(mem_ref[...], g_ref[...]).astype(BF16)
    k = jnp.dot(mn, wk_ref[...], preferred_element_type=F32)
    v_ref[...] = jnp.dot(mn, wv_ref[...], preferred_element_type=F32)
    k_ref[...] = jnp.concatenate(
        [_rms(k[:, h * M_HD:(h + 1) * M_HD], kg_ref[...]) for h in range(M_HEADS)], axis=1)


def _mem_kv(mem, g, wk, wv, kg):
    m = mem.shape[0]
    tm = N_MEM
    row = lambda n: pl.BlockSpec((tm, n), lambda i: (i, 0))
    full = lambda a: pl.BlockSpec(a.shape, lambda i: (0, 0))
    g2, kg2 = g.reshape(1, D_MODEL), kg.reshape(1, M_HD)
    return pl.pallas_call(
        _mem_kv_kernel,
        grid=(m // tm,),
        in_specs=[row(D_MODEL), full(g2), full(wk), full(wv), full(kg2)],
        out_specs=[row(M_W), row(M_W)],
        out_shape=[jax.ShapeDtypeStruct((m, M_W), F32), jax.ShapeDtypeStruct((m, M_W), F32)],
        compiler_params=_cparams(("parallel",)),
        name="mem_kv",
    )(mem, g2, wk, wv, kg2)


def _mem_ffn_kernel(x_ref, mk_ref, mv_ref, gx_ref, wq_ref, qg_ref, wo_ref, gf_ref, w1_ref, w2_ref,
                    o_ref, xn_ref, *, tm, rows_per_batch, nkb):
    j = pl.program_id(1)

    @pl.when(j == 0)
    def _():
        x = x_ref[...]
        q = _dot(_rms(x, gx_ref[...]), wq_ref[...])
        mk = mk_ref[...].reshape(nkb * N_MEM, M_W)
        mv = mv_ref[...].reshape(nkb * N_MEM, M_W)
        if nkb > 1:
            qb = lax.broadcasted_iota(jnp.int32, (tm, nkb * N_MEM), 0) // rows_per_batch
            kb = lax.broadcasted_iota(jnp.int32, (tm, nkb * N_MEM), 1) // N_MEM
            same = qb == kb
        outs = []
        for h in range(M_HEADS):
            sl = slice(h * M_HD, (h + 1) * M_HD)
            qh = _rms(q[:, sl], qg_ref[...])
            logits = _dot_nt(qh, mk[:, sl]) * (M_HD ** -0.5)
            if nkb > 1:
                logits = jnp.where(same, logits, NEG)
            mx = jnp.max(logits, axis=-1, keepdims=True)
            p = jnp.exp(logits - mx)
            p = p / jnp.sum(p, axis=-1, keepdims=True)
            outs.append(_dot(p, mv[:, sl]))
        o = jnp.concatenate(outs, axis=1)
        xm = x + _dot(o, wo_ref[...])
        o_ref[...] = xm
        xn_ref[...] = _rms(xm, gf_ref[...]).astype(BF16)

    h1 = jnp.dot(xn_ref[...], w1_ref[...], preferred_element_type=F32)
    h1 = jnp.square(jnp.maximum(h1, 0.0))
    o_ref[...] += _dot(h1, w2_ref[...])


def _mem_ffn(x, mk, mv, gx, wq, qg, wo, gf, w1, w2, tm, rows_per_batch, tf=1024):
    m = x.shape[0]
    nkb = max(tm // rows_per_batch, 1)
    kern = functools.partial(_mem_ffn_kernel, tm=tm, rows_per_batch=rows_per_batch, nkb=nkb)
    full = lambda a: pl.BlockSpec(a.shape, lambda i, j: (0,) * a.ndim)
    gx2, qg2, gf2 = gx.reshape(1, D_MODEL), qg.reshape(1, M_HD), gf.reshape(1, D_MODEL)
    mem_spec = pl.BlockSpec((nkb, N_MEM, M_W), lambda i, j: ((i * tm) // (rows_per_batch * nkb), 0, 0))
    return pl.pallas_call(
        kern,
        grid=(m // tm, D_FF // tf),
        in_specs=[pl.BlockSpec((tm, D_MODEL), lambda i, j: (i, 0)), mem_spec, mem_spec,
                  full(gx2), full(wq), full(qg2), full(wo), full(gf2),
                  pl.BlockSpec((D_MODEL, tf), lambda i, j: (0, j)),
                  pl.BlockSpec((tf, D_MODEL), lambda i, j: (j, 0))],
        out_specs=pl.BlockSpec((tm, D_MODEL), lambda i, j: (i, 0)),
        out_shape=jax.ShapeDtypeStruct((m, D_MODEL), F32),
        scratch_shapes=[pltpu.VMEM((tm, D_MODEL), BF16)],
        compiler_params=_cparams(("parallel", "arbitrary")),
        name="mem_ffn",
    )(x, mk, mv, gx2, wq, qg2, wo, gf2, w1, w2)


def _half_ones():
    i = lax.broadcasted_iota(jnp.int32, (LANES, LANES), 0) // C_HD
    j = lax.broadcasted_iota(jnp.int32, (LANES, LANES), 1) // C_HD
    return (i == j).astype(BF16)


def _rms_halves(x, g, ones):
    ms = _dot_exact_rhs(x * x, ones) * (1.0 / C_HD)
    return x * lax.rsqrt(ms + RMS_EPS) * g


def _nsa_prep_kernel(q_ref, c_ref, s_ref, w_ref, qg_ref, kg_ref,
                     qn_ref, ks16_ref, vs16_ref, kw16_ref, vw16_ref, kc_ref, vc_ref, rows_ref, *, tm):
    ones = _half_ones()
    scale = C_HD ** -0.5 * LOG2E
    q = q_ref[0]
    for n in range(C_HPG):
        qn_ref[0, n] = (_rms_halves(q[:, n * LANES:(n + 1) * LANES], qg_ref[...], ones) * scale).T.astype(BF16)
    c = c_ref[0]
    rows_ref[0, 0] = c[:, :LANES].T
    rows_ref[0, 1] = c[:, LANES:].T
    s = s_ref[0]
    ks = _rms_halves(s[:, :LANES], kg_ref[1:2, :], ones)
    vs_t = s[:, LANES:].T
    ks16_ref[0] = ks.astype(BF16)
    vs16_ref[0] = vs_t.astype(BF16)
    rows_ref[0, 2] = ks.T
    rows_ref[0, 3] = vs_t
    w = w_ref[0]
    kw = _rms_halves(w[:, :LANES], kg_ref[2:3, :], ones)
    vw_t = w[:, LANES:].T
    kw16_ref[0] = kw.astype(BF16)
    vw16_ref[0] = vw_t.astype(BF16)
    rows_ref[0, 4] = kw.T
    rows_ref[0, 5] = vw_t
    nb = tm // CMP_BLOCK
    bi = lax.broadcasted_iota(jnp.int32, (nb, tm), 0)
    ti = lax.broadcasted_iota(jnp.int32, (nb, tm), 1) // CMP_BLOCK
    pool = (bi == ti).astype(BF16)
    means = _dot_exact_lhs(pool, c) * (1.0 / CMP_BLOCK)
    kc_ref[0] = _rms_halves(means[:, :LANES], kg_ref[0:1, :], ones)
    vc_ref[0] = means[:, LANES:]


def _nsa_prep(proj, qg, kg, tm):
    b, t, _ = proj.shape
    nbt = tm // CMP_BLOCK
    kern = functools.partial(_nsa_prep_kernel, tm=tm)
    tok = lambda n, dt: jax.ShapeDtypeStruct((b, t, n), dt)
    blk = jax.ShapeDtypeStruct((b, t // CMP_BLOCK, LANES), F32)
    col = lambda width, off: pl.BlockSpec((1, tm, width), lambda bi, i: (bi, i, off // width))
    out_tok = lambda n: pl.BlockSpec((1, tm, n), lambda bi, i: (bi, i, 0))
    out_blk = pl.BlockSpec((1, nbt, LANES), lambda bi, i: (bi, i, 0))
    out_t = pl.BlockSpec((1, LANES, tm), lambda bi, i: (bi, 0, i))
    tok_t = jax.ShapeDtypeStruct((b, LANES, t), BF16)
    return pl.pallas_call(
        kern,
        grid=(b, t // tm),
        in_specs=[col(C_W, P_Q), col(2 * LANES, P_KV), col(2 * LANES, P_KV + 2 * LANES),
                  col(2 * LANES, P_KV + 4 * LANES),
                  pl.BlockSpec((1, LANES), lambda bi, i: (0, 0)),
                  pl.BlockSpec((3, LANES), lambda bi, i: (0, 0))],
        out_specs=[pl.BlockSpec((1, C_HPG, LANES, tm), lambda bi, i: (bi, 0, 0, i)),
                   out_tok(LANES), out_t, out_tok(LANES), out_t, out_blk, out_blk,
                   pl.BlockSpec((1, 6, LANES, tm), lambda bi, i: (bi, 0, 0, i))],
        out_shape=[jax.ShapeDtypeStruct((b, C_HPG, LANES, t), BF16),
                   tok(LANES, BF16), tok_t, tok(LANES, BF16), tok_t, blk, blk,
                   jax.ShapeDtypeStruct((b, 6, LANES, t), F32)],
        compiler_params=_cparams(("parallel", "parallel")),
        name="nsa_prep",
    )(proj, proj, proj, proj, jnp.tile(qg.reshape(1, C_HD), (1, 2)), jnp.tile(kg, (1, 2)))


M_INIT = -1e29
ROWS = C_HPG * Q_TILE
FAR_TILES = 8
FRONT_TILES = max(WINDOW // Q_TILE, FAR_TILES)
SEL_PAD = 2 * FRONT_TILES
SUM_ROWS = 16
LOG2E = 1.4426950408889634


def _heads(x):
    return jnp.concatenate([x] * C_HPG, axis=1)


def _softmax_step(qts, k, v_t, state, tail_bias=None, valid=None, block_bias=None):
    m_ref, l_ref, acc_ref = state
    n_keys = k.shape[0]
    old = [(m_ref[g], l_ref[g], acc_ref[g]) for g in range(C_KV)]
    v_ext = jnp.concatenate([v_t, jnp.ones((SUM_ROWS, n_keys), BF16)], axis=0)
    if block_bias is not None:
        blk = lax.broadcasted_iota(jnp.int32, (n_keys, LANES), 0) // CMP_BLOCK
        col = lax.broadcasted_iota(jnp.int32, (n_keys, LANES), 1)
        k = jnp.concatenate([k, (blk == col).astype(BF16)], axis=1)
    new = []
    for g in range(C_KV):
        m_old, l_old, acc_old = old[g]
        rhs = qts[g] if block_bias is None else jnp.concatenate([qts[g], block_bias[g]], axis=0)
        s = jnp.dot(k, rhs, preferred_element_type=F32)
        if tail_bias is not None:
            cut = s.shape[0] - tail_bias[g].shape[0]
            tail = s[cut:] + tail_bias[g]
            s = jnp.concatenate([s[:cut], tail], axis=0) if cut else tail
        if valid is not None:
            s = jnp.where(_heads(valid[g]), s, NEG)
        s = s.astype(BF16)
        m_new = jnp.maximum(m_old, jnp.max(s, axis=0, keepdims=True).astype(F32))
        alpha = jnp.exp2(m_old - m_new)
        p = jnp.exp2(s - m_new.astype(BF16))
        pv = jnp.dot(v_ext, p, preferred_element_type=F32)
        new.append((m_new, alpha * l_old + pv[LANES:LANES + 1], alpha * acc_old + pv[:LANES]))
    for g in range(C_KV):
        m_ref[g], l_ref[g], acc_ref[g] = new[g]


def _softmax_reset(state):
    m_ref, l_ref, acc_ref = state
    m_ref[...] = jnp.full(m_ref.shape, M_INIT, F32)
    l_ref[...] = jnp.zeros(l_ref.shape, F32)
    acc_ref[...] = jnp.zeros(acc_ref.shape, F32)


def _softmax_result(state, g):
    _, l_ref, acc_ref = state
    l = l_ref[g]
    return acc_ref[g] / jnp.where(l > 0.0, l, 1.0)


def _top_blocks(score, n_sel):
    nb = score.shape[0]
    blk = lax.broadcasted_iota(jnp.int32, score.shape, 0).astype(F32)
    sel = jnp.zeros(score.shape, F32)
    work = score
    for _ in range(min(n_sel, nb)):
        mx = jnp.max(work, axis=0, keepdims=True)
        idx = jnp.min(jnp.where(work == mx, blk, float(nb)), axis=0, keepdims=True)
        pick = blk == idx
        sel = jnp.where(pick, 1.0, sel)
        work = jnp.where(pick, -jnp.inf, work)
    return sel


def _nsa_attn_kernel(qt_ref, ks_ref, vs_ref, kw_ref, vw_ref, kc_ref, vc_ref, cg_ref, bnear_ref,
                     bcmp_ref, o_ref, m_ref, l_ref, acc_ref, sel_ref, *, nb):
    i = pl.program_id(1)
    lane = lax.broadcasted_iota(jnp.int32, (Q_TILE, LANES), 1)
    key = lax.broadcasted_iota(jnp.int32, (Q_TILE, LANES), 0)
    causal = key <= lane
    sig_t = _sigmoid(cg_ref[0]).T
    kc = kc_ref[0].astype(BF16)
    vc_t = vc_ref[0].T.astype(BF16)
    blk = lax.broadcasted_iota(jnp.int32, (nb, Q_TILE), 0)
    qpos = i * Q_TILE + lax.broadcasted_iota(jnp.int32, (nb, Q_TILE), 1)
    cur = qpos // CMP_BLOCK
    cmp_valid = _heads(qpos >= blk * CMP_BLOCK + (CMP_BLOCK - 1))
    forced = (blk == 0) | (blk == cur) | (blk == cur - 1)
    state = (m_ref, l_ref, acc_ref)
    n_back = WINDOW // Q_TILE
    k_rows = lambda ref, j, n: ref[0, pl.ds(pl.multiple_of((j + FRONT_TILES) * Q_TILE, Q_TILE), n * Q_TILE), :]
    v_cols = lambda ref, j, n: ref[0, :, pl.ds(pl.multiple_of((j + FRONT_TILES) * Q_TILE, Q_TILE), n * Q_TILE)]

    def key_mask(g, j, n):
        rows = [sel_ref[g, pl.ds(SEL_PAD + 2 * j + r, 1), :] for r in range(2 * n)]
        return jnp.concatenate(
            [jnp.where(key < CMP_BLOCK, rows[2 * t], rows[2 * t + 1]) for t in range(n)], axis=0) > 0.5

    def block_bias(g, j, n, tiles=None):
        rows = sel_ref[g, pl.ds(SEL_PAD + 2 * j, 2 * n), :]
        if tiles is not None:
            t = j + lax.broadcasted_iota(jnp.int32, (2 * n, Q_TILE), 0) // 2
            rows = jnp.where((t >= tiles[0]) & (t < tiles[1]), rows, 0.0)
        bias = _heads((rows - 1.0) * (-NEG))
        return jnp.concatenate([bias, jnp.zeros((LANES - 2 * n, ROWS), F32)], axis=0).astype(BF16)

    q_t = jnp.concatenate([qt_ref[0, n] for n in range(C_HPG)], axis=1)
    feat = lax.broadcasted_iota(jnp.int32, (LANES, ROWS), 0) // C_HD
    qts = [jnp.where(feat == g, q_t, jnp.zeros_like(q_t)) for g in range(C_KV)]

    o_cmp, scores = [], []
    for g in range(C_KV):
        bias_c = pltpu.roll(bcmp_ref[g], (2 * i + 2) % LANES, 0)[:nb, :]
        lc = jnp.dot(kc, qts[g], preferred_element_type=F32) + bias_c
        lc = jnp.where(cmp_valid, lc, NEG)
        mc = jnp.maximum(jnp.max(lc, axis=0, keepdims=True), M_INIT)
        pc = jnp.exp2(lc - mc)
        den = jnp.sum(pc, axis=0, keepdims=True)
        pc = pc / jnp.where(den > 0.0, den, 1.0)
        o_cmp.append(jnp.dot(vc_t, pc.astype(BF16), preferred_element_type=F32))
        psum = sum(pc[:, n * Q_TILE:(n + 1) * Q_TILE] for n in range(C_HPG))
        scores.append(jnp.where((blk > cur) | forced, NEG, psum))

    score = jnp.concatenate(scores, axis=1)
    sel = jnp.where(score > 0.5 * NEG, _top_blocks(score, N_SEL - N_FORCED), 0.0)
    sel = jnp.where(jnp.concatenate([forced] * C_KV, axis=1), 1.0, sel)
    for g in range(C_KV):
        sel_ref[g, 0:SEL_PAD, :] = jnp.zeros((SEL_PAD, Q_TILE), F32)
        sel_ref[g, SEL_PAD:SEL_PAD + nb, :] = sel[:, g * Q_TILE:(g + 1) * Q_TILE]

    _softmax_reset(state)
    n_far = jnp.maximum(i - 1, 0)

    def far_step(j, tiles=None):
        _softmax_step(qts, k_rows(ks_ref, j, FAR_TILES), v_cols(vs_ref, j, FAR_TILES), state,
                      block_bias=[block_bias(g, j, FAR_TILES, tiles) for g in range(C_KV)])

    def far_many(jq, carry):
        far_step(FAR_TILES * jq)
        return carry

    n_many = n_far // FAR_TILES
    lax.fori_loop(0, n_many, far_many, 0)

    @pl.when(n_far % FAR_TILES != 0)
    def _():
        far_step(n_far - FAR_TILES, (FAR_TILES * n_many, n_far))

    near_causal = jnp.concatenate([key >= 0, causal], axis=0)
    bias2 = [bnear_ref[g] for g in range(C_KV)]
    _softmax_step(qts, k_rows(ks_ref, i - 1, 2), v_cols(vs_ref, i - 1, 2), state, tail_bias=bias2,
                  valid=[key_mask(g, i - 1, 2) & near_causal for g in range(C_KV)])
    o_slc = [_softmax_result(state, g) for g in range(C_KV)]

    _softmax_reset(state)
    exists = lambda back: key >= jnp.where(i >= back, 0, Q_TILE)
    win_valid = jnp.concatenate(
        [(key > lane) & exists(n_back)] + [exists(back) for back in range(n_back - 1, 0, -1)] + [causal],
        axis=0)
    _softmax_step(qts, k_rows(kw_ref, i - n_back, n_back + 1), v_cols(vw_ref, i - n_back, n_back + 1), state,
                  tail_bias=bias2, valid=[win_valid] * C_KV)
    o_win = [_softmax_result(state, g) for g in range(C_KV)]

    low = (key // C_HD) == 0
    for n in range(C_HPG):
        cols = slice(n * Q_TILE, (n + 1) * Q_TILE)
        per_group = []
        for g in range(C_KV):
            h = g * C_HPG + n
            gate = lambda br: sig_t[br * C_HEADS + h:br * C_HEADS + h + 1, :]
            per_group.append(gate(0) * o_cmp[g][:, cols] + gate(1) * o_slc[g][:, cols]
                             + gate(2) * o_win[g][:, cols])
        o_ref[0, :, n * LANES:(n + 1) * LANES] = jnp.where(low, per_group[0], per_group[1]).T.astype(o_ref.dtype)


def _nsa_attn(qt, ks16, vs16t, kw16, vw16t, kc, vc, gates, bnear, bcmp):
    b, _, _, t = qt.shape
    nb = t // CMP_BLOCK
    tp = t + FRONT_TILES * Q_TILE
    kern = functools.partial(_nsa_attn_kernel, nb=nb)
    seq = pl.BlockSpec((1, tp, LANES), lambda bi, i: (bi, 0, 0))
    seq_t = pl.BlockSpec((1, LANES, tp), lambda bi, i: (bi, 0, 0))
    blk = pl.BlockSpec((1, nb, LANES), lambda bi, i: (bi, 0, 0))
    return pl.pallas_call(
        kern,
        grid=(b, t // Q_TILE),
        in_specs=[pl.BlockSpec((1, C_HPG, LANES, Q_TILE), lambda bi, i: (bi, 0, 0, i)),
                  seq, seq_t, seq, seq_t, blk, blk,
                  pl.BlockSpec((1, Q_TILE, LANES), lambda bi, i: (bi, i, G_CG // LANES)),
                  pl.BlockSpec(bnear.shape, lambda bi, i: (0, 0, 0)),
                  pl.BlockSpec(bcmp.shape, lambda bi, i: (0, 0, 0))],
        out_specs=pl.BlockSpec((1, Q_TILE, C_W), lambda bi, i: (bi, i, 0)),
        out_shape=jax.ShapeDtypeStruct((b, t, C_W), BRANCH_DTYPE),
        scratch_shapes=[pltpu.VMEM((C_KV, 1, ROWS), F32), pltpu.VMEM((C_KV, 1, ROWS), F32),
                        pltpu.VMEM((C_KV, LANES, ROWS), F32),
                        pltpu.VMEM((C_KV, SEL_PAD + nb, Q_TILE), F32)],
        compiler_params=_cparams(("parallel", "arbitrary")),
        name="nsa_attn",
    )(qt, ks16, vs16t, kw16, vw16t, kc, vc, gates, bnear, bcmp)


def _rel_bucket(dist):
    n = jnp.maximum(dist, 0)
    exact = N_BUCKETS // 2
    ratio = jnp.log(jnp.maximum(n, 1).astype(F32) / exact) / math.log(MAX_DIST / exact)
    large = jnp.minimum(exact + (ratio * (N_BUCKETS - exact)).astype(jnp.int32), N_BUCKETS - 1)
    return jnp.where(n < exact, n, large)


def _one_hot(idx, n):
    return (jnp.clip(idx, 0, n - 1)[..., None] == jnp.arange(n)).astype(F32)


def _distance_table(rel_bias):
    tbl = jnp.dot(_one_hot(_rel_bucket(jnp.arange(MAX_DIST)), N_BUCKETS), rel_bias,
                  precision=lax.Precision.HIGHEST)
    return (tbl - tbl[MAX_DIST - 1:MAX_DIST, :]).T


def _bias_tables(rel_bias):
    tbl = _distance_table(rel_bias) * LOG2E
    tbl = tbl.reshape(C_KV, C_HPG, MAX_DIST)
    qq = jnp.arange(Q_TILE)[:, None]
    cc = jnp.arange(LANES)[None, :]
    look = lambda dist: jnp.einsum('ghd,qcd->ghqc', tbl, _one_hot(dist, MAX_DIST),
                                   precision=lax.Precision.HIGHEST).reshape(C_KV, ROWS, LANES)
    bnear = jnp.stack([look(qq - cc), look(qq - cc + Q_TILE)], axis=1)
    bcmp = look(qq - CMP_BLOCK * cc + CMP_BLOCK * (LANES - 1) - (LANES - 1))
    bnear = bnear.transpose(0, 1, 3, 2)
    return jnp.concatenate([bnear[:, 1], bnear[:, 0]], axis=1), bcmp.transpose(0, 2, 1)


PAGES_PER_STEP = 16
S_ROWS = C_KV * C_HPG
BLK_LANES = 384
NEW_PAD = 8


def _page_specs(layer, n):
    spec = lambda j: pl.BlockSpec((1, 1, LANES, PAGE_SIZE),
                                  lambda b, c, pt, j=j: (layer, pt[b, c * n + j], 0, 0))
    return [spec(j) for j in range(n)]


def _cmp_pages_kernel(pt_ref, *refs, n):
    k_refs, v_refs, (kc_ref, vc_ref) = refs[:n], refs[n:2 * n], refs[2 * n:]
    row = lax.broadcasted_iota(jnp.int32, (n * PAGE_SIZE, 2 * n), 0) // CMP_BLOCK
    col = lax.broadcasted_iota(jnp.int32, (n * PAGE_SIZE, 2 * n), 1)
    pool = (row == col).astype(BF16)
    for refs_in, out in ((k_refs, kc_ref), (v_refs, vc_ref)):
        pages = jnp.concatenate([r[0, 0] for r in refs_in], axis=1)
        out[0, 0] = _dot_exact_rhs(pages, pool) * (1.0 / CMP_BLOCK)


def _cmp_pages(page_table, cache_k, cache_v, layer):
    b, n_pages = page_table.shape
    n = min(PAGES_PER_STEP, n_pages)
    chunks = n_pages // n
    out = jax.ShapeDtypeStruct((b, chunks, LANES, 2 * n), F32)
    out_spec = pl.BlockSpec((1, 1, LANES, 2 * n), lambda bi, c, pt: (bi, c, 0, 0))
    return pl.pallas_call(
        functools.partial(_cmp_pages_kernel, n=n),
        grid_spec=pltpu.PrefetchScalarGridSpec(
            num_scalar_prefetch=1, grid=(b, chunks),
            in_specs=_page_specs(layer, n) + _page_specs(layer, n),
            out_specs=[out_spec, out_spec]),
        out_shape=[out, out],
        compiler_params=_cparams(("parallel", "arbitrary")),
        name="nsa_sample_cmp_pages",
    )(page_table, *([cache_k] * n), *([cache_v] * n))


def _top_blocks_lanes(score, n_sel):
    nb = score.shape[-1]
    blk = lax.broadcasted_iota(jnp.int32, score.shape, 1).astype(F32)
    sel = jnp.zeros(score.shape, F32)
    work = score
    for _ in range(min(n_sel, nb)):
        mx = jnp.max(work, axis=-1, keepdims=True)
        idx = jnp.min(jnp.where(work == mx, blk, float(nb)), axis=-1, keepdims=True)
        pick = blk == idx
        sel = jnp.where(pick, 1.0, sel)
        work = jnp.where(pick, -jnp.inf, work)
    return sel


def _nsa_sample_head_kernel(q_ref, newc_ref, news_ref, neww_ref, kc_ref, vc_ref, wk_ref, wv_ref,
                            qg_ref, kg_ref, kgc_ref, bc_ref, bw_ref, bn_ref,
                            q32_ref, sel_ref, ocmp_ref, owin_ref, new_ref, q_scr, *, ts, past, wbuf):
    n_rows = S_ROWS * ts
    ones = _half_ones()
    scale = C_HD ** -0.5
    lane = lax.broadcasted_iota(jnp.int32, (ts, LANES), 1)
    q = q_ref[0]
    for n in range(C_HPG):
        qn = _rms_halves(q[:, n * LANES:(n + 1) * LANES], qg_ref[...], ones) * scale
        for g in range(C_KV):
            lo = (g * C_HPG + n) * ts
            q_scr[lo:lo + ts, :] = jnp.where((lane // C_HD) == g, qn, 0.0)
    q32 = q_scr[...].astype(BF16)
    q32_ref[0] = q32

    newc, news, neww = newc_ref[0], news_ref[0], neww_ref[0]
    ks_new = _rms_halves(news[:, :LANES], kg_ref[1:2, :], ones)
    kw_new = _rms_halves(neww[:, :LANES], kg_ref[2:3, :], ones)
    new_ref[0, 0] = newc[:, :LANES]
    new_ref[0, 1] = newc[:, LANES:]
    new_ref[0, 2] = ks_new
    new_ref[0, 3] = news[:, LANES:]
    new_ref[0, 4] = kw_new
    new_ref[0, 5] = neww[:, LANES:]

    t_of_row = lax.broadcasted_iota(jnp.int32, (n_rows, 1), 0) % ts
    qpos = past + t_of_row

    n_cached = past // CMP_BLOCK
    pad_rows = jnp.zeros((NEW_PAD - ts, LANES), F32)
    ones_rows = jnp.full((NEW_PAD, BLK_LANES), 1.0, BF16)
    new_mean = lambda rows: _dot_tn_exact(jnp.concatenate([rows, pad_rows], axis=0),
                                          ones_rows) * (1.0 / CMP_BLOCK)
    is_new = lax.broadcasted_iota(jnp.int32, (LANES, BLK_LANES), 1) == n_cached
    kc_all = jnp.where(is_new, new_mean(newc[:, :LANES]), kc_ref[0])
    vc_all = jnp.where(is_new, new_mean(newc[:, LANES:]), vc_ref[0])
    feat = lax.broadcasted_iota(jnp.int32, (LANES, 1), 0) // C_HD
    sq = kc_all * kc_all
    ms = jnp.where(feat == 0, jnp.sum(sq[:C_HD], axis=0, keepdims=True),
                   jnp.sum(sq[C_HD:], axis=0, keepdims=True)) * (1.0 / C_HD)
    kc_n = kc_all * lax.rsqrt(ms + RMS_EPS) * kgc_ref[:, 0:1]
    blk = lax.broadcasted_iota(jnp.int32, (n_rows, BLK_LANES), 1)
    lc = jnp.dot(q32, kc_n.astype(BF16), preferred_element_type=F32) + bc_ref[...]
    lc = jnp.where(qpos >= blk * CMP_BLOCK + (CMP_BLOCK - 1), lc, NEG)
    mc = jnp.maximum(jnp.max(lc, axis=-1, keepdims=True), M_INIT)
    pc = jnp.exp(lc - mc)
    den = jnp.sum(pc, axis=-1, keepdims=True)
    pc = pc / jnp.where(den > 0.0, den, 1.0)
    ocmp_ref[0] = lax.dot_general(pc.astype(BF16), vc_all.astype(BF16), (((1,), (1,)), ((), ())),
                                  preferred_element_type=F32)

    gi = lax.broadcasted_iota(jnp.int32, (C_KV * ts, n_rows), 0)
    ri = lax.broadcasted_iota(jnp.int32, (C_KV * ts, n_rows), 1)
    same = ((gi // ts) == (ri // (C_HPG * ts))) & ((gi % ts) == (ri % ts))
    psum = _dot_exact_lhs(same.astype(BF16), pc, terms=3)
    sblk = lax.broadcasted_iota(jnp.int32, (C_KV * ts, BLK_LANES), 1)
    scur = (past + lax.broadcasted_iota(jnp.int32, (C_KV * ts, 1), 0) % ts) // CMP_BLOCK
    forced = (sblk == 0) | (sblk == scur) | (sblk == scur - 1)
    score = jnp.where((sblk > scur) | forced, NEG, psum)
    sel = jnp.where(score > 0.5 * NEG, _top_blocks_lanes(score, N_SEL - N_FORCED), 0.0)
    sel = jnp.where(forced, 1.0, sel)
    sel_ref[0] = lax.dot_general(same.astype(BF16), sel.astype(BF16), (((0,), (0,)), ((), ())),
                                 preferred_element_type=F32)

    key = lax.broadcasted_iota(jnp.int32, (n_rows, wbuf), 1)
    lw = jnp.dot(q32, wk_ref[0, 0].astype(BF16), preferred_element_type=F32) + bw_ref[...]
    kpos = past - wbuf + key
    dist = qpos - kpos
    lw = jnp.where((dist >= 0) & (dist < WINDOW) & (kpos >= 0), lw, NEG)
    kw_pad = jnp.concatenate([kw_new, pad_rows], axis=0)
    vw_pad = jnp.concatenate([neww[:, LANES:], pad_rows], axis=0)
    nkey = lax.broadcasted_iota(jnp.int32, (n_rows, NEW_PAD), 1)
    ln = lax.dot_general(q32, kw_pad.astype(BF16), (((1,), (1,)), ((), ())),
                         preferred_element_type=F32) + bn_ref[...]
    ln = jnp.where((nkey <= t_of_row) & (nkey < ts), ln, NEG)
    mw = jnp.maximum(jnp.max(lw, axis=-1, keepdims=True), jnp.max(ln, axis=-1, keepdims=True))
    pw, pn = jnp.exp(lw - mw), jnp.exp(ln - mw)
    den = jnp.sum(pw, axis=-1, keepdims=True) + jnp.sum(pn, axis=-1, keepdims=True)
    ow = lax.dot_general(pw.astype(BF16), wv_ref[0, 0].astype(BF16), (((1,), (1,)), ((), ())),
                         preferred_element_type=F32) + _dot(pn, vw_pad)
    owin_ref[0] = ow / den


def _dot_tn_exact(a, b01, terms=2):
    dims = (((0,), (0,)), ((), ()))
    return sum(lax.dot_general(p, b01, dims, preferred_element_type=F32) for p in _split(a, terms))


def _nsa_sample_head(main, kc_t, vc_t, win_k, win_v, layer, qg, kg, tables, past):
    b, ts, _ = main.shape
    wbuf = win_k.shape[-1]
    n_rows = S_ROWS * ts
    bias_c, bias_w, bias_n = tables
    kern = functools.partial(_nsa_sample_head_kernel, ts=ts, past=past, wbuf=wbuf)
    col = lambda width, off: pl.BlockSpec((1, ts, width), lambda i: (i, 0, off // width))
    full = lambda a: pl.BlockSpec(a.shape, lambda i: (0,) * a.ndim)
    per_b = lambda a: pl.BlockSpec((1,) + a.shape[1:], lambda i: (i,) + (0,) * (a.ndim - 1))
    win = pl.BlockSpec((1, 1, LANES, wbuf), lambda i: (layer, i, 0, 0))
    qg2 = jnp.tile(qg.reshape(1, C_HD), (1, 2))
    kg2 = jnp.tile(kg, (1, 2))
    kg_col = jnp.tile(kg2[0].reshape(LANES, 1), (1, LANES))
    rows = lambda n, dt: jax.ShapeDtypeStruct((b, n_rows, n), dt)
    out_rows = lambda n: pl.BlockSpec((1, n_rows, n), lambda i: (i, 0, 0))
    return pl.pallas_call(
        kern,
        grid=(b,),
        in_specs=[col(C_W, P_Q), col(2 * LANES, P_KV), col(2 * LANES, P_KV + 2 * LANES),
                  col(2 * LANES, P_KV + 4 * LANES), per_b(kc_t), per_b(vc_t), win, win,
                  full(qg2), full(kg2), full(kg_col), full(bias_c), full(bias_w), full(bias_n)],
        out_specs=[out_rows(LANES), out_rows(BLK_LANES), out_rows(LANES), out_rows(LANES),
                   pl.BlockSpec((1, 6, ts, LANES), lambda i: (i, 0, 0, 0))],
        out_shape=[rows(LANES, BF16), rows(BLK_LANES, F32), rows(LANES, F32), rows(LANES, F32),
                   jax.ShapeDtypeStruct((b, 6, ts, LANES), F32)],
        scratch_shapes=[pltpu.VMEM((n_rows, LANES), F32)],
        compiler_params=_cparams(("parallel",)),
        name="nsa_sample_head",
    )(main, main, main, main, kc_t, vc_t, win_k, win_v, qg2, kg2, kg_col, bias_c, bias_w, bias_n)


def _slc_pages_kernel(pt_ref, *refs, n, ts, last_chunk):
    k_refs, v_refs = refs[:n], refs[n:2 * n]
    (q_ref, sel_ref, ocmp_ref, owin_ref, gate_ref, new_ref, blast_ref, bn_ref,
     o_ref, m_ref, l_ref, acc_ref) = refs[2 * n:]
    c = pl.program_id(1)
    n_rows = S_ROWS * ts
    nt = (((1,), (1,)), ((), ()))

    @pl.when(c == 0)
    def _():
        m_ref[...] = jnp.full(m_ref.shape, M_INIT, F32)
        l_ref[...] = jnp.zeros(l_ref.shape, F32)
        acc_ref[...] = jnp.zeros(acc_ref.shape, F32)

    q32 = q_ref[0]
    keys = jnp.concatenate([r[0, 0] for r in k_refs], axis=1).astype(BF16)
    vals = jnp.concatenate([r[0, 0] for r in v_refs], axis=1).astype(BF16)
    s = jnp.dot(q32, keys, preferred_element_type=F32)
    cut = (n - 1) * PAGE_SIZE
    tail = s[:, cut:] + jnp.where(c == last_chunk, blast_ref[...], 0.0)
    s = jnp.concatenate([s[:, :cut], tail], axis=1) if cut else tail
    row = lax.broadcasted_iota(jnp.int32, (LANES, n * PAGE_SIZE), 0)
    col = lax.broadcasted_iota(jnp.int32, (LANES, n * PAGE_SIZE), 1) // CMP_BLOCK
    expand = (row == col).astype(BF16)
    picked = jnp.dot(sel_ref[0, 0].astype(BF16), expand, preferred_element_type=F32) > 0.5
    s = jnp.where(picked, s, NEG)
    m_old = m_ref[...]
    m_new = jnp.maximum(m_old, jnp.max(s, axis=-1, keepdims=True))
    alpha = jnp.exp(m_old - m_new)
    p = jnp.exp(s - m_new)
    l_ref[...] = alpha * l_ref[...] + jnp.sum(p, axis=-1, keepdims=True)
    acc_ref[...] = alpha * acc_ref[...] + lax.dot_general(p.astype(BF16), vals, nt,
                                                          preferred_element_type=F32)
    m_ref[...] = m_new

    @pl.when(c == last_chunk)
    def _():
        t_of_row = lax.broadcasted_iota(jnp.int32, (n_rows, 1), 0) % ts
        pad_rows = jnp.zeros((NEW_PAD - ts, LANES), F32)
        k_new = jnp.concatenate([new_ref[0, 2], pad_rows], axis=0)
        v_new = jnp.concatenate([new_ref[0, 3], pad_rows], axis=0)
        nkey = lax.broadcasted_iota(jnp.int32, (n_rows, NEW_PAD), 1)
        sn = lax.dot_general(q32, k_new.astype(BF16), nt, preferred_element_type=F32) + bn_ref[...]
        sn = jnp.where((nkey <= t_of_row) & (nkey < ts), sn, NEG)
        m_prev = m_ref[...]
        m_fin = jnp.maximum(m_prev, jnp.max(sn, axis=-1, keepdims=True))
        a_fin = jnp.exp(m_prev - m_fin)
        pn = jnp.exp(sn - m_fin)
        l_fin = a_fin * l_ref[...] + jnp.sum(pn, axis=-1, keepdims=True)
        o_slc = (a_fin * acc_ref[...] + _dot(pn, v_new)) / l_fin

        ri = lax.broadcasted_iota(jnp.int32, (n_rows, NEW_PAD), 0) % ts
        ti = lax.broadcasted_iota(jnp.int32, (n_rows, NEW_PAD), 1)
        gates = jnp.concatenate([_sigmoid(gate_ref[0]), pad_rows], axis=0)
        per_row = _dot_exact_lhs((ri == ti).astype(BF16), gates)
        head = lax.broadcasted_iota(jnp.int32, (n_rows, LANES), 0) // ts
        glane = lax.broadcasted_iota(jnp.int32, (n_rows, LANES), 1)
        gate = lambda br: jnp.sum(jnp.where(glane == br * C_HEADS + head, per_row, 0.0), axis=-1,
                                  keepdims=True)
        comb = gate(0) * ocmp_ref[0] + gate(1) * o_slc + gate(2) * owin_ref[0]
        low = (lax.broadcasted_iota(jnp.int32, (ts, LANES), 1) // C_HD) == 0
        for n_ in range(C_HPG):
            top = comb[n_ * ts:(n_ + 1) * ts]
            bot = comb[(C_HPG + n_) * ts:(C_HPG + n_ + 1) * ts]
            o_ref[0, :, n_ * LANES:(n_ + 1) * LANES] = jnp.where(low, top, bot)


def _slc_pages(page_table, cache_k, cache_v, layer, q32, sel_chunks, o_cmp, o_win, gates, new_rows,
               bias_last, bias_n):
    b, n_pages = page_table.shape
    n = min(PAGES_PER_STEP, n_pages)
    chunks = n_pages // n
    ts = new_rows.shape[2]
    n_rows = S_ROWS * ts
    kern = functools.partial(_slc_pages_kernel, n=n, ts=ts, last_chunk=chunks - 1)
    per_b = lambda a: pl.BlockSpec((1,) + a.shape[1:], lambda bi, c, pt: (bi,) + (0,) * (a.ndim - 1))
    full = lambda a: pl.BlockSpec(a.shape, lambda bi, c, pt: (0,) * a.ndim)
    return pl.pallas_call(
        kern,
        grid_spec=pltpu.PrefetchScalarGridSpec(
            num_scalar_prefetch=1, grid=(b, chunks),
            in_specs=_page_specs(layer, n) + _page_specs(layer, n) + [
                per_b(q32), pl.BlockSpec((1, 1, n_rows, LANES), lambda bi, c, pt: (bi, c, 0, 0)),
                per_b(o_cmp), per_b(o_win),
                pl.BlockSpec((1, ts, LANES), lambda bi, c, pt: (bi, 0, G_CG // LANES)),
                per_b(new_rows), full(bias_last), full(bias_n)],
            out_specs=pl.BlockSpec((1, ts, C_W), lambda bi, c, pt: (bi, 0, 0)),
            scratch_shapes=[pltpu.VMEM((n_rows, 1), F32), pltpu.VMEM((n_rows, 1), F32),
                            pltpu.VMEM((n_rows, LANES), F32)]),
        out_shape=jax.ShapeDtypeStruct((b, ts, C_W), F32),
        compiler_params=_cparams(("parallel", "arbitrary")),
        name="nsa_sample_slc_pages",
    )(page_table, *([cache_k] * n), *([cache_v] * n), q32, sel_chunks, o_cmp, o_win, gates, new_rows,
      bias_last, bias_n)


def _sample_bias_tables(rel_bias, past, ts, wbuf):
    tbl = jnp.repeat(_distance_table(rel_bias), ts, axis=0)
    t = jnp.tile(jnp.arange(ts), S_ROWS)[:, None]
    look = lambda dist: jnp.einsum('rd,rcd->rc', tbl, _one_hot(dist, MAX_DIST),
                                   precision=lax.Precision.HIGHEST)
    blk = jnp.arange(BLK_LANES)[None, :]
    bias_c = look(past + t - CMP_BLOCK * blk - (CMP_BLOCK - 1))
    bias_w = look(wbuf + t - jnp.arange(wbuf)[None, :])
    bias_n = look(t - jnp.arange(NEW_PAD)[None, :])
    bias_last = look(PAGE_SIZE + t - jnp.arange(PAGE_SIZE)[None, :])
    return bias_c, bias_w, bias_n, bias_last


def _nsa_sample(main, gate, caches, win_k, win_v, page_table, layer, qg, kg, rel_bias):
    b, ts, _ = main.shape
    n_pages = page_table.shape[1]
    past = n_pages * PAGE_SIZE
    wbuf = win_k.shape[-1]
    n_rows = S_ROWS * ts
    bias_c, bias_w, bias_n, bias_last = _sample_bias_tables(rel_bias, past, ts, wbuf)
    kc_chunks, vc_chunks = _cmp_pages(page_table, caches[0], caches[1], layer)
    n_cached = past // CMP_BLOCK
    unchunk = lambda z: z.transpose(0, 2, 1, 3).reshape(b, LANES, n_cached)
    fit = lambda z: jnp.pad(z, ((0, 0), (0, 0), (0, BLK_LANES - n_cached)))
    q32, sel, o_cmp, o_win, new_rows = _nsa_sample_head(
        main, fit(unchunk(kc_chunks)), fit(unchunk(vc_chunks)), win_k, win_v, layer, qg, kg,
        (bias_c, bias_w, bias_n), past)
    per_step = 2 * min(PAGES_PER_STEP, n_pages)
    sel_chunks = sel[:, :, :n_cached].reshape(b, n_rows, n_cached // per_step, per_step)
    sel_chunks = jnp.pad(sel_chunks.transpose(0, 2, 1, 3), ((0, 0), (0, 0), (0, 0), (0, LANES - per_step)))
    o = _slc_pages(page_table, caches[2], caches[3], layer, q32, sel_chunks, o_cmp, o_win,
                   gate.reshape(b, ts, N_GATE), new_rows, bias_last, bias_n)
    return o.reshape(b * ts, C_W), new_rows


def _shift_order(z):
    part = lambda off, n: z[..., off:off + n]
    return jnp.concatenate([part(OFF_R, A_W), part(OFF_K, A_W), part(OFF_V, A_W), part(OFF_WL, A_LORA_W),
                            part(OFF_AL, A_LORA_A), part(OFF_GL, A_LORA_G)], axis=-1)


def _layer_params(l, w_in, shift_mu, w_br_a, w_br_b, w_br_c, w_out, w_mq, w_mk, w_mv, w_mo, w_ff1, w_ff2):
    w = w_in[l]
    cast = lambda a: a.astype(BF16)
    cols = lambda off, n: cast(w[:, off:off + n])
    w_q = cols(OFF_Q, C_W).reshape(D_MODEL, C_KV, C_HPG, C_HD).transpose(0, 2, 1, 3).reshape(D_MODEL, C_W)
    w_main = jnp.concatenate([
        cols(OFF_R, A_W), cols(OFF_K, A_W), cols(OFF_V, A_W), cols(OFF_POOL, B_W), w_q,
        cols(OFF_KV, 6 * C_KVW), cols(OFF_WL, A_LORA_W), cols(OFF_AL, A_LORA_A), cols(OFF_GL, A_LORA_G)],
        axis=1)
    w_gate = jnp.concatenate([cols(OFF_MG, 3 * D_MODEL), cols(OFF_CG, 3 * C_HEADS),
                              jnp.zeros((D_MODEL, N_GATE - G_CG - 3 * C_HEADS), BF16)], axis=1)
    wc = w_br_c[l].reshape(C_KV, C_HPG, C_HD, D_MODEL).transpose(1, 0, 2, 3).reshape(C_W, D_MODEL)
    return dict(
        w_main=w_main, w_gate=w_gate, mu=_shift_order(shift_mu[l].reshape(1, -1)),
        wa=cast(w_br_a[l]), wb=cast(w_br_b[l]), wc=cast(wc), wo=cast(w_out[l]),
        wq=cast(w_mq[l]), wk=cast(w_mk[l]), wv=cast(w_mv[l]), wmo=cast(w_mo[l]),
        w1=cast(w_ff1[l]), w2=cast(w_ff2[l]))


def _shift_row(main_row):
    lora = main_row[:, P_LORA:]
    return jnp.concatenate([
        main_row[:, P_R:P_R + A_W], lora[:, :A_LORA_W], main_row[:, P_K:P_K + A_W],
        main_row[:, P_V:P_V + A_W], lora[:, A_LORA_W:A_LORA_W + A_LORA_A],
        lora[:, A_LORA_W + A_LORA_A:]], axis=1)


def _feature_major(z):
    z = jnp.moveaxis(z, -3, -1)
    return z.reshape(z.shape[:-3] + (C_KVW, z.shape[-1]))


def _token_major(z):
    b, _, rows = z.shape
    return z.reshape(b, C_KV, C_HD, rows).transpose(0, 3, 1, 2)


def kernel(x_prompt, x_sample, state_rwkv, state_shift, state_pool, cache_cmp_k, cache_cmp_v, cache_slc_k, cache_slc_v, cache_win_k, cache_win_v, cache_mem_k, cache_mem_v, page_table, mem_prompt, rel_bias, norm_mix_g, w_in, shift_mu, rwkv_w0, rwkv_w2, rwkv_a0, rwkv_a2, rwkv_g2, rwkv_kk, rwkv_ka, rwkv_rk, rwkv_lnx_g, rwkv_lnx_b, pool_w, pool_scale, nsa_q_g, nsa_k_g, w_br_a, w_br_b, w_br_c, w_out, norm_memx_g, norm_mem_g, w_mq, w_mk, w_mv, mem_q_g, mem_k_g, w_mo, norm_ffn_g, w_ff1, w_ff2):
    bp, t = x_prompt.shape[:2]
    bs, ts = x_sample.shape[:2]
    past = page_table.shape[1] * PAGE_SIZE
    wbuf_p = min(WINDOW, t)
    ts_pad = 16
    bnear, bcmp = _bias_tables(rel_bias)
    caches = [_feature_major(z) for z in (cache_cmp_k, cache_cmp_v, cache_slc_k, cache_slc_v)]
    win_k, win_v = _feature_major(cache_win_k), _feature_major(cache_win_v)
    xp = x_prompt.reshape(bp * t, D_MODEL)
    xs = x_sample.reshape(bs * ts, D_MODEL)
    outs_p = [[] for _ in range(11)]
    outs_s = [[] for _ in range(9)]
    row = lambda a: a.reshape(1, -1)
    for l in range(DEPTH):
        prm = _layer_params(l, w_in, shift_mu, w_br_a, w_br_b, w_br_c, w_out, w_mq, w_mk, w_mv, w_mo,
                            w_ff1, w_ff2)
        rw = (prm["mu"], row(rwkv_w0[l]), rwkv_w2[l].astype(BF16), row(rwkv_a0[l]), rwkv_a2[l].astype(BF16),
              rwkv_g2[l].astype(BF16), row(rwkv_kk[l]), row(rwkv_ka[l]), row(rwkv_rk[l]),
              row(rwkv_lnx_g[l]), row(rwkv_lnx_b[l]))
        pool_wl = pool_w[l].astype(BF16)

        main = _norm_matmul(xp, norm_mix_g[l], prm["w_main"], 512, N_MAIN).reshape(bp, t, N_MAIN)
        gate = _norm_matmul(xp, norm_mix_g[l], prm["w_gate"], 512, N_GATE)
        o_a, s_pairs = _rwkv(main, jnp.zeros((bp, 1, A_COLS), F32), jnp.zeros((bp, 4, 128, 128), F32), rw)
        o_b = _pool(main, jnp.zeros((bp, POOL_HALO, B_W), F32), pool_wl, pool_scale[l], 0, 512)
        qt, ks16, vs16, kw16, vw16, kc, vc, kv_rows = _nsa_prep(main, nsa_q_g[l], nsa_k_g[l], 512)
        front = lambda z: jnp.pad(z, ((0, 0), (FRONT_TILES * Q_TILE, 0), (0, 0)))
        front_t = lambda z: jnp.pad(z, ((0, 0), (0, 0), (FRONT_TILES * Q_TILE, 0)))
        o_c = _nsa_attn(qt, front(ks16), front_t(vs16), front(kw16), front_t(vw16), kc, vc,
                        gate.reshape(bp, t, N_GATE), bnear, bcmp)
        xp = _merge(xp, o_a.reshape(bp * t, A_W), o_b.reshape(bp * t, B_W), o_c.reshape(bp * t, C_W), gate,
                    prm["wa"], prm["wb"], prm["wc"], prm["wo"], 512)
        mk, mv = _mem_kv(mem_prompt.reshape(bp * N_MEM, D_MODEL), norm_mem_g[l], prm["wk"], prm["wv"],
                         mem_k_g[l])
        mk, mv = mk.reshape(bp, N_MEM, M_W), mv.reshape(bp, N_MEM, M_W)
        xp = _mem_ffn(xp, mk, mv, norm_memx_g[l], prm["wq"], mem_q_g[l], prm["wmo"], norm_ffn_g[l],
                      prm["w1"], prm["w2"], 1024, t)
        for dst, val in zip(outs_p, (
                _unpair_states(s_pairs), _shift_row(main[:, -1]), main[:, -POOL_BUF:, P_POOL:P_POOL + B_W],
                _token_major(kv_rows[:, 0]), _token_major(kv_rows[:, 1]), _token_major(kv_rows[:, 2]),
                _token_major(kv_rows[:, 3]), _token_major(kv_rows[:, 4, :, -wbuf_p:]),
                _token_major(kv_rows[:, 5, :, -wbuf_p:]),
                mk.reshape(bp, N_MEM, M_HEADS, M_HD), mv.reshape(bp, N_MEM, M_HEADS, M_HD))):
            dst.append(val)

        main = _norm_matmul(xs, norm_mix_g[l], prm["w_main"], bs * ts, N_MAIN).reshape(bs, ts, N_MAIN)
        gate = _norm_matmul(xs, norm_mix_g[l], prm["w_gate"], bs * ts, N_GATE)
        main_pad = jnp.pad(main, ((0, 0), (0, ts_pad - ts), (0, 0)))
        ulast = _shift_order(state_shift[l]).reshape(bs, 1, A_COLS)
        o_a, s_pairs = _rwkv(main_pad, ulast, _pair_states(state_rwkv[l]), rw, t_valid=ts)
        o_a = o_a[:, :ts]
        hist = jnp.concatenate([jnp.zeros((bs, POOL_HALO - POOL_BUF, B_W), F32), state_pool[l]], axis=1)
        o_b = _pool(main_pad, hist, pool_wl, pool_scale[l], past, ts_pad)[:, :ts]
        u_pool = main[:, :, P_POOL:P_POOL + B_W]
        o_c, new_rows = _nsa_sample(main, gate, caches, win_k, win_v, page_table, l, nsa_q_g[l], nsa_k_g[l],
                                    rel_bias)
        new_kv = [new_rows[:, n].reshape(bs, ts, C_KV, C_HD) for n in range(6)]
        slide = lambda buf, n: _token_major(jnp.concatenate(
            [buf[l][:, :, ts:], new_rows[:, n].transpose(0, 2, 1)], axis=2))
        xs = _merge(xs, o_a.reshape(bs * ts, A_W), o_b.reshape(bs * ts, B_W), o_c, gate,
                    prm["wa"], prm["wb"], prm["wc"], prm["wo"], bs * ts)
        xs = _mem_ffn(xs, cache_mem_k[l].reshape(bs, N_MEM, M_W), cache_mem_v[l].reshape(bs, N_MEM, M_W),
                      norm_memx_g[l], prm["wq"], mem_q_g[l], prm["wmo"], norm_ffn_g[l],
                      prm["w1"], prm["w2"], 32, ts)
        for dst, val in zip(outs_s, (
                _unpair_states(s_pairs), _shift_row(main[:, -1]),
                jnp.concatenate([state_pool[l], u_pool], axis=1)[:, -POOL_BUF:],
                new_kv[0], new_kv[1], new_kv[2], new_kv[3], slide(win_k, 4), slide(win_v, 5))):
            dst.append(val)

    return ((xp.reshape(bp, t, D_MODEL), xs.reshape(bs, ts, D_MODEL))
            + tuple(jnp.stack(o) for o in outs_p) + tuple(jnp.stack(o) for o in outs_s))
```

```python
import functools
import math

import jax
import jax.numpy as jnp
from jax import lax
from jax.experimental import pallas as pl
from jax.experimental.pallas import tpu as pltpu

F32 = jnp.float32
BF16 = jnp.bfloat16

D_MODEL = 1024
DEPTH = 2
PAGE_SIZE = 128
A_HEADS = 8
A_HD = 64
A_W = A_HEADS * A_HD
A_LORA_W = 64
A_LORA_A = 64
A_LORA_G = 128
A_LORA = A_LORA_W + A_LORA_A + A_LORA_G
LNX_EPS = 64e-5
L2_EPS = 1e-24
B_GROUPS = 4
B_GW = 128
B_W = B_GROUPS * B_GW
POOL_WINDOWS = (2, 4, 8, 16)
POOL_BUF = 15
POOL_HALO = 16
C_HEADS = 8
C_HD = 64
C_W = C_HEADS * C_HD
C_KV = 2
C_HPG = C_HEADS // C_KV
C_KVW = C_KV * C_HD
CMP_BLOCK = 64
N_SEL = 16
WINDOW = 512
Q_TILE = 128
N_FORCED = 3
NEG = -1e30
N_BUCKETS = 32
MAX_DIST = 128
N_MEM = 256
M_HEADS = 4
M_HD = 128
M_W = M_HEADS * M_HD
D_FF = 4 * D_MODEL
RMS_EPS = 1e-6

OFF_R = 0
OFF_WL = OFF_R + A_W
OFF_K = OFF_WL + A_LORA_W
OFF_V = OFF_K + A_W
OFF_AL = OFF_V + A_W
OFF_GL = OFF_AL + A_LORA_A
A_COLS = OFF_GL + A_LORA_G
OFF_POOL = A_COLS
OFF_Q = OFF_POOL + B_W
OFF_KV = OFF_Q + C_W
OFF_CG = OFF_KV + 6 * C_KVW
OFF_MG = OFF_CG + 3 * C_HEADS
N_IN = OFF_MG + 3 * D_MODEL

P_R = 0
P_K = P_R + A_W
P_V = P_K + A_W
P_POOL = P_V + A_W
P_Q = P_POOL + B_W
P_KV = P_Q + C_W
P_LORA = P_KV + 6 * C_KVW
N_MAIN = P_LORA + A_LORA
G_MG = 0
G_CG = 3 * D_MODEL
N_GATE = G_CG + 128

BRANCH_DTYPE = BF16
RWKV_CHUNK = 64
RWKV_SUB = 16
LANES = 128
VMEM_LIMIT = 56 * 1024 * 1024


def _cparams(sem):
    return pltpu.CompilerParams(dimension_semantics=sem, vmem_limit_bytes=VMEM_LIMIT)


def _dot(a, b, dims=(((1,), (0,)), ((), ()))):
    return lax.dot_general(a.astype(BF16), b.astype(BF16), dims, preferred_element_type=F32)


def _dot_nt(a, b):
    return _dot(a, b, (((1,), (1,)), ((), ())))


def _split(x, terms):
    parts = []
    for n in range(terms):
        part = x.astype(BF16)
        parts.append(part)
        if n + 1 < terms:
            x = x - part.astype(F32)
    return parts


def _dot_exact_rhs(a, b01, terms=2):
    dims = (((1,), (0,)), ((), ()))
    return sum(lax.dot_general(p, b01, dims, preferred_element_type=F32) for p in _split(a, terms))


def _dot_exact_lhs(a01, b, terms=2):
    dims = (((1,), (0,)), ((), ()))
    return sum(lax.dot_general(a01, p, dims, preferred_element_type=F32) for p in _split(b, terms))


def _sigmoid(x):
    return 1.0 / (1.0 + jnp.exp(-x))


def _rms(x, g):
    return x * lax.rsqrt(jnp.mean(x * x, axis=-1, keepdims=True) + RMS_EPS) * g


def _norm_matmul_kernel(x_ref, g_ref, w_ref, o_ref, xn_ref):
    @pl.when(pl.program_id(1) == 0)
    def _():
        xn_ref[...] = _rms(x_ref[...], g_ref[...]).astype(BF16)

    o_ref[...] = jnp.dot(xn_ref[...], w_ref[...], preferred_element_type=F32)


def _norm_matmul(x, g, w, tm, tn):
    m, k = x.shape
    n = w.shape[1]
    return pl.pallas_call(
        _norm_matmul_kernel,
        grid=(m // tm, n // tn),
        in_specs=[pl.BlockSpec((tm, k), lambda i, j: (i, 0)),
                  pl.BlockSpec((1, k), lambda i, j: (0, 0)),
                  pl.BlockSpec((k, tn), lambda i, j: (0, j))],
        out_specs=pl.BlockSpec((tm, tn), lambda i, j: (i, j)),
        out_shape=jax.ShapeDtypeStruct((m, n), F32),
        scratch_shapes=[pltpu.VMEM((tm, k), BF16)],
        compiler_params=_cparams(("parallel", "arbitrary")),
        name="norm_matmul",
    )(x, g.reshape(1, k), w)


def _rwkv_kernel(r_ref, k_ref, v_ref, lora_ref, ulast_ref, s0_ref, mu_ref, w0_ref, w2_ref, a0_ref,
                 a2_ref, g2_ref, kk_ref, ka_ref, rk_ref, lng_ref, lnb_ref,
                 o_ref, sout_ref, s_ref, prev_ref, ar_ref, bk_ref, vb_ref, y_ref, *, chunk, bt, t_valid):
    c = pl.program_id(1)
    nc = pl.num_programs(1)
    C = chunk
    R = bt * C
    NP = A_HEADS // 2
    PB = 2 * C

    @pl.when(c == 0)
    def _():
        s_ref[...] = s0_ref[...].reshape(bt * NP, 2 * A_HD, 2 * A_HD)
        prev_ref[...] = ulast_ref[:, 0, :]

    row = lax.broadcasted_iota(jnp.int32, (R, 1), 0)

    def shift_mix(u3, lo, hi):
        u = u3.reshape(R, hi - lo)
        u_prev = pltpu.roll(u, 1, 0)
        for b in range(bt):
            u_prev = jnp.where(row == b * C, prev_ref[b:b + 1, lo:hi], u_prev)
        for b in range(bt):
            prev_ref[b:b + 1, lo:hi] = u[(b + 1) * C - 1:(b + 1) * C, :]
        return u + (u_prev - u) * mu_ref[:, lo:hi]

    r = shift_mix(r_ref[...], 0, A_W)
    k = shift_mix(k_ref[...], A_W, 2 * A_W)
    v = shift_mix(v_ref[...], 2 * A_W, 3 * A_W)
    lora = shift_mix(lora_ref[...], 3 * A_W, 3 * A_W + A_LORA)
    wl = lora[:, 0:A_LORA_W]
    al = lora[:, A_LORA_W:A_LORA_W + A_LORA_A]
    gl = lora[:, A_LORA_W + A_LORA_A:A_LORA]

    z = -(w0_ref[...] + _dot(jnp.tanh(wl), w2_ref[...]))
    softplus = jnp.maximum(z, 0.0) + jnp.log(1.0 + jnp.exp(-jnp.abs(z)))
    w = -softplus - 0.5
    a = _sigmoid(a0_ref[...] + _dot(al, a2_ref[...]))
    g = _dot(_sigmoid(gl), g2_ref[...])

    lane = lax.broadcasted_iota(jnp.int32, (A_W, A_W), 1) // A_HD
    sub = lax.broadcasted_iota(jnp.int32, (A_W, A_W), 0) // A_HD
    head_ones = (lane == sub).astype(BF16)

    kkv = k * kk_ref[...]
    head_sum = lambda z: _dot_exact_rhs(z, head_ones, terms=1)
    kkn = kkv * lax.rsqrt(jnp.maximum(head_sum(kkv * kkv), L2_EPS))
    k2 = k * (1.0 + (a - 1.0) * ka_ref[...])
    log_d = -jnp.exp(w)
    t_in = row % C
    if t_valid is not None:
        live = (c * C + t_in) < t_valid
        log_d = jnp.where(live, log_d, 0.0)
        kkn = jnp.where(live, kkn, 0.0)
        k2 = jnp.where(live, k2, 0.0)

    ri = lax.broadcasted_iota(jnp.int32, (R, R), 0)
    ci = lax.broadcasted_iota(jnp.int32, (R, R), 1)
    cum_mask = ((ri // C) == (ci // C)) & (ci <= ri)
    cum = _dot_exact_lhs(cum_mask.astype(BF16), log_d)
    c_incl = jnp.exp(cum)
    c_inv = jnp.exp(-cum)
    a_t = -kkn * jnp.exp(cum - log_d)
    r_t = r * c_incl
    b_t = kkn * a * c_inv
    k_t = k2 * c_inv

    lane = lax.broadcasted_iota(jnp.int32, (1, 2 * A_HD), 1) // A_HD
    def stage(ref, off, x):
        for b in range(bt):
            for p in range(NP):
                blk = x[b * C:(b + 1) * C, 2 * A_HD * p:2 * A_HD * (p + 1)]
                for hh in range(2):
                    ref[b * NP + p, off + hh * C:off + (hh + 1) * C, :] = (
                        jnp.where(lane == hh, blk, 0.0).astype(BF16))

    stage(ar_ref, 0, a_t)
    stage(ar_ref, PB, r_t)
    stage(bk_ref, 0, b_t)
    stage(bk_ref, PB, k_t)
    stage(vb_ref, 0, v)
    ar = ar_ref[...]
    bk = bk_ref[...]
    vb = vb_ref[...]

    nn = (((2,), (1,)), ((0,), (0,)))
    nt = (((2,), (2,)), ((0,), (0,)))
    tn = (((1,), (1,)), ((0,), (0,)))
    bdot = lambda x, y, dims=nn: lax.dot_general(x.astype(BF16), y.astype(BF16), dims,
                                                 preferred_element_type=F32)
    bi = lax.broadcasted_iota(jnp.int32, (PB, PB), 0)
    bj = lax.broadcasted_iota(jnp.int32, (PB, PB), 1)
    same_head = (bi // C) == (bj // C)
    strict = same_head & ((bj % C) < (bi % C))
    lower = same_head & ((bj % C) <= (bi % C))
    sub_blk = (bi // RWKV_SUB) == (bj // RWKV_SUB)
    eye = (bi == bj).astype(F32)

    gram = bdot(ar, bk, nt)
    l_b = jnp.where(strict, gram[:, :PB, :PB], 0.0)
    l_k = jnp.where(strict, gram[:, :PB, PB:], 0.0)
    m_b = jnp.where(lower, gram[:, PB:, :PB], 0.0)
    m_k = jnp.where(lower, gram[:, PB:, PB:], 0.0)
    dg = jnp.where(sub_blk, l_b, 0.0)
    off = l_b - dg
    t_inv = eye + dg
    pw = dg
    for _ in range(int(math.log2(RWKV_SUB)) - 1):
        pw = bdot(pw, pw)
        t_inv = t_inv + bdot(t_inv, pw)
    n1 = bdot(t_inv, off)
    n2 = bdot(n1, n1)
    full = eye + n1 + n2 + bdot(n1, n2)
    t_full = bdot(full, t_inv)

    s_old = s_ref[...]
    w0 = bdot(ar, s_old, nt)
    u = bdot(t_full, w0[:, :PB] + bdot(l_k, vb))
    uv = jnp.concatenate([u, vb.astype(F32)], axis=1)
    yb = w0[:, PB:] + bdot(jnp.concatenate([m_b, m_k], axis=2), uv)
    s_new = s_old + bdot(uv, bk, tn)
    for b in range(bt):
        for p in range(NP):
            sl = slice(2 * A_HD * p, 2 * A_HD * (p + 1))
            idx = b * NP + p
            s_ref[idx] = s_new[idx] * c_incl[(b + 1) * C - 1:(b + 1) * C, sl]
            y_ref[b * C:(b + 1) * C, sl] = yb[idx, :C] + yb[idx, C:]
    y = y_ref[...]

    inv_n = 1.0 / A_HD
    mean = head_sum(y) * inv_n
    yc = y - mean
    var = head_sum(yc * yc) * inv_n
    yn = yc * lax.rsqrt(var + LNX_EPS) * lng_ref[...] + lnb_ref[...]
    bonus = head_sum(r * k2 * rk_ref[...]) * v
    o_ref[...] = ((yn + bonus) * g).reshape(bt, C, A_W).astype(o_ref.dtype)

    @pl.when(c == nc - 1)
    def _():
        sout_ref[...] = s_ref[...].reshape(bt, NP, 2 * A_HD, 2 * A_HD)


RWKV_BATCH_TILE = 4


def _rwkv(proj, ulast, s0, prm, t_valid=None):
    b, t, _ = proj.shape
    C = min(RWKV_CHUNK, t)
    bt = min(RWKV_BATCH_TILE, b)
    nc = t // C
    vec = lambda n: pl.BlockSpec((1, n), lambda i, c: (0, 0))
    mat = lambda m, n: pl.BlockSpec((m, n), lambda i, c: (0, 0))
    col = lambda width, off: pl.BlockSpec((bt, C, width), lambda i, c: (i, c, off // width))
    n_shift = 3 * A_W + A_LORA
    n_pairs = bt * (A_HEADS // 2)
    kern = functools.partial(_rwkv_kernel, chunk=C, bt=bt, t_valid=t_valid)
    return pl.pallas_call(
        kern,
        grid=(b // bt, nc),
        in_specs=[col(A_W, P_R), col(A_W, P_K), col(A_W, P_V), col(A_LORA, P_LORA),
                  pl.BlockSpec((bt, 1, n_shift), lambda i, c: (i, 0, 0)),
                  pl.BlockSpec((bt, 4, 128, 128), lambda i, c: (i, 0, 0, 0)),
                  vec(n_shift), vec(A_W), mat(A_LORA_W, A_W), vec(A_W), mat(A_LORA_A, A_W),
                  mat(A_LORA_G, A_W), vec(A_W), vec(A_W), vec(A_W), vec(A_W), vec(A_W)],
        out_specs=[pl.BlockSpec((bt, C, A_W), lambda i, c: (i, c, 0)),
                   pl.BlockSpec((bt, 4, 128, 128), lambda i, c: (i, 0, 0, 0))],
        out_shape=[jax.ShapeDtypeStruct((b, t, A_W), BRANCH_DTYPE),
                   jax.ShapeDtypeStruct((b, 4, 128, 128), F32)],
        scratch_shapes=[pltpu.VMEM((n_pairs, 128, 128), F32), pltpu.VMEM((bt, n_shift), F32),
                        pltpu.VMEM((n_pairs, 4 * C, 128), BF16), pltpu.VMEM((n_pairs, 4 * C, 128), BF16),
                        pltpu.VMEM((n_pairs, 2 * C, 128), BF16), pltpu.VMEM((bt * C, A_W), F32)],
        compiler_params=_cparams(("parallel", "arbitrary")),
        name="rwkv_chunk",
    )(proj, proj, proj, proj, ulast, s0, *prm)


def _pair_states(s):
    b = s.shape[0]
    s = s.reshape(b, 4, 2, A_HD, A_HD)
    z = jnp.zeros_like(s[:, :, 0])
    top = jnp.concatenate([s[:, :, 0], z], axis=-1)
    bot = jnp.concatenate([z, s[:, :, 1]], axis=-1)
    return jnp.concatenate([top, bot], axis=-2)


def _unpair_states(sp):
    b = sp.shape[0]
    h0 = sp[:, :, :A_HD, :A_HD]
    h1 = sp[:, :, A_HD:, A_HD:]
    return jnp.stack([h0, h1], axis=2).reshape(b, A_HEADS, A_HD, A_HD)


def _pool_kernel(u_ref, halo_ref, hist_ref, pm_ref, ps_ref, o_ref, ext_ref, *, tm, pos0):
    i = pl.program_id(1)

    @pl.when(i == 0)
    def _():
        ext_ref[0:POOL_HALO, :] = hist_ref[0]

    @pl.when(i > 0)
    def _():
        ext_ref[0:POOL_HALO, :] = halo_ref[0]

    cur = u_ref[0]
    ext_ref[POOL_HALO:POOL_HALO + tm, :] = cur
    pos = pos0 + i * tm + lax.broadcasted_iota(jnp.int32, (tm, 1), 0)
    outs = []
    for gi, win in enumerate(POOL_WINDOWS):
        lo, hi = gi * B_GW, (gi + 1) * B_GW
        s = cur[:, lo:hi]
        for back in range(1, win):
            s = s + ext_ref[POOL_HALO - back:POOL_HALO - back + tm, lo:hi]
        cnt = jnp.minimum(win, pos + 1).astype(F32)
        d = s / cnt - cur[:, lo:hi]
        outs.append(_dot(d, pm_ref[gi]))
    o_ref[0] = (jnp.concatenate(outs, axis=1) * ps_ref[...]).astype(o_ref.dtype)


def _pool(proj, hist, pm, ps, pos0, tm):
    b, t, _ = proj.shape
    per = tm // POOL_HALO
    kern = functools.partial(_pool_kernel, tm=tm, pos0=pos0)
    return pl.pallas_call(
        kern,
        grid=(b, t // tm),
        in_specs=[pl.BlockSpec((1, tm, B_W), lambda bi, i: (bi, i, P_POOL // B_W)),
                  pl.BlockSpec((1, POOL_HALO, B_W),
                               lambda bi, i: (bi, jnp.maximum(i * per - 1, 0), P_POOL // B_W)),
                  pl.BlockSpec((1, POOL_HALO, B_W), lambda bi, i: (bi, 0, 0)),
                  pl.BlockSpec((B_GROUPS, B_GW, B_GW), lambda bi, i: (0, 0, 0)),
                  pl.BlockSpec((1, B_W), lambda bi, i: (0, 0))],
        out_specs=pl.BlockSpec((1, tm, B_W), lambda bi, i: (bi, i, 0)),
        out_shape=jax.ShapeDtypeStruct((b, t, B_W), BRANCH_DTYPE),
        scratch_shapes=[pltpu.VMEM((tm + POOL_HALO, B_W), F32)],
        compiler_params=_cparams(("parallel", "arbitrary")),
        name="pool_mix",
    )(proj, proj, hist, pm, ps.reshape(1, B_W))


def _merge_kernel(x_ref, oa_ref, ob_ref, oc_ref, mg_ref, wa_ref, wb_ref, wc_ref, wo_ref, o_ref):
    gate = lambda n: _sigmoid(mg_ref[:, n * D_MODEL:(n + 1) * D_MODEL])
    h = (gate(0) * _dot(oa_ref[...], wa_ref[...]) + gate(1) * _dot(ob_ref[...], wb_ref[...])
         + gate(2) * _dot(oc_ref[...], wc_ref[...]))
    o_ref[...] = x_ref[...] + _dot(h, wo_ref[...])


def _merge(x, oa, ob, oc, gates, wa, wb, wc, wo, tm):
    m = x.shape[0]
    row = lambda n: pl.BlockSpec((tm, n), lambda i: (i, 0))
    full = lambda a: pl.BlockSpec(a.shape, lambda i: (0, 0))
    return pl.pallas_call(
        _merge_kernel,
        grid=(m // tm,),
        in_specs=[row(D_MODEL), row(A_W), row(B_W), row(C_W), row(3 * D_MODEL),
                  full(wa), full(wb), full(wc), full(wo)],
        out_specs=row(D_MODEL),
        out_shape=jax.ShapeDtypeStruct((m, D_MODEL), F32),
        compiler_params=_cparams(("parallel",)),
        name="merge_branches",
    )(x, oa, ob, oc, gates, wa, wb, wc, wo)


def _mem_kv_kernel(mem_ref, g_ref, wk_ref, wv_ref, kg_ref, k_ref, v_ref):
    mn = _rms(mem_ref[...], g_ref[...]).astype(BF16)
    k = jnp.dot(mn, wk_ref[...], preferred_element_type=F32)
    v_ref[...] = jnp.dot(mn, wv_ref[...], preferred_element_type=F32)
    k_ref[...] = jnp.concatenate(
        [_rms(k[:, h * M_HD:(h + 1) * M_HD], kg_ref[...]) for h in range(M_HEADS)], axis=1)


def _mem_kv(mem, g, wk, wv, kg):
    m = mem.shape[0]
    tm = N_MEM
    row = lambda n: pl.BlockSpec((tm, n), lambda i: (i, 0))
    full = lambda a: pl.BlockSpec(a.shape, lambda i: (0, 0))
    g2, kg2 = g.reshape(1, D_MODEL), kg.reshape(1, M_HD)
    return pl.pallas_call(
        _mem_kv_kernel,
        grid=(m // tm,),
        in_specs=[row(D_MODEL), full(g2), full(wk), full(wv), full(kg2)],
        out_specs=[row(M_W), row(M_W)],
        out_shape=[jax.ShapeDtypeStruct((m, M_W), F32), jax.ShapeDtypeStruct((m, M_W), F32)],
        compiler_params=_cparams(("parallel",)),
        name="mem_kv",
    )(mem, g2, wk, wv, kg2)


def _mem_ffn_kernel(x_ref, mk_ref, mv_ref, gx_ref, wq_ref, qg_ref, wo_ref, gf_ref, w1_ref, w2_ref,
                    o_ref, xn_ref, *, tm, rows_per_batch, nkb):
    j = pl.program_id(1)

    @pl.when(j == 0)
    def _():
        x = x_ref[...]
        q = _dot(_rms(x, gx_ref[...]), wq_ref[...])
        mk = mk_ref[...].reshape(nkb * N_MEM, M_W)
        mv = mv_ref[...].reshape(nkb * N_MEM, M_W)
        if nkb > 1:
            qb = lax.broadcasted_iota(jnp.int32, (tm, nkb * N_MEM), 0) // rows_per_batch
            kb = lax.broadcasted_iota(jnp.int32, (tm, nkb * N_MEM), 1) // N_MEM
            same = qb == kb
        outs = []
        for h in range(M_HEADS):
            sl = slice(h * M_HD, (h + 1) * M_HD)
            qh = _rms(q[:, sl], qg_ref[...])
            logits = _dot_nt(qh, mk[:, sl]) * (M_HD ** -0.5)
            if nkb > 1:
                logits = jnp.where(same, logits, NEG)
            mx = jnp.max(logits, axis=-1, keepdims=True)
            p = jnp.exp(logits - mx)
            p = p / jnp.sum(p, axis=-1, keepdims=True)
            outs.append(_dot(p, mv[:, sl]))
        o = jnp.concatenate(outs, axis=1)
        xm = x + _dot(o, wo_ref[...])
        o_ref[...] = xm
        xn_ref[...] = _rms(xm, gf_ref[...]).astype(BF16)

    h1 = jnp.dot(xn_ref[...], w1_ref[...], preferred_element_type=F32)
    h1 = jnp.square(jnp.maximum(h1, 0.0))
    o_ref[...] += _dot(h1, w2_ref[...])


def _mem_ffn(x, mk, mv, gx, wq, qg, wo, gf, w1, w2, tm, rows_per_batch, tf=1024):
    m = x.shape[0]
    nkb = max(tm // rows_per_batch, 1)
    kern = functools.partial(_mem_ffn_kernel, tm=tm, rows_per_batch=rows_per_batch, nkb=nkb)
    full = lambda a: pl.BlockSpec(a.shape, lambda i, j: (0,) * a.ndim)
    gx2, qg2, gf2 = gx.reshape(1, D_MODEL), qg.reshape(1, M_HD), gf.reshape(1, D_MODEL)
    mem_spec = pl.BlockSpec((nkb, N_MEM, M_W), lambda i, j: ((i * tm) // (rows_per_batch * nkb), 0, 0))
    return pl.pallas_call(
        kern,
        grid=(m // tm, D_FF // tf),
        in_specs=[pl.BlockSpec((tm, D_MODEL), lambda i, j: (i, 0)), mem_spec, mem_spec,
                  full(gx2), full(wq), full(qg2), full(wo), full(gf2),
                  pl.BlockSpec((D_MODEL, tf), lambda i, j: (0, j)),
                  pl.BlockSpec((tf, D_MODEL), lambda i, j: (j, 0))],
        out_specs=pl.BlockSpec((tm, D_MODEL), lambda i, j: (i, 0)),
        out_shape=jax.ShapeDtypeStruct((m, D_MODEL), F32),
        scratch_shapes=[pltpu.VMEM((tm, D_MODEL), BF16)],
        compiler_params=_cparams(("parallel", "arbitrary")),
        name="mem_ffn",
    )(x, mk, mv, gx2, wq, qg2, wo, gf2, w1, w2)


def _half_ones():
    i = lax.broadcasted_iota(jnp.int32, (LANES, LANES), 0) // C_HD
    j = lax.broadcasted_iota(jnp.int32, (LANES, LANES), 1) // C_HD
    return (i == j).astype(BF16)


def _rms_halves(x, g, ones):
    ms = _dot_exact_rhs(x * x, ones) * (1.0 / C_HD)
    return x * lax.rsqrt(ms + RMS_EPS) * g


def _nsa_prep_kernel(q_ref, c_ref, s_ref, w_ref, qg_ref, kg_ref,
                     qn_ref, ks16_ref, vs16_ref, kw16_ref, vw16_ref, kc_ref, vc_ref, rows_ref, *, tm):
    ones = _half_ones()
    scale = C_HD ** -0.5 * LOG2E
    q = q_ref[0]
    for n in range(C_HPG):
        qn_ref[0, n] = (_rms_halves(q[:, n * LANES:(n + 1) * LANES], qg_ref[...], ones) * scale).T.astype(BF16)
    c = c_ref[0]
    rows_ref[0, 0] = c[:, :LANES].T
    rows_ref[0, 1] = c[:, LANES:].T
    s = s_ref[0]
    ks = _rms_halves(s[:, :LANES], kg_ref[1:2, :], ones)
    vs_t = s[:, LANES:].T
    ks16_ref[0] = ks.astype(BF16)
    vs16_ref[0] = vs_t.astype(BF16)
    rows_ref[0, 2] = ks.T
    rows_ref[0, 3] = vs_t
    w = w_ref[0]
    kw = _rms_halves(w[:, :LANES], kg_ref[2:3, :], ones)
    vw_t = w[:, LANES:].T
    kw16_ref[0] = kw.astype(BF16)
    vw16_ref[0] = vw_t.astype(BF16)
    rows_ref[0, 4] = kw.T
    rows_ref[0, 5] = vw_t
    nb = tm // CMP_BLOCK
    bi = lax.broadcasted_iota(jnp.int32, (nb, tm), 0)
    ti = lax.broadcasted_iota(jnp.int32, (nb, tm), 1) // CMP_BLOCK
    pool = (bi == ti).astype(BF16)
    means = _dot_exact_lhs(pool, c) * (1.0 / CMP_BLOCK)
    kc_ref[0] = _rms_halves(means[:, :LANES], kg_ref[0:1, :], ones)
    vc_ref[0] = means[:, LANES:]


def _nsa_prep(proj, qg, kg, tm):
    b, t, _ = proj.shape
    nbt = tm // CMP_BLOCK
    kern = functools.partial(_nsa_prep_kernel, tm=tm)
    tok = lambda n, dt: jax.ShapeDtypeStruct((b, t, n), dt)
    blk = jax.ShapeDtypeStruct((b, t // CMP_BLOCK, LANES), F32)
    col = lambda width, off: pl.BlockSpec((1, tm, width), lambda bi, i: (bi, i, off // width))
    out_tok = lambda n: pl.BlockSpec((1, tm, n), lambda bi, i: (bi, i, 0))
    out_blk = pl.BlockSpec((1, nbt, LANES), lambda bi, i: (bi, i, 0))
    out_t = pl.BlockSpec((1, LANES, tm), lambda bi, i: (bi, 0, i))
    tok_t = jax.ShapeDtypeStruct((b, LANES, t), BF16)
    return pl.pallas_call(
        kern,
        grid=(b, t // tm),
        in_specs=[col(C_W, P_Q), col(2 * LANES, P_KV), col(2 * LANES, P_KV + 2 * LANES),
                  col(2 * LANES, P_KV + 4 * LANES),
                  pl.BlockSpec((1, LANES), lambda bi, i: (0, 0)),
                  pl.BlockSpec((3, LANES), lambda bi, i: (0, 0))],
        out_specs=[pl.BlockSpec((1, C_HPG, LANES, tm), lambda bi, i: (bi, 0, 0, i)),
                   out_tok(LANES), out_t, out_tok(LANES), out_t, out_blk, out_blk,
                   pl.BlockSpec((1, 6, LANES, tm), lambda bi, i: (bi, 0, 0, i))],
        out_shape=[jax.ShapeDtypeStruct((b, C_HPG, LANES, t), BF16),
                   tok(LANES, BF16), tok_t, tok(LANES, BF16), tok_t, blk, blk,
                   jax.ShapeDtypeStruct((b, 6, LANES, t), F32)],
        compiler_params=_cparams(("parallel", "parallel")),
        name="nsa_prep",
    )(proj, proj, proj, proj, jnp.tile(qg.reshape(1, C_HD), (1, 2)), jnp.tile(kg, (1, 2)))


M_INIT = -1e29
ROWS = C_HPG * Q_TILE
FAR_TILES = 16
LEFTOVER_STEPS = (FAR_TILES // 4, FAR_TILES // 2, FAR_TILES)
FRONT_TILES = max(WINDOW // Q_TILE, FAR_TILES)
SEL_PAD = 2 * FRONT_TILES
SUM_ROWS = 16
LOG2E = 1.4426950408889634


def _heads(x):
    return jnp.concatenate([x] * C_HPG, axis=1)


def _softmax_step(qts, k, v_t, state, tail_bias=None, valid=None, block_bias=None):
    m_ref, l_ref, acc_ref = state
    n_keys = k.shape[0]
    old = [(m_ref[g], l_ref[g], acc_ref[g]) for g in range(C_KV)]
    v_ext = jnp.concatenate([v_t, jnp.ones((SUM_ROWS, n_keys), BF16)], axis=0)
    if block_bias is not None:
        blk = lax.broadcasted_iota(jnp.int32, (n_keys, LANES), 0) // CMP_BLOCK
        col = lax.broadcasted_iota(jnp.int32, (n_keys, LANES), 1)
        k = jnp.concatenate([k, (blk == col).astype(BF16)], axis=1)
    new = []
    for g in range(C_KV):
        m_old, l_old, acc_old = old[g]
        rhs = qts[g] if block_bias is None else jnp.concatenate([qts[g], block_bias[g]], axis=0)
        s = jnp.dot(k, rhs, preferred_element_type=F32)
        if tail_bias is not None:
            cut = s.shape[0] - tail_bias[g].shape[0]
            tail = s[cut:] + tail_bias[g]
            s = jnp.concatenate([s[:cut], tail], axis=0) if cut else tail
        if valid is not None:
            s = jnp.where(_heads(valid[g]), s, NEG)
        s = s.astype(BF16)
        m_new = jnp.maximum(m_old, jnp.max(s, axis=0, keepdims=True).astype(F32))
        alpha = jnp.exp2(m_old - m_new)
        p = jnp.exp2(s - m_new.astype(BF16))
        pv = jnp.dot(v_ext, p, preferred_element_type=F32)
        new.append((m_new, alpha * l_old + pv[LANES:LANES + 1], alpha * acc_old + pv[:LANES]))
    for g in range(C_KV):
        m_ref[g], l_ref[g], acc_ref[g] = new[g]


def _softmax_reset(state):
    m_ref, l_ref, acc_ref = state
    m_ref[...] = jnp.full(m_ref.shape, M_INIT, F32)
    l_ref[...] = jnp.zeros(l_ref.shape, F32)
    acc_ref[...] = jnp.zeros(acc_ref.shape, F32)


def _softmax_result(state, g):
    _, l_ref, acc_ref = state
    l = l_ref[g]
    return acc_ref[g] / jnp.where(l > 0.0, l, 1.0)


def _top_blocks(score, n_sel):
    nb = score.shape[0]
    blk = lax.broadcasted_iota(jnp.int32, score.shape, 0).astype(F32)
    sel = jnp.zeros(score.shape, F32)
    work = score
    for _ in range(min(n_sel, nb)):
        mx = jnp.max(work, axis=0, keepdims=True)
        idx = jnp.min(jnp.where(work == mx, blk, float(nb)), axis=0, keepdims=True)
        pick = blk == idx
        sel = jnp.where(pick, 1.0, sel)
        work = jnp.where(pick, -jnp.inf, work)
    return sel


def _nsa_attn_kernel(qt_ref, ks_ref, vs_ref, kw_ref, vw_ref, kc_ref, vc_ref, cg_ref, bnear_ref,
                     bcmp_ref, o_ref, m_ref, l_ref, acc_ref, sel_ref, *, nb):
    i = pl.program_id(1)
    lane = lax.broadcasted_iota(jnp.int32, (Q_TILE, LANES), 1)
    key = lax.broadcasted_iota(jnp.int32, (Q_TILE, LANES), 0)
    causal = key <= lane
    sig_t = _sigmoid(cg_ref[0]).T
    kc = kc_ref[0].astype(BF16)
    vc_t = vc_ref[0].T.astype(BF16)
    blk = lax.broadcasted_iota(jnp.int32, (nb, Q_TILE), 0)
    qpos = i * Q_TILE + lax.broadcasted_iota(jnp.int32, (nb, Q_TILE), 1)
    cur = qpos // CMP_BLOCK
    cmp_valid = _heads(qpos >= blk * CMP_BLOCK + (CMP_BLOCK - 1))
    forced = (blk == 0) | (blk == cur) | (blk == cur - 1)
    state = (m_ref, l_ref, acc_ref)
    n_back = WINDOW // Q_TILE
    k_rows = lambda ref, j, n: ref[0, pl.ds(pl.multiple_of((j + FRONT_TILES) * Q_TILE, Q_TILE), n * Q_TILE), :]
    v_cols = lambda ref, j, n: ref[0, :, pl.ds(pl.multiple_of((j + FRONT_TILES) * Q_TILE, Q_TILE), n * Q_TILE)]

    def key_mask(g, j, n):
        rows = [sel_ref[g, pl.ds(SEL_PAD + 2 * j + r, 1), :] for r in range(2 * n)]
        return jnp.concatenate(
            [jnp.where(key < CMP_BLOCK, rows[2 * t], rows[2 * t + 1]) for t in range(n)], axis=0) > 0.5

    def block_bias(g, j, n, tiles=None):
        rows = sel_ref[g, pl.ds(SEL_PAD + 2 * j, 2 * n), :]
        if tiles is not None:
            t = j + lax.broadcasted_iota(jnp.int32, (2 * n, Q_TILE), 0) // 2
            rows = jnp.where((t >= tiles[0]) & (t < tiles[1]), rows, 0.0)
        bias = _heads((rows - 1.0) * (-NEG))
        return jnp.concatenate([bias, jnp.zeros((LANES - 2 * n, ROWS), F32)], axis=0).astype(BF16)

    q_t = jnp.concatenate([qt_ref[0, n] for n in range(C_HPG)], axis=1)
    feat = lax.broadcasted_iota(jnp.int32, (LANES, ROWS), 0) // C_HD
    qts = [jnp.where(feat == g, q_t, jnp.zeros_like(q_t)) for g in range(C_KV)]

    o_cmp, scores = [], []
    for g in range(C_KV):
        bias_c = pltpu.roll(bcmp_ref[g], (2 * i + 2) % LANES, 0)[:nb, :]
        lc = jnp.dot(kc, qts[g], preferred_element_type=F32) + bias_c
        lc = jnp.where(cmp_valid, lc, NEG)
        mc = jnp.maximum(jnp.max(lc, axis=0, keepdims=True), M_INIT)
        pc = jnp.exp2(lc - mc)
        den = jnp.sum(pc, axis=0, keepdims=True)
        pc = pc / jnp.where(den > 0.0, den, 1.0)
        o_cmp.append(jnp.dot(vc_t, pc.astype(BF16), preferred_element_type=F32))
        psum = sum(pc[:, n * Q_TILE:(n + 1) * Q_TILE] for n in range(C_HPG))
        scores.append(jnp.where((blk > cur) | forced, NEG, psum))

    score = jnp.concatenate(scores, axis=1)
    sel = jnp.where(score > 0.5 * NEG, _top_blocks(score, N_SEL - N_FORCED), 0.0)
    sel = jnp.where(jnp.concatenate([forced] * C_KV, axis=1), 1.0, sel)
    for g in range(C_KV):
        sel_ref[g, 0:SEL_PAD, :] = jnp.zeros((SEL_PAD, Q_TILE), F32)
        sel_ref[g, SEL_PAD:SEL_PAD + nb, :] = sel[:, g * Q_TILE:(g + 1) * Q_TILE]

    _softmax_reset(state)
    n_far = jnp.maximum(i - 1, 0)

    def far_step(j, n=FAR_TILES, tiles=None):
        _softmax_step(qts, k_rows(ks_ref, j, n), v_cols(vs_ref, j, n), state,
                      block_bias=[block_bias(g, j, n, tiles) for g in range(C_KV)])

    def far_many(jq, carry):
        far_step(FAR_TILES * jq)
        return carry

    n_many = n_far // FAR_TILES
    lax.fori_loop(0, n_many, far_many, 0)

    n_left = n_far % FAR_TILES
    left = (FAR_TILES * n_many, n_far)
    lo = 0
    for size in LEFTOVER_STEPS:
        @pl.when((n_left > lo) & (n_left <= size))
        def _(size=size):
            far_step(n_far - size, size, left)
        lo = size

    near_causal = jnp.concatenate([key >= 0, causal], axis=0)
    bias2 = [bnear_ref[g] for g in range(C_KV)]
    _softmax_step(qts, k_rows(ks_ref, i - 1, 2), v_cols(vs_ref, i - 1, 2), state, tail_bias=bias2,
                  valid=[key_mask(g, i - 1, 2) & near_causal for g in range(C_KV)])
    o_slc = [_softmax_result(state, g) for g in range(C_KV)]

    _softmax_reset(state)
    exists = lambda back: key >= jnp.where(i >= back, 0, Q_TILE)
    win_valid = jnp.concatenate(
        [(key > lane) & exists(n_back)] + [exists(back) for back in range(n_back - 1, 0, -1)] + [causal],
        axis=0)
    _softmax_step(qts, k_rows(kw_ref, i - n_back, n_back + 1), v_cols(vw_ref, i - n_back, n_back + 1), state,
                  tail_bias=bias2, valid=[win_valid] * C_KV)
    o_win = [_softmax_result(state, g) for g in range(C_KV)]

    low = (key // C_HD) == 0
    for n in range(C_HPG):
        cols = slice(n * Q_TILE, (n + 1) * Q_TILE)
        per_group = []
        for g in range(C_KV):
            h = g * C_HPG + n
            gate = lambda br: sig_t[br * C_HEADS + h:br * C_HEADS + h + 1, :]
            per_group.append(gate(0) * o_cmp[g][:, cols] + gate(1) * o_slc[g][:, cols]
                             + gate(2) * o_win[g][:, cols])
        o_ref[0, :, n * LANES:(n + 1) * LANES] = jnp.where(low, per_group[0], per_group[1]).T.astype(o_ref.dtype)


def _nsa_attn(qt, ks16, vs16t, kw16, vw16t, kc, vc, gates, bnear, bcmp):
    b, _, _, t = qt.shape
    nb = t // CMP_BLOCK
    tp = t + FRONT_TILES * Q_TILE
    kern = functools.partial(_nsa_attn_kernel, nb=nb)
    seq = pl.BlockSpec((1, tp, LANES), lambda bi, i: (bi, 0, 0))
    seq_t = pl.BlockSpec((1, LANES, tp), lambda bi, i: (bi, 0, 0))
    blk = pl.BlockSpec((1, nb, LANES), lambda bi, i: (bi, 0, 0))
    return pl.pallas_call(
        kern,
        grid=(b, t // Q_TILE),
        in_specs=[pl.BlockSpec((1, C_HPG, LANES, Q_TILE), lambda bi, i: (bi, 0, 0, i)),
                  seq, seq_t, seq, seq_t, blk, blk,
                  pl.BlockSpec((1, Q_TILE, LANES), lambda bi, i: (bi, i, G_CG // LANES)),
                  pl.BlockSpec(bnear.shape, lambda bi, i: (0, 0, 0)),
                  pl.BlockSpec(bcmp.shape, lambda bi, i: (0, 0, 0))],
        out_specs=pl.BlockSpec((1, Q_TILE, C_W), lambda bi, i: (bi, i, 0)),
        out_shape=jax.ShapeDtypeStruct((b, t, C_W), BRANCH_DTYPE),
        scratch_shapes=[pltpu.VMEM((C_KV, 1, ROWS), F32), pltpu.VMEM((C_KV, 1, ROWS), F32),
                        pltpu.VMEM((C_KV, LANES, ROWS), F32),
                        pltpu.VMEM((C_KV, SEL_PAD + nb, Q_TILE), F32)],
        compiler_params=_cparams(("parallel", "arbitrary")),
        name="nsa_attn",
    )(qt, ks16, vs16t, kw16, vw16t, kc, vc, gates, bnear, bcmp)


def _rel_bucket(dist):
    n = jnp.maximum(dist, 0)
    exact = N_BUCKETS // 2
    ratio = jnp.log(jnp.maximum(n, 1).astype(F32) / exact) / math.log(MAX_DIST / exact)
    large = jnp.minimum(exact + (ratio * (N_BUCKETS - exact)).astype(jnp.int32), N_BUCKETS - 1)
    return jnp.where(n < exact, n, large)


def _one_hot(idx, n):
    return (jnp.clip(idx, 0, n - 1)[..., None] == jnp.arange(n)).astype(F32)


def _distance_table(rel_bias):
    tbl = jnp.dot(_one_hot(_rel_bucket(jnp.arange(MAX_DIST)), N_BUCKETS), rel_bias,
                  precision=lax.Precision.HIGHEST)
    return (tbl - tbl[MAX_DIST - 1:MAX_DIST, :]).T


def _bias_tables(rel_bias):
    tbl = _distance_table(rel_bias) * LOG2E
    tbl = tbl.reshape(C_KV, C_HPG, MAX_DIST)
    qq = jnp.arange(Q_TILE)[:, None]
    cc = jnp.arange(LANES)[None, :]
    look = lambda dist: jnp.einsum('ghd,qcd->ghqc', tbl, _one_hot(dist, MAX_DIST),
                                   precision=lax.Precision.HIGHEST).reshape(C_KV, ROWS, LANES)
    bnear = jnp.stack([look(qq - cc), look(qq - cc + Q_TILE)], axis=1)
    bcmp = look(qq - CMP_BLOCK * cc + CMP_BLOCK * (LANES - 1) - (LANES - 1))
    bnear = bnear.transpose(0, 1, 3, 2)
    return jnp.concatenate([bnear[:, 1], bnear[:, 0]], axis=1), bcmp.transpose(0, 2, 1)


PAGES_PER_STEP = 16
S_ROWS = C_KV * C_HPG
BLK_LANES = 384
NEW_PAD = 8


def _page_specs(layer, n):
    spec = lambda j: pl.BlockSpec((1, 1, LANES, PAGE_SIZE),
                                  lambda b, c, pt, j=j: (layer, pt[b, c * n + j], 0, 0))
    return [spec(j) for j in range(n)]


def _cmp_pages_kernel(pt_ref, *refs, n):
    k_refs, v_refs, (kc_ref, vc_ref) = refs[:n], refs[n:2 * n], refs[2 * n:]
    row = lax.broadcasted_iota(jnp.int32, (n * PAGE_SIZE, 2 * n), 0) // CMP_BLOCK
    col = lax.broadcasted_iota(jnp.int32, (n * PAGE_SIZE, 2 * n), 1)
    pool = (row == col).astype(BF16)
    for refs_in, out in ((k_refs, kc_ref), (v_refs, vc_ref)):
        pages = jnp.concatenate([r[0, 0] for r in refs_in], axis=1)
        out[0, 0] = _dot_exact_rhs(pages, pool) * (1.0 / CMP_BLOCK)


def _cmp_pages(page_table, cache_k, cache_v, layer):
    b, n_pages = page_table.shape
    n = min(PAGES_PER_STEP, n_pages)
    chunks = n_pages // n
    out = jax.ShapeDtypeStruct((b, chunks, LANES, 2 * n), F32)
    out_spec = pl.BlockSpec((1, 1, LANES, 2 * n), lambda bi, c, pt: (bi, c, 0, 0))
    return pl.pallas_call(
        functools.partial(_cmp_pages_kernel, n=n),
        grid_spec=pltpu.PrefetchScalarGridSpec(
            num_scalar_prefetch=1, grid=(b, chunks),
            in_specs=_page_specs(layer, n) + _page_specs(layer, n),
            out_specs=[out_spec, out_spec]),
        out_shape=[out, out],
        compiler_params=_cparams(("parallel", "arbitrary")),
        name="nsa_sample_cmp_pages",
    )(page_table, *([cache_k] * n), *([cache_v] * n))


def _top_blocks_lanes(score, n_sel):
    nb = score.shape[-1]
    blk = lax.broadcasted_iota(jnp.int32, score.shape, 1).astype(F32)
    sel = jnp.zeros(score.shape, F32)
    work = score
    for _ in range(min(n_sel, nb)):
        mx = jnp.max(work, axis=-1, keepdims=True)
        idx = jnp.min(jnp.where(work == mx, blk, float(nb)), axis=-1, keepdims=True)
        pick = blk == idx
        sel = jnp.where(pick, 1.0, sel)
        work = jnp.where(pick, -jnp.inf, work)
    return sel


def _nsa_sample_head_kernel(q_ref, newc_ref, news_ref, neww_ref, kc_ref, vc_ref, wk_ref, wv_ref,
                            qg_ref, kg_ref, kgc_ref, bc_ref, bw_ref, bn_ref,
                            q32_ref, sel_ref, ocmp_ref, owin_ref, new_ref, q_scr, *, ts, past, wbuf):
    n_rows = S_ROWS * ts
    ones = _half_ones()
    scale = C_HD ** -0.5
    lane = lax.broadcasted_iota(jnp.int32, (ts, LANES), 1)
    q = q_ref[0]
    for n in range(C_HPG):
        qn = _rms_halves(q[:, n * LANES:(n + 1) * LANES], qg_ref[...], ones) * scale
        for g in range(C_KV):
            lo = (g * C_HPG + n) * ts
            q_scr[lo:lo + ts, :] = jnp.where((lane // C_HD) == g, qn, 0.0)
    q32 = q_scr[...].astype(BF16)
    q32_ref[0] = q32

    newc, news, neww = newc_ref[0], news_ref[0], neww_ref[0]
    ks_new = _rms_halves(news[:, :LANES], kg_ref[1:2, :], ones)
    kw_new = _rms_halves(neww[:, :LANES], kg_ref[2:3, :], ones)
    new_ref[0, 0] = newc[:, :LANES]
    new_ref[0, 1] = newc[:, LANES:]
    new_ref[0, 2] = ks_new
    new_ref[0, 3] = news[:, LANES:]
    new_ref[0, 4] = kw_new
    new_ref[0, 5] = neww[:, LANES:]

    t_of_row = lax.broadcasted_iota(jnp.int32, (n_rows, 1), 0) % ts
    qpos = past + t_of_row

    n_cached = past // CMP_BLOCK
    pad_rows = jnp.zeros((NEW_PAD - ts, LANES), F32)
    ones_rows = jnp.full((NEW_PAD, BLK_LANES), 1.0, BF16)
    new_mean = lambda rows: _dot_tn_exact(jnp.concatenate([rows, pad_rows], axis=0),
                                          ones_rows) * (1.0 / CMP_BLOCK)
    is_new = lax.broadcasted_iota(jnp.int32, (LANES, BLK_LANES), 1) == n_cached
    kc_all = jnp.where(is_new, new_mean(newc[:, :LANES]), kc_ref[0])
    vc_all = jnp.where(is_new, new_mean(newc[:, LANES:]), vc_ref[0])
    feat = lax.broadcasted_iota(jnp.int32, (LANES, 1), 0) // C_HD
    sq = kc_all * kc_all
    ms = jnp.where(feat == 0, jnp.sum(sq[:C_HD], axis=0, keepdims=True),
                   jnp.sum(sq[C_HD:], axis=0, keepdims=True)) * (1.0 / C_HD)
    kc_n = kc_all * lax.rsqrt(ms + RMS_EPS) * kgc_ref[:, 0:1]
    blk = lax.broadcasted_iota(jnp.int32, (n_rows, BLK_LANES), 1)
    lc = jnp.dot(q32, kc_n.astype(BF16), preferred_element_type=F32) + bc_ref[...]
    lc = jnp.where(qpos >= blk * CMP_BLOCK + (CMP_BLOCK - 1), lc, NEG)
    mc = jnp.maximum(jnp.max(lc, axis=-1, keepdims=True), M_INIT)
    pc = jnp.exp(lc - mc)
    den = jnp.sum(pc, axis=-1, keepdims=True)
    pc = pc / jnp.where(den > 0.0, den, 1.0)
    ocmp_ref[0] = lax.dot_general(pc.astype(BF16), vc_all.astype(BF16), (((1,), (1,)), ((), ())),
                                  preferred_element_type=F32)

    gi = lax.broadcasted_iota(jnp.int32, (C_KV * ts, n_rows), 0)
    ri = lax.broadcasted_iota(jnp.int32, (C_KV * ts, n_rows), 1)
    same = ((gi // ts) == (ri // (C_HPG * ts))) & ((gi % ts) == (ri % ts))
    psum = _dot_exact_lhs(same.astype(BF16), pc, terms=3)
    sblk = lax.broadcasted_iota(jnp.int32, (C_KV * ts, BLK_LANES), 1)
    scur = (past + lax.broadcasted_iota(jnp.int32, (C_KV * ts, 1), 0) % ts) // CMP_BLOCK
    forced = (sblk == 0) | (sblk == scur) | (sblk == scur - 1)
    score = jnp.where((sblk > scur) | forced, NEG, psum)
    sel = jnp.where(score > 0.5 * NEG, _top_blocks_lanes(score, N_SEL - N_FORCED), 0.0)
    sel = jnp.where(forced, 1.0, sel)
    sel_ref[0] = lax.dot_general(same.astype(BF16), sel.astype(BF16), (((0,), (0,)), ((), ())),
                                 preferred_element_type=F32)

    key = lax.broadcasted_iota(jnp.int32, (n_rows, wbuf), 1)
    lw = jnp.dot(q32, wk_ref[0, 0].astype(BF16), preferred_element_type=F32) + bw_ref[...]
    kpos = past - wbuf + key
    dist = qpos - kpos
    lw = jnp.where((dist >= 0) & (dist < WINDOW) & (kpos >= 0), lw, NEG)
    kw_pad = jnp.concatenate([kw_new, pad_rows], axis=0)
    vw_pad = jnp.concatenate([neww[:, LANES:], pad_rows], axis=0)
    nkey = lax.broadcasted_iota(jnp.int32, (n_rows, NEW_PAD), 1)
    ln = lax.dot_general(q32, kw_pad.astype(BF16), (((1,), (1,)), ((), ())),
                         preferred_element_type=F32) + bn_ref[...]
    ln = jnp.where((nkey <= t_of_row) & (nkey < ts), ln, NEG)
    mw = jnp.maximum(jnp.max(lw, axis=-1, keepdims=True), jnp.max(ln, axis=-1, keepdims=True))
    pw, pn = jnp.exp(lw - mw), jnp.exp(ln - mw)
    den = jnp.sum(pw, axis=-1, keepdims=True) + jnp.sum(pn, axis=-1, keepdims=True)
    ow = lax.dot_general(pw.astype(BF16), wv_ref[0, 0].astype(BF16), (((1,), (1,)), ((), ())),
                         preferred_element_type=F32) + _dot(pn, vw_pad)
    owin_ref[0] = ow / den


def _dot_tn_exact(a, b01, terms=2):
    dims = (((0,), (0,)), ((), ()))
    return sum(lax.dot_general(p, b01, dims, preferred_element_type=F32) for p in _split(a, terms))


def _nsa_sample_head(main, kc_t, vc_t, win_k, win_v, layer, qg, kg, tables, past):
    b, ts, _ = main.shape
    wbuf = win_k.shape[-1]
    n_rows = S_ROWS * ts
    bias_c, bias_w, bias_n = tables
    kern = functools.partial(_nsa_sample_head_kernel, ts=ts, past=past, wbuf=wbuf)
    col = lambda width, off: pl.BlockSpec((1, ts, width), lambda i: (i, 0, off // width))
    full = lambda a: pl.BlockSpec(a.shape, lambda i: (0,) * a.ndim)
    per_b = lambda a: pl.BlockSpec((1,) + a.shape[1:], lambda i: (i,) + (0,) * (a.ndim - 1))
    win = pl.BlockSpec((1, 1, LANES, wbuf), lambda i: (layer, i, 0, 0))
    qg2 = jnp.tile(qg.reshape(1, C_HD), (1, 2))
    kg2 = jnp.tile(kg, (1, 2))
    kg_col = jnp.tile(kg2[0].reshape(LANES, 1), (1, LANES))
    rows = lambda n, dt: jax.ShapeDtypeStruct((b, n_rows, n), dt)
    out_rows = lambda n: pl.BlockSpec((1, n_rows, n), lambda i: (i, 0, 0))
    return pl.pallas_call(
        kern,
        grid=(b,),
        in_specs=[col(C_W, P_Q), col(2 * LANES, P_KV), col(2 * LANES, P_KV + 2 * LANES),
                  col(2 * LANES, P_KV + 4 * LANES), per_b(kc_t), per_b(vc_t), win, win,
                  full(qg2), full(kg2), full(kg_col), full(bias_c), full(bias_w), full(bias_n)],
        out_specs=[out_rows(LANES), out_rows(BLK_LANES), out_rows(LANES), out_rows(LANES),
                   pl.BlockSpec((1, 6, ts, LANES), lambda i: (i, 0, 0, 0))],
        out_shape=[rows(LANES, BF16), rows(BLK_LANES, F32), rows(LANES, F32), rows(LANES, F32),
                   jax.ShapeDtypeStruct((b, 6, ts, LANES), F32)],
        scratch_shapes=[pltpu.VMEM((n_rows, LANES), F32)],
        compiler_params=_cparams(("parallel",)),
        name="nsa_sample_head",
    )(main, main, main, main, kc_t, vc_t, win_k, win_v, qg2, kg2, kg_col, bias_c, bias_w, bias_n)


def _slc_pages_kernel(pt_ref, *refs, n, ts, last_chunk):
    k_refs, v_refs = refs[:n], refs[n:2 * n]
    (q_ref, sel_ref, ocmp_ref, owin_ref, gate_ref, new_ref, blast_ref, bn_ref,
     o_ref, m_ref, l_ref, acc_ref) = refs[2 * n:]
    c = pl.program_id(1)
    n_rows = S_ROWS * ts
    nt = (((1,), (1,)), ((), ()))

    @pl.when(c == 0)
    def _():
        m_ref[...] = jnp.full(m_ref.shape, M_INIT, F32)
        l_ref[...] = jnp.zeros(l_ref.shape, F32)
        acc_ref[...] = jnp.zeros(acc_ref.shape, F32)

    q32 = q_ref[0]
    keys = jnp.concatenate([r[0, 0] for r in k_refs], axis=1).astype(BF16)
    vals = jnp.concatenate([r[0, 0] for r in v_refs], axis=1).astype(BF16)
    s = jnp.dot(q32, keys, preferred_element_type=F32)
    cut = (n - 1) * PAGE_SIZE
    tail = s[:, cut:] + jnp.where(c == last_chunk, blast_ref[...], 0.0)
    s = jnp.concatenate([s[:, :cut], tail], axis=1) if cut else tail
    row = lax.broadcasted_iota(jnp.int32, (LANES, n * PAGE_SIZE), 0)
    col = lax.broadcasted_iota(jnp.int32, (LANES, n * PAGE_SIZE), 1) // CMP_BLOCK
    expand = (row == col).astype(BF16)
    picked = jnp.dot(sel_ref[0, 0].astype(BF16), expand, preferred_element_type=F32) > 0.5
    s = jnp.where(picked, s, NEG)
    m_old = m_ref[...]
    m_new = jnp.maximum(m_old, jnp.max(s, axis=-1, keepdims=True))
    alpha = jnp.exp(m_old - m_new)
    p = jnp.exp(s - m_new)
    l_ref[...] = alpha * l_ref[...] + jnp.sum(p, axis=-1, keepdims=True)
    acc_ref[...] = alpha * acc_ref[...] + lax.dot_general(p.astype(BF16), vals, nt,
                                                          preferred_element_type=F32)
    m_ref[...] = m_new

    @pl.when(c == last_chunk)
    def _():
        t_of_row = lax.broadcasted_iota(jnp.int32, (n_rows, 1), 0) % ts
        pad_rows = jnp.zeros((NEW_PAD - ts, LANES), F32)
        k_new = jnp.concatenate([new_ref[0, 2], pad_rows], axis=0)
        v_new = jnp.concatenate([new_ref[0, 3], pad_rows], axis=0)
        nkey = lax.broadcasted_iota(jnp.int32, (n_rows, NEW_PAD), 1)
        sn = lax.dot_general(q32, k_new.astype(BF16), nt, preferred_element_type=F32) + bn_ref[...]
        sn = jnp.where((nkey <= t_of_row) & (nkey < ts), sn, NEG)
        m_prev = m_ref[...]
        m_fin = jnp.maximum(m_prev, jnp.max(sn, axis=-1, keepdims=True))
        a_fin = jnp.exp(m_prev - m_fin)
        pn = jnp.exp(sn - m_fin)
        l_fin = a_fin * l_ref[...] + jnp.sum(pn, axis=-1, keepdims=True)
        o_slc = (a_fin * acc_ref[...] + _dot(pn, v_new)) / l_fin

        ri = lax.broadcasted_iota(jnp.int32, (n_rows, NEW_PAD), 0) % ts
        ti = lax.broadcasted_iota(jnp.int32, (n_rows, NEW_PAD), 1)
        gates = jnp.concatenate([_sigmoid(gate_ref[0]), pad_rows], axis=0)
        per_row = _dot_exact_lhs((ri == ti).astype(BF16), gates)
        head = lax.broadcasted_iota(jnp.int32, (n_rows, LANES), 0) // ts
        glane = lax.broadcasted_iota(jnp.int32, (n_rows, LANES), 1)
        gate = lambda br: jnp.sum(jnp.where(glane == br * C_HEADS + head, per_row, 0.0), axis=-1,
                                  keepdims=True)
        comb = gate(0) * ocmp_ref[0] + gate(1) * o_slc + gate(2) * owin_ref[0]
        low = (lax.broadcasted_iota(jnp.int32, (ts, LANES), 1) // C_HD) == 0
        for n_ in range(C_HPG):
            top = comb[n_ * ts:(n_ + 1) * ts]
            bot = comb[(C_HPG + n_) * ts:(C_HPG + n_ + 1) * ts]
            o_ref[0, :, n_ * LANES:(n_ + 1) * LANES] = jnp.where(low, top, bot)


def _slc_pages(page_table, cache_k, cache_v, layer, q32, sel_chunks, o_cmp, o_win, gates, new_rows,
               bias_last, bias_n):
    b, n_pages = page_table.shape
    n = min(PAGES_PER_STEP, n_pages)
    chunks = n_pages // n
    ts = new_rows.shape[2]
    n_rows = S_ROWS * ts
    kern = functools.partial(_slc_pages_kernel, n=n, ts=ts, last_chunk=chunks - 1)
    per_b = lambda a: pl.BlockSpec((1,) + a.shape[1:], lambda bi, c, pt: (bi,) + (0,) * (a.ndim - 1))
    full = lambda a: pl.BlockSpec(a.shape, lambda bi, c, pt: (0,) * a.ndim)
    return pl.pallas_call(
        kern,
        grid_spec=pltpu.PrefetchScalarGridSpec(
            num_scalar_prefetch=1, grid=(b, chunks),
            in_specs=_page_specs(layer, n) + _page_specs(layer, n) + [
                per_b(q32), pl.BlockSpec((1, 1, n_rows, LANES), lambda bi, c, pt: (bi, c, 0, 0)),
                per_b(o_cmp), per_b(o_win),
                pl.BlockSpec((1, ts, LANES), lambda bi, c, pt: (bi, 0, G_CG // LANES)),
                per_b(new_rows), full(bias_last), full(bias_n)],
            out_specs=pl.BlockSpec((1, ts, C_W), lambda bi, c, pt: (bi, 0, 0)),
            scratch_shapes=[pltpu.VMEM((n_rows, 1), F32), pltpu.VMEM((n_rows, 1), F32),
                            pltpu.VMEM((n_rows, LANES), F32)]),
        out_shape=jax.ShapeDtypeStruct((b, ts, C_W), F32),
        compiler_params=_cparams(("parallel", "arbitrary")),
        name="nsa_sample_slc_pages",
    )(page_table, *([cache_k] * n), *([cache_v] * n), q32, sel_chunks, o_cmp, o_win, gates, new_rows,
      bias_last, bias_n)


def _sample_bias_tables(rel_bias, past, ts, wbuf):
    tbl = jnp.repeat(_distance_table(rel_bias), ts, axis=0)
    t = jnp.tile(jnp.arange(ts), S_ROWS)[:, None]
    look = lambda dist: jnp.einsum('rd,rcd->rc', tbl, _one_hot(dist, MAX_DIST),
                                   precision=lax.Precision.HIGHEST)
    blk = jnp.arange(BLK_LANES)[None, :]
    bias_c = look(past + t - CMP_BLOCK * blk - (CMP_BLOCK - 1))
    bias_w = look(wbuf + t - jnp.arange(wbuf)[None, :])
    bias_n = look(t - jnp.arange(NEW_PAD)[None, :])
    bias_last = look(PAGE_SIZE + t - jnp.arange(PAGE_SIZE)[None, :])
    return bias_c, bias_w, bias_n, bias_last


def _nsa_sample(main, gate, caches, win_k, win_v, page_table, layer, qg, kg, rel_bias):
    b, ts, _ = main.shape
    n_pages = page_table.shape[1]
    past = n_pages * PAGE_SIZE
    wbuf = win_k.shape[-1]
    n_rows = S_ROWS * ts
    bias_c, bias_w, bias_n, bias_last = _sample_bias_tables(rel_bias, past, ts, wbuf)
    kc_chunks, vc_chunks = _cmp_pages(page_table, caches[0], caches[1], layer)
    n_cached = past // CMP_BLOCK
    unchunk = lambda z: z.transpose(0, 2, 1, 3).reshape(b, LANES, n_cached)
    fit = lambda z: jnp.pad(z, ((0, 0), (0, 0), (0, BLK_LANES - n_cached)))
    q32, sel, o_cmp, o_win, new_rows = _nsa_sample_head(
        main, fit(unchunk(kc_chunks)), fit(unchunk(vc_chunks)), win_k, win_v, layer, qg, kg,
        (bias_c, bias_w, bias_n), past)
    per_step = 2 * min(PAGES_PER_STEP, n_pages)
    sel_chunks = sel[:, :, :n_cached].reshape(b, n_rows, n_cached // per_step, per_step)
    sel_chunks = jnp.pad(sel_chunks.transpose(0, 2, 1, 3), ((0, 0), (0, 0), (0, 0), (0, LANES - per_step)))
    o = _slc_pages(page_table, caches[2], caches[3], layer, q32, sel_chunks, o_cmp, o_win,
                   gate.reshape(b, ts, N_GATE), new_rows, bias_last, bias_n)
    return o.reshape(b * ts, C_W), new_rows


def _shift_order(z):
    part = lambda off, n: z[..., off:off + n]
    return jnp.concatenate([part(OFF_R, A_W), part(OFF_K, A_W), part(OFF_V, A_W), part(OFF_WL, A_LORA_W),
                            part(OFF_AL, A_LORA_A), part(OFF_GL, A_LORA_G)], axis=-1)


def _layer_params(l, w_in, shift_mu, w_br_a, w_br_b, w_br_c, w_out, w_mq, w_mk, w_mv, w_mo, w_ff1, w_ff2):
    w = w_in[l]
    cast = lambda a: a.astype(BF16)
    cols = lambda off, n: cast(w[:, off:off + n])
    w_q = cols(OFF_Q, C_W).reshape(D_MODEL, C_KV, C_HPG, C_HD).transpose(0, 2, 1, 3).reshape(D_MODEL, C_W)
    w_main = jnp.concatenate([
        cols(OFF_R, A_W), cols(OFF_K, A_W), cols(OFF_V, A_W), cols(OFF_POOL, B_W), w_q,
        cols(OFF_KV, 6 * C_KVW), cols(OFF_WL, A_LORA_W), cols(OFF_AL, A_LORA_A), cols(OFF_GL, A_LORA_G)],
        axis=1)
    w_gate = jnp.concatenate([cols(OFF_MG, 3 * D_MODEL), cols(OFF_CG, 3 * C_HEADS),
                              jnp.zeros((D_MODEL, N_GATE - G_CG - 3 * C_HEADS), BF16)], axis=1)
    wc = w_br_c[l].reshape(C_KV, C_HPG, C_HD, D_MODEL).transpose(1, 0, 2, 3).reshape(C_W, D_MODEL)
    return dict(
        w_main=w_main, w_gate=w_gate, mu=_shift_order(shift_mu[l].reshape(1, -1)),
        wa=cast(w_br_a[l]), wb=cast(w_br_b[l]), wc=cast(wc), wo=cast(w_out[l]),
        wq=cast(w_mq[l]), wk=cast(w_mk[l]), wv=cast(w_mv[l]), wmo=cast(w_mo[l]),
        w1=cast(w_ff1[l]), w2=cast(w_ff2[l]))


def _shift_row(main_row):
    lora = main_row[:, P_LORA:]
    return jnp.concatenate([
        main_row[:, P_R:P_R + A_W], lora[:, :A_LORA_W], main_row[:, P_K:P_K + A_W],
        main_row[:, P_V:P_V + A_W], lora[:, A_LORA_W:A_LORA_W + A_LORA_A],
        lora[:, A_LORA_W + A_LORA_A:]], axis=1)


def _feature_major(z):
    z = jnp.moveaxis(z, -3, -1)
    return z.reshape(z.shape[:-3] + (C_KVW, z.shape[-1]))


def _token_major(z):
    b, _, rows = z.shape
    return z.reshape(b, C_KV, C_HD, rows).transpose(0, 3, 1, 2)


def kernel(x_prompt, x_sample, state_rwkv, state_shift, state_pool, cache_cmp_k, cache_cmp_v, cache_slc_k, cache_slc_v, cache_win_k, cache_win_v, cache_mem_k, cache_mem_v, page_table, mem_prompt, rel_bias, norm_mix_g, w_in, shift_mu, rwkv_w0, rwkv_w2, rwkv_a0, rwkv_a2, rwkv_g2, rwkv_kk, rwkv_ka, rwkv_rk, rwkv_lnx_g, rwkv_lnx_b, pool_w, pool_scale, nsa_q_g, nsa_k_g, w_br_a, w_br_b, w_br_c, w_out, norm_memx_g, norm_mem_g, w_mq, w_mk, w_mv, mem_q_g, mem_k_g, w_mo, norm_ffn_g, w_ff1, w_ff2):
    bp, t = x_prompt.shape[:2]
    bs, ts = x_sample.shape[:2]
    past = page_table.shape[1] * PAGE_SIZE
    wbuf_p = min(WINDOW, t)
    ts_pad = 16
    bnear, bcmp = _bias_tables(rel_bias)
    caches = [_feature_major(z) for z in (cache_cmp_k, cache_cmp_v, cache_slc_k, cache_slc_v)]
    win_k, win_v = _feature_major(cache_win_k), _feature_major(cache_win_v)
    xp = x_prompt.reshape(bp * t, D_MODEL)
    xs = x_sample.reshape(bs * ts, D_MODEL)
    outs_p = [[] for _ in range(11)]
    outs_s = [[] for _ in range(9)]
    row = lambda a: a.reshape(1, -1)
    for l in range(DEPTH):
        prm = _layer_params(l, w_in, shift_mu, w_br_a, w_br_b, w_br_c, w_out, w_mq, w_mk, w_mv, w_mo,
                            w_ff1, w_ff2)
        rw = (prm["mu"], row(rwkv_w0[l]), rwkv_w2[l].astype(BF16), row(rwkv_a0[l]), rwkv_a2[l].astype(BF16),
              rwkv_g2[l].astype(BF16), row(rwkv_kk[l]), row(rwkv_ka[l]), row(rwkv_rk[l]),
              row(rwkv_lnx_g[l]), row(rwkv_lnx_b[l]))
        pool_wl = pool_w[l].astype(BF16)

        main = _norm_matmul(xp, norm_mix_g[l], prm["w_main"], 512, N_MAIN).reshape(bp, t, N_MAIN)
        gate = _norm_matmul(xp, norm_mix_g[l], prm["w_gate"], 512, N_GATE)
        o_a, s_pairs = _rwkv(main, jnp.zeros((bp, 1, A_COLS), F32), jnp.zeros((bp, 4, 128, 128), F32), rw)
        o_b = _pool(main, jnp.zeros((bp, POOL_HALO, B_W), F32), pool_wl, pool_scale[l], 0, 512)
        qt, ks16, vs16, kw16, vw16, kc, vc, kv_rows = _nsa_prep(main, nsa_q_g[l], nsa_k_g[l], 512)
        front = lambda z: jnp.pad(z, ((0, 0), (FRONT_TILES * Q_TILE, 0), (0, 0)))
        front_t = lambda z: jnp.pad(z, ((0, 0), (0, 0), (FRONT_TILES * Q_TILE, 0)))
        o_c = _nsa_attn(qt, front(ks16), front_t(vs16), front(kw16), front_t(vw16), kc, vc,
                        gate.reshape(bp, t, N_GATE), bnear, bcmp)
        xp = _merge(xp, o_a.reshape(bp * t, A_W), o_b.reshape(bp * t, B_W), o_c.reshape(bp * t, C_W), gate,
                    prm["wa"], prm["wb"], prm["wc"], prm["wo"], 512)
        mk, mv = _mem_kv(mem_prompt.reshape(bp * N_MEM, D_MODEL), norm_mem_g[l], prm["wk"], prm["wv"],
                         mem_k_g[l])
        mk, mv = mk.reshape(bp, N_MEM, M_W), mv.reshape(bp, N_MEM, M_W)
        xp = _mem_ffn(xp, mk, mv, norm_memx_g[l], prm["wq"], mem_q_g[l], prm["wmo"], norm_ffn_g[l],
                      prm["w1"], prm["w2"], 1024, t)
        for dst, val in zip(outs_p, (
                _unpair_states(s_pairs), _shift_row(main[:, -1]), main[:, -POOL_BUF:, P_POOL:P_POOL + B_W],
                _token_major(kv_rows[:, 0]), _token_major(kv_rows[:, 1]), _token_major(kv_rows[:, 2]),
                _token_major(kv_rows[:, 3]), _token_major(kv_rows[:, 4, :, -wbuf_p:]),
                _token_major(kv_rows[:, 5, :, -wbuf_p:]),
                mk.reshape(bp, N_MEM, M_HEADS, M_HD), mv.reshape(bp, N_MEM, M_HEADS, M_HD))):
            dst.append(val)

        main = _norm_matmul(xs, norm_mix_g[l], prm["w_main"], bs * ts, N_MAIN).reshape(bs, ts, N_MAIN)
        gate = _norm_matmul(xs, norm_mix_g[l], prm["w_gate"], bs * ts, N_GATE)
        main_pad = jnp.pad(main, ((0, 0), (0, ts_pad - ts), (0, 0)))
        ulast = _shift_order(state_shift[l]).reshape(bs, 1, A_COLS)
        o_a, s_pairs = _rwkv(main_pad, ulast, _pair_states(state_rwkv[l]), rw, t_valid=ts)
        o_a = o_a[:, :ts]
        hist = jnp.concatenate([jnp.zeros((bs, POOL_HALO - POOL_BUF, B_W), F32), state_pool[l]], axis=1)
        o_b = _pool(main_pad, hist, pool_wl, pool_scale[l], past, ts_pad)[:, :ts]
        u_pool = main[:, :, P_POOL:P_POOL + B_W]
        o_c, new_rows = _nsa_sample(main, gate, caches, win_k, win_v, page_table, l, nsa_q_g[l], nsa_k_g[l],
                                    rel_bias)
        new_kv = [new_rows[:, n].reshape(bs, ts, C_KV, C_HD) for n in range(6)]
        slide = lambda buf, n: _token_major(jnp.concatenate(
            [buf[l][:, :, ts:], new_rows[:, n].transpose(0, 2, 1)], axis=2))
        xs = _merge(xs, o_a.reshape(bs * ts, A_W), o_b.reshape(bs * ts, B_W), o_c, gate,
                    prm["wa"], prm["wb"], prm["wc"], prm["wo"], bs * ts)
        xs = _mem_ffn(xs, cache_mem_k[l].reshape(bs, N_MEM, M_W), cache_mem_v[l].reshape(bs, N_MEM, M_W),
                      norm_memx_g[l], prm["wq"], mem_q_g[l], prm["wmo"], norm_ffn_g[l],
                      prm["w1"], prm["w2"], 32, ts)
        for dst, val in zip(outs_s, (
                _unpair_states(s_pairs), _shift_row(main[:, -1]),
                jnp.concatenate([state_pool[l], u_pool], axis=1)[:, -POOL_BUF:],
                new_kv[0], new_kv[1], new_kv[2], new_kv[3], slide(win_k, 4), slide(win_v, 5))):
            dst.append(val)

    return ((xp.reshape(bp, t, D_MODEL), xs.reshape(bs, ts, D_MODEL))
            + tuple(jnp.stack(o) for o in outs_p) + tuple(jnp.stack(o) for o in outs_s))
```

```python
import functools
import math

import jax
import jax.numpy as jnp
from jax import lax
from jax.experimental import pallas as pl
from jax.experimental.pallas import tpu as pltpu

F32 = jnp.float32
BF16 = jnp.bfloat16

D_MODEL = 1024
DEPTH = 2
PAGE_SIZE = 128
A_HEADS = 8
A_HD = 64
A_W = A_HEADS * A_HD
A_LORA_W = 64
A_LORA_A = 64
A_LORA_G = 128
A_LORA = A_LORA_W + A_LORA_A + A_LORA_G
LNX_EPS = 64e-5
L2_EPS = 1e-24
B_GROUPS = 4
B_GW = 128
B_W = B_GROUPS * B_GW
POOL_WINDOWS = (2, 4, 8, 16)
POOL_BUF = 15
POOL_HALO = 16
C_HEADS = 8
C_HD = 64
C_W = C_HEADS * C_HD
C_KV = 2
C_HPG = C_HEADS // C_KV
C_KVW = C_KV * C_HD
CMP_BLOCK = 64
N_SEL = 16
WINDOW = 512
Q_TILE = 128
N_FORCED = 3
NEG = -1e30
N_BUCKETS = 32
MAX_DIST = 128
N_MEM = 256
M_HEADS = 4
M_HD = 128
M_W = M_HEADS * M_HD
D_FF = 4 * D_MODEL
RMS_EPS = 1e-6

OFF_R = 0
OFF_WL = OFF_R + A_W
OFF_K = OFF_WL + A_LORA_W
OFF_V = OFF_K + A_W
OFF_AL = OFF_V + A_W
OFF_GL = OFF_AL + A_LORA_A
A_COLS = OFF_GL + A_LORA_G
OFF_POOL = A_COLS
OFF_Q = OFF_POOL + B_W
OFF_KV = OFF_Q + C_W
OFF_CG = OFF_KV + 6 * C_KVW
OFF_MG = OFF_CG + 3 * C_HEADS
N_IN = OFF_MG + 3 * D_MODEL

P_R = 0
P_K = P_R + A_W
P_V = P_K + A_W
P_POOL = P_V + A_W
P_Q = P_POOL + B_W
P_KV = P_Q + C_W
P_LORA = P_KV + 6 * C_KVW
N_MAIN = P_LORA + A_LORA
G_MG = 0
G_CG = 3 * D_MODEL
N_GATE = G_CG + 128

BRANCH_DTYPE = BF16
RWKV_CHUNK = 64
RWKV_SUB = 16
LANES = 128
VMEM_LIMIT = 56 * 1024 * 1024


def _cparams(sem):
    return pltpu.CompilerParams(dimension_semantics=sem, vmem_limit_bytes=VMEM_LIMIT)


def _dot(a, b, dims=(((1,), (0,)), ((), ()))):
    return lax.dot_general(a.astype(BF16), b.astype(BF16), dims, preferred_element_type=F32)


def _dot_nt(a, b):
    return _dot(a, b, (((1,), (1,)), ((), ())))


def _split(x, terms):
    parts = []
    for n in range(terms):
        part = x.astype(BF16)
        parts.append(part)
        if n + 1 < terms:
            x = x - part.astype(F32)
    return parts


def _dot_exact_rhs(a, b01, terms=2):
    dims = (((1,), (0,)), ((), ()))
    return sum(lax.dot_general(p, b01, dims, preferred_element_type=F32) for p in _split(a, terms))


def _dot_exact_lhs(a01, b, terms=2):
    dims = (((1,), (0,)), ((), ()))
    return sum(lax.dot_general(a01, p, dims, preferred_element_type=F32) for p in _split(b, terms))


def _sigmoid(x):
    return 1.0 / (1.0 + jnp.exp(-x))


def _rms(x, g):
    return x * lax.rsqrt(jnp.mean(x * x, axis=-1, keepdims=True) + RMS_EPS) * g


def _norm_matmul_kernel(x_ref, g_ref, w_ref, o_ref, xn_ref):
    @pl.when(pl.program_id(1) == 0)
    def _():
        xn_ref[...] = _rms(x_ref[...], g_ref[...]).astype(BF16)

    o_ref[...] = jnp.dot(xn_ref[...], w_ref[...], preferred_element_type=F32)


def _norm_matmul(x, g, w, tm, tn):
    m, k = x.shape
    n = w.shape[1]
    return pl.pallas_call(
        _norm_matmul_kernel,
        grid=(m // tm, n // tn),
        in_specs=[pl.BlockSpec((tm, k), lambda i, j: (i, 0)),
                  pl.BlockSpec((1, k), lambda i, j: (0, 0)),
                  pl.BlockSpec((k, tn), lambda i, j: (0, j))],
        out_specs=pl.BlockSpec((tm, tn), lambda i, j: (i, j)),
        out_shape=jax.ShapeDtypeStruct((m, n), F32),
        scratch_shapes=[pltpu.VMEM((tm, k), BF16)],
        compiler_params=_cparams(("parallel", "arbitrary")),
        name="norm_matmul",
    )(x, g.reshape(1, k), w)


def _rwkv_kernel(r_ref, k_ref, v_ref, lora_ref, ulast_ref, s0_ref, mu_ref, w0_ref, w2_ref, a0_ref,
                 a2_ref, g2_ref, kk_ref, ka_ref, rk_ref, lng_ref, lnb_ref,
                 o_ref, sout_ref, s_ref, prev_ref, ar_ref, bk_ref, vb_ref, y_ref, *, chunk, bt, t_valid):
    c = pl.program_id(1)
    nc = pl.num_programs(1)
    C = chunk
    R = bt * C
    NP = A_HEADS // 2
    PB = 2 * C

    @pl.when(c == 0)
    def _():
        s_ref[...] = s0_ref[...].reshape(bt * NP, 2 * A_HD, 2 * A_HD)
        prev_ref[...] = ulast_ref[:, 0, :]

    row = lax.broadcasted_iota(jnp.int32, (R, 1), 0)

    def shift_mix(u3, lo, hi):
        u = u3.reshape(R, hi - lo)
        u_prev = pltpu.roll(u, 1, 0)
        for b in range(bt):
            u_prev = jnp.where(row == b * C, prev_ref[b:b + 1, lo:hi], u_prev)
        for b in range(bt):
            prev_ref[b:b + 1, lo:hi] = u[(b + 1) * C - 1:(b + 1) * C, :]
        return u + (u_prev - u) * mu_ref[:, lo:hi]

    r = shift_mix(r_ref[...], 0, A_W)
    k = shift_mix(k_ref[...], A_W, 2 * A_W)
    v = shift_mix(v_ref[...], 2 * A_W, 3 * A_W)
    lora = shift_mix(lora_ref[...], 3 * A_W, 3 * A_W + A_LORA)
    wl = lora[:, 0:A_LORA_W]
    al = lora[:, A_LORA_W:A_LORA_W + A_LORA_A]
    gl = lora[:, A_LORA_W + A_LORA_A:A_LORA]

    z = -(w0_ref[...] + _dot(jnp.tanh(wl), w2_ref[...]))
    softplus = jnp.maximum(z, 0.0) + jnp.log(1.0 + jnp.exp(-jnp.abs(z)))
    w = -softplus - 0.5
    a = _sigmoid(a0_ref[...] + _dot(al, a2_ref[...]))
    g = _dot(_sigmoid(gl), g2_ref[...])

    lane = lax.broadcasted_iota(jnp.int32, (A_W, A_W), 1) // A_HD
    sub = lax.broadcasted_iota(jnp.int32, (A_W, A_W), 0) // A_HD
    head_ones = (lane == sub).astype(BF16)

    kkv = k * kk_ref[...]
    head_sum = lambda z: _dot_exact_rhs(z, head_ones, terms=1)
    kkn = kkv * lax.rsqrt(jnp.maximum(head_sum(kkv * kkv), L2_EPS))
    k2 = k * (1.0 + (a - 1.0) * ka_ref[...])
    log_d = -jnp.exp(w)
    t_in = row % C
    if t_valid is not None:
        live = (c * C + t_in) < t_valid
        log_d = jnp.where(live, log_d, 0.0)
        kkn = jnp.where(live, kkn, 0.0)
        k2 = jnp.where(live, k2, 0.0)

    ri = lax.broadcasted_iota(jnp.int32, (R, R), 0)
    ci = lax.broadcasted_iota(jnp.int32, (R, R), 1)
    cum_mask = ((ri // C) == (ci // C)) & (ci <= ri)
    cum = _dot_exact_lhs(cum_mask.astype(BF16), log_d)
    c_incl = jnp.exp(cum)
    c_inv = jnp.exp(-cum)
    a_t = -kkn * jnp.exp(cum - log_d)
    r_t = r * c_incl
    b_t = kkn * a * c_inv
    k_t = k2 * c_inv

    lane = lax.broadcasted_iota(jnp.int32, (1, 2 * A_HD), 1) // A_HD
    def stage(ref, off, x):
        for b in range(bt):
            for p in range(NP):
                blk = x[b * C:(b + 1) * C, 2 * A_HD * p:2 * A_HD * (p + 1)]
                for hh in range(2):
                    ref[b * NP + p, off + hh * C:off + (hh + 1) * C, :] = (
                        jnp.where(lane == hh, blk, 0.0).astype(BF16))

    stage(ar_ref, 0, a_t)
    stage(ar_ref, PB, r_t)
    stage(bk_ref, 0, b_t)
    stage(bk_ref, PB, k_t)
    stage(vb_ref, 0, v)
    ar = ar_ref[...]
    bk = bk_ref[...]
    vb = vb_ref[...]

    nn = (((2,), (1,)), ((0,), (0,)))
    nt = (((2,), (2,)), ((0,), (0,)))
    tn = (((1,), (1,)), ((0,), (0,)))
    bdot = lambda x, y, dims=nn: lax.dot_general(x.astype(BF16), y.astype(BF16), dims,
                                                 preferred_element_type=F32)
    bi = lax.broadcasted_iota(jnp.int32, (PB, PB), 0)
    bj = lax.broadcasted_iota(jnp.int32, (PB, PB), 1)
    same_head = (bi // C) == (bj // C)
    strict = same_head & ((bj % C) < (bi % C))
    lower = same_head & ((bj % C) <= (bi % C))
    sub_blk = (bi // RWKV_SUB) == (bj // RWKV_SUB)
    eye = (bi == bj).astype(F32)

    gram = bdot(ar, bk, nt)
    l_b = jnp.where(strict, gram[:, :PB, :PB], 0.0)
    l_k = jnp.where(strict, gram[:, :PB, PB:], 0.0)
    m_b = jnp.where(lower, gram[:, PB:, :PB], 0.0)
    m_k = jnp.where(lower, gram[:, PB:, PB:], 0.0)
    dg = jnp.where(sub_blk, l_b, 0.0)
    off = l_b - dg
    t_inv = eye + dg
    pw = dg
    for _ in range(int(math.log2(RWKV_SUB)) - 1):
        pw = bdot(pw, pw)
        t_inv = t_inv + bdot(t_inv, pw)
    n1 = bdot(t_inv, off)
    n2 = bdot(n1, n1)
    full = eye + n1 + n2 + bdot(n1, n2)
    t_full = bdot(full, t_inv)

    s_old = s_ref[...]
    w0 = bdot(ar, s_old, nt)
    u = bdot(t_full, w0[:, :PB] + bdot(l_k, vb))
    uv = jnp.concatenate([u, vb.astype(F32)], axis=1)
    yb = w0[:, PB:] + bdot(jnp.concatenate([m_b, m_k], axis=2), uv)
    s_new = s_old + bdot(uv, bk, tn)
    for b in range(bt):
        for p in range(NP):
            sl = slice(2 * A_HD * p, 2 * A_HD * (p + 1))
            idx = b * NP + p
            s_ref[idx] = s_new[idx] * c_incl[(b + 1) * C - 1:(b + 1) * C, sl]
            y_ref[b * C:(b + 1) * C, sl] = yb[idx, :C] + yb[idx, C:]
    y = y_ref[...]

    inv_n = 1.0 / A_HD
    mean = head_sum(y) * inv_n
    yc = y - mean
    var = head_sum(yc * yc) * inv_n
    yn = yc * lax.rsqrt(var + LNX_EPS) * lng_ref[...] + lnb_ref[...]
    bonus = head_sum(r * k2 * rk_ref[...]) * v
    o_ref[...] = ((yn + bonus) * g).reshape(bt, C, A_W).astype(o_ref.dtype)

    @pl.when(c == nc - 1)
    def _():
        sout_ref[...] = s_ref[...].reshape(bt, NP, 2 * A_HD, 2 * A_HD)


RWKV_BATCH_TILE = 4


def _rwkv(proj, ulast, s0, prm, t_valid=None):
    b, t, _ = proj.shape
    C = min(RWKV_CHUNK, t)
    bt = min(RWKV_BATCH_TILE, b)
    nc = t // C
    vec = lambda n: pl.BlockSpec((1, n), lambda i, c: (0, 0))
    mat = lambda m, n: pl.BlockSpec((m, n), lambda i, c: (0, 0))
    col = lambda width, off: pl.BlockSpec((bt, C, width), lambda i, c: (i, c, off // width))
    n_shift = 3 * A_W + A_LORA
    n_pairs = bt * (A_HEADS // 2)
    kern = functools.partial(_rwkv_kernel, chunk=C, bt=bt, t_valid=t_valid)
    return pl.pallas_call(
        kern,
        grid=(b // bt, nc),
        in_specs=[col(A_W, P_R), col(A_W, P_K), col(A_W, P_V), col(A_LORA, P_LORA),
                  pl.BlockSpec((bt, 1, n_shift), lambda i, c: (i, 0, 0)),
                  pl.BlockSpec((bt, 4, 128, 128), lambda i, c: (i, 0, 0, 0)),
                  vec(n_shift), vec(A_W), mat(A_LORA_W, A_W), vec(A_W), mat(A_LORA_A, A_W),
                  mat(A_LORA_G, A_W), vec(A_W), vec(A_W), vec(A_W), vec(A_W), vec(A_W)],
        out_specs=[pl.BlockSpec((bt, C, A_W), lambda i, c: (i, c, 0)),
                   pl.BlockSpec((bt, 4, 128, 128), lambda i, c: (i, 0, 0, 0))],
        out_shape=[jax.ShapeDtypeStruct((b, t, A_W), BRANCH_DTYPE),
                   jax.ShapeDtypeStruct((b, 4, 128, 128), F32)],
        scratch_shapes=[pltpu.VMEM((n_pairs, 128, 128), F32), pltpu.VMEM((bt, n_shift), F32),
                        pltpu.VMEM((n_pairs, 4 * C, 128), BF16), pltpu.VMEM((n_pairs, 4 * C, 128), BF16),
                        pltpu.VMEM((n_pairs, 2 * C, 128), BF16), pltpu.VMEM((bt * C, A_W), F32)],
        compiler_params=_cparams(("parallel", "arbitrary")),
        name="rwkv_chunk",
    )(proj, proj, proj, proj, ulast, s0, *prm)


def _pair_states(s):
    b = s.shape[0]
    s = s.reshape(b, 4, 2, A_HD, A_HD)
    z = jnp.zeros_like(s[:, :, 0])
    top = jnp.concatenate([s[:, :, 0], z], axis=-1)
    bot = jnp.concatenate([z, s[:, :, 1]], axis=-1)
    return jnp.concatenate([top, bot], axis=-2)


def _unpair_states(sp):
    b = sp.shape[0]
    h0 = sp[:, :, :A_HD, :A_HD]
    h1 = sp[:, :, A_HD:, A_HD:]
    return jnp.stack([h0, h1], axis=2).reshape(b, A_HEADS, A_HD, A_HD)


def _pool_kernel(u_ref, halo_ref, hist_ref, pm_ref, ps_ref, o_ref, ext_ref, *, tm, pos0):
    i = pl.program_id(1)

    @pl.when(i == 0)
    def _():
        ext_ref[0:POOL_HALO, :] = hist_ref[0]

    @pl.when(i > 0)
    def _():
        ext_ref[0:POOL_HALO, :] = halo_ref[0]

    cur = u_ref[0]
    ext_ref[POOL_HALO:POOL_HALO + tm, :] = cur
    pos = pos0 + i * tm + lax.broadcasted_iota(jnp.int32, (tm, 1), 0)
    outs = []
    for gi, win in enumerate(POOL_WINDOWS):
        lo, hi = gi * B_GW, (gi + 1) * B_GW
        s = cur[:, lo:hi]
        for back in range(1, win):
            s = s + ext_ref[POOL_HALO - back:POOL_HALO - back + tm, lo:hi]
        cnt = jnp.minimum(win, pos + 1).astype(F32)
        d = s / cnt - cur[:, lo:hi]
        outs.append(_dot(d, pm_ref[gi]))
    o_ref[0] = (jnp.concatenate(outs, axis=1) * ps_ref[...]).astype(o_ref.dtype)


def _pool(proj, hist, pm, ps, pos0, tm):
    b, t, _ = proj.shape
    per = tm // POOL_HALO
    kern = functools.partial(_pool_kernel, tm=tm, pos0=pos0)
    return pl.pallas_call(
        kern,
        grid=(b, t // tm),
        in_specs=[pl.BlockSpec((1, tm, B_W), lambda bi, i: (bi, i, P_POOL // B_W)),
                  pl.BlockSpec((1, POOL_HALO, B_W),
                               lambda bi, i: (bi, jnp.maximum(i * per - 1, 0), P_POOL // B_W)),
                  pl.BlockSpec((1, POOL_HALO, B_W), lambda bi, i: (bi, 0, 0)),
                  pl.BlockSpec((B_GROUPS, B_GW, B_GW), lambda bi, i: (0, 0, 0)),
                  pl.BlockSpec((1, B_W), lambda bi, i: (0, 0))],
        out_specs=pl.BlockSpec((1, tm, B_W), lambda bi, i: (bi, i, 0)),
        out_shape=jax.ShapeDtypeStruct((b, t, B_W), BRANCH_DTYPE),
        scratch_shapes=[pltpu.VMEM((tm + POOL_HALO, B_W), F32)],
        compiler_params=_cparams(("parallel", "arbitrary")),
        name="pool_mix",
    )(proj, proj, hist, pm, ps.reshape(1, B_W))


def _merge_kernel(x_ref, oa_ref, ob_ref, oc_ref, mg_ref, wa_ref, wb_ref, wc_ref, wo_ref, o_ref):
    gate = lambda n: _sigmoid(mg_ref[:, n * D_MODEL:(n + 1) * D_MODEL])
    h = (gate(0) * _dot(oa_ref[...], wa_ref[...]) + gate(1) * _dot(ob_ref[...], wb_ref[...])
         + gate(2) * _dot(oc_ref[...], wc_ref[...]))
    o_ref[...] = x_ref[...] + _dot(h, wo_ref[...])


def _merge(x, oa, ob, oc, gates, wa, wb, wc, wo, tm):
    m = x.shape[0]
    row = lambda n: pl.BlockSpec((tm, n), lambda i: (i, 0))
    full = lambda a: pl.BlockSpec(a.shape, lambda i: (0, 0))
    return pl.pallas_call(
        _merge_kernel,
        grid=(m // tm,),
        in_specs=[row(D_MODEL), row(A_W), row(B_W), row(C_W), row(3 * D_MODEL),
                  full(wa), full(wb), full(wc), full(wo)],
        out_specs=row(D_MODEL),
        out_shape=jax.ShapeDtypeStruct((m, D_MODEL), F32),
        compiler_params=_cparams(("parallel",)),
        name="merge_branches",
    )(x, oa, ob, oc, gates, wa, wb, wc, wo)


def _mem_kv_kernel(mem_ref, g_ref, wk_ref, wv_ref, kg_ref, k_ref, v_ref):
    mn = _rms(mem_ref[...], g_ref[...]).astype(BF16)
    k = jnp.dot(mn, wk_ref[...], preferred_element_type=F32)
    v_ref[...] = jnp.dot(mn, wv_ref[...], preferred_element_type=F32)
    k_ref[...] = jnp.concatenate(
        [_rms(k[:, h * M_HD:(h + 1) * M_HD], kg_ref[...]) for h in range(M_HEADS)], axis=1)


def _mem_kv(mem, g, wk, wv, kg):
    m = mem.shape[0]
    tm = N_MEM
    row = lambda n: pl.BlockSpec((tm, n), lambda i: (i, 0))
    full = lambda a: pl.BlockSpec(a.shape, lambda i: (0, 0))
    g2, kg2 = g.reshape(1, D_MODEL), kg.reshape(1, M_HD)
    return pl.pallas_call(
        _mem_kv_kernel,
        grid=(m // tm,),
        in_specs=[row(D_MODEL), full(g2), full(wk), full(wv), full(kg2)],
        out_specs=[row(M_W), row(M_W)],
        out_shape=[jax.ShapeDtypeStruct((m, M_W), F32), jax.ShapeDtypeStruct((m, M_W), F32)],
        compiler_params=_cparams(("parallel",)),
        name="mem_kv",
    )(mem, g2, wk, wv, kg2)


def _mem_ffn_kernel(x_ref, mk_ref, mv_ref, gx_ref, wq_ref, qg_ref, wo_ref, gf_ref, w1_ref, w2_ref,
                    o_ref, xn_ref, *, tm, rows_per_batch, nkb):
    j = pl.program_id(1)

    @pl.when(j == 0)
    def _():
        x = x_ref[...]
        q = _dot(_rms(x, gx_ref[...]), wq_ref[...])
        mk = mk_ref[...].reshape(nkb * N_MEM, M_W)
        mv = mv_ref[...].reshape(nkb * N_MEM, M_W)
        if nkb > 1:
            qb = lax.broadcasted_iota(jnp.int32, (tm, nkb * N_MEM), 0) // rows_per_batch
            kb = lax.broadcasted_iota(jnp.int32, (tm, nkb * N_MEM), 1) // N_MEM
            same = qb == kb
        outs = []
        for h in range(M_HEADS):
            sl = slice(h * M_HD, (h + 1) * M_HD)
            qh = _rms(q[:, sl], qg_ref[...])
            logits = _dot_nt(qh, mk[:, sl]) * (M_HD ** -0.5)
            if nkb > 1:
                logits = jnp.where(same, logits, NEG)
            mx = jnp.max(logits, axis=-1, keepdims=True)
            p = jnp.exp(logits - mx)
            p = p / jnp.sum(p, axis=-1, keepdims=True)
            outs.append(_dot(p, mv[:, sl]))
        o = jnp.concatenate(outs, axis=1)
        xm = x + _dot(o, wo_ref[...])
        o_ref[...] = xm
        xn_ref[...] = _rms(xm, gf_ref[...]).astype(BF16)

    h1 = jnp.dot(xn_ref[...], w1_ref[...], preferred_element_type=F32)
    h1 = jnp.square(jnp.maximum(h1, 0.0))
    o_ref[...] += _dot(h1, w2_ref[...])


def _mem_ffn(x, mk, mv, gx, wq, qg, wo, gf, w1, w2, tm, rows_per_batch, tf=1024):
    m = x.shape[0]
    nkb = max(tm // rows_per_batch, 1)
    kern = functools.partial(_mem_ffn_kernel, tm=tm, rows_per_batch=rows_per_batch, nkb=nkb)
    full = lambda a: pl.BlockSpec(a.shape, lambda i, j: (0,) * a.ndim)
    gx2, qg2, gf2 = gx.reshape(1, D_MODEL), qg.reshape(1, M_HD), gf.reshape(1, D_MODEL)
    mem_spec = pl.BlockSpec((nkb, N_MEM, M_W), lambda i, j: ((i * tm) // (rows_per_batch * nkb), 0, 0))
    return pl.pallas_call(
        kern,
        grid=(m // tm, D_FF // tf),
        in_specs=[pl.BlockSpec((tm, D_MODEL), lambda i, j: (i, 0)), mem_spec, mem_spec,
                  full(gx2), full(wq), full(qg2), full(wo), full(gf2),
                  pl.BlockSpec((D_MODEL, tf), lambda i, j: (0, j)),
                  pl.BlockSpec((tf, D_MODEL), lambda i, j: (j, 0))],
        out_specs=pl.BlockSpec((tm, D_MODEL), lambda i, j: (i, 0)),
        out_shape=jax.ShapeDtypeStruct((m, D_MODEL), F32),
        scratch_shapes=[pltpu.VMEM((tm, D_MODEL), BF16)],
        compiler_params=_cparams(("parallel", "arbitrary")),
        name="mem_ffn",
    )(x, mk, mv, gx2, wq, qg2, wo, gf2, w1, w2)


def _half_ones():
    i = lax.broadcasted_iota(jnp.int32, (LANES, LANES), 0) // C_HD
    j = lax.broadcasted_iota(jnp.int32, (LANES, LANES), 1) // C_HD
    return (i == j).astype(BF16)


def _rms_halves(x, g, ones):
    ms = _dot_exact_rhs(x * x, ones) * (1.0 / C_HD)
    return x * lax.rsqrt(ms + RMS_EPS) * g


def _nsa_prep_kernel(q_ref, c_ref, s_ref, w_ref, qg_ref, kg_ref,
                     qn_ref, ks16_ref, vs16_ref, kw16_ref, vw16_ref, kc_ref, vc_ref, rows_ref, *, tm):
    ones = _half_ones()
    scale = C_HD ** -0.5 * LOG2E
    q = q_ref[0]
    for n in range(C_HPG):
        qn_ref[0, n] = (_rms_halves(q[:, n * LANES:(n + 1) * LANES], qg_ref[...], ones) * scale).T.astype(BF16)
    c = c_ref[0]
    rows_ref[0, 0] = c[:, :LANES].T
    rows_ref[0, 1] = c[:, LANES:].T
    s = s_ref[0]
    ks = _rms_halves(s[:, :LANES], kg_ref[1:2, :], ones)
    vs_t = s[:, LANES:].T
    ks16_ref[0] = ks.astype(BF16)
    vs16_ref[0] = vs_t.astype(BF16)
    rows_ref[0, 2] = ks.T
    rows_ref[0, 3] = vs_t
    w = w_ref[0]
    kw = _rms_halves(w[:, :LANES], kg_ref[2:3, :], ones)
    vw_t = w[:, LANES:].T
    kw16_ref[0] = kw.astype(BF16)
    vw16_ref[0] = vw_t.astype(BF16)
    rows_ref[0, 4] = kw.T
    rows_ref[0, 5] = vw_t
    nb = tm // CMP_BLOCK
    bi = lax.broadcasted_iota(jnp.int32, (nb, tm), 0)
    ti = lax.broadcasted_iota(jnp.int32, (nb, tm), 1) // CMP_BLOCK
    pool = (bi == ti).astype(BF16)
    means = _dot_exact_lhs(pool, c) * (1.0 / CMP_BLOCK)
    kc_ref[0] = _rms_halves(means[:, :LANES], kg_ref[0:1, :], ones)
    vc_ref[0] = means[:, LANES:]


def _nsa_prep(proj, qg, kg, tm):
    b, t, _ = proj.shape
    nbt = tm // CMP_BLOCK
    kern = functools.partial(_nsa_prep_kernel, tm=tm)
    tok = lambda n, dt: jax.ShapeDtypeStruct((b, t, n), dt)
    blk = jax.ShapeDtypeStruct((b, t // CMP_BLOCK, LANES), F32)
    col = lambda width, off: pl.BlockSpec((1, tm, width), lambda bi, i: (bi, i, off // width))
    out_tok = lambda n: pl.BlockSpec((1, tm, n), lambda bi, i: (bi, i, 0))
    out_blk = pl.BlockSpec((1, nbt, LANES), lambda bi, i: (bi, i, 0))
    out_t = pl.BlockSpec((1, LANES, tm), lambda bi, i: (bi, 0, i))
    tok_t = jax.ShapeDtypeStruct((b, LANES, t), BF16)
    return pl.pallas_call(
        kern,
        grid=(b, t // tm),
        in_specs=[col(C_W, P_Q), col(2 * LANES, P_KV), col(2 * LANES, P_KV + 2 * LANES),
                  col(2 * LANES, P_KV + 4 * LANES),
                  pl.BlockSpec((1, LANES), lambda bi, i: (0, 0)),
                  pl.BlockSpec((3, LANES), lambda bi, i: (0, 0))],
        out_specs=[pl.BlockSpec((1, C_HPG, LANES, tm), lambda bi, i: (bi, 0, 0, i)),
                   out_tok(LANES), out_t, out_tok(LANES), out_t, out_blk, out_blk,
                   pl.BlockSpec((1, 6, LANES, tm), lambda bi, i: (bi, 0, 0, i))],
        out_shape=[jax.ShapeDtypeStruct((b, C_HPG, LANES, t), BF16),
                   tok(LANES, BF16), tok_t, tok(LANES, BF16), tok_t, blk, blk,
                   jax.ShapeDtypeStruct((b, 6, LANES, t), F32)],
        compiler_params=_cparams(("parallel", "parallel")),
        name="nsa_prep",
    )(proj, proj, proj, proj, jnp.tile(qg.reshape(1, C_HD), (1, 2)), jnp.tile(kg, (1, 2)))


M_INIT = -1e29
ROWS = C_HPG * Q_TILE
FAR_TILES = 16
LEFTOVER_STEPS = (FAR_TILES // 4, FAR_TILES // 2, FAR_TILES)
FRONT_TILES = max(WINDOW // Q_TILE, FAR_TILES)
SEL_PAD = 2 * FRONT_TILES
SUM_ROWS = 16
LOG2E = 1.4426950408889634


def _heads(x):
    return jnp.concatenate([x] * C_HPG, axis=1)


def _softmax_step(qts, k, v_t, state, tail_bias=None, valid=None, block_bias=None):
    m_ref, l_ref, acc_ref = state
    n_keys = k.shape[0]
    old = [(m_ref[g], l_ref[g], acc_ref[g]) for g in range(C_KV)]
    v_ext = jnp.concatenate([v_t, jnp.ones((SUM_ROWS, n_keys), BF16)], axis=0)
    if block_bias is not None:
        blk = lax.broadcasted_iota(jnp.int32, (n_keys, LANES), 0) // CMP_BLOCK
        col = lax.broadcasted_iota(jnp.int32, (n_keys, LANES), 1)
        k = jnp.concatenate([k, (blk == col).astype(BF16)], axis=1)
    logits = []
    for g in range(C_KV):
        rhs = qts[g] if block_bias is None else jnp.concatenate([qts[g], block_bias[g]], axis=0)
        s = jnp.dot(k, rhs, preferred_element_type=F32)
        if tail_bias is not None:
            cut = s.shape[0] - tail_bias[g].shape[0]
            tail = s[cut:] + tail_bias[g]
            s = jnp.concatenate([s[:cut], tail], axis=0) if cut else tail
        if valid is not None:
            s = jnp.where(_heads(valid[g]), s, NEG)
        logits.append(s.astype(BF16))
    weights = []
    for g in range(C_KV):
        m_old, s = old[g][0], logits[g]
        m_new = jnp.maximum(m_old, jnp.max(s, axis=0, keepdims=True).astype(F32))
        weights.append((m_new, jnp.exp2(m_old - m_new), jnp.exp2(s - m_new.astype(BF16))))
    for g in range(C_KV):
        _, l_old, acc_old = old[g]
        m_new, alpha, p = weights[g]
        pv = jnp.dot(v_ext, p, preferred_element_type=F32)
        m_ref[g] = m_new
        l_ref[g] = alpha * l_old + pv[LANES:LANES + 1]
        acc_ref[g] = alpha * acc_old + pv[:LANES]


def _softmax_reset(state):
    m_ref, l_ref, acc_ref = state
    m_ref[...] = jnp.full(m_ref.shape, M_INIT, F32)
    l_ref[...] = jnp.zeros(l_ref.shape, F32)
    acc_ref[...] = jnp.zeros(acc_ref.shape, F32)


def _softmax_result(state, g):
    _, l_ref, acc_ref = state
    l = l_ref[g]
    return acc_ref[g] / jnp.where(l > 0.0, l, 1.0)


def _top_blocks(score, n_sel):
    nb = score.shape[0]
    blk = lax.broadcasted_iota(jnp.int32, score.shape, 0).astype(F32)
    sel = jnp.zeros(score.shape, F32)
    work = score
    for _ in range(min(n_sel, nb)):
        mx = jnp.max(work, axis=0, keepdims=True)
        idx = jnp.min(jnp.where(work == mx, blk, float(nb)), axis=0, keepdims=True)
        pick = blk == idx
        sel = jnp.where(pick, 1.0, sel)
        work = jnp.where(pick, -jnp.inf, work)
    return sel


def _nsa_attn_kernel(qt_ref, ks_ref, vs_ref, kw_ref, vw_ref, kc_ref, vc_ref, cg_ref, bnear_ref,
                     bcmp_ref, o_ref, m_ref, l_ref, acc_ref, sel_ref, *, nb):
    i = pl.program_id(1)
    lane = lax.broadcasted_iota(jnp.int32, (Q_TILE, LANES), 1)
    key = lax.broadcasted_iota(jnp.int32, (Q_TILE, LANES), 0)
    causal = key <= lane
    sig_t = _sigmoid(cg_ref[0]).T
    kc = kc_ref[0].astype(BF16)
    vc_t = vc_ref[0].T.astype(BF16)
    blk = lax.broadcasted_iota(jnp.int32, (nb, Q_TILE), 0)
    qpos = i * Q_TILE + lax.broadcasted_iota(jnp.int32, (nb, Q_TILE), 1)
    cur = qpos // CMP_BLOCK
    cmp_valid = _heads(qpos >= blk * CMP_BLOCK + (CMP_BLOCK - 1))
    forced = (blk == 0) | (blk == cur) | (blk == cur - 1)
    state = (m_ref, l_ref, acc_ref)
    n_back = WINDOW // Q_TILE
    k_rows = lambda ref, j, n: ref[0, pl.ds(pl.multiple_of((j + FRONT_TILES) * Q_TILE, Q_TILE), n * Q_TILE), :]
    v_cols = lambda ref, j, n: ref[0, :, pl.ds(pl.multiple_of((j + FRONT_TILES) * Q_TILE, Q_TILE), n * Q_TILE)]

    def key_mask(g, j, n):
        rows = [sel_ref[g, pl.ds(SEL_PAD + 2 * j + r, 1), :] for r in range(2 * n)]
        return jnp.concatenate(
            [jnp.where(key < CMP_BLOCK, rows[2 * t], rows[2 * t + 1]) for t in range(n)], axis=0) > 0.5

    def block_bias(g, j, n, tiles=None):
        rows = sel_ref[g, pl.ds(SEL_PAD + 2 * j, 2 * n), :]
        if tiles is not None:
            t = j + lax.broadcasted_iota(jnp.int32, (2 * n, Q_TILE), 0) // 2
            rows = jnp.where((t >= tiles[0]) & (t < tiles[1]), rows, 0.0)
        bias = _heads((rows - 1.0) * (-NEG))
        return jnp.concatenate([bias, jnp.zeros((LANES - 2 * n, ROWS), F32)], axis=0).astype(BF16)

    q_t = jnp.concatenate([qt_ref[0, n] for n in range(C_HPG)], axis=1)
    feat = lax.broadcasted_iota(jnp.int32, (LANES, ROWS), 0) // C_HD
    qts = [jnp.where(feat == g, q_t, jnp.zeros_like(q_t)) for g in range(C_KV)]

    o_cmp, scores = [], []
    for g in range(C_KV):
        bias_c = pltpu.roll(bcmp_ref[g], (2 * i + 2) % LANES, 0)[:nb, :]
        lc = jnp.dot(kc, qts[g], preferred_element_type=F32) + bias_c
        lc = jnp.where(cmp_valid, lc, NEG)
        mc = jnp.maximum(jnp.max(lc, axis=0, keepdims=True), M_INIT)
        pc = jnp.exp2(lc - mc)
        den = jnp.sum(pc, axis=0, keepdims=True)
        pc = pc / jnp.where(den > 0.0, den, 1.0)
        o_cmp.append(jnp.dot(vc_t, pc.astype(BF16), preferred_element_type=F32))
        psum = sum(pc[:, n * Q_TILE:(n + 1) * Q_TILE] for n in range(C_HPG))
        scores.append(jnp.where((blk > cur) | forced, NEG, psum))

    score = jnp.concatenate(scores, axis=1)
    sel = jnp.where(score > 0.5 * NEG, _top_blocks(score, N_SEL - N_FORCED), 0.0)
    sel = jnp.where(jnp.concatenate([forced] * C_KV, axis=1), 1.0, sel)
    for g in range(C_KV):
        sel_ref[g, 0:SEL_PAD, :] = jnp.zeros((SEL_PAD, Q_TILE), F32)
        sel_ref[g, SEL_PAD:SEL_PAD + nb, :] = sel[:, g * Q_TILE:(g + 1) * Q_TILE]

    _softmax_reset(state)
    n_far = jnp.maximum(i - 1, 0)

    def far_step(j, n=FAR_TILES, tiles=None):
        _softmax_step(qts, k_rows(ks_ref, j, n), v_cols(vs_ref, j, n), state,
                      block_bias=[block_bias(g, j, n, tiles) for g in range(C_KV)])

    def far_many(jq, carry):
        far_step(FAR_TILES * jq)
        return carry

    n_many = n_far // FAR_TILES
    lax.fori_loop(0, n_many, far_many, 0)

    n_left = n_far % FAR_TILES
    left = (FAR_TILES * n_many, n_far)
    lo = 0
    for size in LEFTOVER_STEPS:
        @pl.when((n_left > lo) & (n_left <= size))
        def _(size=size):
            far_step(n_far - size, size, left)
        lo = size

    near_causal = jnp.concatenate([key >= 0, causal], axis=0)
    bias2 = [bnear_ref[g] for g in range(C_KV)]
    _softmax_step(qts, k_rows(ks_ref, i - 1, 2), v_cols(vs_ref, i - 1, 2), state, tail_bias=bias2,
                  valid=[key_mask(g, i - 1, 2) & near_causal for g in range(C_KV)])
    o_slc = [_softmax_result(state, g) for g in range(C_KV)]

    _softmax_reset(state)
    exists = lambda back: key >= jnp.where(i >= back, 0, Q_TILE)
    win_valid = jnp.concatenate(
        [(key > lane) & exists(n_back)] + [exists(back) for back in range(n_back - 1, 0, -1)] + [causal],
        axis=0)
    _softmax_step(qts, k_rows(kw_ref, i - n_back, n_back + 1), v_cols(vw_ref, i - n_back, n_back + 1), state,
                  tail_bias=bias2, valid=[win_valid] * C_KV)
    o_win = [_softmax_result(state, g) for g in range(C_KV)]

    low = (key // C_HD) == 0
    for n in range(C_HPG):
        cols = slice(n * Q_TILE, (n + 1) * Q_TILE)
        per_group = []
        for g in range(C_KV):
            h = g * C_HPG + n
            gate = lambda br: sig_t[br * C_HEADS + h:br * C_HEADS + h + 1, :]
            per_group.append(gate(0) * o_cmp[g][:, cols] + gate(1) * o_slc[g][:, cols]
                             + gate(2) * o_win[g][:, cols])
        o_ref[0, :, n * LANES:(n + 1) * LANES] = jnp.where(low, per_group[0], per_group[1]).T.astype(o_ref.dtype)


def _nsa_attn(qt, ks16, vs16t, kw16, vw16t, kc, vc, gates, bnear, bcmp):
    b, _, _, t = qt.shape
    nb = t // CMP_BLOCK
    tp = t + FRONT_TILES * Q_TILE
    kern = functools.partial(_nsa_attn_kernel, nb=nb)
    seq = pl.BlockSpec((1, tp, LANES), lambda bi, i: (bi, 0, 0))
    seq_t = pl.BlockSpec((1, LANES, tp), lambda bi, i: (bi, 0, 0))
    blk = pl.BlockSpec((1, nb, LANES), lambda bi, i: (bi, 0, 0))
    return pl.pallas_call(
        kern,
        grid=(b, t // Q_TILE),
        in_specs=[pl.BlockSpec((1, C_HPG, LANES, Q_TILE), lambda bi, i: (bi, 0, 0, i)),
                  seq, seq_t, seq, seq_t, blk, blk,
                  pl.BlockSpec((1, Q_TILE, LANES), lambda bi, i: (bi, i, G_CG // LANES)),
                  pl.BlockSpec(bnear.shape, lambda bi, i: (0, 0, 0)),
                  pl.BlockSpec(bcmp.shape, lambda bi, i: (0, 0, 0))],
        out_specs=pl.BlockSpec((1, Q_TILE, C_W), lambda bi, i: (bi, i, 0)),
        out_shape=jax.ShapeDtypeStruct((b, t, C_W), BRANCH_DTYPE),
        scratch_shapes=[pltpu.VMEM((C_KV, 1, ROWS), F32), pltpu.VMEM((C_KV, 1, ROWS), F32),
                        pltpu.VMEM((C_KV, LANES, ROWS), F32),
                        pltpu.VMEM((C_KV, SEL_PAD + nb, Q_TILE), F32)],
        compiler_params=_cparams(("parallel", "arbitrary")),
        name="nsa_attn",
    )(qt, ks16, vs16t, kw16, vw16t, kc, vc, gates, bnear, bcmp)


def _rel_bucket(dist):
    n = jnp.maximum(dist, 0)
    exact = N_BUCKETS // 2
    ratio = jnp.log(jnp.maximum(n, 1).astype(F32) / exact) / math.log(MAX_DIST / exact)
    large = jnp.minimum(exact + (ratio * (N_BUCKETS - exact)).astype(jnp.int32), N_BUCKETS - 1)
    return jnp.where(n < exact, n, large)


def _one_hot(idx, n):
    return (jnp.clip(idx, 0, n - 1)[..., None] == jnp.arange(n)).astype(F32)


def _distance_table(rel_bias):
    tbl = jnp.dot(_one_hot(_rel_bucket(jnp.arange(MAX_DIST)), N_BUCKETS), rel_bias,
                  precision=lax.Precision.HIGHEST)
    return (tbl - tbl[MAX_DIST - 1:MAX_DIST, :]).T


def _bias_tables(rel_bias):
    tbl = _distance_table(rel_bias) * LOG2E
    tbl = tbl.reshape(C_KV, C_HPG, MAX_DIST)
    qq = jnp.arange(Q_TILE)[:, None]
    cc = jnp.arange(LANES)[None, :]
    look = lambda dist: jnp.einsum('ghd,qcd->ghqc', tbl, _one_hot(dist, MAX_DIST),
                                   precision=lax.Precision.HIGHEST).reshape(C_KV, ROWS, LANES)
    bnear = jnp.stack([look(qq - cc), look(qq - cc + Q_TILE)], axis=1)
    bcmp = look(qq - CMP_BLOCK * cc + CMP_BLOCK * (LANES - 1) - (LANES - 1))
    bnear = bnear.transpose(0, 1, 3, 2)
    return jnp.concatenate([bnear[:, 1], bnear[:, 0]], axis=1), bcmp.transpose(0, 2, 1)


PAGES_PER_STEP = 16
S_ROWS = C_KV * C_HPG
BLK_LANES = 384
NEW_PAD = 8


def _page_specs(layer, n):
    spec = lambda j: pl.BlockSpec((1, 1, LANES, PAGE_SIZE),
                                  lambda b, c, pt, j=j: (layer, pt[b, c * n + j], 0, 0))
    return [spec(j) for j in range(n)]


def _cmp_pages_kernel(pt_ref, *refs, n):
    k_refs, v_refs, (kc_ref, vc_ref) = refs[:n], refs[n:2 * n], refs[2 * n:]
    row = lax.broadcasted_iota(jnp.int32, (n * PAGE_SIZE, 2 * n), 0) // CMP_BLOCK
    col = lax.broadcasted_iota(jnp.int32, (n * PAGE_SIZE, 2 * n), 1)
    pool = (row == col).astype(BF16)
    for refs_in, out in ((k_refs, kc_ref), (v_refs, vc_ref)):
        pages = jnp.concatenate([r[0, 0] for r in refs_in], axis=1)
        out[0, 0] = _dot_exact_rhs(pages, pool) * (1.0 / CMP_BLOCK)


def _cmp_pages(page_table, cache_k, cache_v, layer):
    b, n_pages = page_table.shape
    n = min(PAGES_PER_STEP, n_pages)
    chunks = n_pages // n
    out = jax.ShapeDtypeStruct((b, chunks, LANES, 2 * n), F32)
    out_spec = pl.BlockSpec((1, 1, LANES, 2 * n), lambda bi, c, pt: (bi, c, 0, 0))
    return pl.pallas_call(
        functools.partial(_cmp_pages_kernel, n=n),
        grid_spec=pltpu.PrefetchScalarGridSpec(
            num_scalar_prefetch=1, grid=(b, chunks),
            in_specs=_page_specs(layer, n) + _page_specs(layer, n),
            out_specs=[out_spec, out_spec]),
        out_shape=[out, out],
        compiler_params=_cparams(("parallel", "arbitrary")),
        name="nsa_sample_cmp_pages",
    )(page_table, *([cache_k] * n), *([cache_v] * n))


def _top_blocks_lanes(score, n_sel):
    nb = score.shape[-1]
    blk = lax.broadcasted_iota(jnp.int32, score.shape, 1).astype(F32)
    sel = jnp.zeros(score.shape, F32)
    work = score
    for _ in range(min(n_sel, nb)):
        mx = jnp.max(work, axis=-1, keepdims=True)
        idx = jnp.min(jnp.where(work == mx, blk, float(nb)), axis=-1, keepdims=True)
        pick = blk == idx
        sel = jnp.where(pick, 1.0, sel)
        work = jnp.where(pick, -jnp.inf, work)
    return sel


def _nsa_sample_head_kernel(q_ref, newc_ref, news_ref, neww_ref, kc_ref, vc_ref, wk_ref, wv_ref,
                            qg_ref, kg_ref, kgc_ref, bc_ref, bw_ref, bn_ref,
                            q32_ref, sel_ref, ocmp_ref, owin_ref, new_ref, q_scr, *, ts, past, wbuf):
    n_rows = S_ROWS * ts
    ones = _half_ones()
    scale = C_HD ** -0.5
    lane = lax.broadcasted_iota(jnp.int32, (ts, LANES), 1)
    q = q_ref[0]
    for n in range(C_HPG):
        qn = _rms_halves(q[:, n * LANES:(n + 1) * LANES], qg_ref[...], ones) * scale
        for g in range(C_KV):
            lo = (g * C_HPG + n) * ts
            q_scr[lo:lo + ts, :] = jnp.where((lane // C_HD) == g, qn, 0.0)
    q32 = q_scr[...].astype(BF16)
    q32_ref[0] = q32

    newc, news, neww = newc_ref[0], news_ref[0], neww_ref[0]
    ks_new = _rms_halves(news[:, :LANES], kg_ref[1:2, :], ones)
    kw_new = _rms_halves(neww[:, :LANES], kg_ref[2:3, :], ones)
    new_ref[0, 0] = newc[:, :LANES]
    new_ref[0, 1] = newc[:, LANES:]
    new_ref[0, 2] = ks_new
    new_ref[0, 3] = news[:, LANES:]
    new_ref[0, 4] = kw_new
    new_ref[0, 5] = neww[:, LANES:]

    t_of_row = lax.broadcasted_iota(jnp.int32, (n_rows, 1), 0) % ts
    qpos = past + t_of_row

    n_cached = past // CMP_BLOCK
    pad_rows = jnp.zeros((NEW_PAD - ts, LANES), F32)
    ones_rows = jnp.full((NEW_PAD, BLK_LANES), 1.0, BF16)
    new_mean = lambda rows: _dot_tn_exact(jnp.concatenate([rows, pad_rows], axis=0),
                                          ones_rows) * (1.0 / CMP_BLOCK)
    is_new = lax.broadcasted_iota(jnp.int32, (LANES, BLK_LANES), 1) == n_cached
    kc_all = jnp.where(is_new, new_mean(newc[:, :LANES]), kc_ref[0])
    vc_all = jnp.where(is_new, new_mean(newc[:, LANES:]), vc_ref[0])
    feat = lax.broadcasted_iota(jnp.int32, (LANES, 1), 0) // C_HD
    sq = kc_all * kc_all
    ms = jnp.where(feat == 0, jnp.sum(sq[:C_HD], axis=0, keepdims=True),
                   jnp.sum(sq[C_HD:], axis=0, keepdims=True)) * (1.0 / C_HD)
    kc_n = kc_all * lax.rsqrt(ms + RMS_EPS) * kgc_ref[:, 0:1]
    blk = lax.broadcasted_iota(jnp.int32, (n_rows, BLK_LANES), 1)
    lc = jnp.dot(q32, kc_n.astype(BF16), preferred_element_type=F32) + bc_ref[...]
    lc = jnp.where(qpos >= blk * CMP_BLOCK + (CMP_BLOCK - 1), lc, NEG)
    mc = jnp.maximum(jnp.max(lc, axis=-1, keepdims=True), M_INIT)
    pc = jnp.exp(lc - mc)
    den = jnp.sum(pc, axis=-1, keepdims=True)
    pc = pc / jnp.where(den > 0.0, den, 1.0)
    ocmp_ref[0] = lax.dot_general(pc.astype(BF16), vc_all.astype(BF16), (((1,), (1,)), ((), ())),
                                  preferred_element_type=F32)

    gi = lax.broadcasted_iota(jnp.int32, (C_KV * ts, n_rows), 0)
    ri = lax.broadcasted_iota(jnp.int32, (C_KV * ts, n_rows), 1)
    same = ((gi // ts) == (ri // (C_HPG * ts))) & ((gi % ts) == (ri % ts))
    psum = _dot_exact_lhs(same.astype(BF16), pc, terms=3)
    sblk = lax.broadcasted_iota(jnp.int32, (C_KV * ts, BLK_LANES), 1)
    scur = (past + lax.broadcasted_iota(jnp.int32, (C_KV * ts, 1), 0) % ts) // CMP_BLOCK
    forced = (sblk == 0) | (sblk == scur) | (sblk == scur - 1)
    score = jnp.where((sblk > scur) | forced, NEG, psum)
    sel = jnp.where(score > 0.5 * NEG, _top_blocks_lanes(score, N_SEL - N_FORCED), 0.0)
    sel = jnp.where(forced, 1.0, sel)
    sel_ref[0] = lax.dot_general(same.astype(BF16), sel.astype(BF16), (((0,), (0,)), ((), ())),
                                 preferred_element_type=F32)

    key = lax.broadcasted_iota(jnp.int32, (n_rows, wbuf), 1)
    lw = jnp.dot(q32, wk_ref[0, 0].astype(BF16), preferred_element_type=F32) + bw_ref[...]
    kpos = past - wbuf + key
    dist = qpos - kpos
    lw = jnp.where((dist >= 0) & (dist < WINDOW) & (kpos >= 0), lw, NEG)
    kw_pad = jnp.concatenate([kw_new, pad_rows], axis=0)
    vw_pad = jnp.concatenate([neww[:, LANES:], pad_rows], axis=0)
    nkey = lax.broadcasted_iota(jnp.int32, (n_rows, NEW_PAD), 1)
    ln = lax.dot_general(q32, kw_pad.astype(BF16), (((1,), (1,)), ((), ())),
                         preferred_element_type=F32) + bn_ref[...]
    ln = jnp.where((nkey <= t_of_row) & (nkey < ts), ln, NEG)
    mw = jnp.maximum(jnp.max(lw, axis=-1, keepdims=True), jnp.max(ln, axis=-1, keepdims=True))
    pw, pn = jnp.exp(lw - mw), jnp.exp(ln - mw)
    den = jnp.sum(pw, axis=-1, keepdims=True) + jnp.sum(pn, axis=-1, keepdims=True)
    ow = lax.dot_general(pw.astype(BF16), wv_ref[0, 0].astype(BF16), (((1,), (1,)), ((), ())),
                         preferred_element_type=F32) + _dot(pn, vw_pad)
    owin_ref[0] = ow / den


def _dot_tn_exact(a, b01, terms=2):
    dims = (((0,), (0,)), ((), ()))
    return sum(lax.dot_general(p, b01, dims, preferred_element_type=F32) for p in _split(a, terms))


def _nsa_sample_head(main, kc_t, vc_t, win_k, win_v, layer, qg, kg, tables, past):
    b, ts, _ = main.shape
    wbuf = win_k.shape[-1]
    n_rows = S_ROWS * ts
    bias_c, bias_w, bias_n = tables
    kern = functools.partial(_nsa_sample_head_kernel, ts=ts, past=past, wbuf=wbuf)
    col = lambda width, off: pl.BlockSpec((1, ts, width), lambda i: (i, 0, off // width))
    full = lambda a: pl.BlockSpec(a.shape, lambda i: (0,) * a.ndim)
    per_b = lambda a: pl.BlockSpec((1,) + a.shape[1:], lambda i: (i,) + (0,) * (a.ndim - 1))
    win = pl.BlockSpec((1, 1, LANES, wbuf), lambda i: (layer, i, 0, 0))
    qg2 = jnp.tile(qg.reshape(1, C_HD), (1, 2))
    kg2 = jnp.tile(kg, (1, 2))
    kg_col = jnp.tile(kg2[0].reshape(LANES, 1), (1, LANES))
    rows = lambda n, dt: jax.ShapeDtypeStruct((b, n_rows, n), dt)
    out_rows = lambda n: pl.BlockSpec((1, n_rows, n), lambda i: (i, 0, 0))
    return pl.pallas_call(
        kern,
        grid=(b,),
        in_specs=[col(C_W, P_Q), col(2 * LANES, P_KV), col(2 * LANES, P_KV + 2 * LANES),
                  col(2 * LANES, P_KV + 4 * LANES), per_b(kc_t), per_b(vc_t), win, win,
                  full(qg2), full(kg2), full(kg_col), full(bias_c), full(bias_w), full(bias_n)],
        out_specs=[out_rows(LANES), out_rows(BLK_LANES), out_rows(LANES), out_rows(LANES),
                   pl.BlockSpec((1, 6, ts, LANES), lambda i: (i, 0, 0, 0))],
        out_shape=[rows(LANES, BF16), rows(BLK_LANES, F32), rows(LANES, F32), rows(LANES, F32),
                   jax.ShapeDtypeStruct((b, 6, ts, LANES), F32)],
        scratch_shapes=[pltpu.VMEM((n_rows, LANES), F32)],
        compiler_params=_cparams(("parallel",)),
        name="nsa_sample_head",
    )(main, main, main, main, kc_t, vc_t, win_k, win_v, qg2, kg2, kg_col, bias_c, bias_w, bias_n)


def _slc_pages_kernel(pt_ref, *refs, n, ts, last_chunk):
    k_refs, v_refs = refs[:n], refs[n:2 * n]
    (q_ref, sel_ref, ocmp_ref, owin_ref, gate_ref, new_ref, blast_ref, bn_ref,
     o_ref, m_ref, l_ref, acc_ref) = refs[2 * n:]
    c = pl.program_id(1)
    n_rows = S_ROWS * ts
    nt = (((1,), (1,)), ((), ()))

    @pl.when(c == 0)
    def _():
        m_ref[...] = jnp.full(m_ref.shape, M_INIT, F32)
        l_ref[...] = jnp.zeros(l_ref.shape, F32)
        acc_ref[...] = jnp.zeros(acc_ref.shape, F32)

    q32 = q_ref[0]
    keys = jnp.concatenate([r[0, 0] for r in k_refs], axis=1).astype(BF16)
    vals = jnp.concatenate([r[0, 0] for r in v_refs], axis=1).astype(BF16)
    s = jnp.dot(q32, keys, preferred_element_type=F32)
    cut = (n - 1) * PAGE_SIZE
    tail = s[:, cut:] + jnp.where(c == last_chunk, blast_ref[...], 0.0)
    s = jnp.concatenate([s[:, :cut], tail], axis=1) if cut else tail
    row = lax.broadcasted_iota(jnp.int32, (LANES, n * PAGE_SIZE), 0)
    col = lax.broadcasted_iota(jnp.int32, (LANES, n * PAGE_SIZE), 1) // CMP_BLOCK
    expand = (row == col).astype(BF16)
    picked = jnp.dot(sel_ref[0, 0].astype(BF16), expand, preferred_element_type=F32) > 0.5
    s = jnp.where(picked, s, NEG)
    m_old = m_ref[...]
    m_new = jnp.maximum(m_old, jnp.max(s, axis=-1, keepdims=True))
    alpha = jnp.exp(m_old - m_new)
    p = jnp.exp(s - m_new)
    l_ref[...] = alpha * l_ref[...] + jnp.sum(p, axis=-1, keepdims=True)
    acc_ref[...] = alpha * acc_ref[...] + lax.dot_general(p.astype(BF16), vals, nt,
                                                          preferred_element_type=F32)
    m_ref[...] = m_new

    @pl.when(c == last_chunk)
    def _():
        t_of_row = lax.broadcasted_iota(jnp.int32, (n_rows, 1), 0) % ts
        pad_rows = jnp.zeros((NEW_PAD - ts, LANES), F32)
        k_new = jnp.concatenate([new_ref[0, 2], pad_rows], axis=0)
        v_new = jnp.concatenate([new_ref[0, 3], pad_rows], axis=0)
        nkey = lax.broadcasted_iota(jnp.int32, (n_rows, NEW_PAD), 1)
        sn = lax.dot_general(q32, k_new.astype(BF16), nt, preferred_element_type=F32) + bn_ref[...]
        sn = jnp.where((nkey <= t_of_row) & (nkey < ts), sn, NEG)
        m_prev = m_ref[...]
        m_fin = jnp.maximum(m_prev, jnp.max(sn, axis=-1, keepdims=True))
        a_fin = jnp.exp(m_prev - m_fin)
        pn = jnp.exp(sn - m_fin)
        l_fin = a_fin * l_ref[...] + jnp.sum(pn, axis=-1, keepdims=True)
        o_slc = (a_fin * acc_ref[...] + _dot(pn, v_new)) / l_fin

        ri = lax.broadcasted_iota(jnp.int32, (n_rows, NEW_PAD), 0) % ts
        ti = lax.broadcasted_iota(jnp.int32, (n_rows, NEW_PAD), 1)
        gates = jnp.concatenate([_sigmoid(gate_ref[0]), pad_rows], axis=0)
        per_row = _dot_exact_lhs((ri == ti).astype(BF16), gates)
        head = lax.broadcasted_iota(jnp.int32, (n_rows, LANES), 0) // ts
        glane = lax.broadcasted_iota(jnp.int32, (n_rows, LANES), 1)
        gate = lambda br: jnp.sum(jnp.where(glane == br * C_HEADS + head, per_row, 0.0), axis=-1,
                                  keepdims=True)
        comb = gate(0) * ocmp_ref[0] + gate(1) * o_slc + gate(2) * owin_ref[0]
        low = (lax.broadcasted_iota(jnp.int32, (ts, LANES), 1) // C_HD) == 0
        for n_ in range(C_HPG):
            top = comb[n_ * ts:(n_ + 1) * ts]
            bot = comb[(C_HPG + n_) * ts:(C_HPG + n_ + 1) * ts]
            o_ref[0, :, n_ * LANES:(n_ + 1) * LANES] = jnp.where(low, top, bot)


def _slc_pages(page_table, cache_k, cache_v, layer, q32, sel_chunks, o_cmp, o_win, gates, new_rows,
               bias_last, bias_n):
    b, n_pages = page_table.shape
    n = min(PAGES_PER_STEP, n_pages)
    chunks = n_pages // n
    ts = new_rows.shape[2]
    n_rows = S_ROWS * ts
    kern = functools.partial(_slc_pages_kernel, n=n, ts=ts, last_chunk=chunks - 1)
    per_b = lambda a: pl.BlockSpec((1,) + a.shape[1:], lambda bi, c, pt: (bi,) + (0,) * (a.ndim - 1))
    full = lambda a: pl.BlockSpec(a.shape, lambda bi, c, pt: (0,) * a.ndim)
    return pl.pallas_call(
        kern,
        grid_spec=pltpu.PrefetchScalarGridSpec(
            num_scalar_prefetch=1, grid=(b, chunks),
            in_specs=_page_specs(layer, n) + _page_specs(layer, n) + [
                per_b(q32), pl.BlockSpec((1, 1, n_rows, LANES), lambda bi, c, pt: (bi, c, 0, 0)),
                per_b(o_cmp), per_b(o_win),
                pl.BlockSpec((1, ts, LANES), lambda bi, c, pt: (bi, 0, G_CG // LANES)),
                per_b(new_rows), full(bias_last), full(bias_n)],
            out_specs=pl.BlockSpec((1, ts, C_W), lambda bi, c, pt: (bi, 0, 0)),
            scratch_shapes=[pltpu.VMEM((n_rows, 1), F32), pltpu.VMEM((n_rows, 1), F32),
                            pltpu.VMEM((n_rows, LANES), F32)]),
        out_shape=jax.ShapeDtypeStruct((b, ts, C_W), F32),
        compiler_params=_cparams(("parallel", "arbitrary")),
        name="nsa_sample_slc_pages",
    )(page_table, *([cache_k] * n), *([cache_v] * n), q32, sel_chunks, o_cmp, o_win, gates, new_rows,
      bias_last, bias_n)


def _sample_bias_tables(rel_bias, past, ts, wbuf):
    tbl = jnp.repeat(_distance_table(rel_bias), ts, axis=0)
    t = jnp.tile(jnp.arange(ts), S_ROWS)[:, None]
    look = lambda dist: jnp.einsum('rd,rcd->rc', tbl, _one_hot(dist, MAX_DIST),
                                   precision=lax.Precision.HIGHEST)
    blk = jnp.arange(BLK_LANES)[None, :]
    bias_c = look(past + t - CMP_BLOCK * blk - (CMP_BLOCK - 1))
    bias_w = look(wbuf + t - jnp.arange(wbuf)[None, :])
    bias_n = look(t - jnp.arange(NEW_PAD)[None, :])
    bias_last = look(PAGE_SIZE + t - jnp.arange(PAGE_SIZE)[None, :])
    return bias_c, bias_w, bias_n, bias_last


def _nsa_sample(main, gate, caches, win_k, win_v, page_table, layer, qg, kg, rel_bias):
    b, ts, _ = main.shape
    n_pages = page_table.shape[1]
    past = n_pages * PAGE_SIZE
    wbuf = win_k.shape[-1]
    n_rows = S_ROWS * ts
    bias_c, bias_w, bias_n, bias_last = _sample_bias_tables(rel_bias, past, ts, wbuf)
    kc_chunks, vc_chunks = _cmp_pages(page_table, caches[0], caches[1], layer)
    n_cached = past // CMP_BLOCK
    unchunk = lambda z: z.transpose(0, 2, 1, 3).reshape(b, LANES, n_cached)
    fit = lambda z: jnp.pad(z, ((0, 0), (0, 0), (0, BLK_LANES - n_cached)))
    q32, sel, o_cmp, o_win, new_rows = _nsa_sample_head(
        main, fit(unchunk(kc_chunks)), fit(unchunk(vc_chunks)), win_k, win_v, layer, qg, kg,
        (bias_c, bias_w, bias_n), past)
    per_step = 2 * min(PAGES_PER_STEP, n_pages)
    sel_chunks = sel[:, :, :n_cached].reshape(b, n_rows, n_cached // per_step, per_step)
    sel_chunks = jnp.pad(sel_chunks.transpose(0, 2, 1, 3), ((0, 0), (0, 0), (0, 0), (0, LANES - per_step)))
    o = _slc_pages(page_table, caches[2], caches[3], layer, q32, sel_chunks, o_cmp, o_win,
                   gate.reshape(b, ts, N_GATE), new_rows, bias_last, bias_n)
    return o.reshape(b * ts, C_W), new_rows


def _shift_order(z):
    part = lambda off, n: z[..., off:off + n]
    return jnp.concatenate([part(OFF_R, A_W), part(OFF_K, A_W), part(OFF_V, A_W), part(OFF_WL, A_LORA_W),
                            part(OFF_AL, A_LORA_A), part(OFF_GL, A_LORA_G)], axis=-1)


def _layer_params(l, w_in, shift_mu, w_br_a, w_br_b, w_br_c, w_out, w_mq, w_mk, w_mv, w_mo, w_ff1, w_ff2):
    w = w_in[l]
    cast = lambda a: a.astype(BF16)
    cols = lambda off, n: cast(w[:, off:off + n])
    w_q = cols(OFF_Q, C_W).reshape(D_MODEL, C_KV, C_HPG, C_HD).transpose(0, 2, 1, 3).reshape(D_MODEL, C_W)
    w_main = jnp.concatenate([
        cols(OFF_R, A_W), cols(OFF_K, A_W), cols(OFF_V, A_W), cols(OFF_POOL, B_W), w_q,
        cols(OFF_KV, 6 * C_KVW), cols(OFF_WL, A_LORA_W), cols(OFF_AL, A_LORA_A), cols(OFF_GL, A_LORA_G)],
        axis=1)
    w_gate = jnp.concatenate([cols(OFF_MG, 3 * D_MODEL), cols(OFF_CG, 3 * C_HEADS),
                              jnp.zeros((D_MODEL, N_GATE - G_CG - 3 * C_HEADS), BF16)], axis=1)
    wc = w_br_c[l].reshape(C_KV, C_HPG, C_HD, D_MODEL).transpose(1, 0, 2, 3).reshape(C_W, D_MODEL)
    return dict(
        w_main=w_main, w_gate=w_gate, mu=_shift_order(shift_mu[l].reshape(1, -1)),
        wa=cast(w_br_a[l]), wb=cast(w_br_b[l]), wc=cast(wc), wo=cast(w_out[l]),
        wq=cast(w_mq[l]), wk=cast(w_mk[l]), wv=cast(w_mv[l]), wmo=cast(w_mo[l]),
        w1=cast(w_ff1[l]), w2=cast(w_ff2[l]))


def _shift_row(main_row):
    lora = main_row[:, P_LORA:]
    return jnp.concatenate([
        main_row[:, P_R:P_R + A_W], lora[:, :A_LORA_W], main_row[:, P_K:P_K + A_W],
        main_row[:, P_V:P_V + A_W], lora[:, A_LORA_W:A_LORA_W + A_LORA_A],
        lora[:, A_LORA_W + A_LORA_A:]], axis=1)


def _feature_major(z):
    z = jnp.moveaxis(z, -3, -1)
    return z.reshape(z.shape[:-3] + (C_KVW, z.shape[-1]))


def _token_major(z):
    b, _, rows = z.shape
    return z.reshape(b, C_KV, C_HD, rows).transpose(0, 3, 1, 2)


def kernel(x_prompt, x_sample, state_rwkv, state_shift, state_pool, cache_cmp_k, cache_cmp_v, cache_slc_k, cache_slc_v, cache_win_k, cache_win_v, cache_mem_k, cache_mem_v, page_table, mem_prompt, rel_bias, norm_mix_g, w_in, shift_mu, rwkv_w0, rwkv_w2, rwkv_a0, rwkv_a2, rwkv_g2, rwkv_kk, rwkv_ka, rwkv_rk, rwkv_lnx_g, rwkv_lnx_b, pool_w, pool_scale, nsa_q_g, nsa_k_g, w_br_a, w_br_b, w_br_c, w_out, norm_memx_g, norm_mem_g, w_mq, w_mk, w_mv, mem_q_g, mem_k_g, w_mo, norm_ffn_g, w_ff1, w_ff2):
    bp, t = x_prompt.shape[:2]
    bs, ts = x_sample.shape[:2]
    past = page_table.shape[1] * PAGE_SIZE
    wbuf_p = min(WINDOW, t)
    ts_pad = 16
    bnear, bcmp = _bias_tables(rel_bias)
    caches = [_feature_major(z) for z in (cache_cmp_k, cache_cmp_v, cache_slc_k, cache_slc_v)]
    win_k, win_v = _feature_major(cache_win_k), _feature_major(cache_win_v)
    xp = x_prompt.reshape(bp * t, D_MODEL)
    xs = x_sample.reshape(bs * ts, D_MODEL)
    outs_p = [[] for _ in range(11)]
    outs_s = [[] for _ in range(9)]
    row = lambda a: a.reshape(1, -1)
    for l in range(DEPTH):
        prm = _layer_params(l, w_in, shift_mu, w_br_a, w_br_b, w_br_c, w_out, w_mq, w_mk, w_mv, w_mo,
                            w_ff1, w_ff2)
        rw = (prm["mu"], row(rwkv_w0[l]), rwkv_w2[l].astype(BF16), row(rwkv_a0[l]), rwkv_a2[l].astype(BF16),
              rwkv_g2[l].astype(BF16), row(rwkv_kk[l]), row(rwkv_ka[l]), row(rwkv_rk[l]),
              row(rwkv_lnx_g[l]), row(rwkv_lnx_b[l]))
        pool_wl = pool_w[l].astype(BF16)

        main = _norm_matmul(xp, norm_mix_g[l], prm["w_main"], 512, N_MAIN).reshape(bp, t, N_MAIN)
        gate = _norm_matmul(xp, norm_mix_g[l], prm["w_gate"], 512, N_GATE)
        o_a, s_pairs = _rwkv(main, jnp.zeros((bp, 1, A_COLS), F32), jnp.zeros((bp, 4, 128, 128), F32), rw)
        o_b = _pool(main, jnp.zeros((bp, POOL_HALO, B_W), F32), pool_wl, pool_scale[l], 0, 512)
        qt, ks16, vs16, kw16, vw16, kc, vc, kv_rows = _nsa_prep(main, nsa_q_g[l], nsa_k_g[l], 512)
        front = lambda z: jnp.pad(z, ((0, 0), (FRONT_TILES * Q_TILE, 0), (0, 0)))
        front_t = lambda z: jnp.pad(z, ((0, 0), (0, 0), (FRONT_TILES * Q_TILE, 0)))
        o_c = _nsa_attn(qt, front(ks16), front_t(vs16), front(kw16), front_t(vw16), kc, vc,
                        gate.reshape(bp, t, N_GATE), bnear, bcmp)
        xp = _merge(xp, o_a.reshape(bp * t, A_W), o_b.reshape(bp * t, B_W), o_c.reshape(bp * t, C_W), gate,
                    prm["wa"], prm["wb"], prm["wc"], prm["wo"], 512)
        mk, mv = _mem_kv(mem_prompt.reshape(bp * N_MEM, D_MODEL), norm_mem_g[l], prm["wk"], prm["wv"],
                         mem_k_g[l])
        mk, mv = mk.reshape(bp, N_MEM, M_W), mv.reshape(bp, N_MEM, M_W)
        xp = _mem_ffn(xp, mk, mv, norm_memx_g[l], prm["wq"], mem_q_g[l], prm["wmo"], norm_ffn_g[l],
                      prm["w1"], prm["w2"], 1024, t)
        for dst, val in zip(outs_p, (
                _unpair_states(s_pairs), _shift_row(main[:, -1]), main[:, -POOL_BUF:, P_POOL:P_POOL + B_W],
                _token_major(kv_rows[:, 0]), _token_major(kv_rows[:, 1]), _token_major(kv_rows[:, 2]),
                _token_major(kv_rows[:, 3]), _token_major(kv_rows[:, 4, :, -wbuf_p:]),
                _token_major(kv_rows[:, 5, :, -wbuf_p:]),
                mk.reshape(bp, N_MEM, M_HEADS, M_HD), mv.reshape(bp, N_MEM, M_HEADS, M_HD))):
            dst.append(val)

        main = _norm_matmul(xs, norm_mix_g[l], prm["w_main"], bs * ts, N_MAIN).reshape(bs, ts, N_MAIN)
        gate = _norm_matmul(xs, norm_mix_g[l], prm["w_gate"], bs * ts, N_GATE)
        main_pad = jnp.pad(main, ((0, 0), (0, ts_pad - ts), (0, 0)))
        ulast = _shift_order(state_shift[l]).reshape(bs, 1, A_COLS)
        o_a, s_pairs = _rwkv(main_pad, ulast, _pair_states(state_rwkv[l]), rw, t_valid=ts)
        o_a = o_a[:, :ts]
        hist = jnp.concatenate([jnp.zeros((bs, POOL_HALO - POOL_BUF, B_W), F32), state_pool[l]], axis=1)
        o_b = _pool(main_pad, hist, pool_wl, pool_scale[l], past, ts_pad)[:, :ts]
        u_pool = main[:, :, P_POOL:P_POOL + B_W]
        o_c, new_rows = _nsa_sample(main, gate, caches, win_k, win_v, page_table, l, nsa_q_g[l], nsa_k_g[l],
                                    rel_bias)
        new_kv = [new_rows[:, n].reshape(bs, ts, C_KV, C_HD) for n in range(6)]
        slide = lambda buf, n: _token_major(jnp.concatenate(
            [buf[l][:, :, ts:], new_rows[:, n].transpose(0, 2, 1)], axis=2))
        xs = _merge(xs, o_a.reshape(bs * ts, A_W), o_b.reshape(bs * ts, B_W), o_c, gate,
                    prm["wa"], prm["wb"], prm["wc"], prm["wo"], bs * ts)
        xs = _mem_ffn(xs, cache_mem_k[l].reshape(bs, N_MEM, M_W), cache_mem_v[l].reshape(bs, N_MEM, M_W),
                      norm_memx_g[l], prm["wq"], mem_q_g[l], prm["wmo"], norm_ffn_g[l],
                      prm["w1"], prm["w2"], 32, ts)
        for dst, val in zip(outs_s, (
                _unpair_states(s_pairs), _shift_row(main[:, -1]),
                jnp.concatenate([state_pool[l], u_pool], axis=1)[:, -POOL_BUF:],
                new_kv[0], new_kv[1], new_kv[2], new_kv[3], slide(win_k, 4), slide(win_v, 5))):
            dst.append(val)

    return ((xp.reshape(bp, t, D_MODEL), xs.reshape(bs, ts, D_MODEL))
            + tuple(jnp.stack(o) for o in outs_p) + tuple(jnp.stack(o) for o in outs_s))
```

```python
import functools
import math

import jax
import jax.numpy as jnp
from jax import lax
from jax.experimental import pallas as pl
from jax.experimental.pallas import tpu as pltpu

F32 = jnp.float32
BF16 = jnp.bfloat16

D_MODEL = 1024
DEPTH = 2
PAGE_SIZE = 128
A_HEADS = 8
A_HD = 64
A_W = A_HEADS * A_HD
A_LORA_W = 64
A_LORA_A = 64
A_LORA_G = 128
A_LORA = A_LORA_W + A_LORA_A + A_LORA_G
LNX_EPS = 64e-5
L2_EPS = 1e-24
B_GROUPS = 4
B_GW = 128
B_W = B_GROUPS * B_GW
POOL_WINDOWS = (2, 4, 8, 16)
POOL_BUF = 15
POOL_HALO = 16
C_HEADS = 8
C_HD = 64
C_W = C_HEADS * C_HD
C_KV = 2
C_HPG = C_HEADS // C_KV
C_KVW = C_KV * C_HD
CMP_BLOCK = 64
N_SEL = 16
WINDOW = 512
Q_TILE = 128
N_FORCED = 3
NEG = -1e30
N_BUCKETS = 32
MAX_DIST = 128
N_MEM = 256
M_HEADS = 4
M_HD = 128
M_W = M_HEADS * M_HD
D_FF = 4 * D_MODEL
RMS_EPS = 1e-6

OFF_R = 0
OFF_WL = OFF_R + A_W
OFF_K = OFF_WL + A_LORA_W
OFF_V = OFF_K + A_W
OFF_AL = OFF_V + A_W
OFF_GL = OFF_AL + A_LORA_A
A_COLS = OFF_GL + A_LORA_G
OFF_POOL = A_COLS
OFF_Q = OFF_POOL + B_W
OFF_KV = OFF_Q + C_W
OFF_CG = OFF_KV + 6 * C_KVW
OFF_MG = OFF_CG + 3 * C_HEADS
N_IN = OFF_MG + 3 * D_MODEL

P_R = 0
P_K = P_R + A_W
P_V = P_K + A_W
P_POOL = P_V + A_W
P_Q = P_POOL + B_W
P_KV = P_Q + C_W
P_LORA = P_KV + 6 * C_KVW
N_MAIN = P_LORA + A_LORA
G_MG = 0
G_CG = 3 * D_MODEL
N_GATE = G_CG + 128

BRANCH_DTYPE = BF16
RWKV_CHUNK = 64
RWKV_SUB = 16
LANES = 128
VMEM_LIMIT = 56 * 1024 * 1024


def _cparams(sem):
    return pltpu.CompilerParams(dimension_semantics=sem, vmem_limit_bytes=VMEM_LIMIT)


def _dot(a, b, dims=(((1,), (0,)), ((), ()))):
    return lax.dot_general(a.astype(BF16), b.astype(BF16), dims, preferred_element_type=F32)


def _dot_nt(a, b):
    return _dot(a, b, (((1,), (1,)), ((), ())))


def _split(x, terms):
    parts = []
    for n in range(terms):
        part = x.astype(BF16)
        parts.append(part)
        if n + 1 < terms:
            x = x - part.astype(F32)
    return parts


def _dot_exact_rhs(a, b01, terms=2):
    dims = (((1,), (0,)), ((), ()))
    return sum(lax.dot_general(p, b01, dims, preferred_element_type=F32) for p in _split(a, terms))


def _dot_exact_lhs(a01, b, terms=2):
    dims = (((1,), (0,)), ((), ()))
    return sum(lax.dot_general(a01, p, dims, preferred_element_type=F32) for p in _split(b, terms))


def _sigmoid(x):
    return 1.0 / (1.0 + jnp.exp(-x))


def _rms(x, g):
    return x * lax.rsqrt(jnp.mean(x * x, axis=-1, keepdims=True) + RMS_EPS) * g


def _norm_matmul_kernel(x_ref, g_ref, w_ref, o_ref, xn_ref):
    @pl.when(pl.program_id(1) == 0)
    def _():
        xn_ref[...] = _rms(x_ref[...], g_ref[...]).astype(BF16)

    o_ref[...] = jnp.dot(xn_ref[...], w_ref[...], preferred_element_type=F32)


def _norm_matmul(x, g, w, tm, tn):
    m, k = x.shape
    n = w.shape[1]
    return pl.pallas_call(
        _norm_matmul_kernel,
        grid=(m // tm, n // tn),
        in_specs=[pl.BlockSpec((tm, k), lambda i, j: (i, 0)),
                  pl.BlockSpec((1, k), lambda i, j: (0, 0)),
                  pl.BlockSpec((k, tn), lambda i, j: (0, j))],
        out_specs=pl.BlockSpec((tm, tn), lambda i, j: (i, j)),
        out_shape=jax.ShapeDtypeStruct((m, n), F32),
        scratch_shapes=[pltpu.VMEM((tm, k), BF16)],
        compiler_params=_cparams(("parallel", "arbitrary")),
        name="norm_matmul",
    )(x, g.reshape(1, k), w)


def _rwkv_kernel(r_ref, k_ref, v_ref, lora_ref, ulast_ref, s0_ref, mu_ref, w0_ref, w2_ref, a0_ref,
                 a2_ref, g2_ref, kk_ref, ka_ref, rk_ref, lng_ref, lnb_ref,
                 o_ref, sout_ref, s_ref, prev_ref, ar_ref, bk_ref, vb_ref, y_ref, *, chunk, bt, t_valid):
    c = pl.program_id(1)
    nc = pl.num_programs(1)
    C = chunk
    R = bt * C
    NP = A_HEADS // 2
    PB = 2 * C

    @pl.when(c == 0)
    def _():
        s_ref[...] = s0_ref[...].reshape(bt * NP, 2 * A_HD, 2 * A_HD)
        prev_ref[...] = ulast_ref[:, 0, :]

    row = lax.broadcasted_iota(jnp.int32, (R, 1), 0)

    def shift_mix(u3, lo, hi):
        u = u3.reshape(R, hi - lo)
        u_prev = pltpu.roll(u, 1, 0)
        for b in range(bt):
            u_prev = jnp.where(row == b * C, prev_ref[b:b + 1, lo:hi], u_prev)
        for b in range(bt):
            prev_ref[b:b + 1, lo:hi] = u[(b + 1) * C - 1:(b + 1) * C, :]
        return u + (u_prev - u) * mu_ref[:, lo:hi]

    r = shift_mix(r_ref[...], 0, A_W)
    k = shift_mix(k_ref[...], A_W, 2 * A_W)
    v = shift_mix(v_ref[...], 2 * A_W, 3 * A_W)
    lora = shift_mix(lora_ref[...], 3 * A_W, 3 * A_W + A_LORA)
    wl = lora[:, 0:A_LORA_W]
    al = lora[:, A_LORA_W:A_LORA_W + A_LORA_A]
    gl = lora[:, A_LORA_W + A_LORA_A:A_LORA]

    z = -(w0_ref[...] + _dot(jnp.tanh(wl), w2_ref[...]))
    softplus = jnp.maximum(z, 0.0) + jnp.log(1.0 + jnp.exp(-jnp.abs(z)))
    w = -softplus - 0.5
    a = _sigmoid(a0_ref[...] + _dot(al, a2_ref[...]))
    g = _dot(_sigmoid(gl), g2_ref[...])

    lane = lax.broadcasted_iota(jnp.int32, (A_W, A_W), 1) // A_HD
    sub = lax.broadcasted_iota(jnp.int32, (A_W, A_W), 0) // A_HD
    head_ones = (lane == sub).astype(BF16)

    kkv = k * kk_ref[...]
    head_sum = lambda z: _dot_exact_rhs(z, head_ones, terms=1)
    kkn = kkv * lax.rsqrt(jnp.maximum(head_sum(kkv * kkv), L2_EPS))
    k2 = k * (1.0 + (a - 1.0) * ka_ref[...])
    log_d = -jnp.exp(w)
    t_in = row % C
    if t_valid is not None:
        live = (c * C + t_in) < t_valid
        log_d = jnp.where(live, log_d, 0.0)
        kkn = jnp.where(live, kkn, 0.0)
        k2 = jnp.where(live, k2, 0.0)

    ri = lax.broadcasted_iota(jnp.int32, (R, R), 0)
    ci = lax.broadcasted_iota(jnp.int32, (R, R), 1)
    cum_mask = ((ri // C) == (ci // C)) & (ci <= ri)
    cum = _dot_exact_lhs(cum_mask.astype(BF16), log_d)
    c_incl = jnp.exp(cum)
    c_inv = jnp.exp(-cum)
    a_t = -kkn * jnp.exp(cum - log_d)
    r_t = r * c_incl
    b_t = kkn * a * c_inv
    k_t = k2 * c_inv

    lane = lax.broadcasted_iota(jnp.int32, (1, 2 * A_HD), 1) // A_HD
    def stage(ref, off, x):
        for b in range(bt):
            for p in range(NP):
                blk = x[b * C:(b + 1) * C, 2 * A_HD * p:2 * A_HD * (p + 1)]
                for hh in range(2):
                    ref[b * NP + p, off + hh * C:off + (hh + 1) * C, :] = (
                        jnp.where(lane == hh, blk, 0.0).astype(BF16))

    stage(ar_ref, 0, a_t)
    stage(ar_ref, PB, r_t)
    stage(bk_ref, 0, b_t)
    stage(bk_ref, PB, k_t)
    stage(vb_ref, 0, v)
    ar = ar_ref[...]
    bk = bk_ref[...]
    vb = vb_ref[...]

    nn = (((2,), (1,)), ((0,), (0,)))
    nt = (((2,), (2,)), ((0,), (0,)))
    tn = (((1,), (1,)), ((0,), (0,)))
    bdot = lambda x, y, dims=nn: lax.dot_general(x.astype(BF16), y.astype(BF16), dims,
                                                 preferred_element_type=F32)
    bi = lax.broadcasted_iota(jnp.int32, (PB, PB), 0)
    bj = lax.broadcasted_iota(jnp.int32, (PB, PB), 1)
    same_head = (bi // C) == (bj // C)
    strict = same_head & ((bj % C) < (bi % C))
    lower = same_head & ((bj % C) <= (bi % C))
    sub_blk = (bi // RWKV_SUB) == (bj // RWKV_SUB)
    eye = (bi == bj).astype(F32)

    gram = bdot(ar, bk, nt)
    l_b = jnp.where(strict, gram[:, :PB, :PB], 0.0)
    l_k = jnp.where(strict, gram[:, :PB, PB:], 0.0)
    m_b = jnp.where(lower, gram[:, PB:, :PB], 0.0)
    m_k = jnp.where(lower, gram[:, PB:, PB:], 0.0)
    dg = jnp.where(sub_blk, l_b, 0.0)
    off = l_b - dg
    t_inv = eye + dg
    pw = dg
    for _ in range(int(math.log2(RWKV_SUB)) - 1):
        pw = bdot(pw, pw)
        t_inv = t_inv + bdot(t_inv, pw)
    n1 = bdot(t_inv, off)
    n2 = bdot(n1, n1)
    full = eye + n1 + n2 + bdot(n1, n2)
    t_full = bdot(full, t_inv)

    s_old = s_ref[...]
    w0 = bdot(ar, s_old, nt)
    u = bdot(t_full, w0[:, :PB] + bdot(l_k, vb))
    uv = jnp.concatenate([u, vb.astype(F32)], axis=1)
    yb = w0[:, PB:] + bdot(jnp.concatenate([m_b, m_k], axis=2), uv)
    s_new = s_old + bdot(uv, bk, tn)
    for b in range(bt):
        for p in range(NP):
            sl = slice(2 * A_HD * p, 2 * A_HD * (p + 1))
            idx = b * NP + p
            s_ref[idx] = s_new[idx] * c_incl[(b + 1) * C - 1:(b + 1) * C, sl]
            y_ref[b * C:(b + 1) * C, sl] = yb[idx, :C] + yb[idx, C:]
    y = y_ref[...]

    inv_n = 1.0 / A_HD
    mean = head_sum(y) * inv_n
    yc = y - mean
    var = head_sum(yc * yc) * inv_n
    yn = yc * lax.rsqrt(var + LNX_EPS) * lng_ref[...] + lnb_ref[...]
    bonus = head_sum(r * k2 * rk_ref[...]) * v
    o_ref[...] = ((yn + bonus) * g).reshape(bt, C, A_W).astype(o_ref.dtype)

    @pl.when(c == nc - 1)
    def _():
        sout_ref[...] = s_ref[...].reshape(bt, NP, 2 * A_HD, 2 * A_HD)


RWKV_BATCH_TILE = 4


def _rwkv(proj, ulast, s0, prm, t_valid=None):
    b, t, _ = proj.shape
    C = min(RWKV_CHUNK, t)
    bt = min(RWKV_BATCH_TILE, b)
    nc = t // C
    vec = lambda n: pl.BlockSpec((1, n), lambda i, c: (0, 0))
    mat = lambda m, n: pl.BlockSpec((m, n), lambda i, c: (0, 0))
    col = lambda width, off: pl.BlockSpec((bt, C, width), lambda i, c: (i, c, off // width))
    n_shift = 3 * A_W + A_LORA
    n_pairs = bt * (A_HEADS // 2)
    kern = functools.partial(_rwkv_kernel, chunk=C, bt=bt, t_valid=t_valid)
    return pl.pallas_call(
        kern,
        grid=(b // bt, nc),
        in_specs=[col(A_W, P_R), col(A_W, P_K), col(A_W, P_V), col(A_LORA, P_LORA),
                  pl.BlockSpec((bt, 1, n_shift), lambda i, c: (i, 0, 0)),
                  pl.BlockSpec((bt, 4, 128, 128), lambda i, c: (i, 0, 0, 0)),
                  vec(n_shift), vec(A_W), mat(A_LORA_W, A_W), vec(A_W), mat(A_LORA_A, A_W),
                  mat(A_LORA_G, A_W), vec(A_W), vec(A_W), vec(A_W), vec(A_W), vec(A_W)],
        out_specs=[pl.BlockSpec((bt, C, A_W), lambda i, c: (i, c, 0)),
                   pl.BlockSpec((bt, 4, 128, 128), lambda i, c: (i, 0, 0, 0))],
        out_shape=[jax.ShapeDtypeStruct((b, t, A_W), BRANCH_DTYPE),
                   jax.ShapeDtypeStruct((b, 4, 128, 128), F32)],
        scratch_shapes=[pltpu.VMEM((n_pairs, 128, 128), F32), pltpu.VMEM((bt, n_shift), F32),
                        pltpu.VMEM((n_pairs, 4 * C, 128), BF16), pltpu.VMEM((n_pairs, 4 * C, 128), BF16),
                        pltpu.VMEM((n_pairs, 2 * C, 128), BF16), pltpu.VMEM((bt * C, A_W), F32)],
        compiler_params=_cparams(("parallel", "arbitrary")),
        name="rwkv_chunk",
    )(proj, proj, proj, proj, ulast, s0, *prm)


def _pair_states(s):
    b = s.shape[0]
    s = s.reshape(b, 4, 2, A_HD, A_HD)
    z = jnp.zeros_like(s[:, :, 0])
    top = jnp.concatenate([s[:, :, 0], z], axis=-1)
    bot = jnp.concatenate([z, s[:, :, 1]], axis=-1)
    return jnp.concatenate([top, bot], axis=-2)


def _unpair_states(sp):
    b = sp.shape[0]
    h0 = sp[:, :, :A_HD, :A_HD]
    h1 = sp[:, :, A_HD:, A_HD:]
    return jnp.stack([h0, h1], axis=2).reshape(b, A_HEADS, A_HD, A_HD)


def _pool_kernel(u_ref, halo_ref, hist_ref, pm_ref, ps_ref, o_ref, ext_ref, *, tm, pos0):
    i = pl.program_id(1)

    @pl.when(i == 0)
    def _():
        ext_ref[0:POOL_HALO, :] = hist_ref[0]

    @pl.when(i > 0)
    def _():
        ext_ref[0:POOL_HALO, :] = halo_ref[0]

    cur = u_ref[0]
    ext_ref[POOL_HALO:POOL_HALO + tm, :] = cur
    pos = pos0 + i * tm + lax.broadcasted_iota(jnp.int32, (tm, 1), 0)
    outs = []
    for gi, win in enumerate(POOL_WINDOWS):
        lo, hi = gi * B_GW, (gi + 1) * B_GW
        s = cur[:, lo:hi]
        for back in range(1, win):
            s = s + ext_ref[POOL_HALO - back:POOL_HALO - back + tm, lo:hi]
        cnt = jnp.minimum(win, pos + 1).astype(F32)
        d = s / cnt - cur[:, lo:hi]
        outs.append(_dot(d, pm_ref[gi]))
    o_ref[0] = (jnp.concatenate(outs, axis=1) * ps_ref[...]).astype(o_ref.dtype)


def _pool(proj, hist, pm, ps, pos0, tm):
    b, t, _ = proj.shape
    per = tm // POOL_HALO
    kern = functools.partial(_pool_kernel, tm=tm, pos0=pos0)
    return pl.pallas_call(
        kern,
        grid=(b, t // tm),
        in_specs=[pl.BlockSpec((1, tm, B_W), lambda bi, i: (bi, i, P_POOL // B_W)),
                  pl.BlockSpec((1, POOL_HALO, B_W),
                               lambda bi, i: (bi, jnp.maximum(i * per - 1, 0), P_POOL // B_W)),
                  pl.BlockSpec((1, POOL_HALO, B_W), lambda bi, i: (bi, 0, 0)),
                  pl.BlockSpec((B_GROUPS, B_GW, B_GW), lambda bi, i: (0, 0, 0)),
                  pl.BlockSpec((1, B_W), lambda bi, i: (0, 0))],
        out_specs=pl.BlockSpec((1, tm, B_W), lambda bi, i: (bi, i, 0)),
        out_shape=jax.ShapeDtypeStruct((b, t, B_W), BRANCH_DTYPE),
        scratch_shapes=[pltpu.VMEM((tm + POOL_HALO, B_W), F32)],
        compiler_params=_cparams(("parallel", "arbitrary")),
        name="pool_mix",
    )(proj, proj, hist, pm, ps.reshape(1, B_W))


def _merge_kernel(x_ref, oa_ref, ob_ref, oc_ref, mg_ref, wa_ref, wb_ref, wc_ref, wo_ref, o_ref):
    gate = lambda n: _sigmoid(mg_ref[:, n * D_MODEL:(n + 1) * D_MODEL])
    h = (gate(0) * _dot(oa_ref[...], wa_ref[...]) + gate(1) * _dot(ob_ref[...], wb_ref[...])
         + gate(2) * _dot(oc_ref[...], wc_ref[...]))
    o_ref[...] = x_ref[...] + _dot(h, wo_ref[...])


def _merge(x, oa, ob, oc, gates, wa, wb, wc, wo, tm):
    m = x.shape[0]
    row = lambda n: pl.BlockSpec((tm, n), lambda i: (i, 0))
    full = lambda a: pl.BlockSpec(a.shape, lambda i: (0, 0))
    return pl.pallas_call(
        _merge_kernel,
        grid=(m // tm,),
        in_specs=[row(D_MODEL), row(A_W), row(B_W), row(C_W), row(3 * D_MODEL),
                  full(wa), full(wb), full(wc), full(wo)],
        out_specs=row(D_MODEL),
        out_shape=jax.ShapeDtypeStruct((m, D_MODEL), F32),
        compiler_params=_cparams(("parallel",)),
        name="merge_branches",
    )(x, oa, ob, oc, gates, wa, wb, wc, wo)


def _mem_kv_kernel(mem_ref, g_ref, wk_ref, wv_ref, kg_ref, k_ref, v_ref):
    mn = _rms(mem_ref[...], g_ref[...]).astype(BF16)
    k = jnp.dot(mn, wk_ref[...], preferred_element_type=F32)
    v_ref[...] = jnp.dot(mn, wv_ref[...], preferred_element_type=F32)
    k_ref[...] = jnp.concatenate(
        [_rms(k[:, h * M_HD:(h + 1) * M_HD], kg_ref[...]) for h in range(M_HEADS)], axis=1)


def _mem_kv(mem, g, wk, wv, kg):
    m = mem.shape[0]
    tm = N_MEM
    row = lambda n: pl.BlockSpec((tm, n), lambda i: (i, 0))
    full = lambda a: pl.BlockSpec(a.shape, lambda i: (0, 0))
    g2, kg2 = g.reshape(1, D_MODEL), kg.reshape(1, M_HD)
    return pl.pallas_call(
        _mem_kv_kernel,
        grid=(m // tm,),
        in_specs=[row(D_MODEL), full(g2), full(wk), full(wv), full(kg2)],
        out_specs=[row(M_W), row(M_W)],
        out_shape=[jax.ShapeDtypeStruct((m, M_W), F32), jax.ShapeDtypeStruct((m, M_W), F32)],
        compiler_params=_cparams(("parallel",)),
        name="mem_kv",
    )(mem, g2, wk, wv, kg2)


def _mem_ffn_kernel(x_ref, mk_ref, mv_ref, gx_ref, wq_ref, qg_ref, wo_ref, gf_ref, w1_ref, w2_ref,
                    o_ref, xn_ref, *, tm, rows_per_batch, nkb):
    j = pl.program_id(1)

    @pl.when(j == 0)
    def _():
        x = x_ref[...]
        q = _dot(_rms(x, gx_ref[...]), wq_ref[...])
        mk = mk_ref[...].reshape(nkb * N_MEM, M_W)
        mv = mv_ref[...].reshape(nkb * N_MEM, M_W)
        if nkb > 1:
            qb = lax.broadcasted_iota(jnp.int32, (tm, nkb * N_MEM), 0) // rows_per_batch
            kb = lax.broadcasted_iota(jnp.int32, (tm, nkb * N_MEM), 1) // N_MEM
            same = qb == kb
        outs = []
        for h in range(M_HEADS):
            sl = slice(h * M_HD, (h + 1) * M_HD)
            qh = _rms(q[:, sl], qg_ref[...])
            logits = _dot_nt(qh, mk[:, sl]) * (M_HD ** -0.5)
            if nkb > 1:
                logits = jnp.where(same, logits, NEG)
            mx = jnp.max(logits, axis=-1, keepdims=True)
            p = jnp.exp(logits - mx)
            p = p / jnp.sum(p, axis=-1, keepdims=True)
            outs.append(_dot(p, mv[:, sl]))
        o = jnp.concatenate(outs, axis=1)
        xm = x + _dot(o, wo_ref[...])
        o_ref[...] = xm
        xn_ref[...] = _rms(xm, gf_ref[...]).astype(BF16)

    h1 = jnp.dot(xn_ref[...], w1_ref[...], preferred_element_type=F32)
    h1 = jnp.square(jnp.maximum(h1, 0.0))
    o_ref[...] += _dot(h1, w2_ref[...])


def _mem_ffn(x, mk, mv, gx, wq, qg, wo, gf, w1, w2, tm, rows_per_batch, tf=1024):
    m = x.shape[0]
    nkb = max(tm // rows_per_batch, 1)
    kern = functools.partial(_mem_ffn_kernel, tm=tm, rows_per_batch=rows_per_batch, nkb=nkb)
    full = lambda a: pl.BlockSpec(a.shape, lambda i, j: (0,) * a.ndim)
    gx2, qg2, gf2 = gx.reshape(1, D_MODEL), qg.reshape(1, M_HD), gf.reshape(1, D_MODEL)
    mem_spec = pl.BlockSpec((nkb, N_MEM, M_W), lambda i, j: ((i * tm) // (rows_per_batch * nkb), 0, 0))
    return pl.pallas_call(
        kern,
        grid=(m // tm, D_FF // tf),
        in_specs=[pl.BlockSpec((tm, D_MODEL), lambda i, j: (i, 0)), mem_spec, mem_spec,
                  full(gx2), full(wq), full(qg2), full(wo), full(gf2),
                  pl.BlockSpec((D_MODEL, tf), lambda i, j: (0, j)),
                  pl.BlockSpec((tf, D_MODEL), lambda i, j: (j, 0))],
        out_specs=pl.BlockSpec((tm, D_MODEL), lambda i, j: (i, 0)),
        out_shape=jax.ShapeDtypeStruct((m, D_MODEL), F32),
        scratch_shapes=[pltpu.VMEM((tm, D_MODEL), BF16)],
        compiler_params=_cparams(("parallel", "arbitrary")),
        name="mem_ffn",
    )(x, mk, mv, gx2, wq, qg2, wo, gf2, w1, w2)


def _half_ones():
    i = lax.broadcasted_iota(jnp.int32, (LANES, LANES), 0) // C_HD
    j = lax.broadcasted_iota(jnp.int32, (LANES, LANES), 1) // C_HD
    return (i == j).astype(BF16)


def _rms_halves(x, g, ones):
    ms = _dot_exact_rhs(x * x, ones) * (1.0 / C_HD)
    return x * lax.rsqrt(ms + RMS_EPS) * g


def _nsa_prep_kernel(q_ref, c_ref, s_ref, w_ref, qg_ref, kg_ref,
                     qn_ref, ks16_ref, vs16_ref, kw16_ref, vw16_ref, kc_ref, vc_ref, rows_ref, *, tm):
    ones = _half_ones()
    scale = C_HD ** -0.5 * LOG2E
    q = q_ref[0]
    for n in range(C_HPG):
        qn_ref[0, n] = (_rms_halves(q[:, n * LANES:(n + 1) * LANES], qg_ref[...], ones) * scale).T.astype(BF16)
    c = c_ref[0]
    rows_ref[0, 0] = c[:, :LANES].T
    rows_ref[0, 1] = c[:, LANES:].T
    s = s_ref[0]
    ks = _rms_halves(s[:, :LANES], kg_ref[1:2, :], ones)
    vs_t = s[:, LANES:].T
    ks16_ref[0] = ks.astype(BF16)
    vs16_ref[0] = vs_t.astype(BF16)
    rows_ref[0, 2] = ks.T
    rows_ref[0, 3] = vs_t
    w = w_ref[0]
    kw = _rms_halves(w[:, :LANES], kg_ref[2:3, :], ones)
    vw_t = w[:, LANES:].T
    kw16_ref[0] = kw.astype(BF16)
    vw16_ref[0] = vw_t.astype(BF16)
    rows_ref[0, 4] = kw.T
    rows_ref[0, 5] = vw_t
    nb = tm // CMP_BLOCK
    bi = lax.broadcasted_iota(jnp.int32, (nb, tm), 0)
    ti = lax.broadcasted_iota(jnp.int32, (nb, tm), 1) // CMP_BLOCK
    pool = (bi == ti).astype(BF16)
    means = _dot_exact_lhs(pool, c) * (1.0 / CMP_BLOCK)
    kc_ref[0] = _rms_halves(means[:, :LANES], kg_ref[0:1, :], ones)
    vc_ref[0] = means[:, LANES:]


def _nsa_prep(proj, qg, kg, tm):
    b, t, _ = proj.shape
    nbt = tm // CMP_BLOCK
    kern = functools.partial(_nsa_prep_kernel, tm=tm)
    tok = lambda n, dt: jax.ShapeDtypeStruct((b, t, n), dt)
    blk = jax.ShapeDtypeStruct((b, t // CMP_BLOCK, LANES), F32)
    col = lambda width, off: pl.BlockSpec((1, tm, width), lambda bi, i: (bi, i, off // width))
    out_tok = lambda n: pl.BlockSpec((1, tm, n), lambda bi, i: (bi, i, 0))
    out_blk = pl.BlockSpec((1, nbt, LANES), lambda bi, i: (bi, i, 0))
    out_t = pl.BlockSpec((1, LANES, tm), lambda bi, i: (bi, 0, i))
    tok_t = jax.ShapeDtypeStruct((b, LANES, t), BF16)
    return pl.pallas_call(
        kern,
        grid=(b, t // tm),
        in_specs=[col(C_W, P_Q), col(2 * LANES, P_KV), col(2 * LANES, P_KV + 2 * LANES),
                  col(2 * LANES, P_KV + 4 * LANES),
                  pl.BlockSpec((1, LANES), lambda bi, i: (0, 0)),
                  pl.BlockSpec((3, LANES), lambda bi, i: (0, 0))],
        out_specs=[pl.BlockSpec((1, C_HPG, LANES, tm), lambda bi, i: (bi, 0, 0, i)),
                   out_tok(LANES), out_t, out_tok(LANES), out_t, out_blk, out_blk,
                   pl.BlockSpec((1, 6, LANES, tm), lambda bi, i: (bi, 0, 0, i))],
        out_shape=[jax.ShapeDtypeStruct((b, C_HPG, LANES, t), BF16),
                   tok(LANES, BF16), tok_t, tok(LANES, BF16), tok_t, blk, blk,
                   jax.ShapeDtypeStruct((b, 6, LANES, t), F32)],
        compiler_params=_cparams(("parallel", "parallel")),
        name="nsa_prep",
    )(proj, proj, proj, proj, jnp.tile(qg.reshape(1, C_HD), (1, 2)), jnp.tile(kg, (1, 2)))


M_INIT = -1e29
ROWS = C_HPG * Q_TILE
FAR_TILES = 16
LEFTOVER_STEPS = (FAR_TILES // 4, FAR_TILES // 2, FAR_TILES)
FRONT_TILES = max(WINDOW // Q_TILE, FAR_TILES)
SEL_PAD = 2 * FRONT_TILES
SUM_ROWS = 16
LOG2E = 1.4426950408889634


def _heads(x):
    return jnp.concatenate([x] * C_HPG, axis=1)


def _softmax_step(qts, k, v_t, state, tail_bias=None, valid=None, block_bias=None):
    m_ref, l_ref, acc_ref = state
    n_keys = k.shape[0]
    old = [(m_ref[g], l_ref[g], acc_ref[g]) for g in range(C_KV)]
    ones = jnp.ones((SUM_ROWS, n_keys), BF16)
    v_ext = [jnp.concatenate([v_t[g * C_HD:(g + 1) * C_HD], ones], axis=0) for g in range(C_KV)]
    if block_bias is not None:
        blk = lax.broadcasted_iota(jnp.int32, (n_keys, LANES), 0) // CMP_BLOCK
        col = lax.broadcasted_iota(jnp.int32, (n_keys, LANES), 1)
        k = jnp.concatenate([k, (blk == col).astype(BF16)], axis=1)
    logits = []
    for g in range(C_KV):
        rhs = qts[g] if block_bias is None else jnp.concatenate([qts[g], block_bias[g]], axis=0)
        s = jnp.dot(k, rhs, preferred_element_type=F32)
        if tail_bias is not None:
            cut = s.shape[0] - tail_bias[g].shape[0]
            tail = s[cut:] + tail_bias[g]
            s = jnp.concatenate([s[:cut], tail], axis=0) if cut else tail
        if valid is not None:
            s = jnp.where(_heads(valid[g]), s, NEG)
        logits.append(s.astype(BF16))
    weights = []
    for g in range(C_KV):
        m_old, s = old[g][0], logits[g]
        m_new = jnp.maximum(m_old, jnp.max(s, axis=0, keepdims=True).astype(F32))
        weights.append((m_new, jnp.exp2(m_old - m_new), jnp.exp2(s - m_new.astype(BF16))))
    for g in range(C_KV):
        _, l_old, acc_old = old[g]
        m_new, alpha, p = weights[g]
        pv = jnp.dot(v_ext[g], p, preferred_element_type=F32)
        m_ref[g] = m_new
        l_ref[g] = alpha * l_old + pv[C_HD:C_HD + 1]
        acc_ref[g] = alpha * acc_old + pv[:C_HD]


def _softmax_reset(state):
    m_ref, l_ref, acc_ref = state
    m_ref[...] = jnp.full(m_ref.shape, M_INIT, F32)
    l_ref[...] = jnp.zeros(l_ref.shape, F32)
    acc_ref[...] = jnp.zeros(acc_ref.shape, F32)


def _softmax_result(state, g):
    _, l_ref, acc_ref = state
    l = l_ref[g]
    return acc_ref[g] / jnp.where(l > 0.0, l, 1.0)


def _top_blocks(score, n_sel):
    nb = score.shape[0]
    blk = lax.broadcasted_iota(jnp.int32, score.shape, 0).astype(F32)
    sel = jnp.zeros(score.shape, F32)
    work = score
    for _ in range(min(n_sel, nb)):
        mx = jnp.max(work, axis=0, keepdims=True)
        idx = jnp.min(jnp.where(work == mx, blk, float(nb)), axis=0, keepdims=True)
        pick = blk == idx
        sel = jnp.where(pick, 1.0, sel)
        work = jnp.where(pick, -jnp.inf, work)
    return sel


def _nsa_attn_kernel(qt_ref, ks_ref, vs_ref, kw_ref, vw_ref, kc_ref, vc_ref, cg_ref, bnear_ref,
                     bcmp_ref, o_ref, m_ref, l_ref, acc_ref, sel_ref, *, nb):
    i = pl.program_id(1)
    lane = lax.broadcasted_iota(jnp.int32, (Q_TILE, LANES), 1)
    key = lax.broadcasted_iota(jnp.int32, (Q_TILE, LANES), 0)
    causal = key <= lane
    sig_t = _sigmoid(cg_ref[0]).T
    kc = kc_ref[0].astype(BF16)
    vc_t = vc_ref[0].T.astype(BF16)
    blk = lax.broadcasted_iota(jnp.int32, (nb, Q_TILE), 0)
    qpos = i * Q_TILE + lax.broadcasted_iota(jnp.int32, (nb, Q_TILE), 1)
    cur = qpos // CMP_BLOCK
    cmp_valid = _heads(qpos >= blk * CMP_BLOCK + (CMP_BLOCK - 1))
    forced = (blk == 0) | (blk == cur) | (blk == cur - 1)
    state = (m_ref, l_ref, acc_ref)
    n_back = WINDOW // Q_TILE
    k_rows = lambda ref, j, n: ref[0, pl.ds(pl.multiple_of((j + FRONT_TILES) * Q_TILE, Q_TILE), n * Q_TILE), :]
    v_cols = lambda ref, j, n: ref[0, :, pl.ds(pl.multiple_of((j + FRONT_TILES) * Q_TILE, Q_TILE), n * Q_TILE)]

    def key_mask(g, j, n):
        rows = [sel_ref[g, pl.ds(SEL_PAD + 2 * j + r, 1), :] for r in range(2 * n)]
        return jnp.concatenate(
            [jnp.where(key < CMP_BLOCK, rows[2 * t], rows[2 * t + 1]) for t in range(n)], axis=0) > 0.5

    def block_bias(g, j, n, tiles=None):
        rows = sel_ref[g, pl.ds(SEL_PAD + 2 * j, 2 * n), :]
        if tiles is not None:
            t = j + lax.broadcasted_iota(jnp.int32, (2 * n, Q_TILE), 0) // 2
            rows = jnp.where((t >= tiles[0]) & (t < tiles[1]), rows, 0.0)
        bias = _heads((rows - 1.0) * (-NEG))
        return jnp.concatenate([bias, jnp.zeros((LANES - 2 * n, ROWS), F32)], axis=0).astype(BF16)

    q_t = jnp.concatenate([qt_ref[0, n] for n in range(C_HPG)], axis=1)
    feat = lax.broadcasted_iota(jnp.int32, (LANES, ROWS), 0) // C_HD
    qts = [jnp.where(feat == g, q_t, jnp.zeros_like(q_t)) for g in range(C_KV)]

    o_cmp, scores = [], []
    for g in range(C_KV):
        bias_c = pltpu.roll(bcmp_ref[g], (2 * i + 2) % LANES, 0)[:nb, :]
        lc = jnp.dot(kc, qts[g], preferred_element_type=F32) + bias_c
        lc = jnp.where(cmp_valid, lc, NEG)
        mc = jnp.maximum(jnp.max(lc, axis=0, keepdims=True), M_INIT)
        pc = jnp.exp2(lc - mc)
        den = jnp.sum(pc, axis=0, keepdims=True)
        pc = pc / jnp.where(den > 0.0, den, 1.0)
        o_cmp.append(jnp.dot(vc_t[g * C_HD:(g + 1) * C_HD], pc.astype(BF16),
                             preferred_element_type=F32))
        psum = sum(pc[:, n * Q_TILE:(n + 1) * Q_TILE] for n in range(C_HPG))
        scores.append(jnp.where((blk > cur) | forced, NEG, psum))

    score = jnp.concatenate(scores, axis=1)
    sel = jnp.where(score > 0.5 * NEG, _top_blocks(score, N_SEL - N_FORCED), 0.0)
    sel = jnp.where(jnp.concatenate([forced] * C_KV, axis=1), 1.0, sel)
    for g in range(C_KV):
        sel_ref[g, 0:SEL_PAD, :] = jnp.zeros((SEL_PAD, Q_TILE), F32)
        sel_ref[g, SEL_PAD:SEL_PAD + nb, :] = sel[:, g * Q_TILE:(g + 1) * Q_TILE]

    _softmax_reset(state)
    n_far = jnp.maximum(i - 1, 0)

    def far_step(j, n=FAR_TILES, tiles=None):
        _softmax_step(qts, k_rows(ks_ref, j, n), v_cols(vs_ref, j, n), state,
                      block_bias=[block_bias(g, j, n, tiles) for g in range(C_KV)])

    def far_many(jq, carry):
        far_step(FAR_TILES * jq)
        return carry

    n_many = n_far // FAR_TILES
    lax.fori_loop(0, n_many, far_many, 0)

    n_left = n_far % FAR_TILES
    left = (FAR_TILES * n_many, n_far)
    lo = 0
    for size in LEFTOVER_STEPS:
        @pl.when((n_left > lo) & (n_left <= size))
        def _(size=size):
            far_step(n_far - size, size, left)
        lo = size

    near_causal = jnp.concatenate([key >= 0, causal], axis=0)
    bias2 = [bnear_ref[g] for g in range(C_KV)]
    _softmax_step(qts, k_rows(ks_ref, i - 1, 2), v_cols(vs_ref, i - 1, 2), state, tail_bias=bias2,
                  valid=[key_mask(g, i - 1, 2) & near_causal for g in range(C_KV)])
    o_slc = [_softmax_result(state, g) for g in range(C_KV)]

    _softmax_reset(state)
    exists = lambda back: key >= jnp.where(i >= back, 0, Q_TILE)
    win_valid = jnp.concatenate(
        [(key > lane) & exists(n_back)] + [exists(back) for back in range(n_back - 1, 0, -1)] + [causal],
        axis=0)
    _softmax_step(qts, k_rows(kw_ref, i - n_back, n_back + 1), v_cols(vw_ref, i - n_back, n_back + 1), state,
                  tail_bias=bias2, valid=[win_valid] * C_KV)
    o_win = [_softmax_result(state, g) for g in range(C_KV)]

    for n in range(C_HPG):
        cols = slice(n * Q_TILE, (n + 1) * Q_TILE)
        per_group = []
        for g in range(C_KV):
            h = g * C_HPG + n
            gate = lambda br: sig_t[br * C_HEADS + h:br * C_HEADS + h + 1, :]
            per_group.append(gate(0) * o_cmp[g][:, cols] + gate(1) * o_slc[g][:, cols]
                             + gate(2) * o_win[g][:, cols])
        o_ref[0, :, n * LANES:(n + 1) * LANES] = jnp.concatenate(per_group, axis=0).T.astype(o_ref.dtype)


def _nsa_attn(qt, ks16, vs16t, kw16, vw16t, kc, vc, gates, bnear, bcmp):
    b, _, _, t = qt.shape
    nb = t // CMP_BLOCK
    tp = t + FRONT_TILES * Q_TILE
    kern = functools.partial(_nsa_attn_kernel, nb=nb)
    seq = pl.BlockSpec((1, tp, LANES), lambda bi, i: (bi, 0, 0))
    seq_t = pl.BlockSpec((1, LANES, tp), lambda bi, i: (bi, 0, 0))
    blk = pl.BlockSpec((1, nb, LANES), lambda bi, i: (bi, 0, 0))
    return pl.pallas_call(
        kern,
        grid=(b, t // Q_TILE),
        in_specs=[pl.BlockSpec((1, C_HPG, LANES, Q_TILE), lambda bi, i: (bi, 0, 0, i)),
                  seq, seq_t, seq, seq_t, blk, blk,
                  pl.BlockSpec((1, Q_TILE, LANES), lambda bi, i: (bi, i, G_CG // LANES)),
                  pl.BlockSpec(bnear.shape, lambda bi, i: (0, 0, 0)),
                  pl.BlockSpec(bcmp.shape, lambda bi, i: (0, 0, 0))],
        out_specs=pl.BlockSpec((1, Q_TILE, C_W), lambda bi, i: (bi, i, 0)),
        out_shape=jax.ShapeDtypeStruct((b, t, C_W), BRANCH_DTYPE),
        scratch_shapes=[pltpu.VMEM((C_KV, 1, ROWS), F32), pltpu.VMEM((C_KV, 1, ROWS), F32),
                        pltpu.VMEM((C_KV, C_HD, ROWS), F32),
                        pltpu.VMEM((C_KV, SEL_PAD + nb, Q_TILE), F32)],
        compiler_params=_cparams(("parallel", "arbitrary")),
        name="nsa_attn",
    )(qt, ks16, vs16t, kw16, vw16t, kc, vc, gates, bnear, bcmp)


def _rel_bucket(dist):
    n = jnp.maximum(dist, 0)
    exact = N_BUCKETS // 2
    ratio = jnp.log(jnp.maximum(n, 1).astype(F32) / exact) / math.log(MAX_DIST / exact)
    large = jnp.minimum(exact + (ratio * (N_BUCKETS - exact)).astype(jnp.int32), N_BUCKETS - 1)
    return jnp.where(n < exact, n, large)


def _one_hot(idx, n):
    return (jnp.clip(idx, 0, n - 1)[..., None] == jnp.arange(n)).astype(F32)


def _distance_table(rel_bias):
    tbl = jnp.dot(_one_hot(_rel_bucket(jnp.arange(MAX_DIST)), N_BUCKETS), rel_bias,
                  precision=lax.Precision.HIGHEST)
    return (tbl - tbl[MAX_DIST - 1:MAX_DIST, :]).T


def _bias_tables(rel_bias):
    tbl = _distance_table(rel_bias) * LOG2E
    tbl = tbl.reshape(C_KV, C_HPG, MAX_DIST)
    qq = jnp.arange(Q_TILE)[:, None]
    cc = jnp.arange(LANES)[None, :]
    look = lambda dist: jnp.einsum('ghd,qcd->ghqc', tbl, _one_hot(dist, MAX_DIST),
                                   precision=lax.Precision.HIGHEST).reshape(C_KV, ROWS, LANES)
    bnear = jnp.stack([look(qq - cc), look(qq - cc + Q_TILE)], axis=1)
    bcmp = look(qq - CMP_BLOCK * cc + CMP_BLOCK * (LANES - 1) - (LANES - 1))
    bnear = bnear.transpose(0, 1, 3, 2)
    return jnp.concatenate([bnear[:, 1], bnear[:, 0]], axis=1), bcmp.transpose(0, 2, 1)


PAGES_PER_STEP = 16
S_ROWS = C_KV * C_HPG
BLK_LANES = 384
NEW_PAD = 8


def _page_specs(layer, n):
    spec = lambda j: pl.BlockSpec((1, 1, LANES, PAGE_SIZE),
                                  lambda b, c, pt, j=j: (layer, pt[b, c * n + j], 0, 0))
    return [spec(j) for j in range(n)]


def _cmp_pages_kernel(pt_ref, *refs, n):
    k_refs, v_refs, (kc_ref, vc_ref) = refs[:n], refs[n:2 * n], refs[2 * n:]
    row = lax.broadcasted_iota(jnp.int32, (n * PAGE_SIZE, 2 * n), 0) // CMP_BLOCK
    col = lax.broadcasted_iota(jnp.int32, (n * PAGE_SIZE, 2 * n), 1)
    pool = (row == col).astype(BF16)
    for refs_in, out in ((k_refs, kc_ref), (v_refs, vc_ref)):
        pages = jnp.concatenate([r[0, 0] for r in refs_in], axis=1)
        out[0, 0] = _dot_exact_rhs(pages, pool) * (1.0 / CMP_BLOCK)


def _cmp_pages(page_table, cache_k, cache_v, layer):
    b, n_pages = page_table.shape
    n = min(PAGES_PER_STEP, n_pages)
    chunks = n_pages // n
    out = jax.ShapeDtypeStruct((b, chunks, LANES, 2 * n), F32)
    out_spec = pl.BlockSpec((1, 1, LANES, 2 * n), lambda bi, c, pt: (bi, c, 0, 0))
    return pl.pallas_call(
        functools.partial(_cmp_pages_kernel, n=n),
        grid_spec=pltpu.PrefetchScalarGridSpec(
            num_scalar_prefetch=1, grid=(b, chunks),
            in_specs=_page_specs(layer, n) + _page_specs(layer, n),
            out_specs=[out_spec, out_spec]),
        out_shape=[out, out],
        compiler_params=_cparams(("parallel", "arbitrary")),
        name="nsa_sample_cmp_pages",
    )(page_table, *([cache_k] * n), *([cache_v] * n))


def _top_blocks_lanes(score, n_sel):
    nb = score.shape[-1]
    blk = lax.broadcasted_iota(jnp.int32, score.shape, 1).astype(F32)
    sel = jnp.zeros(score.shape, F32)
    work = score
    for _ in range(min(n_sel, nb)):
        mx = jnp.max(work, axis=-1, keepdims=True)
        idx = jnp.min(jnp.where(work == mx, blk, float(nb)), axis=-1, keepdims=True)
        pick = blk == idx
        sel = jnp.where(pick, 1.0, sel)
        work = jnp.where(pick, -jnp.inf, work)
    return sel


def _nsa_sample_head_kernel(q_ref, newc_ref, news_ref, neww_ref, kc_ref, vc_ref, wk_ref, wv_ref,
                            qg_ref, kg_ref, kgc_ref, bc_ref, bw_ref, bn_ref,
                            q32_ref, sel_ref, ocmp_ref, owin_ref, new_ref, q_scr, *, ts, past, wbuf):
    n_rows = S_ROWS * ts
    ones = _half_ones()
    scale = C_HD ** -0.5
    lane = lax.broadcasted_iota(jnp.int32, (ts, LANES), 1)
    q = q_ref[0]
    for n in range(C_HPG):
        qn = _rms_halves(q[:, n * LANES:(n + 1) * LANES], qg_ref[...], ones) * scale
        for g in range(C_KV):
            lo = (g * C_HPG + n) * ts
            q_scr[lo:lo + ts, :] = jnp.where((lane // C_HD) == g, qn, 0.0)
    q32 = q_scr[...].astype(BF16)
    q32_ref[0] = q32

    newc, news, neww = newc_ref[0], news_ref[0], neww_ref[0]
    ks_new = _rms_halves(news[:, :LANES], kg_ref[1:2, :], ones)
    kw_new = _rms_halves(neww[:, :LANES], kg_ref[2:3, :], ones)
    new_ref[0, 0] = newc[:, :LANES]
    new_ref[0, 1] = newc[:, LANES:]
    new_ref[0, 2] = ks_new
    new_ref[0, 3] = news[:, LANES:]
    new_ref[0, 4] = kw_new
    new_ref[0, 5] = neww[:, LANES:]

    t_of_row = lax.broadcasted_iota(jnp.int32, (n_rows, 1), 0) % ts
    qpos = past + t_of_row

    n_cached = past // CMP_BLOCK
    pad_rows = jnp.zeros((NEW_PAD - ts, LANES), F32)
    ones_rows = jnp.full((NEW_PAD, BLK_LANES), 1.0, BF16)
    new_mean = lambda rows: _dot_tn_exact(jnp.concatenate([rows, pad_rows], axis=0),
                                          ones_rows) * (1.0 / CMP_BLOCK)
    is_new = lax.broadcasted_iota(jnp.int32, (LANES, BLK_LANES), 1) == n_cached
    kc_all = jnp.where(is_new, new_mean(newc[:, :LANES]), kc_ref[0])
    vc_all = jnp.where(is_new, new_mean(newc[:, LANES:]), vc_ref[0])
    feat = lax.broadcasted_iota(jnp.int32, (LANES, 1), 0) // C_HD
    sq = kc_all * kc_all
    ms = jnp.where(feat == 0, jnp.sum(sq[:C_HD], axis=0, keepdims=True),
                   jnp.sum(sq[C_HD:], axis=0, keepdims=True)) * (1.0 / C_HD)
    kc_n = kc_all * lax.rsqrt(ms + RMS_EPS) * kgc_ref[:, 0:1]
    blk = lax.broadcasted_iota(jnp.int32, (n_rows, BLK_LANES), 1)
    lc = jnp.dot(q32, kc_n.astype(BF16), preferred_element_type=F32) + bc_ref[...]
    lc = jnp.where(qpos >= blk * CMP_BLOCK + (CMP_BLOCK - 1), lc, NEG)
    mc = jnp.maximum(jnp.max(lc, axis=-1, keepdims=True), M_INIT)
    pc = jnp.exp(lc - mc)
    den = jnp.sum(pc, axis=-1, keepdims=True)
    pc = pc / jnp.where(den > 0.0, den, 1.0)
    ocmp_ref[0] = lax.dot_general(pc.astype(BF16), vc_all.astype(BF16), (((1,), (1,)), ((), ())),
                                  preferred_element_type=F32)

    gi = lax.broadcasted_iota(jnp.int32, (C_KV * ts, n_rows), 0)
    ri = lax.broadcasted_iota(jnp.int32, (C_KV * ts, n_rows), 1)
    same = ((gi // ts) == (ri // (C_HPG * ts))) & ((gi % ts) == (ri % ts))
    psum = _dot_exact_lhs(same.astype(BF16), pc, terms=3)
    sblk = lax.broadcasted_iota(jnp.int32, (C_KV * ts, BLK_LANES), 1)
    scur = (past + lax.broadcasted_iota(jnp.int32, (C_KV * ts, 1), 0) % ts) // CMP_BLOCK
    forced = (sblk == 0) | (sblk == scur) | (sblk == scur - 1)
    score = jnp.where((sblk > scur) | forced, NEG, psum)
    sel = jnp.where(score > 0.5 * NEG, _top_blocks_lanes(score, N_SEL - N_FORCED), 0.0)
    sel = jnp.where(forced, 1.0, sel)
    sel_ref[0] = lax.dot_general(same.astype(BF16), sel.astype(BF16), (((0,), (0,)), ((), ())),
                                 preferred_element_type=F32)

    key = lax.broadcasted_iota(jnp.int32, (n_rows, wbuf), 1)
    lw = jnp.dot(q32, wk_ref[0, 0].astype(BF16), preferred_element_type=F32) + bw_ref[...]
    kpos = past - wbuf + key
    dist = qpos - kpos
    lw = jnp.where((dist >= 0) & (dist < WINDOW) & (kpos >= 0), lw, NEG)
    kw_pad = jnp.concatenate([kw_new, pad_rows], axis=0)
    vw_pad = jnp.concatenate([neww[:, LANES:], pad_rows], axis=0)
    nkey = lax.broadcasted_iota(jnp.int32, (n_rows, NEW_PAD), 1)
    ln = lax.dot_general(q32, kw_pad.astype(BF16), (((1,), (1,)), ((), ())),
                         preferred_element_type=F32) + bn_ref[...]
    ln = jnp.where((nkey <= t_of_row) & (nkey < ts), ln, NEG)
    mw = jnp.maximum(jnp.max(lw, axis=-1, keepdims=True), jnp.max(ln, axis=-1, keepdims=True))
    pw, pn = jnp.exp(lw - mw), jnp.exp(ln - mw)
    den = jnp.sum(pw, axis=-1, keepdims=True) + jnp.sum(pn, axis=-1, keepdims=True)
    ow = lax.dot_general(pw.astype(BF16), wv_ref[0, 0].astype(BF16), (((1,), (1,)), ((), ())),
                         preferred_element_type=F32) + _dot(pn, vw_pad)
    owin_ref[0] = ow / den


def _dot_tn_exact(a, b01, terms=2):
    dims = (((0,), (0,)), ((), ()))
    return sum(lax.dot_general(p, b01, dims, preferred_element_type=F32) for p in _split(a, terms))


def _nsa_sample_head(main, kc_t, vc_t, win_k, win_v, layer, qg, kg, tables, past):
    b, ts, _ = main.shape
    wbuf = win_k.shape[-1]
    n_rows = S_ROWS * ts
    bias_c, bias_w, bias_n = tables
    kern = functools.partial(_nsa_sample_head_kernel, ts=ts, past=past, wbuf=wbuf)
    col = lambda width, off: pl.BlockSpec((1, ts, width), lambda i: (i, 0, off // width))
    full = lambda a: pl.BlockSpec(a.shape, lambda i: (0,) * a.ndim)
    per_b = lambda a: pl.BlockSpec((1,) + a.shape[1:], lambda i: (i,) + (0,) * (a.ndim - 1))
    win = pl.BlockSpec((1, 1, LANES, wbuf), lambda i: (layer, i, 0, 0))
    qg2 = jnp.tile(qg.reshape(1, C_HD), (1, 2))
    kg2 = jnp.tile(kg, (1, 2))
    kg_col = jnp.tile(kg2[0].reshape(LANES, 1), (1, LANES))
    rows = lambda n, dt: jax.ShapeDtypeStruct((b, n_rows, n), dt)
    out_rows = lambda n: pl.BlockSpec((1, n_rows, n), lambda i: (i, 0, 0))
    return pl.pallas_call(
        kern,
        grid=(b,),
        in_specs=[col(C_W, P_Q), col(2 * LANES, P_KV), col(2 * LANES, P_KV + 2 * LANES),
                  col(2 * LANES, P_KV + 4 * LANES), per_b(kc_t), per_b(vc_t), win, win,
                  full(qg2), full(kg2), full(kg_col), full(bias_c), full(bias_w), full(bias_n)],
        out_specs=[out_rows(LANES), out_rows(BLK_LANES), out_rows(LANES), out_rows(LANES),
                   pl.BlockSpec((1, 6, ts, LANES), lambda i: (i, 0, 0, 0))],
        out_shape=[rows(LANES, BF16), rows(BLK_LANES, F32), rows(LANES, F32), rows(LANES, F32),
                   jax.ShapeDtypeStruct((b, 6, ts, LANES), F32)],
        scratch_shapes=[pltpu.VMEM((n_rows, LANES), F32)],
        compiler_params=_cparams(("parallel",)),
        name="nsa_sample_head",
    )(main, main, main, main, kc_t, vc_t, win_k, win_v, qg2, kg2, kg_col, bias_c, bias_w, bias_n)


def _slc_pages_kernel(pt_ref, *refs, n, ts, last_chunk):
    k_refs, v_refs = refs[:n], refs[n:2 * n]
    (q_ref, sel_ref, ocmp_ref, owin_ref, gate_ref, new_ref, blast_ref, bn_ref,
     o_ref, m_ref, l_ref, acc_ref) = refs[2 * n:]
    c = pl.program_id(1)
    n_rows = S_ROWS * ts
    nt = (((1,), (1,)), ((), ()))

    @pl.when(c == 0)
    def _():
        m_ref[...] = jnp.full(m_ref.shape, M_INIT, F32)
        l_ref[...] = jnp.zeros(l_ref.shape, F32)
        acc_ref[...] = jnp.zeros(acc_ref.shape, F32)

    q32 = q_ref[0]
    keys = jnp.concatenate([r[0, 0] for r in k_refs], axis=1).astype(BF16)
    vals = jnp.concatenate([r[0, 0] for r in v_refs], axis=1).astype(BF16)
    s = jnp.dot(q32, keys, preferred_element_type=F32)
    cut = (n - 1) * PAGE_SIZE
    tail = s[:, cut:] + jnp.where(c == last_chunk, blast_ref[...], 0.0)
    s = jnp.concatenate([s[:, :cut], tail], axis=1) if cut else tail
    row = lax.broadcasted_iota(jnp.int32, (LANES, n * PAGE_SIZE), 0)
    col = lax.broadcasted_iota(jnp.int32, (LANES, n * PAGE_SIZE), 1) // CMP_BLOCK
    expand = (row == col).astype(BF16)
    picked = jnp.dot(sel_ref[0, 0].astype(BF16), expand, preferred_element_type=F32) > 0.5
    s = jnp.where(picked, s, NEG)
    m_old = m_ref[...]
    m_new = jnp.maximum(m_old, jnp.max(s, axis=-1, keepdims=True))
    alpha = jnp.exp(m_old - m_new)
    p = jnp.exp(s - m_new)
    l_ref[...] = alpha * l_ref[...] + jnp.sum(p, axis=-1, keepdims=True)
    acc_ref[...] = alpha * acc_ref[...] + lax.dot_general(p.astype(BF16), vals, nt,
                                                          preferred_element_type=F32)
    m_ref[...] = m_new

    @pl.when(c == last_chunk)
    def _():
        t_of_row = lax.broadcasted_iota(jnp.int32, (n_rows, 1), 0) % ts
        pad_rows = jnp.zeros((NEW_PAD - ts, LANES), F32)
        k_new = jnp.concatenate([new_ref[0, 2], pad_rows], axis=0)
        v_new = jnp.concatenate([new_ref[0, 3], pad_rows], axis=0)
        nkey = lax.broadcasted_iota(jnp.int32, (n_rows, NEW_PAD), 1)
        sn = lax.dot_general(q32, k_new.astype(BF16), nt, preferred_element_type=F32) + bn_ref[...]
        sn = jnp.where((nkey <= t_of_row) & (nkey < ts), sn, NEG)
        m_prev = m_ref[...]
        m_fin = jnp.maximum(m_prev, jnp.max(sn, axis=-1, keepdims=True))
        a_fin = jnp.exp(m_prev - m_fin)
        pn = jnp.exp(sn - m_fin)
        l_fin = a_fin * l_ref[...] + jnp.sum(pn, axis=-1, keepdims=True)
        o_slc = (a_fin * acc_ref[...] + _dot(pn, v_new)) / l_fin

        ri = lax.broadcasted_iota(jnp.int32, (n_rows, NEW_PAD), 0) % ts
        ti = lax.broadcasted_iota(jnp.int32, (n_rows, NEW_PAD), 1)
        gates = jnp.concatenate([_sigmoid(gate_ref[0]), pad_rows], axis=0)
        per_row = _dot_exact_lhs((ri == ti).astype(BF16), gates)
        head = lax.broadcasted_iota(jnp.int32, (n_rows, LANES), 0) // ts
        glane = lax.broadcasted_iota(jnp.int32, (n_rows, LANES), 1)
        gate = lambda br: jnp.sum(jnp.where(glane == br * C_HEADS + head, per_row, 0.0), axis=-1,
                                  keepdims=True)
        comb = gate(0) * ocmp_ref[0] + gate(1) * o_slc + gate(2) * owin_ref[0]
        low = (lax.broadcasted_iota(jnp.int32, (ts, LANES), 1) // C_HD) == 0
        for n_ in range(C_HPG):
            top = comb[n_ * ts:(n_ + 1) * ts]
            bot = comb[(C_HPG + n_) * ts:(C_HPG + n_ + 1) * ts]
            o_ref[0, :, n_ * LANES:(n_ + 1) * LANES] = jnp.where(low, top, bot)


def _slc_pages(page_table, cache_k, cache_v, layer, q32, sel_chunks, o_cmp, o_win, gates, new_rows,
               bias_last, bias_n):
    b, n_pages = page_table.shape
    n = min(PAGES_PER_STEP, n_pages)
    chunks = n_pages // n
    ts = new_rows.shape[2]
    n_rows = S_ROWS * ts
    kern = functools.partial(_slc_pages_kernel, n=n, ts=ts, last_chunk=chunks - 1)
    per_b = lambda a: pl.BlockSpec((1,) + a.shape[1:], lambda bi, c, pt: (bi,) + (0,) * (a.ndim - 1))
    full = lambda a: pl.BlockSpec(a.shape, lambda bi, c, pt: (0,) * a.ndim)
    return pl.pallas_call(
        kern,
        grid_spec=pltpu.PrefetchScalarGridSpec(
            num_scalar_prefetch=1, grid=(b, chunks),
            in_specs=_page_specs(layer, n) + _page_specs(layer, n) + [
                per_b(q32), pl.BlockSpec((1, 1, n_rows, LANES), lambda bi, c, pt: (bi, c, 0, 0)),
                per_b(o_cmp), per_b(o_win),
                pl.BlockSpec((1, ts, LANES), lambda bi, c, pt: (bi, 0, G_CG // LANES)),
                per_b(new_rows), full(bias_last), full(bias_n)],
            out_specs=pl.BlockSpec((1, ts, C_W), lambda bi, c, pt: (bi, 0, 0)),
            scratch_shapes=[pltpu.VMEM((n_rows, 1), F32), pltpu.VMEM((n_rows, 1), F32),
                            pltpu.VMEM((n_rows, LANES), F32)]),
        out_shape=jax.ShapeDtypeStruct((b, ts, C_W), F32),
        compiler_params=_cparams(("parallel", "arbitrary")),
        name="nsa_sample_slc_pages",
    )(page_table, *([cache_k] * n), *([cache_v] * n), q32, sel_chunks, o_cmp, o_win, gates, new_rows,
      bias_last, bias_n)


def _sample_bias_tables(rel_bias, past, ts, wbuf):
    tbl = jnp.repeat(_distance_table(rel_bias), ts, axis=0)
    t = jnp.tile(jnp.arange(ts), S_ROWS)[:, None]
    look = lambda dist: jnp.einsum('rd,rcd->rc', tbl, _one_hot(dist, MAX_DIST),
                                   precision=lax.Precision.HIGHEST)
    blk = jnp.arange(BLK_LANES)[None, :]
    bias_c = look(past + t - CMP_BLOCK * blk - (CMP_BLOCK - 1))
    bias_w = look(wbuf + t - jnp.arange(wbuf)[None, :])
    bias_n = look(t - jnp.arange(NEW_PAD)[None, :])
    bias_last = look(PAGE_SIZE + t - jnp.arange(PAGE_SIZE)[None, :])
    return bias_c, bias_w, bias_n, bias_last


def _nsa_sample(main, gate, caches, win_k, win_v, page_table, layer, qg, kg, rel_bias):
    b, ts, _ = main.shape
    n_pages = page_table.shape[1]
    past = n_pages * PAGE_SIZE
    wbuf = win_k.shape[-1]
    n_rows = S_ROWS * ts
    bias_c, bias_w, bias_n, bias_last = _sample_bias_tables(rel_bias, past, ts, wbuf)
    kc_chunks, vc_chunks = _cmp_pages(page_table, caches[0], caches[1], layer)
    n_cached = past // CMP_BLOCK
    unchunk = lambda z: z.transpose(0, 2, 1, 3).reshape(b, LANES, n_cached)
    fit = lambda z: jnp.pad(z, ((0, 0), (0, 0), (0, BLK_LANES - n_cached)))
    q32, sel, o_cmp, o_win, new_rows = _nsa_sample_head(
        main, fit(unchunk(kc_chunks)), fit(unchunk(vc_chunks)), win_k, win_v, layer, qg, kg,
        (bias_c, bias_w, bias_n), past)
    per_step = 2 * min(PAGES_PER_STEP, n_pages)
    sel_chunks = sel[:, :, :n_cached].reshape(b, n_rows, n_cached // per_step, per_step)
    sel_chunks = jnp.pad(sel_chunks.transpose(0, 2, 1, 3), ((0, 0), (0, 0), (0, 0), (0, LANES - per_step)))
    o = _slc_pages(page_table, caches[2], caches[3], layer, q32, sel_chunks, o_cmp, o_win,
                   gate.reshape(b, ts, N_GATE), new_rows, bias_last, bias_n)
    return o.reshape(b * ts, C_W), new_rows


def _shift_order(z):
    part = lambda off, n: z[..., off:off + n]
    return jnp.concatenate([part(OFF_R, A_W), part(OFF_K, A_W), part(OFF_V, A_W), part(OFF_WL, A_LORA_W),
                            part(OFF_AL, A_LORA_A), part(OFF_GL, A_LORA_G)], axis=-1)


def _layer_params(l, w_in, shift_mu, w_br_a, w_br_b, w_br_c, w_out, w_mq, w_mk, w_mv, w_mo, w_ff1, w_ff2):
    w = w_in[l]
    cast = lambda a: a.astype(BF16)
    cols = lambda off, n: cast(w[:, off:off + n])
    w_q = cols(OFF_Q, C_W).reshape(D_MODEL, C_KV, C_HPG, C_HD).transpose(0, 2, 1, 3).reshape(D_MODEL, C_W)
    w_main = jnp.concatenate([
        cols(OFF_R, A_W), cols(OFF_K, A_W), cols(OFF_V, A_W), cols(OFF_POOL, B_W), w_q,
        cols(OFF_KV, 6 * C_KVW), cols(OFF_WL, A_LORA_W), cols(OFF_AL, A_LORA_A), cols(OFF_GL, A_LORA_G)],
        axis=1)
    w_gate = jnp.concatenate([cols(OFF_MG, 3 * D_MODEL), cols(OFF_CG, 3 * C_HEADS),
                              jnp.zeros((D_MODEL, N_GATE - G_CG - 3 * C_HEADS), BF16)], axis=1)
    wc = w_br_c[l].reshape(C_KV, C_HPG, C_HD, D_MODEL).transpose(1, 0, 2, 3).reshape(C_W, D_MODEL)
    return dict(
        w_main=w_main, w_gate=w_gate, mu=_shift_order(shift_mu[l].reshape(1, -1)),
        wa=cast(w_br_a[l]), wb=cast(w_br_b[l]), wc=cast(wc), wo=cast(w_out[l]),
        wq=cast(w_mq[l]), wk=cast(w_mk[l]), wv=cast(w_mv[l]), wmo=cast(w_mo[l]),
        w1=cast(w_ff1[l]), w2=cast(w_ff2[l]))


def _shift_row(main_row):
    lora = main_row[:, P_LORA:]
    return jnp.concatenate([
        main_row[:, P_R:P_R + A_W], lora[:, :A_LORA_W], main_row[:, P_K:P_K + A_W],
        main_row[:, P_V:P_V + A_W], lora[:, A_LORA_W:A_LORA_W + A_LORA_A],
        lora[:, A_LORA_W + A_LORA_A:]], axis=1)


def _feature_major(z):
    z = jnp.moveaxis(z, -3, -1)
    return z.reshape(z.shape[:-3] + (C_KVW, z.shape[-1]))


def _token_major(z):
    b, _, rows = z.shape
    return z.reshape(b, C_KV, C_HD, rows).transpose(0, 3, 1, 2)


def kernel(x_prompt, x_sample, state_rwkv, state_shift, state_pool, cache_cmp_k, cache_cmp_v, cache_slc_k, cache_slc_v, cache_win_k, cache_win_v, cache_mem_k, cache_mem_v, page_table, mem_prompt, rel_bias, norm_mix_g, w_in, shift_mu, rwkv_w0, rwkv_w2, rwkv_a0, rwkv_a2, rwkv_g2, rwkv_kk, rwkv_ka, rwkv_rk, rwkv_lnx_g, rwkv_lnx_b, pool_w, pool_scale, nsa_q_g, nsa_k_g, w_br_a, w_br_b, w_br_c, w_out, norm_memx_g, norm_mem_g, w_mq, w_mk, w_mv, mem_q_g, mem_k_g, w_mo, norm_ffn_g, w_ff1, w_ff2):
    bp, t = x_prompt.shape[:2]
    bs, ts = x_sample.shape[:2]
    past = page_table.shape[1] * PAGE_SIZE
    wbuf_p = min(WINDOW, t)
    ts_pad = 16
    bnear, bcmp = _bias_tables(rel_bias)
    caches = [_feature_major(z) for z in (cache_cmp_k, cache_cmp_v, cache_slc_k, cache_slc_v)]
    win_k, win_v = _feature_major(cache_win_k), _feature_major(cache_win_v)
    xp = x_prompt.reshape(bp * t, D_MODEL)
    xs = x_sample.reshape(bs * ts, D_MODEL)
    outs_p = [[] for _ in range(11)]
    outs_s = [[] for _ in range(9)]
    row = lambda a: a.reshape(1, -1)
    for l in range(DEPTH):
        prm = _layer_params(l, w_in, shift_mu, w_br_a, w_br_b, w_br_c, w_out, w_mq, w_mk, w_mv, w_mo,
                            w_ff1, w_ff2)
        rw = (prm["mu"], row(rwkv_w0[l]), rwkv_w2[l].astype(BF16), row(rwkv_a0[l]), rwkv_a2[l].astype(BF16),
              rwkv_g2[l].astype(BF16), row(rwkv_kk[l]), row(rwkv_ka[l]), row(rwkv_rk[l]),
              row(rwkv_lnx_g[l]), row(rwkv_lnx_b[l]))
        pool_wl = pool_w[l].astype(BF16)

        main = _norm_matmul(xp, norm_mix_g[l], prm["w_main"], 512, N_MAIN).reshape(bp, t, N_MAIN)
        gate = _norm_matmul(xp, norm_mix_g[l], prm["w_gate"], 512, N_GATE)
        o_a, s_pairs = _rwkv(main, jnp.zeros((bp, 1, A_COLS), F32), jnp.zeros((bp, 4, 128, 128), F32), rw)
        o_b = _pool(main, jnp.zeros((bp, POOL_HALO, B_W), F32), pool_wl, pool_scale[l], 0, 512)
        qt, ks16, vs16, kw16, vw16, kc, vc, kv_rows = _nsa_prep(main, nsa_q_g[l], nsa_k_g[l], 512)
        front = lambda z: jnp.pad(z, ((0, 0), (FRONT_TILES * Q_TILE, 0), (0, 0)))
        front_t = lambda z: jnp.pad(z, ((0, 0), (0, 0), (FRONT_TILES * Q_TILE, 0)))
        o_c = _nsa_attn(qt, front(ks16), front_t(vs16), front(kw16), front_t(vw16), kc, vc,
                        gate.reshape(bp, t, N_GATE), bnear, bcmp)
        xp = _merge(xp, o_a.reshape(bp * t, A_W), o_b.reshape(bp * t, B_W), o_c.reshape(bp * t, C_W), gate,
                    prm["wa"], prm["wb"], prm["wc"], prm["wo"], 512)
        mk, mv = _mem_kv(mem_prompt.reshape(bp * N_MEM, D_MODEL), norm_mem_g[l], prm["wk"], prm["wv"],
                         mem_k_g[l])
        mk, mv = mk.reshape(bp, N_MEM, M_W), mv.reshape(bp, N_MEM, M_W)
        xp = _mem_ffn(xp, mk, mv, norm_memx_g[l], prm["wq"], mem_q_g[l], prm["wmo"], norm_ffn_g[l],
                      prm["w1"], prm["w2"], 1024, t)
        for dst, val in zip(outs_p, (
                _unpair_states(s_pairs), _shift_row(main[:, -1]), main[:, -POOL_BUF:, P_POOL:P_POOL + B_W],
                _token_major(kv_rows[:, 0]), _token_major(kv_rows[:, 1]), _token_major(kv_rows[:, 2]),
                _token_major(kv_rows[:, 3]), _token_major(kv_rows[:, 4, :, -wbuf_p:]),
                _token_major(kv_rows[:, 5, :, -wbuf_p:]),
                mk.reshape(bp, N_MEM, M_HEADS, M_HD), mv.reshape(bp, N_MEM, M_HEADS, M_HD))):
            dst.append(val)

        main = _norm_matmul(xs, norm_mix_g[l], prm["w_main"], bs * ts, N_MAIN).reshape(bs, ts, N_MAIN)
        gate = _norm_matmul(xs, norm_mix_g[l], prm["w_gate"], bs * ts, N_GATE)
        main_pad = jnp.pad(main, ((0, 0), (0, ts_pad - ts), (0, 0)))
        ulast = _shift_order(state_shift[l]).reshape(bs, 1, A_COLS)
        o_a, s_pairs = _rwkv(main_pad, ulast, _pair_states(state_rwkv[l]), rw, t_valid=ts)
        o_a = o_a[:, :ts]
        hist = jnp.concatenate([jnp.zeros((bs, POOL_HALO - POOL_BUF, B_W), F32), state_pool[l]], axis=1)
        o_b = _pool(main_pad, hist, pool_wl, pool_scale[l], past, ts_pad)[:, :ts]
        u_pool = main[:, :, P_POOL:P_POOL + B_W]
        o_c, new_rows = _nsa_sample(main, gate, caches, win_k, win_v, page_table, l, nsa_q_g[l], nsa_k_g[l],
                                    rel_bias)
        new_kv = [new_rows[:, n].reshape(bs, ts, C_KV, C_HD) for n in range(6)]
        slide = lambda buf, n: _token_major(jnp.concatenate(
            [buf[l][:, :, ts:], new_rows[:, n].transpose(0, 2, 1)], axis=2))
        xs = _merge(xs, o_a.reshape(bs * ts, A_W), o_b.reshape(bs * ts, B_W), o_c, gate,
                    prm["wa"], prm["wb"], prm["wc"], prm["wo"], bs * ts)
        xs = _mem_ffn(xs, cache_mem_k[l].reshape(bs, N_MEM, M_W), cache_mem_v[l].reshape(bs, N_MEM, M_W),
                      norm_memx_g[l], prm["wq"], mem_q_g[l], prm["wmo"], norm_ffn_g[l],
                      prm["w1"], prm["w2"], 32, ts)
        for dst, val in zip(outs_s, (
                _unpair_states(s_pairs), _shift_row(main[:, -1]),
                jnp.concatenate([state_pool[l], u_pool], axis=1)[:, -POOL_BUF:],
                new_kv[0], new_kv[1], new_kv[2], new_kv[3], slide(win_k, 4), slide(win_v, 5))):
            dst.append(val)

    return ((xp.reshape(bp, t, D_MODEL), xs.reshape(bs, ts, D_MODEL))
            + tuple(jnp.stack(o) for o in outs_p) + tuple(jnp.stack(o) for o in outs_s))
```

```python
import functools
import math

import jax
import jax.numpy as jnp
from jax import lax
from jax.experimental import pallas as pl
from jax.experimental.pallas import tpu as pltpu

F32 = jnp.float32
BF16 = jnp.bfloat16

D_MODEL = 1024
DEPTH = 2
PAGE_SIZE = 128
A_HEADS = 8
A_HD = 64
A_W = A_HEADS * A_HD
A_LORA_W = 64
A_LORA_A = 64
A_LORA_G = 128
A_LORA = A_LORA_W + A_LORA_A + A_LORA_G
LNX_EPS = 64e-5
L2_EPS = 1e-24
B_GROUPS = 4
B_GW = 128
B_W = B_GROUPS * B_GW
POOL_WINDOWS = (2, 4, 8, 16)
POOL_BUF = 15
POOL_HALO = 16
C_HEADS = 8
C_HD = 64
C_W = C_HEADS * C_HD
C_KV = 2
C_HPG = C_HEADS // C_KV
C_KVW = C_KV * C_HD
CMP_BLOCK = 64
N_SEL = 16
WINDOW = 512
Q_TILE = 128
N_FORCED = 3
NEG = -1e30
N_BUCKETS = 32
MAX_DIST = 128
N_MEM = 256
M_HEADS = 4
M_HD = 128
M_W = M_HEADS * M_HD
D_FF = 4 * D_MODEL
RMS_EPS = 1e-6

OFF_R = 0
OFF_WL = OFF_R + A_W
OFF_K = OFF_WL + A_LORA_W
OFF_V = OFF_K + A_W
OFF_AL = OFF_V + A_W
OFF_GL = OFF_AL + A_LORA_A
A_COLS = OFF_GL + A_LORA_G
OFF_POOL = A_COLS
OFF_Q = OFF_POOL + B_W
OFF_KV = OFF_Q + C_W
OFF_CG = OFF_KV + 6 * C_KVW
OFF_MG = OFF_CG + 3 * C_HEADS
N_IN = OFF_MG + 3 * D_MODEL

P_R = 0
P_K = P_R + A_W
P_V = P_K + A_W
P_POOL = P_V + A_W
P_Q = P_POOL + B_W
P_KV = P_Q + C_W
P_LORA = P_KV + 6 * C_KVW
N_MAIN = P_LORA + A_LORA
G_MG = 0
G_CG = 3 * D_MODEL
N_GATE = G_CG + 128

BRANCH_DTYPE = BF16
RWKV_CHUNK = 64
RWKV_SUB = 16
LANES = 128
VMEM_LIMIT = 56 * 1024 * 1024


def _cparams(sem):
    return pltpu.CompilerParams(dimension_semantics=sem, vmem_limit_bytes=VMEM_LIMIT)


def _dot(a, b, dims=(((1,), (0,)), ((), ()))):
    return lax.dot_general(a.astype(BF16), b.astype(BF16), dims, preferred_element_type=F32)


def _dot_nt(a, b):
    return _dot(a, b, (((1,), (1,)), ((), ())))


def _split(x, terms):
    parts = []
    for n in range(terms):
        part = x.astype(BF16)
        parts.append(part)
        if n + 1 < terms:
            x = x - part.astype(F32)
    return parts


def _dot_exact_rhs(a, b01, terms=2):
    dims = (((1,), (0,)), ((), ()))
    return sum(lax.dot_general(p, b01, dims, preferred_element_type=F32) for p in _split(a, terms))


def _dot_exact_lhs(a01, b, terms=2):
    dims = (((1,), (0,)), ((), ()))
    return sum(lax.dot_general(a01, p, dims, preferred_element_type=F32) for p in _split(b, terms))


def _sigmoid(x):
    return 1.0 / (1.0 + jnp.exp(-x))


def _rms(x, g):
    return x * lax.rsqrt(jnp.mean(x * x, axis=-1, keepdims=True) + RMS_EPS) * g


def _norm_matmul_kernel(x_ref, g_ref, w_ref, o_ref, xn_ref):
    @pl.when(pl.program_id(1) == 0)
    def _():
        xn_ref[...] = _rms(x_ref[...], g_ref[...]).astype(BF16)

    o_ref[...] = jnp.dot(xn_ref[...], w_ref[...], preferred_element_type=F32)


def _norm_matmul(x, g, w, tm, tn):
    m, k = x.shape
    n = w.shape[1]
    return pl.pallas_call(
        _norm_matmul_kernel,
        grid=(m // tm, n // tn),
        in_specs=[pl.BlockSpec((tm, k), lambda i, j: (i, 0)),
                  pl.BlockSpec((1, k), lambda i, j: (0, 0)),
                  pl.BlockSpec((k, tn), lambda i, j: (0, j))],
        out_specs=pl.BlockSpec((tm, tn), lambda i, j: (i, j)),
        out_shape=jax.ShapeDtypeStruct((m, n), F32),
        scratch_shapes=[pltpu.VMEM((tm, k), BF16)],
        compiler_params=_cparams(("parallel", "arbitrary")),
        name="norm_matmul",
    )(x, g.reshape(1, k), w)


def _rwkv_kernel(r_ref, k_ref, v_ref, lora_ref, ulast_ref, s0_ref, mu_ref, w0_ref, w2_ref, a0_ref,
                 a2_ref, g2_ref, kk_ref, ka_ref, rk_ref, lng_ref, lnb_ref,
                 o_ref, sout_ref, s_ref, prev_ref, ar_ref, bk_ref, vb_ref, y_ref, *, chunk, bt, t_valid):
    c = pl.program_id(1)
    nc = pl.num_programs(1)
    C = chunk
    R = bt * C
    NP = A_HEADS // 2
    PB = 2 * C

    @pl.when(c == 0)
    def _():
        s_ref[...] = s0_ref[...].reshape(bt * NP, 2 * A_HD, 2 * A_HD)
        prev_ref[...] = ulast_ref[:, 0, :]

    row = lax.broadcasted_iota(jnp.int32, (R, 1), 0)

    def shift_mix(u3, lo, hi):
        u = u3.reshape(R, hi - lo)
        u_prev = pltpu.roll(u, 1, 0)
        for b in range(bt):
            u_prev = jnp.where(row == b * C, prev_ref[b:b + 1, lo:hi], u_prev)
        for b in range(bt):
            prev_ref[b:b + 1, lo:hi] = u[(b + 1) * C - 1:(b + 1) * C, :]
        return u + (u_prev - u) * mu_ref[:, lo:hi]

    r = shift_mix(r_ref[...], 0, A_W)
    k = shift_mix(k_ref[...], A_W, 2 * A_W)
    v = shift_mix(v_ref[...], 2 * A_W, 3 * A_W)
    lora = shift_mix(lora_ref[...], 3 * A_W, 3 * A_W + A_LORA)
    wl = lora[:, 0:A_LORA_W]
    al = lora[:, A_LORA_W:A_LORA_W + A_LORA_A]
    gl = lora[:, A_LORA_W + A_LORA_A:A_LORA]

    z = -(w0_ref[...] + _dot(jnp.tanh(wl), w2_ref[...]))
    softplus = jnp.maximum(z, 0.0) + jnp.log(1.0 + jnp.exp(-jnp.abs(z)))
    w = -softplus - 0.5
    a = _sigmoid(a0_ref[...] + _dot(al, a2_ref[...]))
    g = _dot(_sigmoid(gl), g2_ref[...])

    lane = lax.broadcasted_iota(jnp.int32, (A_W, A_W), 1) // A_HD
    sub = lax.broadcasted_iota(jnp.int32, (A_W, A_W), 0) // A_HD
    head_ones = (lane == sub).astype(BF16)

    kkv = k * kk_ref[...]
    head_sum = lambda z: _dot_exact_rhs(z, head_ones, terms=1)
    kkn = kkv * lax.rsqrt(jnp.maximum(head_sum(kkv * kkv), L2_EPS))
    k2 = k * (1.0 + (a - 1.0) * ka_ref[...])
    log_d = -jnp.exp(w)
    t_in = row % C
    if t_valid is not None:
        live = (c * C + t_in) < t_valid
        log_d = jnp.where(live, log_d, 0.0)
        kkn = jnp.where(live, kkn, 0.0)
        k2 = jnp.where(live, k2, 0.0)

    ri = lax.broadcasted_iota(jnp.int32, (R, R), 0)
    ci = lax.broadcasted_iota(jnp.int32, (R, R), 1)
    cum_mask = ((ri // C) == (ci // C)) & (ci <= ri)
    cum = _dot_exact_lhs(cum_mask.astype(BF16), log_d)
    c_incl = jnp.exp(cum)
    c_inv = jnp.exp(-cum)
    a_t = -kkn * jnp.exp(cum - log_d)
    r_t = r * c_incl
    b_t = kkn * a * c_inv
    k_t = k2 * c_inv

    lane = lax.broadcasted_iota(jnp.int32, (1, 2 * A_HD), 1) // A_HD
    def stage(ref, off, x):
        for b in range(bt):
            for p in range(NP):
                blk = x[b * C:(b + 1) * C, 2 * A_HD * p:2 * A_HD * (p + 1)]
                for hh in range(2):
                    ref[b * NP + p, off + hh * C:off + (hh + 1) * C, :] = (
                        jnp.where(lane == hh, blk, 0.0).astype(BF16))

    stage(ar_ref, 0, a_t)
    stage(ar_ref, PB, r_t)
    stage(bk_ref, 0, b_t)
    stage(bk_ref, PB, k_t)
    stage(vb_ref, 0, v)
    ar = ar_ref[...]
    bk = bk_ref[...]
    vb = vb_ref[...]

    nn = (((2,), (1,)), ((0,), (0,)))
    nt = (((2,), (2,)), ((0,), (0,)))
    tn = (((1,), (1,)), ((0,), (0,)))
    bdot = lambda x, y, dims=nn: lax.dot_general(x.astype(BF16), y.astype(BF16), dims,
                                                 preferred_element_type=F32)
    bi = lax.broadcasted_iota(jnp.int32, (PB, PB), 0)
    bj = lax.broadcasted_iota(jnp.int32, (PB, PB), 1)
    same_head = (bi // C) == (bj // C)
    strict = same_head & ((bj % C) < (bi % C))
    lower = same_head & ((bj % C) <= (bi % C))
    sub_blk = (bi // RWKV_SUB) == (bj // RWKV_SUB)
    eye = (bi == bj).astype(F32)

    gram = bdot(ar, bk, nt)
    l_b = jnp.where(strict, gram[:, :PB, :PB], 0.0)
    l_k = jnp.where(strict, gram[:, :PB, PB:], 0.0)
    m_b = jnp.where(lower, gram[:, PB:, :PB], 0.0)
    m_k = jnp.where(lower, gram[:, PB:, PB:], 0.0)
    dg = jnp.where(sub_blk, l_b, 0.0)
    off = l_b - dg
    t_inv = eye + dg
    pw = dg
    for _ in range(int(math.log2(RWKV_SUB)) - 1):
        pw = bdot(pw, pw)
        t_inv = t_inv + bdot(t_inv, pw)
    n1 = bdot(t_inv, off)
    n2 = bdot(n1, n1)
    full = eye + n1 + n2 + bdot(n1, n2)
    t_full = bdot(full, t_inv)

    s_old = s_ref[...]
    w0 = bdot(ar, s_old, nt)
    u = bdot(t_full, w0[:, :PB] + bdot(l_k, vb))
    uv = jnp.concatenate([u, vb.astype(F32)], axis=1)
    yb = w0[:, PB:] + bdot(jnp.concatenate([m_b, m_k], axis=2), uv)
    s_new = s_old + bdot(uv, bk, tn)
    for b in range(bt):
        for p in range(NP):
            sl = slice(2 * A_HD * p, 2 * A_HD * (p + 1))
            idx = b * NP + p
            s_ref[idx] = s_new[idx] * c_incl[(b + 1) * C - 1:(b + 1) * C, sl]
            y_ref[b * C:(b + 1) * C, sl] = yb[idx, :C] + yb[idx, C:]
    y = y_ref[...]

    inv_n = 1.0 / A_HD
    mean = head_sum(y) * inv_n
    yc = y - mean
    var = head_sum(yc * yc) * inv_n
    yn = yc * lax.rsqrt(var + LNX_EPS) * lng_ref[...] + lnb_ref[...]
    bonus = head_sum(r * k2 * rk_ref[...]) * v
    o_ref[...] = ((yn + bonus) * g).reshape(bt, C, A_W).astype(o_ref.dtype)

    @pl.when(c == nc - 1)
    def _():
        sout_ref[...] = s_ref[...].reshape(bt, NP, 2 * A_HD, 2 * A_HD)


RWKV_BATCH_TILE = 4


def _rwkv(proj, ulast, s0, prm, t_valid=None):
    b, t, _ = proj.shape
    C = min(RWKV_CHUNK, t)
    bt = min(RWKV_BATCH_TILE, b)
    nc = t // C
    vec = lambda n: pl.BlockSpec((1, n), lambda i, c: (0, 0))
    mat = lambda m, n: pl.BlockSpec((m, n), lambda i, c: (0, 0))
    col = lambda width, off: pl.BlockSpec((bt, C, width), lambda i, c: (i, c, off // width))
    n_shift = 3 * A_W + A_LORA
    n_pairs = bt * (A_HEADS // 2)
    kern = functools.partial(_rwkv_kernel, chunk=C, bt=bt, t_valid=t_valid)
    return pl.pallas_call(
        kern,
        grid=(b // bt, nc),
        in_specs=[col(A_W, P_R), col(A_W, P_K), col(A_W, P_V), col(A_LORA, P_LORA),
                  pl.BlockSpec((bt, 1, n_shift), lambda i, c: (i, 0, 0)),
                  pl.BlockSpec((bt, 4, 128, 128), lambda i, c: (i, 0, 0, 0)),
                  vec(n_shift), vec(A_W), mat(A_LORA_W, A_W), vec(A_W), mat(A_LORA_A, A_W),
                  mat(A_LORA_G, A_W), vec(A_W), vec(A_W), vec(A_W), vec(A_W), vec(A_W)],
        out_specs=[pl.BlockSpec((bt, C, A_W), lambda i, c: (i, c, 0)),
                   pl.BlockSpec((bt, 4, 128, 128), lambda i, c: (i, 0, 0, 0))],
        out_shape=[jax.ShapeDtypeStruct((b, t, A_W), BRANCH_DTYPE),
                   jax.ShapeDtypeStruct((b, 4, 128, 128), F32)],
        scratch_shapes=[pltpu.VMEM((n_pairs, 128, 128), F32), pltpu.VMEM((bt, n_shift), F32),
                        pltpu.VMEM((n_pairs, 4 * C, 128), BF16), pltpu.VMEM((n_pairs, 4 * C, 128), BF16),
                        pltpu.VMEM((n_pairs, 2 * C, 128), BF16), pltpu.VMEM((bt * C, A_W), F32)],
        compiler_params=_cparams(("parallel", "arbitrary")),
        name="rwkv_chunk",
    )(proj, proj, proj, proj, ulast, s0, *prm)


def _pair_states(s):
    b = s.shape[0]
    s = s.reshape(b, 4, 2, A_HD, A_HD)
    z = jnp.zeros_like(s[:, :, 0])
    top = jnp.concatenate([s[:, :, 0], z], axis=-1)
    bot = jnp.concatenate([z, s[:, :, 1]], axis=-1)
    return jnp.concatenate([top, bot], axis=-2)


def _unpair_states(sp):
    b = sp.shape[0]
    h0 = sp[:, :, :A_HD, :A_HD]
    h1 = sp[:, :, A_HD:, A_HD:]
    return jnp.stack([h0, h1], axis=2).reshape(b, A_HEADS, A_HD, A_HD)


def _pool_kernel(u_ref, halo_ref, hist_ref, pm_ref, ps_ref, o_ref, ext_ref, *, tm, pos0):
    i = pl.program_id(1)

    @pl.when(i == 0)
    def _():
        ext_ref[0:POOL_HALO, :] = hist_ref[0]

    @pl.when(i > 0)
    def _():
        ext_ref[0:POOL_HALO, :] = halo_ref[0]

    cur = u_ref[0]
    ext_ref[POOL_HALO:POOL_HALO + tm, :] = cur
    pos = pos0 + i * tm + lax.broadcasted_iota(jnp.int32, (tm, 1), 0)
    outs = []
    for gi, win in enumerate(POOL_WINDOWS):
        lo, hi = gi * B_GW, (gi + 1) * B_GW
        s = cur[:, lo:hi]
        for back in range(1, win):
            s = s + ext_ref[POOL_HALO - back:POOL_HALO - back + tm, lo:hi]
        cnt = jnp.minimum(win, pos + 1).astype(F32)
        d = s / cnt - cur[:, lo:hi]
        outs.append(_dot(d, pm_ref[gi]))
    o_ref[0] = (jnp.concatenate(outs, axis=1) * ps_ref[...]).astype(o_ref.dtype)


def _pool(proj, hist, pm, ps, pos0, tm):
    b, t, _ = proj.shape
    per = tm // POOL_HALO
    kern = functools.partial(_pool_kernel, tm=tm, pos0=pos0)
    return pl.pallas_call(
        kern,
        grid=(b, t // tm),
        in_specs=[pl.BlockSpec((1, tm, B_W), lambda bi, i: (bi, i, P_POOL // B_W)),
                  pl.BlockSpec((1, POOL_HALO, B_W),
                               lambda bi, i: (bi, jnp.maximum(i * per - 1, 0), P_POOL // B_W)),
                  pl.BlockSpec((1, POOL_HALO, B_W), lambda bi, i: (bi, 0, 0)),
                  pl.BlockSpec((B_GROUPS, B_GW, B_GW), lambda bi, i: (0, 0, 0)),
                  pl.BlockSpec((1, B_W), lambda bi, i: (0, 0))],
        out_specs=pl.BlockSpec((1, tm, B_W), lambda bi, i: (bi, i, 0)),
        out_shape=jax.ShapeDtypeStruct((b, t, B_W), BRANCH_DTYPE),
        scratch_shapes=[pltpu.VMEM((tm + POOL_HALO, B_W), F32)],
        compiler_params=_cparams(("parallel", "arbitrary")),
        name="pool_mix",
    )(proj, proj, hist, pm, ps.reshape(1, B_W))


def _merge_kernel(x_ref, oa_ref, ob_ref, oc_ref, mg_ref, wa_ref, wb_ref, wc_ref, wo_ref, o_ref):
    gate = lambda n: _sigmoid(mg_ref[:, n * D_MODEL:(n + 1) * D_MODEL])
    h = (gate(0) * _dot(oa_ref[...], wa_ref[...]) + gate(1) * _dot(ob_ref[...], wb_ref[...])
         + gate(2) * _dot(oc_ref[...], wc_ref[...]))
    o_ref[...] = x_ref[...] + _dot(h, wo_ref[...])


def _merge(x, oa, ob, oc, gates, wa, wb, wc, wo, tm):
    m = x.shape[0]
    row = lambda n: pl.BlockSpec((tm, n), lambda i: (i, 0))
    full = lambda a: pl.BlockSpec(a.shape, lambda i: (0, 0))
    return pl.pallas_call(
        _merge_kernel,
        grid=(m // tm,),
        in_specs=[row(D_MODEL), row(A_W), row(B_W), row(C_W), row(3 * D_MODEL),
                  full(wa), full(wb), full(wc), full(wo)],
        out_specs=row(D_MODEL),
        out_shape=jax.ShapeDtypeStruct((m, D_MODEL), F32),
        compiler_params=_cparams(("parallel",)),
        name="merge_branches",
    )(x, oa, ob, oc, gates, wa, wb, wc, wo)


def _mem_kv_kernel(mem_ref, g_ref, wk_ref, wv_ref, kg_ref, k_ref, v_ref):
    mn = _rms(mem_ref[...], g_ref[...]).astype(BF16)
    k = jnp.dot(mn, wk_ref[...], preferred_element_type=F32)
    v_ref[...] = jnp.dot(mn, wv_ref[...], preferred_element_type=F32)
    k_ref[...] = jnp.concatenate(
        [_rms(k[:, h * M_HD:(h + 1) * M_HD], kg_ref[...]) for h in range(M_HEADS)], axis=1)


def _mem_kv(mem, g, wk, wv, kg):
    m = mem.shape[0]
    tm = N_MEM
    row = lambda n: pl.BlockSpec((tm, n), lambda i: (i, 0))
    full = lambda a: pl.BlockSpec(a.shape, lambda i: (0, 0))
    g2, kg2 = g.reshape(1, D_MODEL), kg.reshape(1, M_HD)
    return pl.pallas_call(
        _mem_kv_kernel,
        grid=(m // tm,),
        in_specs=[row(D_MODEL), full(g2), full(wk), full(wv), full(kg2)],
        out_specs=[row(M_W), row(M_W)],
        out_shape=[jax.ShapeDtypeStruct((m, M_W), F32), jax.ShapeDtypeStruct((m, M_W), F32)],
        compiler_params=_cparams(("parallel",)),
        name="mem_kv",
    )(mem, g2, wk, wv, kg2)


def _mem_ffn_kernel(x_ref, mk_ref, mv_ref, gx_ref, wq_ref, qg_ref, wo_ref, gf_ref, w1_ref, w2_ref,
                    o_ref, xn_ref, *, tm, rows_per_batch, nkb):
    j = pl.program_id(1)

    @pl.when(j == 0)
    def _():
        x = x_ref[...]
        q = _dot(_rms(x, gx_ref[...]), wq_ref[...])
        mk = mk_ref[...].reshape(nkb * N_MEM, M_W)
        mv = mv_ref[...].reshape(nkb * N_MEM, M_W)
        if nkb > 1:
            qb = lax.broadcasted_iota(jnp.int32, (tm, nkb * N_MEM), 0) // rows_per_batch
            kb = lax.broadcasted_iota(jnp.int32, (tm, nkb * N_MEM), 1) // N_MEM
            same = qb == kb
        outs = []
        for h in range(M_HEADS):
            sl = slice(h * M_HD, (h + 1) * M_HD)
            qh = _rms(q[:, sl], qg_ref[...])
            logits = _dot_nt(qh, mk[:, sl]) * (M_HD ** -0.5)
            if nkb > 1:
                logits = jnp.where(same, logits, NEG)
            mx = jnp.max(logits, axis=-1, keepdims=True)
            p = jnp.exp(logits - mx)
            p = p / jnp.sum(p, axis=-1, keepdims=True)
            outs.append(_dot(p, mv[:, sl]))
        o = jnp.concatenate(outs, axis=1)
        xm = x + _dot(o, wo_ref[...])
        o_ref[...] = xm
        xn_ref[...] = _rms(xm, gf_ref[...]).astype(BF16)

    h1 = jnp.dot(xn_ref[...], w1_ref[...], preferred_element_type=F32)
    h1 = jnp.square(jnp.maximum(h1, 0.0))
    o_ref[...] += _dot(h1, w2_ref[...])


def _mem_ffn(x, mk, mv, gx, wq, qg, wo, gf, w1, w2, tm, rows_per_batch, tf=1024):
    m = x.shape[0]
    nkb = max(tm // rows_per_batch, 1)
    kern = functools.partial(_mem_ffn_kernel, tm=tm, rows_per_batch=rows_per_batch, nkb=nkb)
    full = lambda a: pl.BlockSpec(a.shape, lambda i, j: (0,) * a.ndim)
    gx2, qg2, gf2 = gx.reshape(1, D_MODEL), qg.reshape(1, M_HD), gf.reshape(1, D_MODEL)
    mem_spec = pl.BlockSpec((nkb, N_MEM, M_W), lambda i, j: ((i * tm) // (rows_per_batch * nkb), 0, 0))
    return pl.pallas_call(
        kern,
        grid=(m // tm, D_FF // tf),
        in_specs=[pl.BlockSpec((tm, D_MODEL), lambda i, j: (i, 0)), mem_spec, mem_spec,
                  full(gx2), full(wq), full(qg2), full(wo), full(gf2),
                  pl.BlockSpec((D_MODEL, tf), lambda i, j: (0, j)),
                  pl.BlockSpec((tf, D_MODEL), lambda i, j: (j, 0))],
        out_specs=pl.BlockSpec((tm, D_MODEL), lambda i, j: (i, 0)),
        out_shape=jax.ShapeDtypeStruct((m, D_MODEL), F32),
        scratch_shapes=[pltpu.VMEM((tm, D_MODEL), BF16)],
        compiler_params=_cparams(("parallel", "arbitrary")),
        name="mem_ffn",
    )(x, mk, mv, gx2, wq, qg2, wo, gf2, w1, w2)


def _half_ones():
    i = lax.broadcasted_iota(jnp.int32, (LANES, LANES), 0) // C_HD
    j = lax.broadcasted_iota(jnp.int32, (LANES, LANES), 1) // C_HD
    return (i == j).astype(BF16)


def _rms_halves(x, g, ones):
    ms = _dot_exact_rhs(x * x, ones) * (1.0 / C_HD)
    return x * lax.rsqrt(ms + RMS_EPS) * g


def _nsa_prep_kernel(q_ref, c_ref, s_ref, w_ref, qg_ref, kg_ref,
                     qn_ref, ks16_ref, vs16_ref, kw16_ref, vw16_ref, kc_ref, vc_ref, rows_ref, *, tm):
    ones = _half_ones()
    scale = C_HD ** -0.5 * LOG2E
    q = q_ref[0]
    for n in range(C_HPG):
        qn_ref[0, n] = (_rms_halves(q[:, n * LANES:(n + 1) * LANES], qg_ref[...], ones) * scale).T.astype(BF16)
    c = c_ref[0]
    rows_ref[0, 0] = c[:, :LANES].T
    rows_ref[0, 1] = c[:, LANES:].T
    s = s_ref[0]
    ks = _rms_halves(s[:, :LANES], kg_ref[1:2, :], ones)
    vs_t = s[:, LANES:].T
    ks16_ref[0] = ks.astype(BF16)
    vs16_ref[0] = vs_t.astype(BF16)
    rows_ref[0, 2] = ks.T
    rows_ref[0, 3] = vs_t
    w = w_ref[0]
    kw = _rms_halves(w[:, :LANES], kg_ref[2:3, :], ones)
    vw_t = w[:, LANES:].T
    kw16_ref[0] = kw.astype(BF16)
    vw16_ref[0] = vw_t.astype(BF16)
    rows_ref[0, 4] = kw.T
    rows_ref[0, 5] = vw_t
    nb = tm // CMP_BLOCK
    bi = lax.broadcasted_iota(jnp.int32, (nb, tm), 0)
    ti = lax.broadcasted_iota(jnp.int32, (nb, tm), 1) // CMP_BLOCK
    pool = (bi == ti).astype(BF16)
    means = _dot_exact_lhs(pool, c) * (1.0 / CMP_BLOCK)
    kc_ref[0] = _rms_halves(means[:, :LANES], kg_ref[0:1, :], ones)
    vc_ref[0] = means[:, LANES:]


def _nsa_prep(proj, qg, kg, tm):
    b, t, _ = proj.shape
    nbt = tm // CMP_BLOCK
    kern = functools.partial(_nsa_prep_kernel, tm=tm)
    tok = lambda n, dt: jax.ShapeDtypeStruct((b, t, n), dt)
    blk = jax.ShapeDtypeStruct((b, t // CMP_BLOCK, LANES), F32)
    col = lambda width, off: pl.BlockSpec((1, tm, width), lambda bi, i: (bi, i, off // width))
    out_tok = lambda n: pl.BlockSpec((1, tm, n), lambda bi, i: (bi, i, 0))
    out_blk = pl.BlockSpec((1, nbt, LANES), lambda bi, i: (bi, i, 0))
    out_t = pl.BlockSpec((1, LANES, tm), lambda bi, i: (bi, 0, i))
    tok_t = jax.ShapeDtypeStruct((b, LANES, t), BF16)
    return pl.pallas_call(
        kern,
        grid=(b, t // tm),
        in_specs=[col(C_W, P_Q), col(2 * LANES, P_KV), col(2 * LANES, P_KV + 2 * LANES),
                  col(2 * LANES, P_KV + 4 * LANES),
                  pl.BlockSpec((1, LANES), lambda bi, i: (0, 0)),
                  pl.BlockSpec((3, LANES), lambda bi, i: (0, 0))],
        out_specs=[pl.BlockSpec((1, C_HPG, LANES, tm), lambda bi, i: (bi, 0, 0, i)),
                   out_tok(LANES), out_t, out_tok(LANES), out_t, out_blk, out_blk,
                   pl.BlockSpec((1, 6, LANES, tm), lambda bi, i: (bi, 0, 0, i))],
        out_shape=[jax.ShapeDtypeStruct((b, C_HPG, LANES, t), BF16),
                   tok(LANES, BF16), tok_t, tok(LANES, BF16), tok_t, blk, blk,
                   jax.ShapeDtypeStruct((b, 6, LANES, t), F32)],
        compiler_params=_cparams(("parallel", "parallel")),
        name="nsa_prep",
    )(proj, proj, proj, proj, jnp.tile(qg.reshape(1, C_HD), (1, 2)), jnp.tile(kg, (1, 2)))


M_INIT = -1e29
ROWS = C_HPG * Q_TILE
FAR_TILES = 16
LEFTOVER_STEPS = (FAR_TILES // 4, FAR_TILES // 2, FAR_TILES)
FRONT_TILES = max(WINDOW // Q_TILE, FAR_TILES)
SEL_PAD = 2 * FRONT_TILES
SUM_ROWS = 16
LOG2E = 1.4426950408889634


def _heads(x):
    return jnp.concatenate([x] * C_HPG, axis=1)


def _softmax_step(qts, k, v_t, state, tail_bias=None, valid=None, block_bias=None):
    m_ref, l_ref, acc_ref = state
    n_keys = k.shape[0]
    old = [(m_ref[g], l_ref[g], acc_ref[g]) for g in range(C_KV)]
    ones = jnp.ones((SUM_ROWS, n_keys), BF16)
    v_ext = [jnp.concatenate([v_t[g * C_HD:(g + 1) * C_HD], ones], axis=0) for g in range(C_KV)]
    if block_bias is not None:
        blk = lax.broadcasted_iota(jnp.int32, (n_keys, LANES), 0) // CMP_BLOCK
        col = lax.broadcasted_iota(jnp.int32, (n_keys, LANES), 1)
        k = jnp.concatenate([k, (blk == col).astype(BF16)], axis=1)
    logits = []
    for g in range(C_KV):
        rhs = qts[g] if block_bias is None else jnp.concatenate([qts[g], block_bias[g]], axis=0)
        s = jnp.dot(k, rhs, preferred_element_type=F32)
        if tail_bias is not None:
            cut = s.shape[0] - tail_bias[g].shape[0]
            tail = s[cut:] + tail_bias[g]
            s = jnp.concatenate([s[:cut], tail], axis=0) if cut else tail
        if valid is not None:
            s = jnp.where(_heads(valid[g]), s, NEG)
        logits.append(s.astype(BF16))
    weights = []
    for g in range(C_KV):
        m_old, s = old[g][0], logits[g]
        m_new = jnp.maximum(m_old, jnp.max(s, axis=0, keepdims=True).astype(F32))
        weights.append((m_new, jnp.exp2(m_old - m_new), jnp.exp2(s - m_new.astype(BF16))))
    for g in range(C_KV):
        _, l_old, acc_old = old[g]
        m_new, alpha, p = weights[g]
        pv = jnp.dot(v_ext[g], p, preferred_element_type=F32)
        m_ref[g] = m_new
        l_ref[g] = alpha * l_old + pv[C_HD:C_HD + 1]
        acc_ref[g] = alpha * acc_old + pv[:C_HD]


def _softmax_reset(state):
    m_ref, l_ref, acc_ref = state
    m_ref[...] = jnp.full(m_ref.shape, M_INIT, F32)
    l_ref[...] = jnp.zeros(l_ref.shape, F32)
    acc_ref[...] = jnp.zeros(acc_ref.shape, F32)


def _softmax_result(state, g):
    _, l_ref, acc_ref = state
    l = l_ref[g]
    return acc_ref[g] / jnp.where(l > 0.0, l, 1.0)


def _top_blocks(score, n_sel):
    nb = score.shape[0]
    blk = lax.broadcasted_iota(jnp.int32, score.shape, 0).astype(F32)
    sel = jnp.zeros(score.shape, F32)
    work = score
    for _ in range(min(n_sel, nb)):
        mx = jnp.max(work, axis=0, keepdims=True)
        idx = jnp.min(jnp.where(work == mx, blk, float(nb)), axis=0, keepdims=True)
        pick = blk == idx
        sel = jnp.where(pick, 1.0, sel)
        work = jnp.where(pick, -jnp.inf, work)
    return sel


def _nsa_attn_kernel(qt_ref, ks_ref, vs_ref, kw_ref, vw_ref, kc_ref, vc_ref, cg_ref, bnear_ref,
                     bcmp_ref, o_ref, m_ref, l_ref, acc_ref, sel_ref, *, nb):
    i = pl.program_id(1)
    lane = lax.broadcasted_iota(jnp.int32, (Q_TILE, LANES), 1)
    key = lax.broadcasted_iota(jnp.int32, (Q_TILE, LANES), 0)
    causal = key <= lane
    sig_t = _sigmoid(cg_ref[0]).T
    kc = kc_ref[0].astype(BF16)
    vc_t = vc_ref[0].T.astype(BF16)
    blk = lax.broadcasted_iota(jnp.int32, (nb, Q_TILE), 0)
    qpos = i * Q_TILE + lax.broadcasted_iota(jnp.int32, (nb, Q_TILE), 1)
    cur = qpos // CMP_BLOCK
    cmp_valid = _heads(qpos >= blk * CMP_BLOCK + (CMP_BLOCK - 1))
    forced = (blk == 0) | (blk == cur) | (blk == cur - 1)
    state = (m_ref, l_ref, acc_ref)
    n_back = WINDOW // Q_TILE
    k_rows = lambda ref, j, n: ref[0, pl.ds(pl.multiple_of((j + FRONT_TILES) * Q_TILE, Q_TILE), n * Q_TILE), :]
    v_cols = lambda ref, j, n: ref[0, :, pl.ds(pl.multiple_of((j + FRONT_TILES) * Q_TILE, Q_TILE), n * Q_TILE)]

    def key_mask(g, j, n):
        rows = [sel_ref[g, pl.ds(SEL_PAD + 2 * j + r, 1), :] for r in range(2 * n)]
        return jnp.concatenate(
            [jnp.where(key < CMP_BLOCK, rows[2 * t], rows[2 * t + 1]) for t in range(n)], axis=0) > 0.5

    def block_bias(g, j, n, tiles=None):
        rows = sel_ref[g, pl.ds(SEL_PAD + 2 * j, 2 * n), :]
        if tiles is not None:
            t = j + lax.broadcasted_iota(jnp.int32, (2 * n, Q_TILE), 0) // 2
            rows = jnp.where((t >= tiles[0]) & (t < tiles[1]), rows, 0.0)
        bias = _heads((rows - 1.0) * (-NEG))
        return jnp.concatenate([bias, jnp.zeros((LANES - 2 * n, ROWS), F32)], axis=0).astype(BF16)

    q_t = jnp.concatenate([qt_ref[0, n] for n in range(C_HPG)], axis=1)
    feat = lax.broadcasted_iota(jnp.int32, (LANES, ROWS), 0) // C_HD
    qts = [jnp.where(feat == g, q_t, jnp.zeros_like(q_t)) for g in range(C_KV)]

    o_cmp, scores = [], []
    for g in range(C_KV):
        bias_c = pltpu.roll(bcmp_ref[g], (2 * i + 2) % LANES, 0)[:nb, :]
        lc = jnp.dot(kc, qts[g], preferred_element_type=F32) + bias_c
        lc = jnp.where(cmp_valid, lc, NEG)
        mc = jnp.maximum(jnp.max(lc, axis=0, keepdims=True), M_INIT)
        pc = jnp.exp2(lc - mc)
        den = jnp.sum(pc, axis=0, keepdims=True)
        pc = pc / jnp.where(den > 0.0, den, 1.0)
        o_cmp.append(jnp.dot(vc_t[g * C_HD:(g + 1) * C_HD], pc.astype(BF16),
                             preferred_element_type=F32))
        psum = sum(pc[:, n * Q_TILE:(n + 1) * Q_TILE] for n in range(C_HPG))
        scores.append(jnp.where((blk > cur) | forced, NEG, psum))

    score = jnp.concatenate(scores, axis=1)
    sel = jnp.where(score > 0.5 * NEG, _top_blocks(score, N_SEL - N_FORCED), 0.0)
    sel = jnp.where(jnp.concatenate([forced] * C_KV, axis=1), 1.0, sel)
    for g in range(C_KV):
        sel_ref[g, 0:SEL_PAD, :] = jnp.zeros((SEL_PAD, Q_TILE), F32)
        sel_ref[g, SEL_PAD:SEL_PAD + nb, :] = sel[:, g * Q_TILE:(g + 1) * Q_TILE]

    _softmax_reset(state)
    n_far = jnp.maximum(i - 1, 0)

    def far_step(j, n=FAR_TILES, tiles=None):
        _softmax_step(qts, k_rows(ks_ref, j, n), v_cols(vs_ref, j, n), state,
                      block_bias=[block_bias(g, j, n, tiles) for g in range(C_KV)])

    def far_many(jq, carry):
        far_step(FAR_TILES * jq)
        return carry

    n_many = n_far // FAR_TILES
    lax.fori_loop(0, n_many, far_many, 0)

    n_left = n_far % FAR_TILES
    left = (FAR_TILES * n_many, n_far)
    lo = 0
    for size in LEFTOVER_STEPS:
        @pl.when((n_left > lo) & (n_left <= size))
        def _(size=size):
            far_step(n_far - size, size, left)
        lo = size

    near_causal = jnp.concatenate([key >= 0, causal], axis=0)
    bias2 = [bnear_ref[g] for g in range(C_KV)]
    _softmax_step(qts, k_rows(ks_ref, i - 1, 2), v_cols(vs_ref, i - 1, 2), state, tail_bias=bias2,
                  valid=[key_mask(g, i - 1, 2) & near_causal for g in range(C_KV)])
    o_slc = [_softmax_result(state, g) for g in range(C_KV)]

    _softmax_reset(state)
    exists = lambda back: key >= jnp.where(i >= back, 0, Q_TILE)
    win_valid = jnp.concatenate(
        [(key > lane) & exists(n_back)] + [exists(back) for back in range(n_back - 1, 0, -1)] + [causal],
        axis=0)
    _softmax_step(qts, k_rows(kw_ref, i - n_back, n_back + 1), v_cols(vw_ref, i - n_back, n_back + 1), state,
                  tail_bias=bias2, valid=[win_valid] * C_KV)
    o_win = [_softmax_result(state, g) for g in range(C_KV)]

    for n in range(C_HPG):
        cols = slice(n * Q_TILE, (n + 1) * Q_TILE)
        per_group = []
        for g in range(C_KV):
            h = g * C_HPG + n
            gate = lambda br: sig_t[br * C_HEADS + h:br * C_HEADS + h + 1, :]
            per_group.append(gate(0) * o_cmp[g][:, cols] + gate(1) * o_slc[g][:, cols]
                             + gate(2) * o_win[g][:, cols])
        o_ref[0, :, n * LANES:(n + 1) * LANES] = jnp.concatenate(per_group, axis=0).T.astype(o_ref.dtype)


def _nsa_attn(qt, ks16, vs16t, kw16, vw16t, kc, vc, gates, bnear, bcmp):
    b, _, _, t = qt.shape
    nb = t // CMP_BLOCK
    tp = t + FRONT_TILES * Q_TILE
    kern = functools.partial(_nsa_attn_kernel, nb=nb)
    seq = pl.BlockSpec((1, tp, LANES), lambda bi, i: (bi, 0, 0))
    seq_t = pl.BlockSpec((1, LANES, tp), lambda bi, i: (bi, 0, 0))
    blk = pl.BlockSpec((1, nb, LANES), lambda bi, i: (bi, 0, 0))
    return pl.pallas_call(
        kern,
        grid=(b, t // Q_TILE),
        in_specs=[pl.BlockSpec((1, C_HPG, LANES, Q_TILE), lambda bi, i: (bi, 0, 0, i)),
                  seq, seq_t, seq, seq_t, blk, blk,
                  pl.BlockSpec((1, Q_TILE, LANES), lambda bi, i: (bi, i, G_CG // LANES)),
                  pl.BlockSpec(bnear.shape, lambda bi, i: (0, 0, 0)),
                  pl.BlockSpec(bcmp.shape, lambda bi, i: (0, 0, 0))],
        out_specs=pl.BlockSpec((1, Q_TILE, C_W), lambda bi, i: (bi, i, 0)),
        out_shape=jax.ShapeDtypeStruct((b, t, C_W), BRANCH_DTYPE),
        scratch_shapes=[pltpu.VMEM((C_KV, 1, ROWS), F32), pltpu.VMEM((C_KV, 1, ROWS), F32),
                        pltpu.VMEM((C_KV, C_HD, ROWS), F32),
                        pltpu.VMEM((C_KV, SEL_PAD + nb, Q_TILE), F32)],
        compiler_params=_cparams(("parallel", "arbitrary")),
        name="nsa_attn",
    )(qt, ks16, vs16t, kw16, vw16t, kc, vc, gates, bnear, bcmp)


def _rel_bucket(dist):
    n = jnp.maximum(dist, 0)
    exact = N_BUCKETS // 2
    ratio = jnp.log(jnp.maximum(n, 1).astype(F32) / exact) / math.log(MAX_DIST / exact)
    large = jnp.minimum(exact + (ratio * (N_BUCKETS - exact)).astype(jnp.int32), N_BUCKETS - 1)
    return jnp.where(n < exact, n, large)


def _one_hot(idx, n):
    return (jnp.clip(idx, 0, n - 1)[..., None] == jnp.arange(n)).astype(F32)


def _distance_table(rel_bias):
    tbl = jnp.dot(_one_hot(_rel_bucket(jnp.arange(MAX_DIST)), N_BUCKETS), rel_bias,
                  precision=lax.Precision.HIGHEST)
    return (tbl - tbl[MAX_DIST - 1:MAX_DIST, :]).T


def _bias_tables(rel_bias):
    tbl = _distance_table(rel_bias) * LOG2E
    tbl = tbl.reshape(C_KV, C_HPG, MAX_DIST)
    qq = jnp.arange(Q_TILE)[:, None]
    cc = jnp.arange(LANES)[None, :]
    look = lambda dist: jnp.einsum('ghd,qcd->ghqc', tbl, _one_hot(dist, MAX_DIST),
                                   precision=lax.Precision.HIGHEST).reshape(C_KV, ROWS, LANES)
    bnear = jnp.stack([look(qq - cc), look(qq - cc + Q_TILE)], axis=1)
    bcmp = look(qq - CMP_BLOCK * cc + CMP_BLOCK * (LANES - 1) - (LANES - 1))
    bnear = bnear.transpose(0, 1, 3, 2)
    return jnp.concatenate([bnear[:, 1], bnear[:, 0]], axis=1), bcmp.transpose(0, 2, 1)


PAGES_PER_STEP = 32
S_ROWS = C_KV * C_HPG
BLK_LANES = 384
NEW_PAD = 8


def _page_specs(layer, n):
    spec = lambda j: pl.BlockSpec((1, 1, LANES, PAGE_SIZE),
                                  lambda b, c, pt, j=j: (layer, pt[b, c * n + j], 0, 0))
    return [spec(j) for j in range(n)]


def _cmp_pages_kernel(pt_ref, *refs, n):
    k_refs, v_refs, (kc_ref, vc_ref) = refs[:n], refs[n:2 * n], refs[2 * n:]
    row = lax.broadcasted_iota(jnp.int32, (n * PAGE_SIZE, 2 * n), 0) // CMP_BLOCK
    col = lax.broadcasted_iota(jnp.int32, (n * PAGE_SIZE, 2 * n), 1)
    pool = (row == col).astype(BF16)
    for refs_in, out in ((k_refs, kc_ref), (v_refs, vc_ref)):
        pages = jnp.concatenate([r[0, 0] for r in refs_in], axis=1)
        out[0, 0] = _dot_exact_rhs(pages, pool) * (1.0 / CMP_BLOCK)


def _cmp_pages(page_table, cache_k, cache_v, layer):
    b, n_pages = page_table.shape
    n = min(PAGES_PER_STEP, n_pages)
    chunks = n_pages // n
    out = jax.ShapeDtypeStruct((b, chunks, LANES, 2 * n), F32)
    out_spec = pl.BlockSpec((1, 1, LANES, 2 * n), lambda bi, c, pt: (bi, c, 0, 0))
    return pl.pallas_call(
        functools.partial(_cmp_pages_kernel, n=n),
        grid_spec=pltpu.PrefetchScalarGridSpec(
            num_scalar_prefetch=1, grid=(b, chunks),
            in_specs=_page_specs(layer, n) + _page_specs(layer, n),
            out_specs=[out_spec, out_spec]),
        out_shape=[out, out],
        compiler_params=_cparams(("parallel", "arbitrary")),
        name="nsa_sample_cmp_pages",
    )(page_table, *([cache_k] * n), *([cache_v] * n))


def _top_blocks_lanes(score, n_sel):
    nb = score.shape[-1]
    blk = lax.broadcasted_iota(jnp.int32, score.shape, 1).astype(F32)
    sel = jnp.zeros(score.shape, F32)
    work = score
    for _ in range(min(n_sel, nb)):
        mx = jnp.max(work, axis=-1, keepdims=True)
        idx = jnp.min(jnp.where(work == mx, blk, float(nb)), axis=-1, keepdims=True)
        pick = blk == idx
        sel = jnp.where(pick, 1.0, sel)
        work = jnp.where(pick, -jnp.inf, work)
    return sel


def _nsa_sample_head_kernel(q_ref, newc_ref, news_ref, neww_ref, kc_ref, vc_ref, wk_ref, wv_ref,
                            qg_ref, kg_ref, kgc_ref, bc_ref, bw_ref, bn_ref,
                            q32_ref, sel_ref, ocmp_ref, owin_ref, new_ref, q_scr, *, ts, past, wbuf):
    n_rows = S_ROWS * ts
    ones = _half_ones()
    scale = C_HD ** -0.5
    lane = lax.broadcasted_iota(jnp.int32, (ts, LANES), 1)
    q = q_ref[0]
    for n in range(C_HPG):
        qn = _rms_halves(q[:, n * LANES:(n + 1) * LANES], qg_ref[...], ones) * scale
        for g in range(C_KV):
            lo = (g * C_HPG + n) * ts
            q_scr[lo:lo + ts, :] = jnp.where((lane // C_HD) == g, qn, 0.0)
    q32 = q_scr[...].astype(BF16)
    q32_ref[0] = q32

    newc, news, neww = newc_ref[0], news_ref[0], neww_ref[0]
    ks_new = _rms_halves(news[:, :LANES], kg_ref[1:2, :], ones)
    kw_new = _rms_halves(neww[:, :LANES], kg_ref[2:3, :], ones)
    new_ref[0, 0] = newc[:, :LANES]
    new_ref[0, 1] = newc[:, LANES:]
    new_ref[0, 2] = ks_new
    new_ref[0, 3] = news[:, LANES:]
    new_ref[0, 4] = kw_new
    new_ref[0, 5] = neww[:, LANES:]

    t_of_row = lax.broadcasted_iota(jnp.int32, (n_rows, 1), 0) % ts
    qpos = past + t_of_row

    n_cached = past // CMP_BLOCK
    pad_rows = jnp.zeros((NEW_PAD - ts, LANES), F32)
    ones_rows = jnp.full((NEW_PAD, BLK_LANES), 1.0, BF16)
    new_mean = lambda rows: _dot_tn_exact(jnp.concatenate([rows, pad_rows], axis=0),
                                          ones_rows) * (1.0 / CMP_BLOCK)
    is_new = lax.broadcasted_iota(jnp.int32, (LANES, BLK_LANES), 1) == n_cached
    kc_all = jnp.where(is_new, new_mean(newc[:, :LANES]), kc_ref[0])
    vc_all = jnp.where(is_new, new_mean(newc[:, LANES:]), vc_ref[0])
    feat = lax.broadcasted_iota(jnp.int32, (LANES, 1), 0) // C_HD
    sq = kc_all * kc_all
    ms = jnp.where(feat == 0, jnp.sum(sq[:C_HD], axis=0, keepdims=True),
                   jnp.sum(sq[C_HD:], axis=0, keepdims=True)) * (1.0 / C_HD)
    kc_n = kc_all * lax.rsqrt(ms + RMS_EPS) * kgc_ref[:, 0:1]
    blk = lax.broadcasted_iota(jnp.int32, (n_rows, BLK_LANES), 1)
    lc = jnp.dot(q32, kc_n.astype(BF16), preferred_element_type=F32) + bc_ref[...]
    lc = jnp.where(qpos >= blk * CMP_BLOCK + (CMP_BLOCK - 1), lc, NEG)
    mc = jnp.maximum(jnp.max(lc, axis=-1, keepdims=True), M_INIT)
    pc = jnp.exp(lc - mc)
    den = jnp.sum(pc, axis=-1, keepdims=True)
    pc = pc / jnp.where(den > 0.0, den, 1.0)
    ocmp_ref[0] = lax.dot_general(pc.astype(BF16), vc_all.astype(BF16), (((1,), (1,)), ((), ())),
                                  preferred_element_type=F32)

    gi = lax.broadcasted_iota(jnp.int32, (C_KV * ts, n_rows), 0)
    ri = lax.broadcasted_iota(jnp.int32, (C_KV * ts, n_rows), 1)
    same = ((gi // ts) == (ri // (C_HPG * ts))) & ((gi % ts) == (ri % ts))
    psum = _dot_exact_lhs(same.astype(BF16), pc, terms=3)
    sblk = lax.broadcasted_iota(jnp.int32, (C_KV * ts, BLK_LANES), 1)
    scur = (past + lax.broadcasted_iota(jnp.int32, (C_KV * ts, 1), 0) % ts) // CMP_BLOCK
    forced = (sblk == 0) | (sblk == scur) | (sblk == scur - 1)
    score = jnp.where((sblk > scur) | forced, NEG, psum)
    sel = jnp.where(score > 0.5 * NEG, _top_blocks_lanes(score, N_SEL - N_FORCED), 0.0)
    sel = jnp.where(forced, 1.0, sel)
    sel_ref[0] = lax.dot_general(same.astype(BF16), sel.astype(BF16), (((0,), (0,)), ((), ())),
                                 preferred_element_type=F32)

    key = lax.broadcasted_iota(jnp.int32, (n_rows, wbuf), 1)
    lw = jnp.dot(q32, wk_ref[0, 0].astype(BF16), preferred_element_type=F32) + bw_ref[...]
    kpos = past - wbuf + key
    dist = qpos - kpos
    lw = jnp.where((dist >= 0) & (dist < WINDOW) & (kpos >= 0), lw, NEG)
    kw_pad = jnp.concatenate([kw_new, pad_rows], axis=0)
    vw_pad = jnp.concatenate([neww[:, LANES:], pad_rows], axis=0)
    nkey = lax.broadcasted_iota(jnp.int32, (n_rows, NEW_PAD), 1)
    ln = lax.dot_general(q32, kw_pad.astype(BF16), (((1,), (1,)), ((), ())),
                         preferred_element_type=F32) + bn_ref[...]
    ln = jnp.where((nkey <= t_of_row) & (nkey < ts), ln, NEG)
    mw = jnp.maximum(jnp.max(lw, axis=-1, keepdims=True), jnp.max(ln, axis=-1, keepdims=True))
    pw, pn = jnp.exp(lw - mw), jnp.exp(ln - mw)
    den = jnp.sum(pw, axis=-1, keepdims=True) + jnp.sum(pn, axis=-1, keepdims=True)
    ow = lax.dot_general(pw.astype(BF16), wv_ref[0, 0].astype(BF16), (((1,), (1,)), ((), ())),
                         preferred_element_type=F32) + _dot(pn, vw_pad)
    owin_ref[0] = ow / den


def _dot_tn_exact(a, b01, terms=2):
    dims = (((0,), (0,)), ((), ()))
    return sum(lax.dot_general(p, b01, dims, preferred_element_type=F32) for p in _split(a, terms))


def _nsa_sample_head(main, kc_t, vc_t, win_k, win_v, layer, qg, kg, tables, past):
    b, ts, _ = main.shape
    wbuf = win_k.shape[-1]
    n_rows = S_ROWS * ts
    bias_c, bias_w, bias_n = tables
    kern = functools.partial(_nsa_sample_head_kernel, ts=ts, past=past, wbuf=wbuf)
    col = lambda width, off: pl.BlockSpec((1, ts, width), lambda i: (i, 0, off // width))
    full = lambda a: pl.BlockSpec(a.shape, lambda i: (0,) * a.ndim)
    per_b = lambda a: pl.BlockSpec((1,) + a.shape[1:], lambda i: (i,) + (0,) * (a.ndim - 1))
    win = pl.BlockSpec((1, 1, LANES, wbuf), lambda i: (layer, i, 0, 0))
    qg2 = jnp.tile(qg.reshape(1, C_HD), (1, 2))
    kg2 = jnp.tile(kg, (1, 2))
    kg_col = jnp.tile(kg2[0].reshape(LANES, 1), (1, LANES))
    rows = lambda n, dt: jax.ShapeDtypeStruct((b, n_rows, n), dt)
    out_rows = lambda n: pl.BlockSpec((1, n_rows, n), lambda i: (i, 0, 0))
    return pl.pallas_call(
        kern,
        grid=(b,),
        in_specs=[col(C_W, P_Q), col(2 * LANES, P_KV), col(2 * LANES, P_KV + 2 * LANES),
                  col(2 * LANES, P_KV + 4 * LANES), per_b(kc_t), per_b(vc_t), win, win,
                  full(qg2), full(kg2), full(kg_col), full(bias_c), full(bias_w), full(bias_n)],
        out_specs=[out_rows(LANES), out_rows(BLK_LANES), out_rows(LANES), out_rows(LANES),
                   pl.BlockSpec((1, 6, ts, LANES), lambda i: (i, 0, 0, 0))],
        out_shape=[rows(LANES, BF16), rows(BLK_LANES, F32), rows(LANES, F32), rows(LANES, F32),
                   jax.ShapeDtypeStruct((b, 6, ts, LANES), F32)],
        scratch_shapes=[pltpu.VMEM((n_rows, LANES), F32)],
        compiler_params=_cparams(("parallel",)),
        name="nsa_sample_head",
    )(main, main, main, main, kc_t, vc_t, win_k, win_v, qg2, kg2, kg_col, bias_c, bias_w, bias_n)


def _slc_pages_kernel(pt_ref, *refs, n, ts, last_chunk):
    k_refs, v_refs = refs[:n], refs[n:2 * n]
    (q_ref, sel_ref, ocmp_ref, owin_ref, gate_ref, new_ref, blast_ref, bn_ref,
     o_ref, m_ref, l_ref, acc_ref) = refs[2 * n:]
    c = pl.program_id(1)
    n_rows = S_ROWS * ts
    nt = (((1,), (1,)), ((), ()))

    @pl.when(c == 0)
    def _():
        m_ref[...] = jnp.full(m_ref.shape, M_INIT, F32)
        l_ref[...] = jnp.zeros(l_ref.shape, F32)
        acc_ref[...] = jnp.zeros(acc_ref.shape, F32)

    q32 = q_ref[0]
    keys = jnp.concatenate([r[0, 0] for r in k_refs], axis=1).astype(BF16)
    vals = jnp.concatenate([r[0, 0] for r in v_refs], axis=1).astype(BF16)
    s = jnp.dot(q32, keys, preferred_element_type=F32)
    cut = (n - 1) * PAGE_SIZE
    tail = s[:, cut:] + jnp.where(c == last_chunk, blast_ref[...], 0.0)
    s = jnp.concatenate([s[:, :cut], tail], axis=1) if cut else tail
    row = lax.broadcasted_iota(jnp.int32, (LANES, n * PAGE_SIZE), 0)
    col = lax.broadcasted_iota(jnp.int32, (LANES, n * PAGE_SIZE), 1) // CMP_BLOCK
    expand = (row == col).astype(BF16)
    picked = jnp.dot(sel_ref[0, 0].astype(BF16), expand, preferred_element_type=F32) > 0.5
    s = jnp.where(picked, s, NEG)
    m_old = m_ref[...]
    m_new = jnp.maximum(m_old, jnp.max(s, axis=-1, keepdims=True))
    alpha = jnp.exp(m_old - m_new)
    p = jnp.exp(s - m_new)
    l_ref[...] = alpha * l_ref[...] + jnp.sum(p, axis=-1, keepdims=True)
    acc_ref[...] = alpha * acc_ref[...] + lax.dot_general(p.astype(BF16), vals, nt,
                                                          preferred_element_type=F32)
    m_ref[...] = m_new

    @pl.when(c == last_chunk)
    def _():
        t_of_row = lax.broadcasted_iota(jnp.int32, (n_rows, 1), 0) % ts
        pad_rows = jnp.zeros((NEW_PAD - ts, LANES), F32)
        k_new = jnp.concatenate([new_ref[0, 2], pad_rows], axis=0)
        v_new = jnp.concatenate([new_ref[0, 3], pad_rows], axis=0)
        nkey = lax.broadcasted_iota(jnp.int32, (n_rows, NEW_PAD), 1)
        sn = lax.dot_general(q32, k_new.astype(BF16), nt, preferred_element_type=F32) + bn_ref[...]
        sn = jnp.where((nkey <= t_of_row) & (nkey < ts), sn, NEG)
        m_prev = m_ref[...]
        m_fin = jnp.maximum(m_prev, jnp.max(sn, axis=-1, keepdims=True))
        a_fin = jnp.exp(m_prev - m_fin)
        pn = jnp.exp(sn - m_fin)
        l_fin = a_fin * l_ref[...] + jnp.sum(pn, axis=-1, keepdims=True)
        o_slc = (a_fin * acc_ref[...] + _dot(pn, v_new)) / l_fin

        ri = lax.broadcasted_iota(jnp.int32, (n_rows, NEW_PAD), 0) % ts
        ti = lax.broadcasted_iota(jnp.int32, (n_rows, NEW_PAD), 1)
        gates = jnp.concatenate([_sigmoid(gate_ref[0]), pad_rows], axis=0)
        per_row = _dot_exact_lhs((ri == ti).astype(BF16), gates)
        head = lax.broadcasted_iota(jnp.int32, (n_rows, LANES), 0) // ts
        glane = lax.broadcasted_iota(jnp.int32, (n_rows, LANES), 1)
        gate = lambda br: jnp.sum(jnp.where(glane == br * C_HEADS + head, per_row, 0.0), axis=-1,
                                  keepdims=True)
        comb = gate(0) * ocmp_ref[0] + gate(1) * o_slc + gate(2) * owin_ref[0]
        low = (lax.broadcasted_iota(jnp.int32, (ts, LANES), 1) // C_HD) == 0
        for n_ in range(C_HPG):
            top = comb[n_ * ts:(n_ + 1) * ts]
            bot = comb[(C_HPG + n_) * ts:(C_HPG + n_ + 1) * ts]
            o_ref[0, :, n_ * LANES:(n_ + 1) * LANES] = jnp.where(low, top, bot)


def _slc_pages(page_table, cache_k, cache_v, layer, q32, sel_chunks, o_cmp, o_win, gates, new_rows,
               bias_last, bias_n):
    b, n_pages = page_table.shape
    n = min(PAGES_PER_STEP, n_pages)
    chunks = n_pages // n
    ts = new_rows.shape[2]
    n_rows = S_ROWS * ts
    kern = functools.partial(_slc_pages_kernel, n=n, ts=ts, last_chunk=chunks - 1)
    per_b = lambda a: pl.BlockSpec((1,) + a.shape[1:], lambda bi, c, pt: (bi,) + (0,) * (a.ndim - 1))
    full = lambda a: pl.BlockSpec(a.shape, lambda bi, c, pt: (0,) * a.ndim)
    return pl.pallas_call(
        kern,
        grid_spec=pltpu.PrefetchScalarGridSpec(
            num_scalar_prefetch=1, grid=(b, chunks),
            in_specs=_page_specs(layer, n) + _page_specs(layer, n) + [
                per_b(q32), pl.BlockSpec((1, 1, n_rows, LANES), lambda bi, c, pt: (bi, c, 0, 0)),
                per_b(o_cmp), per_b(o_win),
                pl.BlockSpec((1, ts, LANES), lambda bi, c, pt: (bi, 0, G_CG // LANES)),
                per_b(new_rows), full(bias_last), full(bias_n)],
            out_specs=pl.BlockSpec((1, ts, C_W), lambda bi, c, pt: (bi, 0, 0)),
            scratch_shapes=[pltpu.VMEM((n_rows, 1), F32), pltpu.VMEM((n_rows, 1), F32),
                            pltpu.VMEM((n_rows, LANES), F32)]),
        out_shape=jax.ShapeDtypeStruct((b, ts, C_W), F32),
        compiler_params=_cparams(("parallel", "arbitrary")),
        name="nsa_sample_slc_pages",
    )(page_table, *([cache_k] * n), *([cache_v] * n), q32, sel_chunks, o_cmp, o_win, gates, new_rows,
      bias_last, bias_n)


def _sample_bias_tables(rel_bias, past, ts, wbuf):
    tbl = jnp.repeat(_distance_table(rel_bias), ts, axis=0)
    t = jnp.tile(jnp.arange(ts), S_ROWS)[:, None]
    look = lambda dist: jnp.einsum('rd,rcd->rc', tbl, _one_hot(dist, MAX_DIST),
                                   precision=lax.Precision.HIGHEST)
    blk = jnp.arange(BLK_LANES)[None, :]
    bias_c = look(past + t - CMP_BLOCK * blk - (CMP_BLOCK - 1))
    bias_w = look(wbuf + t - jnp.arange(wbuf)[None, :])
    bias_n = look(t - jnp.arange(NEW_PAD)[None, :])
    bias_last = look(PAGE_SIZE + t - jnp.arange(PAGE_SIZE)[None, :])
    return bias_c, bias_w, bias_n, bias_last


def _nsa_sample(main, gate, caches, win_k, win_v, page_table, layer, qg, kg, rel_bias):
    b, ts, _ = main.shape
    n_pages = page_table.shape[1]
    past = n_pages * PAGE_SIZE
    wbuf = win_k.shape[-1]
    n_rows = S_ROWS * ts
    bias_c, bias_w, bias_n, bias_last = _sample_bias_tables(rel_bias, past, ts, wbuf)
    kc_chunks, vc_chunks = _cmp_pages(page_table, caches[0], caches[1], layer)
    n_cached = past // CMP_BLOCK
    unchunk = lambda z: z.transpose(0, 2, 1, 3).reshape(b, LANES, n_cached)
    fit = lambda z: jnp.pad(z, ((0, 0), (0, 0), (0, BLK_LANES - n_cached)))
    q32, sel, o_cmp, o_win, new_rows = _nsa_sample_head(
        main, fit(unchunk(kc_chunks)), fit(unchunk(vc_chunks)), win_k, win_v, layer, qg, kg,
        (bias_c, bias_w, bias_n), past)
    per_step = 2 * min(PAGES_PER_STEP, n_pages)
    sel_chunks = sel[:, :, :n_cached].reshape(b, n_rows, n_cached // per_step, per_step)
    sel_chunks = jnp.pad(sel_chunks.transpose(0, 2, 1, 3), ((0, 0), (0, 0), (0, 0), (0, LANES - per_step)))
    o = _slc_pages(page_table, caches[2], caches[3], layer, q32, sel_chunks, o_cmp, o_win,
                   gate.reshape(b, ts, N_GATE), new_rows, bias_last, bias_n)
    return o.reshape(b * ts, C_W), new_rows


def _shift_order(z):
    part = lambda off, n: z[..., off:off + n]
    return jnp.concatenate([part(OFF_R, A_W), part(OFF_K, A_W), part(OFF_V, A_W), part(OFF_WL, A_LORA_W),
                            part(OFF_AL, A_LORA_A), part(OFF_GL, A_LORA_G)], axis=-1)


def _layer_params(l, w_in, shift_mu, w_br_a, w_br_b, w_br_c, w_out, w_mq, w_mk, w_mv, w_mo, w_ff1, w_ff2):
    w = w_in[l]
    cast = lambda a: a.astype(BF16)
    cols = lambda off, n: cast(w[:, off:off + n])
    w_q = cols(OFF_Q, C_W).reshape(D_MODEL, C_KV, C_HPG, C_HD).transpose(0, 2, 1, 3).reshape(D_MODEL, C_W)
    w_main = jnp.concatenate([
        cols(OFF_R, A_W), cols(OFF_K, A_W), cols(OFF_V, A_W), cols(OFF_POOL, B_W), w_q,
        cols(OFF_KV, 6 * C_KVW), cols(OFF_WL, A_LORA_W), cols(OFF_AL, A_LORA_A), cols(OFF_GL, A_LORA_G)],
        axis=1)
    w_gate = jnp.concatenate([cols(OFF_MG, 3 * D_MODEL), cols(OFF_CG, 3 * C_HEADS),
                              jnp.zeros((D_MODEL, N_GATE - G_CG - 3 * C_HEADS), BF16)], axis=1)
    wc = w_br_c[l].reshape(C_KV, C_HPG, C_HD, D_MODEL).transpose(1, 0, 2, 3).reshape(C_W, D_MODEL)
    return dict(
        w_main=w_main, w_gate=w_gate, mu=_shift_order(shift_mu[l].reshape(1, -1)),
        wa=cast(w_br_a[l]), wb=cast(w_br_b[l]), wc=cast(wc), wo=cast(w_out[l]),
        wq=cast(w_mq[l]), wk=cast(w_mk[l]), wv=cast(w_mv[l]), wmo=cast(w_mo[l]),
        w1=cast(w_ff1[l]), w2=cast(w_ff2[l]))


def _shift_row(main_row):
    lora = main_row[:, P_LORA:]
    return jnp.concatenate([
        main_row[:, P_R:P_R + A_W], lora[:, :A_LORA_W], main_row[:, P_K:P_K + A_W],
        main_row[:, P_V:P_V + A_W], lora[:, A_LORA_W:A_LORA_W + A_LORA_A],
        lora[:, A_LORA_W + A_LORA_A:]], axis=1)


def _feature_major(z):
    z = jnp.moveaxis(z, -3, -1)
    return z.reshape(z.shape[:-3] + (C_KVW, z.shape[-1]))


def _token_major(z):
    b, _, rows = z.shape
    return z.reshape(b, C_KV, C_HD, rows).transpose(0, 3, 1, 2)


def kernel(x_prompt, x_sample, state_rwkv, state_shift, state_pool, cache_cmp_k, cache_cmp_v, cache_slc_k, cache_slc_v, cache_win_k, cache_win_v, cache_mem_k, cache_mem_v, page_table, mem_prompt, rel_bias, norm_mix_g, w_in, shift_mu, rwkv_w0, rwkv_w2, rwkv_a0, rwkv_a2, rwkv_g2, rwkv_kk, rwkv_ka, rwkv_rk, rwkv_lnx_g, rwkv_lnx_b, pool_w, pool_scale, nsa_q_g, nsa_k_g, w_br_a, w_br_b, w_br_c, w_out, norm_memx_g, norm_mem_g, w_mq, w_mk, w_mv, mem_q_g, mem_k_g, w_mo, norm_ffn_g, w_ff1, w_ff2):
    bp, t = x_prompt.shape[:2]
    bs, ts = x_sample.shape[:2]
    past = page_table.shape[1] * PAGE_SIZE
    wbuf_p = min(WINDOW, t)
    ts_pad = 16
    bnear, bcmp = _bias_tables(rel_bias)
    caches = [_feature_major(z) for z in (cache_cmp_k, cache_cmp_v, cache_slc_k, cache_slc_v)]
    win_k, win_v = _feature_major(cache_win_k), _feature_major(cache_win_v)
    xp = x_prompt.reshape(bp * t, D_MODEL)
    xs = x_sample.reshape(bs * ts, D_MODEL)
    outs_p = [[] for _ in range(11)]
    outs_s = [[] for _ in range(9)]
    row = lambda a: a.reshape(1, -1)
    for l in range(DEPTH):
        prm = _layer_params(l, w_in, shift_mu, w_br_a, w_br_b, w_br_c, w_out, w_mq, w_mk, w_mv, w_mo,
                            w_ff1, w_ff2)
        rw = (prm["mu"], row(rwkv_w0[l]), rwkv_w2[l].astype(BF16), row(rwkv_a0[l]), rwkv_a2[l].astype(BF16),
              rwkv_g2[l].astype(BF16), row(rwkv_kk[l]), row(rwkv_ka[l]), row(rwkv_rk[l]),
              row(rwkv_lnx_g[l]), row(rwkv_lnx_b[l]))
        pool_wl = pool_w[l].astype(BF16)

        main = _norm_matmul(xp, norm_mix_g[l], prm["w_main"], 512, N_MAIN).reshape(bp, t, N_MAIN)
        gate = _norm_matmul(xp, norm_mix_g[l], prm["w_gate"], 512, N_GATE)
        o_a, s_pairs = _rwkv(main, jnp.zeros((bp, 1, A_COLS), F32), jnp.zeros((bp, 4, 128, 128), F32), rw)
        o_b = _pool(main, jnp.zeros((bp, POOL_HALO, B_W), F32), pool_wl, pool_scale[l], 0, 512)
        qt, ks16, vs16, kw16, vw16, kc, vc, kv_rows = _nsa_prep(main, nsa_q_g[l], nsa_k_g[l], 512)
        front = lambda z: jnp.pad(z, ((0, 0), (FRONT_TILES * Q_TILE, 0), (0, 0)))
        front_t = lambda z: jnp.pad(z, ((0, 0), (0, 0), (FRONT_TILES * Q_TILE, 0)))
        o_c = _nsa_attn(qt, front(ks16), front_t(vs16), front(kw16), front_t(vw16), kc, vc,
                        gate.reshape(bp, t, N_GATE), bnear, bcmp)
        xp = _merge(xp, o_a.reshape(bp * t, A_W), o_b.reshape(bp * t, B_W), o_c.reshape(bp * t, C_W), gate,
                    prm["wa"], prm["wb"], prm["wc"], prm["wo"], 512)
        mk, mv = _mem_kv(mem_prompt.reshape(bp * N_MEM, D_MODEL), norm_mem_g[l], prm["wk"], prm["wv"],
                         mem_k_g[l])
        mk, mv = mk.reshape(bp, N_MEM, M_W), mv.reshape(bp, N_MEM, M_W)
        xp = _mem_ffn(xp, mk, mv, norm_memx_g[l], prm["wq"], mem_q_g[l], prm["wmo"], norm_ffn_g[l],
                      prm["w1"], prm["w2"], 1024, t)
        for dst, val in zip(outs_p, (
                _unpair_states(s_pairs), _shift_row(main[:, -1]), main[:, -POOL_BUF:, P_POOL:P_POOL + B_W],
                _token_major(kv_rows[:, 0]), _token_major(kv_rows[:, 1]), _token_major(kv_rows[:, 2]),
                _token_major(kv_rows[:, 3]), _token_major(kv_rows[:, 4, :, -wbuf_p:]),
                _token_major(kv_rows[:, 5, :, -wbuf_p:]),
                mk.reshape(bp, N_MEM, M_HEADS, M_HD), mv.reshape(bp, N_MEM, M_HEADS, M_HD))):
            dst.append(val)

        main = _norm_matmul(xs, norm_mix_g[l], prm["w_main"], bs * ts, N_MAIN).reshape(bs, ts, N_MAIN)
        gate = _norm_matmul(xs, norm_mix_g[l], prm["w_gate"], bs * ts, N_GATE)
        main_pad = jnp.pad(main, ((0, 0), (0, ts_pad - ts), (0, 0)))
        ulast = _shift_order(state_shift[l]).reshape(bs, 1, A_COLS)
        o_a, s_pairs = _rwkv(main_pad, ulast, _pair_states(state_rwkv[l]), rw, t_valid=ts)
        o_a = o_a[:, :ts]
        hist = jnp.concatenate([jnp.zeros((bs, POOL_HALO - POOL_BUF, B_W), F32), state_pool[l]], axis=1)
        o_b = _pool(main_pad, hist, pool_wl, pool_scale[l], past, ts_pad)[:, :ts]
        u_pool = main[:, :, P_POOL:P_POOL + B_W]
        o_c, new_rows = _nsa_sample(main, gate, caches, win_k, win_v, page_table, l, nsa_q_g[l], nsa_k_g[l],
                                    rel_bias)
        new_kv = [new_rows[:, n].reshape(bs, ts, C_KV, C_HD) for n in range(6)]
        slide = lambda buf, n: _token_major(jnp.concatenate(
            [buf[l][:, :, ts:], new_rows[:, n].transpose(0, 2, 1)], axis=2))
        xs = _merge(xs, o_a.reshape(bs * ts, A_W), o_b.reshape(bs * ts, B_W), o_c, gate,
                    prm["wa"], prm["wb"], prm["wc"], prm["wo"], bs * ts)
        xs = _mem_ffn(xs, cache_mem_k[l].reshape(bs, N_MEM, M_W), cache_mem_v[l].reshape(bs, N_MEM, M_W),
                      norm_memx_g[l], prm["wq"], mem_q_g[l], prm["wmo"], norm_ffn_g[l],
                      prm["w1"], prm["w2"], 32, ts)
        for dst, val in zip(outs_s, (
                _unpair_states(s_pairs), _shift_row(main[:, -1]),
                jnp.concatenate([state_pool[l], u_pool], axis=1)[:, -POOL_BUF:],
                new_kv[0], new_kv[1], new_kv[2], new_kv[3], slide(win_k, 4), slide(win_v, 5))):
            dst.append(val)

    return ((xp.reshape(bp, t, D_MODEL), xs.reshape(bs, ts, D_MODEL))
            + tuple(jnp.stack(o) for o in outs_p) + tuple(jnp.stack(o) for o in outs_s))
```
